```python
import jax, jax.numpy as jnp
from jax import lax
import numpy as np

D_MODEL = 1024
BATCH = 16
SEQ = 2048
DEPTH = 1

CHUNK = 64
LEFT_CHUNKS = 8
BAND = LEFT_CHUNKS + 1
N_MEM = 256

D_MIX = D_MODEL
ATT_HEADS = 8
ATT_HEAD_DIM = 64
D_ATT = ATT_HEADS * ATT_HEAD_DIM
CONV_GROUPS = 8
CONV_GROUP_DIM = 64
D_CONV = CONV_GROUPS * CONV_GROUP_DIM
CONV_W = 3
MAX_REL = 256
N_REL = 2 * MAX_REL + 1
D_IN_PROJ = 3 * D_ATT + 3 * D_CONV

X_HEADS = 4
X_HEAD_DIM = 128
D_X = X_HEADS * X_HEAD_DIM

N_EXPERTS = 32
TOP_K = 4
D_FF = D_MODEL
SWIGLU_LIMIT = 7.0
SWIGLU_ALPHA = 1.702
EXPERT_BLOCK = 128

EPS = 1e-6
NEG_INF = -1e30

kernel_name = "hybrid_chunk_attn_shortconv_memxattn_moe"


def _rmsnorm(x, g):
    xf = x.astype(jnp.float32)
    y = xf * lax.rsqrt(jnp.mean(xf * xf, axis=-1, keepdims=True) + EPS)
    return (y * g.astype(jnp.float32)).astype(x.dtype)


def _rel_bias_band(rel_bias):
    qi = np.arange(CHUNK)[:, None] + LEFT_CHUNKS * CHUNK
    kk = np.arange(BAND * CHUNK)[None, :]
    idx = np.clip(qi - kk, -MAX_REL, MAX_REL) + MAX_REL
    return rel_bias[:, idx]


def _chunk_attention(q, k, v, q_g, k_g, rel_bias):
    B, S = q.shape[0], q.shape[1]
    nc = S // CHUNK
    q = _rmsnorm(q, q_g).reshape(B, nc, CHUNK, ATT_HEADS, ATT_HEAD_DIM)
    k = _rmsnorm(k, k_g).reshape(B, nc, CHUNK, ATT_HEADS, ATT_HEAD_DIM)
    v = v.reshape(B, nc, CHUNK, ATT_HEADS, ATT_HEAD_DIM)
    pad = ((0, 0), (LEFT_CHUNKS, 0), (0, 0), (0, 0), (0, 0))
    kp, vp = jnp.pad(k, pad), jnp.pad(v, pad)
    k_band = jnp.concatenate([kp[:, i:i + nc] for i in range(BAND)], axis=2)
    v_band = jnp.concatenate([vp[:, i:i + nc] for i in range(BAND)], axis=2)
    scale = ATT_HEAD_DIM ** -0.5
    s = jnp.einsum('bcqhd,bckhd->bhcqk', q, k_band,
                   preferred_element_type=jnp.float32) * scale
    s = s + _rel_bias_band(rel_bias).astype(jnp.float32)[None, :, None]
    key_chunk = (jnp.arange(nc)[:, None] - LEFT_CHUNKS
                 + (jnp.arange(BAND * CHUNK)[None, :] // CHUNK))
    s = jnp.where((key_chunk >= 0)[None, None, :, None, :], s, NEG_INF)
    p = jax.nn.softmax(s, axis=-1).astype(v.dtype)
    o = jnp.einsum('bhcqk,bckhd->bcqhd', p, v_band)
    return o.reshape(B, S, D_ATT)


def _short_conv(bg, cg, xv, conv_w):
    S = xv.shape[1]
    u = jnp.pad(cg * xv, ((0, 0), (CONV_W - 1, 0), (0, 0)))
    conv = conv_w[0] * u[:, 0:S] + conv_w[1] * u[:, 1:S + 1] + conv_w[2] * u[:, 2:S + 2]
    return bg * conv


def _memory_xattn(h, mem, norm_mem_g, w_xq, w_xkv, xq_g, xk_g, w_xo):
    B, S, _ = h.shape
    hm = _rmsnorm(mem, norm_mem_g)
    q = (h @ w_xq).reshape(B, S, X_HEADS, X_HEAD_DIM)
    k, v = jnp.split(hm @ w_xkv, 2, axis=-1)
    k = k.reshape(B, N_MEM, X_HEADS, X_HEAD_DIM)
    v = v.reshape(B, N_MEM, X_HEADS, X_HEAD_DIM)
    q, k = _rmsnorm(q, xq_g), _rmsnorm(k, xk_g)
    s = jnp.einsum('bshd,bmhd->bhsm', q, k,
                   preferred_element_type=jnp.float32) * (X_HEAD_DIM ** -0.5)
    p = jax.nn.softmax(s, axis=-1).astype(v.dtype)
    o = jnp.einsum('bhsm,bmhd->bshd', p, v).reshape(B, S, D_X)
    return o @ w_xo


def _clamped_swiglu(gu):
    g = jnp.minimum(gu[..., ::2], SWIGLU_LIMIT)
    lin = jnp.clip(gu[..., 1::2], -SWIGLU_LIMIT, SWIGLU_LIMIT)
    return g * jax.nn.sigmoid(SWIGLU_ALPHA * g) * (lin + 1.0)


def _moe(h, w_router, b_router, w_gate_up, b_gate_up, w_down, b_down):
    B, S, D = h.shape
    n_tok = B * S
    hf = h.reshape(n_tok, D)
    logits = (hf @ w_router).astype(jnp.float32) + b_router.astype(jnp.float32)
    top_val, top_idx = lax.top_k(logits, TOP_K)
    gates = jax.nn.softmax(top_val, axis=-1)
    n_assign = n_tok * TOP_K
    expert_ids = top_idx.reshape(-1)
    token_ids = jnp.arange(n_assign, dtype=jnp.int32) // TOP_K
    order = jnp.argsort(expert_ids)
    sorted_e = expert_ids[order]
    sorted_tok = token_ids[order]
    sorted_gate = gates.reshape(-1)[order]
    counts = jnp.bincount(expert_ids, length=N_EXPERTS)
    padded = (counts + EXPERT_BLOCK - 1) // EXPERT_BLOCK * EXPERT_BLOCK
    start = jnp.cumsum(counts) - counts
    pend = jnp.cumsum(padded)
    pstart = pend - padded
    dest = pstart[sorted_e] + (jnp.arange(n_assign, dtype=jnp.int32) - start[sorted_e])
    n_pad = n_assign + N_EXPERTS * EXPERT_BLOCK
    n_blocks = n_pad // EXPERT_BLOCK
    tok_pad = jnp.zeros((n_pad,), jnp.int32).at[dest].set(sorted_tok)
    block_e = jnp.searchsorted(pend, jnp.arange(n_blocks) * EXPERT_BLOCK, side='right')
    block_e = jnp.minimum(block_e, N_EXPERTS - 1).astype(jnp.int32)

    def expert_block(args):
        tok, e = args
        xb = hf[tok]
        act = _clamped_swiglu(xb @ w_gate_up[e] + b_gate_up[e])
        return act @ w_down[e] + b_down[e]

    y_pad = lax.map(expert_block, (tok_pad.reshape(n_blocks, EXPERT_BLOCK), block_e))
    y_sorted = y_pad.reshape(n_pad, D)[dest].astype(jnp.float32)
    out = jnp.zeros((n_tok, D), jnp.float32).at[sorted_tok].add(sorted_gate[:, None] * y_sorted)
    return out.astype(h.dtype).reshape(B, S, D)


def setup_inputs(seed: int = 0) -> dict:
    key = jax.random.key(seed)
    ks = jax.random.split(key, 26)
    f32 = jnp.float32

    def nrm(k, shape, scale):
        return jax.random.normal(k, shape, f32) * scale

    def gain(k, shape):
        return 1.0 + 0.1 * jax.random.normal(k, shape, f32)

    L = DEPTH
    return {
        "x": nrm(ks[0], (BATCH, SEQ, D_MODEL), 1.0),
        "mem": nrm(ks[1], (BATCH, N_MEM, D_MODEL), 1.0),
        "norm_mix_g": gain(ks[2], (L, D_MODEL)),
        "w_in": nrm(ks[3], (L, D_MODEL, D_IN_PROJ), D_MODEL ** -0.5),
        "q_norm_g": gain(ks[4], (L, ATT_HEAD_DIM)),
        "k_norm_g": gain(ks[5], (L, ATT_HEAD_DIM)),
        "rel_bias": nrm(ks[6], (L, ATT_HEADS, N_REL), 0.5),
        "conv_w": nrm(ks[7], (L, CONV_W, D_CONV), 0.5),
        "out_norm_a_g": gain(ks[8], (L, D_ATT)),
        "out_norm_b_g": gain(ks[9], (L, D_CONV)),
        "w_out": nrm(ks[10], (L, D_MIX, D_MODEL), D_MIX ** -0.5),
        "norm_xattn_g": gain(ks[11], (L, D_MODEL)),
        "norm_mem_g": gain(ks[12], (L, D_MODEL)),
        "w_xq": nrm(ks[13], (L, D_MODEL, D_X), D_MODEL ** -0.5),
        "w_xkv": nrm(ks[14], (L, D_MODEL, 2 * D_X), D_MODEL ** -0.5),
        "xq_norm_g": gain(ks[15], (L, X_HEAD_DIM)),
        "xk_norm_g": gain(ks[16], (L, X_HEAD_DIM)),
        "w_xo": nrm(ks[17], (L, D_X, D_MODEL), D_X ** -0.5),
        "norm_ffn_g": gain(ks[18], (L, D_MODEL)),
        "w_router": nrm(ks[19], (L, D_MODEL, N_EXPERTS), D_MODEL ** -0.5),
        "b_router": nrm(ks[20], (L, N_EXPERTS), 0.01),
        "w_gate_up": nrm(ks[21], (L, N_EXPERTS, D_MODEL, 2 * D_FF), D_MODEL ** -0.5),
        "b_gate_up": nrm(ks[22], (L, N_EXPERTS, 2 * D_FF), 0.02),
        "w_down": nrm(ks[23], (L, N_EXPERTS, D_FF, D_MODEL), D_FF ** -0.5),
        "b_down": nrm(ks[24], (L, N_EXPERTS, D_MODEL), 0.02),
    }


def reference(x, mem, norm_mix_g, w_in, q_norm_g, k_norm_g, rel_bias, conv_w,
              out_norm_a_g, out_norm_b_g, w_out, norm_xattn_g, norm_mem_g,
              w_xq, w_xkv, xq_norm_g, xk_norm_g, w_xo, norm_ffn_g,
              w_router, b_router, w_gate_up, b_gate_up, w_down, b_down):
    B, S, _ = x.shape
    splits = [D_ATT, 2 * D_ATT, 3 * D_ATT, 3 * D_ATT + D_CONV, 3 * D_ATT + 2 * D_CONV]
    for l in range(DEPTH):
        h = _rmsnorm(x, norm_mix_g[l])
        proj = h @ w_in[l]
        q, k, v, bg, cg, xv = jnp.split(proj, splits, axis=-1)
        shp = (B, S, ATT_HEADS, ATT_HEAD_DIM)
        y_a = _chunk_attention(q.reshape(shp), k.reshape(shp), v.reshape(shp),
                               q_norm_g[l], k_norm_g[l], rel_bias[l])
        y_b = _short_conv(bg, cg, xv, conv_w[l])
        y = jnp.concatenate([_rmsnorm(y_a, out_norm_a_g[l]),
                             _rmsnorm(y_b, out_norm_b_g[l])], axis=-1)
        x = x + y @ w_out[l]
        x = x + _memory_xattn(_rmsnorm(x, norm_xattn_g[l]), mem, norm_mem_g[l],
                              w_xq[l], w_xkv[l], xq_norm_g[l], xk_norm_g[l], w_xo[l])
        x = x + _moe(_rmsnorm(x, norm_ffn_g[l]), w_router[l], b_router[l],
                     w_gate_up[l], b_gate_up[l], w_down[l], b_down[l])
    return x
```

```python
import functools

import jax
import jax.numpy as jnp
import numpy as np
from jax import lax
from jax.experimental import pallas as pl
from jax.experimental.pallas import tpu as pltpu

D_MODEL = 1024
CHUNK = 64
LEFT_CHUNKS = 8
N_MEM = 256
ATT_HEADS = 8
ATT_HEAD_DIM = 64
D_ATT = ATT_HEADS * ATT_HEAD_DIM
D_CONV = 512
MAX_REL = 256
X_HEADS = 4
X_HEAD_DIM = 128
D_X = X_HEADS * X_HEAD_DIM
N_EXPERTS = 32
TOP_K = 4
D_FF = D_MODEL
SWIGLU_LIMIT = 7.0
SWIGLU_ALPHA = 1.702
EPS = 1e-6
NEG_INF = -1e30

LANES = 128
SEQ_TILE = 512
Q_TILE = 128
K_BAND = Q_TILE + LEFT_CHUNKS * CHUNK
EXPERT_TILE = 256
COMBINE_TILE = 256
VMEM_LIMIT = 56 * 1024 * 1024

bf16 = jnp.bfloat16
f32 = jnp.float32


def _rms(x, g):
    return x * lax.rsqrt(jnp.mean(x * x, axis=-1, keepdims=True) + EPS) * g


def _dot(a, b):
    return jnp.dot(a, b, preferred_element_type=f32)


def _dot_nt(a, b):
    return lax.dot_general(a, b, (((1,), (1,)), ((), ())), preferred_element_type=f32)


def _mix_in_kernel(x_ref, g_ref, w_ref, qg_ref, kg_ref, hm_ref, cw_ref, gb_ref,
                   q_ref, k_ref, v_ref, yb_ref, ub_ref):
    ts = x_ref.shape[1]

    @pl.when(pl.program_id(1) == 0)
    def _():
        ub_ref[0:8, :] = jnp.zeros((8, D_CONV), f32)

    hb = _rms(x_ref[0], g_ref[...]).astype(bf16)

    def head_norm(t, gain):
        ms = _dot((t * t).astype(bf16), hm_ref[...]) * (1.0 / ATT_HEAD_DIM)
        return t * lax.rsqrt(ms + EPS) * gain

    q = _dot(hb, w_ref[:, 0:D_ATT])
    q_ref[0] = head_norm(q, qg_ref[...]).astype(bf16)
    k = _dot(hb, w_ref[:, D_ATT:2 * D_ATT])
    k_ref[0] = head_norm(k, kg_ref[...]).astype(bf16)
    v_ref[0] = _dot(hb, w_ref[:, 2 * D_ATT:3 * D_ATT]).astype(bf16)

    o = 3 * D_ATT
    bg = _dot(hb, w_ref[:, o:o + D_CONV])
    cg = _dot(hb, w_ref[:, o + D_CONV:o + 2 * D_CONV])
    xv = _dot(hb, w_ref[:, o + 2 * D_CONV:o + 3 * D_CONV])
    u = cg * xv
    ub_ref[8:8 + ts, :] = u
    conv = (cw_ref[0:1, :] * ub_ref[6:6 + ts, :] + cw_ref[1:2, :] * ub_ref[7:7 + ts, :]
            + cw_ref[2:3, :] * u)
    ub_ref[0:8, :] = u[ts - 8:ts, :]
    yb_ref[0] = _rms(bg * conv, gb_ref[...]).astype(bf16)


def _mix_in(x, g, w_in, qg, kg, hm, cw, gb):
    B, S, D = x.shape
    ts = SEQ_TILE
    full = lambda shape: pl.BlockSpec(shape, lambda b, j: (0,) * len(shape))
    tile = lambda w: pl.BlockSpec((1, ts, w), lambda b, j: (b, j, 0))
    out = jax.ShapeDtypeStruct((B, S, D_ATT), bf16)
    return pl.pallas_call(
        _mix_in_kernel,
        grid=(B, S // ts),
        in_specs=[tile(D), full((1, D)), full(w_in.shape), full((1, D_ATT)), full((1, D_ATT)),
                  full((D_ATT, D_ATT)), full((3, D_CONV)), full((1, D_CONV))],
        out_specs=[tile(D_ATT)] * 4,
        out_shape=[out] * 4,
        scratch_shapes=[pltpu.VMEM((ts + 8, D_CONV), f32)],
        compiler_params=pltpu.CompilerParams(
            dimension_semantics=("arbitrary", "arbitrary"), vmem_limit_bytes=VMEM_LIMIT),
        name="mix_in",
    )(x, g, w_in, qg, kg, hm, cw, gb)


def _attn_out_kernel(q_ref, k_ref, v_ref, bias_ref, yb_ref, x_ref, w_ref, ga_ref,
                     o_ref, kb_ref, vb_ref):
    tq = q_ref.shape[1]
    n_kc = K_BAND // CHUNK
    c0 = pl.program_id(1) * (tq // CHUNK)
    for kc in range(n_kc):
        src = pl.multiple_of(jnp.maximum(c0 - LEFT_CHUNKS + kc, 0) * CHUNK, CHUNK)
        kb_ref[kc * CHUNK:(kc + 1) * CHUNK, :] = k_ref[0, pl.ds(src, CHUNK), :]
        vb_ref[kc * CHUNK:(kc + 1) * CHUNK, :] = v_ref[0, pl.ds(src, CHUNK), :]

    col_chunk = lax.broadcasted_iota(jnp.int32, (1, K_BAND), 1) // CHUNK
    start_mask = jnp.where(col_chunk >= LEFT_CHUNKS - c0, 0.0, NEG_INF).astype(f32)
    lane = lax.broadcasted_iota(jnp.int32, (1, LANES), 1)
    low = lane < ATT_HEAD_DIM

    pairs = []
    for hp in range(ATT_HEADS // 2):
        sl = slice(hp * LANES, (hp + 1) * LANES)
        qp = q_ref[0, :, sl]
        kp = kb_ref[:, sl]
        vp = vb_ref[:, sl]
        outs = []
        for sub in range(2):
            keep = low if sub == 0 else jnp.logical_not(low)
            qm = jnp.where(keep, qp, jnp.zeros_like(qp))
            s = _dot_nt(qm, kp) + bias_ref[2 * hp + sub] + start_mask
            m = jnp.max(s, axis=-1, keepdims=True)
            p = jnp.exp(s - m)
            l = jnp.sum(p, axis=-1, keepdims=True)
            outs.append(_dot(p.astype(bf16), vp) / l)
        pairs.append(jnp.where(low, outs[0], outs[1]))
    ya = jnp.concatenate(pairs, axis=-1)
    yan = _rms(ya, ga_ref[...]).astype(bf16)
    o_ref[0] = (x_ref[0] + _dot(yan, w_ref[0:D_ATT, :]) + _dot(yb_ref[0], w_ref[D_ATT:, :]))


def _attn_out(q, k, v, bias, yb, x, w_out, ga):
    B, S, D = x.shape
    tq = Q_TILE
    full = lambda shape: pl.BlockSpec(shape, lambda b, j: (0,) * len(shape))
    tile = lambda w: pl.BlockSpec((1, tq, w), lambda b, j: (b, j, 0))
    seq = pl.BlockSpec((1, S, D_ATT), lambda b, j: (b, 0, 0))
    return pl.pallas_call(
        _attn_out_kernel,
        grid=(B, S // tq),
        in_specs=[tile(D_ATT), seq, seq, full(bias.shape), tile(D_CONV), tile(D),
                  full(w_out.shape), full((1, D_ATT))],
        out_specs=tile(D),
        out_shape=jax.ShapeDtypeStruct((B, S, D), f32),
        scratch_shapes=[pltpu.VMEM((K_BAND, D_ATT), bf16), pltpu.VMEM((K_BAND, D_ATT), bf16)],
        compiler_params=pltpu.CompilerParams(
            dimension_semantics=("parallel", "parallel"), vmem_limit_bytes=VMEM_LIMIT),
        name="attn_out",
    )(q, k, v, bias, yb, x, w_out, ga)


def _mem_kv_kernel(m_ref, g_ref, w_ref, kg_ref, k_ref, v_ref):
    hm = _rms(m_ref[0], g_ref[...]).astype(bf16)
    kv = _dot(hm, w_ref[...])
    for h in range(X_HEADS):
        sl = slice(h * X_HEAD_DIM, (h + 1) * X_HEAD_DIM)
        k_ref[0, :, sl] = _rms(kv[:, sl], kg_ref[...]).astype(bf16)
    v_ref[0] = kv[:, D_X:].astype(bf16)


def _mem_kv(mem, g, w_xkv, kg):
    B = mem.shape[0]
    full = lambda shape: pl.BlockSpec(shape, lambda b: (0,) * len(shape))
    out = jax.ShapeDtypeStruct((B, N_MEM, D_X), bf16)
    return pl.pallas_call(
        _mem_kv_kernel,
        grid=(B,),
        in_specs=[pl.BlockSpec((1, N_MEM, D_MODEL), lambda b: (b, 0, 0)), full((1, D_MODEL)),
                  full(w_xkv.shape), full((1, X_HEAD_DIM))],
        out_specs=[pl.BlockSpec((1, N_MEM, D_X), lambda b: (b, 0, 0))] * 2,
        out_shape=[out, out],
        compiler_params=pltpu.CompilerParams(
            dimension_semantics=("parallel",), vmem_limit_bytes=VMEM_LIMIT),
        name="mem_kv",
    )(mem, g, w_xkv, kg)


def _xattn_router_kernel(x_ref, gx_ref, wq_ref, qg_ref, k_ref, v_ref, wo_ref, gf_ref,
                         wrh_ref, wrl_ref, br_ref,
                         x2_ref, hf_ref, idx_ref, gate_ref):
    x1 = x_ref[0]
    hb = _rms(x1, gx_ref[...]).astype(bf16)
    q = _dot(hb, wq_ref[...])
    heads = []
    for h in range(X_HEADS):
        sl = slice(h * X_HEAD_DIM, (h + 1) * X_HEAD_DIM)
        qh = _rms(q[:, sl], qg_ref[...]).astype(bf16)
        s = _dot_nt(qh, k_ref[0, :, sl])
        m = jnp.max(s, axis=-1, keepdims=True)
        p = jnp.exp(s - m)
        l = jnp.sum(p, axis=-1, keepdims=True)
        heads.append((_dot(p.astype(bf16), v_ref[0, :, sl]) / l).astype(bf16))
    o = jnp.concatenate(heads, axis=-1)
    x2 = x1 + _dot(o, wo_ref[...])
    x2_ref[0] = x2

    hf = _rms(x2, gf_ref[...])
    hf_hi = hf.astype(bf16)
    hf_ref[0] = hf_hi
    hf_lo = (hf - hf_hi.astype(f32)).astype(bf16)
    logits = (_dot(hf_hi, wrh_ref[...]) + _dot(hf_lo, wrh_ref[...]) + _dot(hf_hi, wrl_ref[...])
              + br_ref[...])
    lane = lax.broadcasted_iota(jnp.int32, logits.shape, 1)
    vals, idxs = [], []
    for _ in range(TOP_K):
        m = jnp.max(logits, axis=-1, keepdims=True)
        i = jnp.min(jnp.where(logits == m, lane, LANES), axis=-1, keepdims=True)
        logits = jnp.where(lane == i, -jnp.inf, logits)
        vals.append(m)
        idxs.append(i)
    es = [jnp.exp(v - vals[0]) for v in vals]
    tot = es[0] + es[1] + es[2] + es[3]
    idx_out = jnp.zeros(logits.shape, jnp.int32)
    gate_out = jnp.zeros(logits.shape, f32)
    for kk in range(TOP_K):
        idx_out = jnp.where(lane == kk, idxs[kk], idx_out)
        gate_out = jnp.where(lane == kk, es[kk] / tot, gate_out)
    idx_ref[0] = idx_out
    gate_ref[0] = gate_out


def _xattn_router(x1, gx, w_xq, qg, kx, vx, w_xo, gf, wr_hi, wr_lo, br):
    B, S, D = x1.shape
    ts = SEQ_TILE
    full = lambda shape: pl.BlockSpec(shape, lambda b, j: (0,) * len(shape))
    tile = lambda w: pl.BlockSpec((1, ts, w), lambda b, j: (b, j, 0))
    mem = pl.BlockSpec((1, N_MEM, D_X), lambda b, j: (b, 0, 0))
    return pl.pallas_call(
        _xattn_router_kernel,
        grid=(B, S // ts),
        in_specs=[tile(D), full((1, D)), full(w_xq.shape), full((1, X_HEAD_DIM)), mem, mem,
                  full(w_xo.shape), full((1, D)), full(wr_hi.shape), full(wr_lo.shape),
                  full((1, LANES))],
        out_specs=[tile(D), tile(D), tile(LANES), tile(LANES)],
        out_shape=[jax.ShapeDtypeStruct((B, S, D), f32), jax.ShapeDtypeStruct((B, S, D), bf16),
                   jax.ShapeDtypeStruct((B, S, LANES), jnp.int32),
                   jax.ShapeDtypeStruct((B, S, LANES), f32)],
        compiler_params=pltpu.CompilerParams(
            dimension_semantics=("parallel", "parallel"), vmem_limit_bytes=VMEM_LIMIT),
        name="xattn_router",
    )(x1, gx, w_xq, qg, kx, vx, w_xo, gf, wr_hi, wr_lo, br)


def _expert_kernel(be_ref, x_ref, wgu_ref, bgu_ref, wdn_ref, bdn_ref, y_ref):
    del be_ref
    gu = _dot(x_ref[...], wgu_ref[0]) + bgu_ref[0]
    lane = lax.broadcasted_iota(jnp.int32, (1, LANES), 1)
    even = (lane & 1) == 0
    g = jnp.minimum(gu, SWIGLU_LIMIT)
    glu = g * jax.nn.sigmoid(SWIGLU_ALPHA * g)
    lin = jnp.clip(gu, -SWIGLU_LIMIT, SWIGLU_LIMIT) + 1.0
    merged = []
    for p in range(D_FF // LANES):
        ta = jnp.where(even, glu[:, (2 * p) * LANES:(2 * p + 1) * LANES],
                       lin[:, (2 * p) * LANES:(2 * p + 1) * LANES])
        tb = jnp.where(even, glu[:, (2 * p + 1) * LANES:(2 * p + 2) * LANES],
                       lin[:, (2 * p + 1) * LANES:(2 * p + 2) * LANES])
        merged.append(jnp.where(even, ta * pltpu.roll(ta, LANES - 1, axis=1),
                                tb * pltpu.roll(tb, 1, axis=1)))
    act = jnp.concatenate(merged, axis=-1).astype(bf16)
    y_ref[...] = _dot(act, wdn_ref[0]) + bdn_ref[0]


def _expert_mlp(block_e, xs, w_gu, b_gu, w_dn, b_dn):
    n_pad, D = xs.shape
    tm = EXPERT_TILE
    grid_spec = pltpu.PrefetchScalarGridSpec(
        num_scalar_prefetch=1,
        grid=(n_pad // tm,),
        in_specs=[pl.BlockSpec((tm, D), lambda b, be: (b, 0)),
                  pl.BlockSpec((1, D, 2 * D_FF), lambda b, be: (be[b], 0, 0)),
                  pl.BlockSpec((1, 1, 2 * D_FF), lambda b, be: (be[b], 0, 0)),
                  pl.BlockSpec((1, D_FF, D), lambda b, be: (be[b], 0, 0)),
                  pl.BlockSpec((1, 1, D), lambda b, be: (be[b], 0, 0))],
        out_specs=pl.BlockSpec((tm, D), lambda b, be: (b, 0)),
    )
    return pl.pallas_call(
        _expert_kernel,
        grid_spec=grid_spec,
        out_shape=jax.ShapeDtypeStruct((n_pad, D), f32),
        compiler_params=pltpu.CompilerParams(
            dimension_semantics=("arbitrary",), vmem_limit_bytes=VMEM_LIMIT),
        name="expert_mlp",
    )(block_e, xs, w_gu, b_gu, w_dn, b_dn)


def _combine_kernel(x_ref, y_ref, g_ref, o_ref):
    acc = x_ref[...]
    for kk in range(TOP_K):
        acc = acc + g_ref[:, kk:kk + 1] * y_ref[:, kk * D_MODEL:(kk + 1) * D_MODEL]
    o_ref[...] = acc


def _combine(x2, y4, gates):
    N, D = x2.shape
    tc = COMBINE_TILE
    return pl.pallas_call(
        _combine_kernel,
        grid=(N // tc,),
        in_specs=[pl.BlockSpec((tc, D), lambda i: (i, 0)),
                  pl.BlockSpec((tc, TOP_K * D), lambda i: (i, 0)),
                  pl.BlockSpec((tc, LANES), lambda i: (i, 0))],
        out_specs=pl.BlockSpec((tc, D), lambda i: (i, 0)),
        out_shape=jax.ShapeDtypeStruct((N, D), f32),
        compiler_params=pltpu.CompilerParams(
            dimension_semantics=("parallel",), vmem_limit_bytes=VMEM_LIMIT),
        name="combine",
    )(x2, y4, gates)


def _band_bias(rel_bias):
    qi = np.arange(Q_TILE)[:, None]
    kk = np.arange(K_BAND)[None, :]
    idx = np.clip(qi + LEFT_CHUNKS * CHUNK - kk, -MAX_REL, MAX_REL) + MAX_REL
    qc, kc = qi // CHUNK, kk // CHUNK
    visible = (kc >= qc) & (kc <= qc + LEFT_CHUNKS)
    return jnp.where(visible[None], rel_bias[:, idx], NEG_INF).astype(f32)


def _layer(x, mem, norm_mix_g, w_in, q_norm_g, k_norm_g, rel_bias, conv_w, out_norm_a_g,
           out_norm_b_g, w_out, norm_xattn_g, norm_mem_g, w_xq, w_xkv, xq_norm_g, xk_norm_g,
           w_xo, norm_ffn_g, w_router, b_router, w_gate_up, b_gate_up, w_down, b_down):
    B, S, D = x.shape
    n_tok = B * S
    row = lambda a: a.reshape(1, -1).astype(f32)

    qg = row(jnp.tile(q_norm_g, ATT_HEADS) * (ATT_HEAD_DIM ** -0.5))
    kg = row(jnp.tile(k_norm_g, ATT_HEADS))
    head_ones = jnp.asarray(np.kron(np.eye(ATT_HEADS), np.ones((ATT_HEAD_DIM, ATT_HEAD_DIM))), bf16)
    q, k, v, yb = _mix_in(x, row(norm_mix_g), w_in.astype(bf16), qg, kg, head_ones,
                          conv_w.astype(f32), row(out_norm_b_g))
    x1 = _attn_out(q, k, v, _band_bias(rel_bias), yb, x, w_out.astype(bf16), row(out_norm_a_g))

    kx, vx = _mem_kv(mem, row(norm_mem_g), w_xkv.astype(bf16), row(xk_norm_g))
    wr = jnp.zeros((D, LANES), f32).at[:, :N_EXPERTS].set(w_router)
    wr_hi = wr.astype(bf16)
    wr_lo = (wr - wr_hi.astype(f32)).astype(bf16)
    br = jnp.full((1, LANES), -jnp.inf, f32).at[0, :N_EXPERTS].set(b_router)
    x2, hf, idx, gates = _xattn_router(
        x1, row(norm_xattn_g), w_xq.astype(bf16), row(xq_norm_g * (X_HEAD_DIM ** -0.5)), kx, vx,
        w_xo.astype(bf16), row(norm_ffn_g), wr_hi, wr_lo, br)

    tm = EXPERT_TILE
    n_assign = n_tok * TOP_K
    n_pad = n_assign + N_EXPERTS * tm
    expert_ids = idx.reshape(n_tok, LANES)[:, :TOP_K].reshape(-1)
    onehot = (expert_ids[:, None] == jnp.arange(N_EXPERTS)[None, :]).astype(jnp.int32)
    rank = jnp.take_along_axis(jnp.cumsum(onehot, axis=0), expert_ids[:, None], axis=1)[:, 0] - 1
    counts = jnp.sum(onehot, axis=0)
    padded = (counts + tm - 1) // tm * tm
    pend = jnp.cumsum(padded)
    dest = (pend - padded)[expert_ids] + rank
    token_ids = jnp.arange(n_assign, dtype=jnp.int32) // TOP_K
    tok_pad = jnp.zeros((n_pad,), jnp.int32).at[dest].set(token_ids)
    block_e = jnp.searchsorted(pend, jnp.arange(n_pad // tm) * tm, side='right')
    block_e = jnp.minimum(block_e, N_EXPERTS - 1).astype(jnp.int32)

    hf2 = hf.reshape(n_tok, D)
    xs = hf2[tok_pad]
    w_dn = (w_down.reshape(N_EXPERTS, D_FF // LANES, 2, LANES // 2, D)
            .transpose(0, 1, 3, 2, 4).reshape(N_EXPERTS, D_FF, D).astype(bf16))
    y_pad = _expert_mlp(block_e, xs, w_gate_up.astype(bf16),
                        b_gate_up.reshape(N_EXPERTS, 1, 2 * D_FF).astype(f32),
                        w_dn, b_down.reshape(N_EXPERTS, 1, D).astype(f32))
    y4 = y_pad[dest].reshape(n_tok, TOP_K * D)
    out = _combine(x2.reshape(n_tok, D), y4, gates.reshape(n_tok, LANES))
    return out.reshape(B, S, D)


def kernel(x, mem, norm_mix_g, w_in, q_norm_g, k_norm_g, rel_bias, conv_w, out_norm_a_g,
           out_norm_b_g, w_out, norm_xattn_g, norm_mem_g, w_xq, w_xkv, xq_norm_g, xk_norm_g,
           w_xo, norm_ffn_g, w_router, b_router, w_gate_up, b_gate_up, w_down, b_down):
    depth = norm_mix_g.shape[0]
    for l in range(depth):
        x = _layer(x, mem, norm_mix_g[l], w_in[l], q_norm_g[l], k_norm_g[l], rel_bias[l],
                   conv_w[l], out_norm_a_g[l], out_norm_b_g[l], w_out[l], norm_xattn_g[l],
                   norm_mem_g[l], w_xq[l], w_xkv[l], xq_norm_g[l], xk_norm_g[l], w_xo[l],
                   norm_ffn_g[l], w_router[l], b_router[l], w_gate_up[l], b_gate_up[l],
                   w_down[l], b_down[l])
    return x
```

```python
import functools

import jax
import jax.numpy as jnp
import numpy as np
from jax import lax
from jax.experimental import pallas as pl
from jax.experimental.pallas import tpu as pltpu

D_MODEL = 1024
CHUNK = 64
LEFT_CHUNKS = 8
N_MEM = 256
ATT_HEADS = 8
ATT_HEAD_DIM = 64
D_ATT = ATT_HEADS * ATT_HEAD_DIM
D_CONV = 512
MAX_REL = 256
X_HEADS = 4
X_HEAD_DIM = 128
D_X = X_HEADS * X_HEAD_DIM
N_EXPERTS = 32
TOP_K = 4
D_FF = D_MODEL
SWIGLU_LIMIT = 7.0
SWIGLU_ALPHA = 1.702
EPS = 1e-6
NEG_INF = -1e30

LANES = 128
SEQ_TILE = 512
Q_TILE = 128
K_BAND = Q_TILE + LEFT_CHUNKS * CHUNK
EXPERT_TILE = 256
GROUP = 256
SUBLANES = 8
GROUP_ROWS = 1280
VMEM_LIMIT = 56 * 1024 * 1024

bf16 = jnp.bfloat16
f32 = jnp.float32


def _rms(x, g):
    return x * lax.rsqrt(jnp.mean(x * x, axis=-1, keepdims=True) + EPS) * g


def _dot(a, b):
    return jnp.dot(a, b, preferred_element_type=f32)


def _dot_nt(a, b):
    return lax.dot_general(a, b, (((1,), (1,)), ((), ())), preferred_element_type=f32)


def _mix_in_kernel(x_ref, g_ref, w_ref, qg_ref, kg_ref, hm_ref, cw_ref, gb_ref,
                   q_ref, k_ref, v_ref, yb_ref, ub_ref):
    ts = x_ref.shape[1]

    @pl.when(pl.program_id(1) == 0)
    def _():
        ub_ref[0:8, :] = jnp.zeros((8, D_CONV), f32)

    hb = _rms(x_ref[0], g_ref[...]).astype(bf16)

    def head_norm(t, gain):
        ms = _dot((t * t).astype(bf16), hm_ref[...]) * (1.0 / ATT_HEAD_DIM)
        return t * lax.rsqrt(ms + EPS) * gain

    q = _dot(hb, w_ref[:, 0:D_ATT])
    q_ref[0] = head_norm(q, qg_ref[...]).astype(bf16)
    k = _dot(hb, w_ref[:, D_ATT:2 * D_ATT])
    k_ref[0] = head_norm(k, kg_ref[...]).astype(bf16)
    v_ref[0] = _dot(hb, w_ref[:, 2 * D_ATT:3 * D_ATT]).astype(bf16)

    o = 3 * D_ATT
    bg = _dot(hb, w_ref[:, o:o + D_CONV])
    cg = _dot(hb, w_ref[:, o + D_CONV:o + 2 * D_CONV])
    xv = _dot(hb, w_ref[:, o + 2 * D_CONV:o + 3 * D_CONV])
    u = cg * xv
    ub_ref[8:8 + ts, :] = u
    conv = (cw_ref[0:1, :] * ub_ref[6:6 + ts, :] + cw_ref[1:2, :] * ub_ref[7:7 + ts, :]
            + cw_ref[2:3, :] * u)
    ub_ref[0:8, :] = u[ts - 8:ts, :]
    yb_ref[0] = _rms(bg * conv, gb_ref[...]).astype(bf16)


def _mix_in(x, g, w_in, qg, kg, hm, cw, gb):
    B, S, D = x.shape
    ts = SEQ_TILE
    full = lambda shape: pl.BlockSpec(shape, lambda b, j: (0,) * len(shape))
    tile = lambda w: pl.BlockSpec((1, ts, w), lambda b, j: (b, j, 0))
    out = jax.ShapeDtypeStruct((B, S, D_ATT), bf16)
    return pl.pallas_call(
        _mix_in_kernel,
        grid=(B, S // ts),
        in_specs=[tile(D), full((1, D)), full(w_in.shape), full((1, D_ATT)), full((1, D_ATT)),
                  full((D_ATT, D_ATT)), full((3, D_CONV)), full((1, D_CONV))],
        out_specs=[tile(D_ATT)] * 4,
        out_shape=[out] * 4,
        scratch_shapes=[pltpu.VMEM((ts + 8, D_CONV), f32)],
        compiler_params=pltpu.CompilerParams(
            dimension_semantics=("arbitrary", "arbitrary"), vmem_limit_bytes=VMEM_LIMIT),
        name="mix_in",
    )(x, g, w_in, qg, kg, hm, cw, gb)


def _attn_out_kernel(q_ref, k_ref, v_ref, bias_ref, yb_ref, x_ref, w_ref, ga_ref,
                     o_ref, kb_ref, vb_ref):
    tq = q_ref.shape[1]
    n_kc = K_BAND // CHUNK
    c0 = pl.program_id(1) * (tq // CHUNK)
    for kc in range(n_kc):
        src = pl.multiple_of(jnp.maximum(c0 - LEFT_CHUNKS + kc, 0) * CHUNK, CHUNK)
        kb_ref[kc * CHUNK:(kc + 1) * CHUNK, :] = k_ref[0, pl.ds(src, CHUNK), :]
        vb_ref[kc * CHUNK:(kc + 1) * CHUNK, :] = v_ref[0, pl.ds(src, CHUNK), :]

    col_chunk = lax.broadcasted_iota(jnp.int32, (1, K_BAND), 1) // CHUNK
    start_mask = jnp.where(col_chunk >= LEFT_CHUNKS - c0, 0.0, NEG_INF).astype(f32)
    lane = lax.broadcasted_iota(jnp.int32, (1, LANES), 1)
    low = lane < ATT_HEAD_DIM

    pairs = []
    for hp in range(ATT_HEADS // 2):
        sl = slice(hp * LANES, (hp + 1) * LANES)
        qp = q_ref[0, :, sl]
        kp = kb_ref[:, sl]
        vp = vb_ref[:, sl]
        outs = []
        for sub in range(2):
            keep = low if sub == 0 else jnp.logical_not(low)
            qm = jnp.where(keep, qp, jnp.zeros_like(qp))
            s = _dot_nt(qm, kp) + bias_ref[2 * hp + sub] + start_mask
            m = jnp.max(s, axis=-1, keepdims=True)
            p = jnp.exp(s - m)
            l = jnp.sum(p, axis=-1, keepdims=True)
            outs.append(_dot(p.astype(bf16), vp) / l)
        pairs.append(jnp.where(low, outs[0], outs[1]))
    ya = jnp.concatenate(pairs, axis=-1)
    yan = _rms(ya, ga_ref[...]).astype(bf16)
    o_ref[0] = (x_ref[0] + _dot(yan, w_ref[0:D_ATT, :]) + _dot(yb_ref[0], w_ref[D_ATT:, :]))


def _attn_out(q, k, v, bias, yb, x, w_out, ga):
    B, S, D = x.shape
    tq = Q_TILE
    full = lambda shape: pl.BlockSpec(shape, lambda b, j: (0,) * len(shape))
    tile = lambda w: pl.BlockSpec((1, tq, w), lambda b, j: (b, j, 0))
    seq = pl.BlockSpec((1, S, D_ATT), lambda b, j: (b, 0, 0))
    return pl.pallas_call(
        _attn_out_kernel,
        grid=(B, S // tq),
        in_specs=[tile(D_ATT), seq, seq, full(bias.shape), tile(D_CONV), tile(D),
                  full(w_out.shape), full((1, D_ATT))],
        out_specs=tile(D),
        out_shape=jax.ShapeDtypeStruct((B, S, D), f32),
        scratch_shapes=[pltpu.VMEM((K_BAND, D_ATT), bf16), pltpu.VMEM((K_BAND, D_ATT), bf16)],
        compiler_params=pltpu.CompilerParams(
            dimension_semantics=("parallel", "parallel"), vmem_limit_bytes=VMEM_LIMIT),
        name="attn_out",
    )(q, k, v, bias, yb, x, w_out, ga)


def _mem_kv_kernel(m_ref, g_ref, w_ref, kg_ref, k_ref, v_ref):
    hm = _rms(m_ref[0], g_ref[...]).astype(bf16)
    kv = _dot(hm, w_ref[...])
    for h in range(X_HEADS):
        sl = slice(h * X_HEAD_DIM, (h + 1) * X_HEAD_DIM)
        k_ref[0, :, sl] = _rms(kv[:, sl], kg_ref[...]).astype(bf16)
    v_ref[0] = kv[:, D_X:].astype(bf16)


def _mem_kv(mem, g, w_xkv, kg):
    B = mem.shape[0]
    full = lambda shape: pl.BlockSpec(shape, lambda b: (0,) * len(shape))
    out = jax.ShapeDtypeStruct((B, N_MEM, D_X), bf16)
    return pl.pallas_call(
        _mem_kv_kernel,
        grid=(B,),
        in_specs=[pl.BlockSpec((1, N_MEM, D_MODEL), lambda b: (b, 0, 0)), full((1, D_MODEL)),
                  full(w_xkv.shape), full((1, X_HEAD_DIM))],
        out_specs=[pl.BlockSpec((1, N_MEM, D_X), lambda b: (b, 0, 0))] * 2,
        out_shape=[out, out],
        compiler_params=pltpu.CompilerParams(
            dimension_semantics=("parallel",), vmem_limit_bytes=VMEM_LIMIT),
        name="mem_kv",
    )(mem, g, w_xkv, kg)


def _xattn_router_kernel(x_ref, gx_ref, wq_ref, qg_ref, k_ref, v_ref, wo_ref, gf_ref,
                         wrh_ref, wrl_ref, br_ref,
                         x2_ref, hf_ref, idx_ref, gate_ref, cnt_ref):
    x1 = x_ref[0]
    hb = _rms(x1, gx_ref[...]).astype(bf16)
    q = _dot(hb, wq_ref[...])
    heads = []
    for h in range(X_HEADS):
        sl = slice(h * X_HEAD_DIM, (h + 1) * X_HEAD_DIM)
        qh = _rms(q[:, sl], qg_ref[...]).astype(bf16)
        s = _dot_nt(qh, k_ref[0, :, sl])
        m = jnp.max(s, axis=-1, keepdims=True)
        p = jnp.exp(s - m)
        l = jnp.sum(p, axis=-1, keepdims=True)
        heads.append((_dot(p.astype(bf16), v_ref[0, :, sl]) / l).astype(bf16))
    o = jnp.concatenate(heads, axis=-1)
    x2 = x1 + _dot(o, wo_ref[...])
    x2_ref[0] = x2

    hf = _rms(x2, gf_ref[...])
    hf_hi = hf.astype(bf16)
    hf_ref[0] = hf_hi
    hf_lo = (hf - hf_hi.astype(f32)).astype(bf16)
    logits = (_dot(hf_hi, wrh_ref[...]) + _dot(hf_lo, wrh_ref[...]) + _dot(hf_hi, wrl_ref[...])
              + br_ref[...])
    lane = lax.broadcasted_iota(jnp.int32, logits.shape, 1)
    vals, idxs = [], []
    for _ in range(TOP_K):
        m = jnp.max(logits, axis=-1, keepdims=True)
        i = jnp.min(jnp.where(logits == m, lane, LANES), axis=-1, keepdims=True)
        logits = jnp.where(lane == i, -jnp.inf, logits)
        vals.append(m)
        idxs.append(i)
    es = [jnp.exp(v - vals[0]) for v in vals]
    tot = es[0] + es[1] + es[2] + es[3]
    idx_out = jnp.zeros(logits.shape, jnp.int32)
    gate_out = jnp.zeros(logits.shape, f32)
    sel = jnp.zeros(logits.shape, f32)
    for kk in range(TOP_K):
        idx_out = jnp.where(lane == kk, idxs[kk], idx_out)
        gate_out = jnp.where(lane == kk, es[kk] / tot, gate_out)
        sel = jnp.where(lane == idxs[kk], 1.0, sel)
    idx_ref[0] = idx_out
    gate_ref[0] = gate_out
    for j in range(cnt_ref.shape[0]):
        cnt_ref[j] = jnp.sum(sel[j * GROUP:(j + 1) * GROUP], axis=0, keepdims=True)


def _xattn_router(x1, gx, w_xq, qg, kx, vx, w_xo, gf, wr_hi, wr_lo, br):
    B, S, D = x1.shape
    ts = SEQ_TILE
    full = lambda shape: pl.BlockSpec(shape, lambda b, j: (0,) * len(shape))
    tile = lambda w: pl.BlockSpec((1, ts, w), lambda b, j: (b, j, 0))
    mem = pl.BlockSpec((1, N_MEM, D_X), lambda b, j: (b, 0, 0))
    return pl.pallas_call(
        _xattn_router_kernel,
        grid=(B, S // ts),
        in_specs=[tile(D), full((1, D)), full(w_xq.shape), full((1, X_HEAD_DIM)), mem, mem,
                  full(w_xo.shape), full((1, D)), full(wr_hi.shape), full(wr_lo.shape),
                  full((1, LANES))],
        out_specs=[tile(D), tile(D), tile(LANES), tile(LANES),
                   pl.BlockSpec((ts // GROUP, 1, LANES), lambda b, j: (b * (S // ts) + j, 0, 0))],
        out_shape=[jax.ShapeDtypeStruct((B, S, D), f32), jax.ShapeDtypeStruct((B, S, D), bf16),
                   jax.ShapeDtypeStruct((B, S, LANES), jnp.int32),
                   jax.ShapeDtypeStruct((B, S, LANES), f32),
                   jax.ShapeDtypeStruct((B * S // GROUP, 1, LANES), f32)],
        compiler_params=pltpu.CompilerParams(
            dimension_semantics=("parallel", "parallel"), vmem_limit_bytes=VMEM_LIMIT),
        name="xattn_router",
    )(x1, gx, w_xq, qg, kx, vx, w_xo, gf, wr_hi, wr_lo, br)


def _for_cells(g, fn):
    def body(e, carry):
        fn(g * N_EXPERTS + e)
        return carry
    lax.fori_loop(0, N_EXPERTS, body, 0)


def _group_positions(idx_ref, coffv_ref, ltri_ref):
    idxv = idx_ref[...]
    lane = lax.broadcasted_iota(jnp.int32, idxv.shape, 1)
    hots = [lane == idxv[:, kk:kk + 1] for kk in range(TOP_K)]
    multi = jnp.zeros(idxv.shape, f32)
    for h in hots:
        multi = jnp.where(h, 1.0, multi)
    rank = _dot(ltri_ref[...], multi.astype(bf16))
    posmat = coffv_ref[0] + rank
    return [jnp.sum(jnp.where(h, posmat, 0.0), axis=-1, keepdims=True) for h in hots]


def _dispatch_kernel(coff_s, rows_s, gbase_s, toff_s, tn_s, nu_s,
                     hf_ref, idx_ref, coffv_ref, ltri_ref,
                     xs_hbm, pos_ref, xbuf, zbuf, sem, zsem, *, n_groups):
    g = pl.program_id(0)
    slot = lax.rem(g, 2)

    def cell_copy(sl, i):
        n = pl.multiple_of(rows_s[i], SUBLANES)
        src = pl.multiple_of(coff_s[i], SUBLANES)
        dst = pl.multiple_of(gbase_s[i], SUBLANES)
        return n, pltpu.make_async_copy(xbuf.at[sl, pl.ds(src, n)], xs_hbm.at[pl.ds(dst, n)],
                                        sem.at[sl])

    def start_cell(sl, i):
        n, cp = cell_copy(sl, i)
        pl.when(n > 0)(cp.start)

    def wait_cell(sl, i):
        n, cp = cell_copy(sl, i)
        pl.when(n > 0)(cp.wait)

    @pl.when(g >= 2)
    def _():
        _for_cells(g - 2, functools.partial(wait_cell, slot))

    pos = _group_positions(idx_ref, coffv_ref, ltri_ref)
    lane = lax.broadcasted_iota(jnp.int32, (GROUP, LANES), 1)
    riota = lax.broadcasted_iota(jnp.int32, (GROUP, GROUP_ROWS), 1)
    pt = jnp.zeros((GROUP, GROUP_ROWS), f32)
    pos_out = jnp.zeros((GROUP, LANES), f32)
    for kk in range(TOP_K):
        pt = jnp.where(riota == pos[kk].astype(jnp.int32), 1.0, pt)
        pos_out = jnp.where(lane == kk, pos[kk], pos_out)
    pos_ref[...] = pos_out
    xbuf[slot] = lax.dot_general(pt.astype(bf16), hf_ref[...], (((0,), (0,)), ((), ())),
                                 preferred_element_type=f32)
    _for_cells(g, functools.partial(start_cell, slot))

    @pl.when(g == n_groups - 1)
    def _():
        if n_groups >= 2:
            _for_cells(g - 1, functools.partial(wait_cell, 1 - slot))
        _for_cells(g, functools.partial(wait_cell, slot))
        zbuf[...] = jnp.zeros(zbuf.shape, f32)

        def tail_copy(e):
            n = pl.multiple_of(tn_s[e], SUBLANES)
            dst = pl.multiple_of(toff_s[e], SUBLANES)
            return n, pltpu.make_async_copy(zbuf.at[pl.ds(0, n)], xs_hbm.at[pl.ds(dst, n)], zsem)

        def tail(e, carry):
            n, cp = tail_copy(e)

            @pl.when(n > 0)
            def _():
                cp.start()
                cp.wait()
            return carry
        lax.fori_loop(0, N_EXPERTS, tail, 0)

        def unused(b, carry):
            dst = pl.multiple_of(b * EXPERT_TILE, EXPERT_TILE)
            cp = pltpu.make_async_copy(zbuf, xs_hbm.at[pl.ds(dst, EXPERT_TILE)], zsem)
            cp.start()
            cp.wait()
            return carry
        lax.fori_loop(nu_s[0], xs_hbm.shape[0] // EXPERT_TILE, unused, 0)


def _dispatch(plan, hf, idx, n_rows):
    n_tok, D = hf.shape
    n_groups = n_tok // GROUP
    ltri = jnp.asarray(np.tril(np.ones((GROUP, GROUP)), -1), bf16)
    grid_spec = pltpu.PrefetchScalarGridSpec(
        num_scalar_prefetch=6,
        grid=(n_groups,),
        in_specs=[pl.BlockSpec((GROUP, D), lambda g, *_: (g, 0)),
                  pl.BlockSpec((GROUP, LANES), lambda g, *_: (g, 0)),
                  pl.BlockSpec((1, 1, LANES), lambda g, *_: (g, 0, 0)),
                  pl.BlockSpec((GROUP, GROUP), lambda g, *_: (0, 0))],
        out_specs=[pl.BlockSpec(memory_space=pl.ANY),
                   pl.BlockSpec((GROUP, LANES), lambda g, *_: (g, 0))],
        scratch_shapes=[pltpu.VMEM((2, GROUP_ROWS, D), f32), pltpu.VMEM((EXPERT_TILE, D), f32),
                        pltpu.SemaphoreType.DMA((2,)), pltpu.SemaphoreType.DMA(())],
    )
    return pl.pallas_call(
        functools.partial(_dispatch_kernel, n_groups=n_groups),
        grid_spec=grid_spec,
        out_shape=[jax.ShapeDtypeStruct((n_rows, D), f32),
                   jax.ShapeDtypeStruct((n_tok, LANES), f32)],
        compiler_params=pltpu.CompilerParams(
            dimension_semantics=("arbitrary",), vmem_limit_bytes=VMEM_LIMIT),
        name="dispatch",
    )(plan["coff"], plan["rows"], plan["gbase"], plan["tailoff"], plan["tailn"], plan["n_used"],
      hf, idx, plan["coffv"], ltri)


def _expert_kernel(be_ref, nu_ref, x_ref, wgu_ref, bgu_ref, wdn_ref, bdn_ref, pm_ref, y_ref,
                   wgu_b, wdn_b):
    b = pl.program_id(0)

    @pl.when(b < nu_ref[0])
    def _():
        @pl.when((b == 0) | (be_ref[b] != be_ref[jnp.maximum(b - 1, 0)]))
        def _():
            for c in range(D_MODEL // LANES):
                rows = slice(c * LANES, (c + 1) * LANES)
                wgu_b[rows, :] = wgu_ref[0, rows, :].astype(bf16)
                wdn_b[rows, :] = _dot(pm_ref[...], wdn_ref[0, rows, :].astype(bf16)).astype(bf16)

        _expert_block(x_ref, bgu_ref, bdn_ref, y_ref, wgu_b, wdn_b)

    @pl.when(b >= nu_ref[0])
    def _():
        y_ref[...] = jnp.zeros(y_ref.shape, f32)


def _expert_block(x_ref, bgu_ref, bdn_ref, y_ref, wgu_b, wdn_b):
    gu = _dot(x_ref[...].astype(bf16), wgu_b[...]) + bgu_ref[0]
    lane = lax.broadcasted_iota(jnp.int32, (1, LANES), 1)
    even = (lane & 1) == 0
    g = jnp.minimum(gu, SWIGLU_LIMIT)
    glu = g * jax.nn.sigmoid(SWIGLU_ALPHA * g)
    lin = jnp.clip(gu, -SWIGLU_LIMIT, SWIGLU_LIMIT) + 1.0
    merged = []
    for p in range(D_FF // LANES):
        ta = jnp.where(even, glu[:, (2 * p) * LANES:(2 * p + 1) * LANES],
                       lin[:, (2 * p) * LANES:(2 * p + 1) * LANES])
        tb = jnp.where(even, glu[:, (2 * p + 1) * LANES:(2 * p + 2) * LANES],
                       lin[:, (2 * p + 1) * LANES:(2 * p + 2) * LANES])
        merged.append(jnp.where(even, ta * pltpu.roll(ta, LANES - 1, axis=1),
                                tb * pltpu.roll(tb, 1, axis=1)))
    act = jnp.concatenate(merged, axis=-1).astype(bf16)
    y_ref[...] = _dot(act, wdn_b[...]) + bdn_ref[0]


def _hidden_perm():
    r = np.arange(LANES)
    p = np.zeros((LANES, LANES), np.float32)
    p[r, (LANES // 2) * (r & 1) + (r >> 1)] = 1.0
    return jnp.asarray(p, bf16)


def _expert_mlp(plan, xs, w_gu, b_gu, w_dn, b_dn):
    n_rows, D = xs.shape
    tm = EXPERT_TILE
    row_blk = lambda b, be, nu: (jnp.minimum(b, nu[0] - 1), 0)
    per_e = lambda b, be, nu: (be[b], 0, 0)
    grid_spec = pltpu.PrefetchScalarGridSpec(
        num_scalar_prefetch=2,
        grid=(n_rows // tm,),
        in_specs=[pl.BlockSpec((tm, D), row_blk),
                  pl.BlockSpec((1, D, 2 * D_FF), per_e),
                  pl.BlockSpec((1, 1, 2 * D_FF), per_e),
                  pl.BlockSpec((1, D_FF, D), per_e),
                  pl.BlockSpec((1, 1, D), per_e),
                  pl.BlockSpec((LANES, LANES), lambda b, be, nu: (0, 0))],
        out_specs=pl.BlockSpec((tm, D), lambda b, be, nu: (b, 0)),
        scratch_shapes=[pltpu.VMEM((D, 2 * D_FF), bf16), pltpu.VMEM((D_FF, D), bf16)],
    )
    return pl.pallas_call(
        _expert_kernel,
        grid_spec=grid_spec,
        out_shape=jax.ShapeDtypeStruct((n_rows, D), f32),
        compiler_params=pltpu.CompilerParams(
            dimension_semantics=("arbitrary",), vmem_limit_bytes=VMEM_LIMIT),
        name="expert_mlp",
    )(plan["block_e"], plan["n_used"], xs, w_gu, b_gu, w_dn, b_dn, _hidden_perm())


def _combine_kernel(coff_s, rows_s, gbase_s, x_ref, pos_ref, gate_ref, y_hbm, o_ref, ybuf, sem,
                    *, n_groups):
    g = pl.program_id(0)
    slot = lax.rem(g, 2)

    def cell_copy(sl, i):
        n = pl.multiple_of(rows_s[i], SUBLANES)
        loc = pl.multiple_of(coff_s[i], SUBLANES)
        src = pl.multiple_of(gbase_s[i], SUBLANES)
        return n, pltpu.make_async_copy(y_hbm.at[pl.ds(src, n)], ybuf.at[sl, pl.ds(loc, n)],
                                        sem.at[sl])

    def start_cell(sl, i):
        n, cp = cell_copy(sl, i)
        pl.when(n > 0)(cp.start)

    def wait_cell(sl, i):
        n, cp = cell_copy(sl, i)
        pl.when(n > 0)(cp.wait)

    @pl.when(g == 0)
    def _():
        ybuf[...] = jnp.zeros(ybuf.shape, f32)
        _for_cells(g, functools.partial(start_cell, slot))

    @pl.when(g + 1 < n_groups)
    def _():
        _for_cells(g + 1, functools.partial(start_cell, 1 - slot))

    _for_cells(g, functools.partial(wait_cell, slot))

    riota = lax.broadcasted_iota(jnp.int32, (GROUP, GROUP_ROWS), 1)
    pt = jnp.zeros((GROUP, GROUP_ROWS), f32)
    for kk in range(TOP_K):
        pk = pos_ref[:, kk:kk + 1].astype(jnp.int32)
        pt = jnp.where(riota == pk, gate_ref[:, kk:kk + 1], pt)
    o_ref[...] = x_ref[...] + _dot(pt.astype(bf16), ybuf[slot].astype(bf16))


def _combine(plan, x2, pos, gates, y_rows):
    n_tok, D = x2.shape
    n_groups = n_tok // GROUP
    tok = lambda w: pl.BlockSpec((GROUP, w), lambda g, *_: (g, 0))
    grid_spec = pltpu.PrefetchScalarGridSpec(
        num_scalar_prefetch=3,
        grid=(n_groups,),
        in_specs=[tok(D), tok(LANES), tok(LANES), pl.BlockSpec(memory_space=pl.ANY)],
        out_specs=tok(D),
        scratch_shapes=[pltpu.VMEM((2, GROUP_ROWS, D), f32), pltpu.SemaphoreType.DMA((2,))],
    )
    return pl.pallas_call(
        functools.partial(_combine_kernel, n_groups=n_groups),
        grid_spec=grid_spec,
        out_shape=jax.ShapeDtypeStruct((n_tok, D), f32),
        compiler_params=pltpu.CompilerParams(
            dimension_semantics=("arbitrary",), vmem_limit_bytes=VMEM_LIMIT),
        name="combine",
    )(plan["coff"], plan["rows"], plan["gbase"], x2, pos, gates, y_rows)


def _routing_plan(cnt, n_tok):
    tm = EXPERT_TILE
    n_groups = n_tok // GROUP
    cnt = cnt.reshape(n_groups, LANES)[:, :N_EXPERTS].astype(jnp.int32)
    rows = (cnt + SUBLANES - 1) // SUBLANES * SUBLANES
    coff = jnp.cumsum(rows, axis=1) - rows
    tot = jnp.sum(rows, axis=0)
    padded = (tot + tm - 1) // tm * tm
    pend = jnp.cumsum(padded)
    ebase = pend - padded
    gbase = ebase[None, :] + jnp.cumsum(rows, axis=0) - rows
    n_blocks = _max_rows(n_tok) // tm
    n_used = pend[-1] // tm
    blk = jnp.arange(n_blocks, dtype=jnp.int32)
    block_e = jnp.searchsorted(pend, jnp.minimum(blk, n_used - 1) * tm, side='right')
    flat = lambda a: a.reshape(-1).astype(jnp.int32)
    return dict(
        coff=flat(coff), rows=flat(rows), gbase=flat(gbase),
        tailoff=flat(ebase + tot), tailn=flat(padded - tot),
        block_e=flat(jnp.minimum(block_e, N_EXPERTS - 1)), n_used=flat(n_used),
        coffv=jnp.zeros((n_groups, 1, LANES), f32).at[:, 0, :N_EXPERTS].set(coff.astype(f32)))


def _max_rows(n_tok):
    tm = EXPERT_TILE
    worst = (n_tok * TOP_K + (n_tok // GROUP) * N_EXPERTS * (SUBLANES - 1)
             + N_EXPERTS * (tm - SUBLANES))
    return (worst + tm - 1) // tm * tm


def _moe(x2, hf, idx, gates, cnt, w_gate_up, b_gate_up, w_down, b_down):
    n_tok, D = x2.shape
    plan = _routing_plan(cnt, n_tok)
    xs, pos = _dispatch(plan, hf, idx, _max_rows(n_tok))
    y_rows = _expert_mlp(plan, xs, w_gate_up, b_gate_up.reshape(N_EXPERTS, 1, 2 * D_FF),
                         w_down, b_down.reshape(N_EXPERTS, 1, D))
    return _combine(plan, x2, pos, gates, y_rows)


def _band_bias(rel_bias):
    qi = np.arange(Q_TILE)[:, None]
    kk = np.arange(K_BAND)[None, :]
    idx = np.clip(qi + LEFT_CHUNKS * CHUNK - kk, -MAX_REL, MAX_REL) + MAX_REL
    qc, kc = qi // CHUNK, kk // CHUNK
    visible = (kc >= qc) & (kc <= qc + LEFT_CHUNKS)
    return jnp.where(visible[None], rel_bias[:, idx], NEG_INF).astype(f32)


def _layer(x, mem, norm_mix_g, w_in, q_norm_g, k_norm_g, rel_bias, conv_w, out_norm_a_g,
           out_norm_b_g, w_out, norm_xattn_g, norm_mem_g, w_xq, w_xkv, xq_norm_g, xk_norm_g,
           w_xo, norm_ffn_g, w_router, b_router, w_gate_up, b_gate_up, w_down, b_down):
    B, S, D = x.shape
    n_tok = B * S
    row = lambda a: a.reshape(1, -1).astype(f32)

    qg = row(jnp.tile(q_norm_g, ATT_HEADS) * (ATT_HEAD_DIM ** -0.5))
    kg = row(jnp.tile(k_norm_g, ATT_HEADS))
    head_ones = jnp.asarray(np.kron(np.eye(ATT_HEADS), np.ones((ATT_HEAD_DIM, ATT_HEAD_DIM))), bf16)
    q, k, v, yb = _mix_in(x, row(norm_mix_g), w_in.astype(bf16), qg, kg, head_ones,
                          conv_w.astype(f32), row(out_norm_b_g))
    x1 = _attn_out(q, k, v, _band_bias(rel_bias), yb, x, w_out.astype(bf16), row(out_norm_a_g))

    kx, vx = _mem_kv(mem, row(norm_mem_g), w_xkv.astype(bf16), row(xk_norm_g))
    wr = jnp.zeros((D, LANES), f32).at[:, :N_EXPERTS].set(w_router)
    wr_hi = wr.astype(bf16)
    wr_lo = (wr - wr_hi.astype(f32)).astype(bf16)
    br = jnp.full((1, LANES), -jnp.inf, f32).at[0, :N_EXPERTS].set(b_router)
    x2, hf, idx, gates, cnt = _xattn_router(
        x1, row(norm_xattn_g), w_xq.astype(bf16), row(xq_norm_g * (X_HEAD_DIM ** -0.5)), kx, vx,
        w_xo.astype(bf16), row(norm_ffn_g), wr_hi, wr_lo, br)

    out = _moe(x2.reshape(n_tok, D), hf.reshape(n_tok, D), idx.reshape(n_tok, LANES),
               gates.reshape(n_tok, LANES), cnt, w_gate_up, b_gate_up, w_down, b_down)
    return out.reshape(B, S, D)


def kernel(x, mem, norm_mix_g, w_in, q_norm_g, k_norm_g, rel_bias, conv_w, out_norm_a_g,
           out_norm_b_g, w_out, norm_xattn_g, norm_mem_g, w_xq, w_xkv, xq_norm_g, xk_norm_g,
           w_xo, norm_ffn_g, w_router, b_router, w_gate_up, b_gate_up, w_down, b_down):
    depth = norm_mix_g.shape[0]
    for l in range(depth):
        x = _layer(x, mem, norm_mix_g[l], w_in[l], q_norm_g[l], k_norm_g[l], rel_bias[l],
                   conv_w[l], out_norm_a_g[l], out_norm_b_g[l], w_out[l], norm_xattn_g[l],
                   norm_mem_g[l], w_xq[l], w_xkv[l], xq_norm_g[l], xk_norm_g[l], w_xo[l],
                   norm_ffn_g[l], w_router[l], b_router[l], w_gate_up[l], b_gate_up[l],
                   w_down[l], b_down[l])
    return x
```

```python
import functools

import jax
import jax.numpy as jnp
import numpy as np
from jax import lax
from jax.experimental import pallas as pl
from jax.experimental.pallas import tpu as pltpu

D_MODEL = 1024
CHUNK = 64
LEFT_CHUNKS = 8
N_MEM = 256
ATT_HEADS = 8
ATT_HEAD_DIM = 64
D_ATT = ATT_HEADS * ATT_HEAD_DIM
D_CONV = 512
MAX_REL = 256
X_HEADS = 4
X_HEAD_DIM = 128
D_X = X_HEADS * X_HEAD_DIM
N_EXPERTS = 32
TOP_K = 4
D_FF = D_MODEL
SWIGLU_LIMIT = 7.0
SWIGLU_ALPHA = 1.702
EPS = 1e-6
NEG_INF = -1e30

LANES = 128
SEQ_TILE = 512
Q_TILE = 128
K_BAND = Q_TILE + LEFT_CHUNKS * CHUNK
EXPERT_TILE = 256
GU_TILE = 2 * LANES
GROUP = 256
SUBLANES = 8
GROUP_ROWS = 1280
VMEM_LIMIT = 56 * 1024 * 1024

bf16 = jnp.bfloat16
f32 = jnp.float32


def _rms(x, g):
    return x * lax.rsqrt(jnp.mean(x * x, axis=-1, keepdims=True) + EPS) * g


def _dot(a, b):
    return jnp.dot(a, b, preferred_element_type=f32)


def _dot_nt(a, b):
    return lax.dot_general(a, b, (((1,), (1,)), ((), ())), preferred_element_type=f32)


def _mix_in_kernel(x_ref, g_ref, w_ref, qg_ref, kg_ref, hm_ref, cw_ref, gb_ref,
                   q_ref, k_ref, v_ref, yb_ref, ub_ref):
    ts = x_ref.shape[1]

    @pl.when(pl.program_id(1) == 0)
    def _():
        ub_ref[0:8, :] = jnp.zeros((8, D_CONV), f32)

    hb = _rms(x_ref[0], g_ref[...]).astype(bf16)

    def head_norm(t, gain):
        ms = _dot((t * t).astype(bf16), hm_ref[...]) * (1.0 / ATT_HEAD_DIM)
        return t * lax.rsqrt(ms + EPS) * gain

    q = _dot(hb, w_ref[:, 0:D_ATT])
    q_ref[0] = head_norm(q, qg_ref[...]).astype(bf16)
    k = _dot(hb, w_ref[:, D_ATT:2 * D_ATT])
    k_ref[0] = head_norm(k, kg_ref[...]).astype(bf16)
    v_ref[0] = _dot(hb, w_ref[:, 2 * D_ATT:3 * D_ATT]).astype(bf16)

    o = 3 * D_ATT
    bg = _dot(hb, w_ref[:, o:o + D_CONV])
    cg = _dot(hb, w_ref[:, o + D_CONV:o + 2 * D_CONV])
    xv = _dot(hb, w_ref[:, o + 2 * D_CONV:o + 3 * D_CONV])
    u = cg * xv
    ub_ref[8:8 + ts, :] = u
    conv = (cw_ref[0:1, :] * ub_ref[6:6 + ts, :] + cw_ref[1:2, :] * ub_ref[7:7 + ts, :]
            + cw_ref[2:3, :] * u)
    ub_ref[0:8, :] = u[ts - 8:ts, :]
    yb_ref[0] = _rms(bg * conv, gb_ref[...]).astype(bf16)


def _mix_in(x, g, w_in, qg, kg, hm, cw, gb):
    B, S, D = x.shape
    ts = SEQ_TILE
    full = lambda shape: pl.BlockSpec(shape, lambda b, j: (0,) * len(shape))
    tile = lambda w: pl.BlockSpec((1, ts, w), lambda b, j: (b, j, 0))
    out = jax.ShapeDtypeStruct((B, S, D_ATT), bf16)
    return pl.pallas_call(
        _mix_in_kernel,
        grid=(B, S // ts),
        in_specs=[tile(D), full((1, D)), full(w_in.shape), full((1, D_ATT)), full((1, D_ATT)),
                  full((D_ATT, D_ATT)), full((3, D_CONV)), full((1, D_CONV))],
        out_specs=[tile(D_ATT)] * 4,
        out_shape=[out] * 4,
        scratch_shapes=[pltpu.VMEM((ts + 8, D_CONV), f32)],
        compiler_params=pltpu.CompilerParams(
            dimension_semantics=("arbitrary", "arbitrary"), vmem_limit_bytes=VMEM_LIMIT),
        name="mix_in",
    )(x, g, w_in, qg, kg, hm, cw, gb)


def _attn_out_kernel(q_ref, k_ref, v_ref, bias_ref, yb_ref, x_ref, w_ref, ga_ref,
                     o_ref, kb_ref, vb_ref):
    tq = q_ref.shape[1]
    n_kc = K_BAND // CHUNK
    c0 = pl.program_id(1) * (tq // CHUNK)
    for kc in range(n_kc):
        src = pl.multiple_of(jnp.maximum(c0 - LEFT_CHUNKS + kc, 0) * CHUNK, CHUNK)
        kb_ref[kc * CHUNK:(kc + 1) * CHUNK, :] = k_ref[0, pl.ds(src, CHUNK), :]
        vb_ref[kc * CHUNK:(kc + 1) * CHUNK, :] = v_ref[0, pl.ds(src, CHUNK), :]

    col_chunk = lax.broadcasted_iota(jnp.int32, (1, K_BAND), 1) // CHUNK
    start_mask = jnp.where(col_chunk >= LEFT_CHUNKS - c0, 0.0, NEG_INF).astype(f32)
    lane = lax.broadcasted_iota(jnp.int32, (1, LANES), 1)
    low = lane < ATT_HEAD_DIM

    pairs = []
    for hp in range(ATT_HEADS // 2):
        sl = slice(hp * LANES, (hp + 1) * LANES)
        qp = q_ref[0, :, sl]
        kp = kb_ref[:, sl]
        vp = vb_ref[:, sl]
        outs = []
        for sub in range(2):
            keep = low if sub == 0 else jnp.logical_not(low)
            qm = jnp.where(keep, qp, jnp.zeros_like(qp))
            s = _dot_nt(qm, kp) + bias_ref[2 * hp + sub] + start_mask
            m = jnp.max(s, axis=-1, keepdims=True)
            p = jnp.exp(s - m)
            l = jnp.sum(p, axis=-1, keepdims=True)
            outs.append(_dot(p.astype(bf16), vp) / l)
        pairs.append(jnp.where(low, outs[0], outs[1]))
    ya = jnp.concatenate(pairs, axis=-1)
    yan = _rms(ya, ga_ref[...]).astype(bf16)
    o_ref[0] = (x_ref[0] + _dot(yan, w_ref[0:D_ATT, :]) + _dot(yb_ref[0], w_ref[D_ATT:, :]))


def _attn_out(q, k, v, bias, yb, x, w_out, ga):
    B, S, D = x.shape
    tq = Q_TILE
    full = lambda shape: pl.BlockSpec(shape, lambda b, j: (0,) * len(shape))
    tile = lambda w: pl.BlockSpec((1, tq, w), lambda b, j: (b, j, 0))
    seq = pl.BlockSpec((1, S, D_ATT), lambda b, j: (b, 0, 0))
    return pl.pallas_call(
        _attn_out_kernel,
        grid=(B, S // tq),
        in_specs=[tile(D_ATT), seq, seq, full(bias.shape), tile(D_CONV), tile(D),
                  full(w_out.shape), full((1, D_ATT))],
        out_specs=tile(D),
        out_shape=jax.ShapeDtypeStruct((B, S, D), f32),
        scratch_shapes=[pltpu.VMEM((K_BAND, D_ATT), bf16), pltpu.VMEM((K_BAND, D_ATT), bf16)],
        compiler_params=pltpu.CompilerParams(
            dimension_semantics=("parallel", "parallel"), vmem_limit_bytes=VMEM_LIMIT),
        name="attn_out",
    )(q, k, v, bias, yb, x, w_out, ga)


def _mem_kv_kernel(m_ref, g_ref, w_ref, kg_ref, k_ref, v_ref):
    hm = _rms(m_ref[0], g_ref[...]).astype(bf16)
    kv = _dot(hm, w_ref[...])
    for h in range(X_HEADS):
        sl = slice(h * X_HEAD_DIM, (h + 1) * X_HEAD_DIM)
        k_ref[0, :, sl] = _rms(kv[:, sl], kg_ref[...]).astype(bf16)
    v_ref[0] = kv[:, D_X:].astype(bf16)


def _mem_kv(mem, g, w_xkv, kg):
    B = mem.shape[0]
    full = lambda shape: pl.BlockSpec(shape, lambda b: (0,) * len(shape))
    out = jax.ShapeDtypeStruct((B, N_MEM, D_X), bf16)
    return pl.pallas_call(
        _mem_kv_kernel,
        grid=(B,),
        in_specs=[pl.BlockSpec((1, N_MEM, D_MODEL), lambda b: (b, 0, 0)), full((1, D_MODEL)),
                  full(w_xkv.shape), full((1, X_HEAD_DIM))],
        out_specs=[pl.BlockSpec((1, N_MEM, D_X), lambda b: (b, 0, 0))] * 2,
        out_shape=[out, out],
        compiler_params=pltpu.CompilerParams(
            dimension_semantics=("parallel",), vmem_limit_bytes=VMEM_LIMIT),
        name="mem_kv",
    )(mem, g, w_xkv, kg)


def _xattn_router_kernel(x_ref, gx_ref, wq_ref, qg_ref, k_ref, v_ref, wo_ref, gf_ref,
                         wrh_ref, wrl_ref, br_ref,
                         x2_ref, hf_ref, idx_ref, gate_ref, cnt_ref):
    x1 = x_ref[0]
    hb = _rms(x1, gx_ref[...]).astype(bf16)
    q = _dot(hb, wq_ref[...])
    heads = []
    for h in range(X_HEADS):
        sl = slice(h * X_HEAD_DIM, (h + 1) * X_HEAD_DIM)
        qh = _rms(q[:, sl], qg_ref[...]).astype(bf16)
        s = _dot_nt(qh, k_ref[0, :, sl])
        m = jnp.max(s, axis=-1, keepdims=True)
        p = jnp.exp(s - m)
        l = jnp.sum(p, axis=-1, keepdims=True)
        heads.append((_dot(p.astype(bf16), v_ref[0, :, sl]) / l).astype(bf16))
    o = jnp.concatenate(heads, axis=-1)
    x2 = x1 + _dot(o, wo_ref[...])
    x2_ref[0] = x2

    hf = _rms(x2, gf_ref[...])
    hf_hi = hf.astype(bf16)
    hf_ref[0] = hf_hi
    hf_lo = (hf - hf_hi.astype(f32)).astype(bf16)
    logits = (_dot(hf_hi, wrh_ref[...]) + _dot(hf_lo, wrh_ref[...]) + _dot(hf_hi, wrl_ref[...])
              + br_ref[...])
    lane = lax.broadcasted_iota(jnp.int32, logits.shape, 1)
    vals, idxs = [], []
    for _ in range(TOP_K):
        m = jnp.max(logits, axis=-1, keepdims=True)
        i = jnp.min(jnp.where(logits == m, lane, LANES), axis=-1, keepdims=True)
        logits = jnp.where(lane == i, -jnp.inf, logits)
        vals.append(m)
        idxs.append(i)
    es = [jnp.exp(v - vals[0]) for v in vals]
    tot = es[0] + es[1] + es[2] + es[3]
    idx_out = jnp.zeros(logits.shape, jnp.int32)
    gate_out = jnp.zeros(logits.shape, f32)
    sel = jnp.zeros(logits.shape, f32)
    for kk in range(TOP_K):
        idx_out = jnp.where(lane == kk, idxs[kk], idx_out)
        gate_out = jnp.where(lane == kk, es[kk] / tot, gate_out)
        sel = jnp.where(lane == idxs[kk], 1.0, sel)
    idx_ref[0] = idx_out
    gate_ref[0] = gate_out
    for j in range(cnt_ref.shape[0]):
        cnt_ref[j] = jnp.sum(sel[j * GROUP:(j + 1) * GROUP], axis=0, keepdims=True)


def _xattn_router(x1, gx, w_xq, qg, kx, vx, w_xo, gf, wr_hi, wr_lo, br):
    B, S, D = x1.shape
    ts = SEQ_TILE
    full = lambda shape: pl.BlockSpec(shape, lambda b, j: (0,) * len(shape))
    tile = lambda w: pl.BlockSpec((1, ts, w), lambda b, j: (b, j, 0))
    mem = pl.BlockSpec((1, N_MEM, D_X), lambda b, j: (b, 0, 0))
    return pl.pallas_call(
        _xattn_router_kernel,
        grid=(B, S // ts),
        in_specs=[tile(D), full((1, D)), full(w_xq.shape), full((1, X_HEAD_DIM)), mem, mem,
                  full(w_xo.shape), full((1, D)), full(wr_hi.shape), full(wr_lo.shape),
                  full((1, LANES))],
        out_specs=[tile(D), tile(D), tile(LANES), tile(LANES),
                   pl.BlockSpec((ts // GROUP, 1, LANES), lambda b, j: (b * (S // ts) + j, 0, 0))],
        out_shape=[jax.ShapeDtypeStruct((B, S, D), f32), jax.ShapeDtypeStruct((B, S, D), bf16),
                   jax.ShapeDtypeStruct((B, S, LANES), jnp.int32),
                   jax.ShapeDtypeStruct((B, S, LANES), f32),
                   jax.ShapeDtypeStruct((B * S // GROUP, 1, LANES), f32)],
        compiler_params=pltpu.CompilerParams(
            dimension_semantics=("parallel", "parallel"), vmem_limit_bytes=VMEM_LIMIT),
        name="xattn_router",
    )(x1, gx, w_xq, qg, kx, vx, w_xo, gf, wr_hi, wr_lo, br)


def _for_cells(g, fn):
    def body(e, carry):
        fn(g * N_EXPERTS + e)
        return carry
    lax.fori_loop(0, N_EXPERTS, body, 0)


def _group_positions(idx_ref, coffv_ref, ltri_ref):
    idxv = idx_ref[...]
    lane = lax.broadcasted_iota(jnp.int32, idxv.shape, 1)
    hots = [lane == idxv[:, kk:kk + 1] for kk in range(TOP_K)]
    multi = jnp.zeros(idxv.shape, f32)
    for h in hots:
        multi = jnp.where(h, 1.0, multi)
    rank = _dot(ltri_ref[...], multi.astype(bf16))
    posmat = coffv_ref[0] + rank
    return [jnp.sum(jnp.where(h, posmat, 0.0), axis=-1, keepdims=True) for h in hots]


def _dispatch_kernel(coff_s, rows_s, gbase_s, toff_s, tn_s, nu_s,
                     hf_ref, idx_ref, coffv_ref, ltri_ref,
                     xs_hbm, pos_ref, xbuf, zbuf, sem, zsem, *, n_groups):
    g = pl.program_id(0)
    slot = lax.rem(g, 2)

    def cell_copy(sl, i):
        n = pl.multiple_of(rows_s[i], SUBLANES)
        src = pl.multiple_of(coff_s[i], SUBLANES)
        dst = pl.multiple_of(gbase_s[i], SUBLANES)
        return n, pltpu.make_async_copy(xbuf.at[sl, pl.ds(src, n)], xs_hbm.at[pl.ds(dst, n)],
                                        sem.at[sl])

    def start_cell(sl, i):
        n, cp = cell_copy(sl, i)
        pl.when(n > 0)(cp.start)

    def wait_cell(sl, i):
        n, cp = cell_copy(sl, i)
        pl.when(n > 0)(cp.wait)

    @pl.when(g >= 2)
    def _():
        _for_cells(g - 2, functools.partial(wait_cell, slot))

    pos = _group_positions(idx_ref, coffv_ref, ltri_ref)
    lane = lax.broadcasted_iota(jnp.int32, (GROUP, LANES), 1)
    riota = lax.broadcasted_iota(jnp.int32, (GROUP, GROUP_ROWS), 1)
    pt = jnp.zeros((GROUP, GROUP_ROWS), f32)
    pos_out = jnp.zeros((GROUP, LANES), f32)
    for kk in range(TOP_K):
        pt = jnp.where(riota == pos[kk].astype(jnp.int32), 1.0, pt)
        pos_out = jnp.where(lane == kk, pos[kk], pos_out)
    pos_ref[...] = pos_out
    xbuf[slot] = lax.dot_general(pt.astype(bf16), hf_ref[...], (((0,), (0,)), ((), ())),
                                 preferred_element_type=f32)
    _for_cells(g, functools.partial(start_cell, slot))

    @pl.when(g == n_groups - 1)
    def _():
        if n_groups >= 2:
            _for_cells(g - 1, functools.partial(wait_cell, 1 - slot))
        _for_cells(g, functools.partial(wait_cell, slot))
        zbuf[...] = jnp.zeros(zbuf.shape, f32)

        def tail_copy(e):
            n = pl.multiple_of(tn_s[e], SUBLANES)
            dst = pl.multiple_of(toff_s[e], SUBLANES)
            return n, pltpu.make_async_copy(zbuf.at[pl.ds(0, n)], xs_hbm.at[pl.ds(dst, n)], zsem)

        def tail(e, carry):
            n, cp = tail_copy(e)

            @pl.when(n > 0)
            def _():
                cp.start()
                cp.wait()
            return carry
        lax.fori_loop(0, N_EXPERTS, tail, 0)

        def unused(b, carry):
            dst = pl.multiple_of(b * EXPERT_TILE, EXPERT_TILE)
            cp = pltpu.make_async_copy(zbuf, xs_hbm.at[pl.ds(dst, EXPERT_TILE)], zsem)
            cp.start()
            cp.wait()
            return carry
        lax.fori_loop(nu_s[0], xs_hbm.shape[0] // EXPERT_TILE, unused, 0)


def _dispatch(plan, hf, idx, n_rows):
    n_tok, D = hf.shape
    n_groups = n_tok // GROUP
    ltri = jnp.asarray(np.tril(np.ones((GROUP, GROUP)), -1), bf16)
    grid_spec = pltpu.PrefetchScalarGridSpec(
        num_scalar_prefetch=6,
        grid=(n_groups,),
        in_specs=[pl.BlockSpec((GROUP, D), lambda g, *_: (g, 0)),
                  pl.BlockSpec((GROUP, LANES), lambda g, *_: (g, 0)),
                  pl.BlockSpec((1, 1, LANES), lambda g, *_: (g, 0, 0)),
                  pl.BlockSpec((GROUP, GROUP), lambda g, *_: (0, 0))],
        out_specs=[pl.BlockSpec(memory_space=pl.ANY),
                   pl.BlockSpec((GROUP, LANES), lambda g, *_: (g, 0))],
        scratch_shapes=[pltpu.VMEM((2, GROUP_ROWS, D), f32), pltpu.VMEM((EXPERT_TILE, D), f32),
                        pltpu.SemaphoreType.DMA((2,)), pltpu.SemaphoreType.DMA(())],
    )
    return pl.pallas_call(
        functools.partial(_dispatch_kernel, n_groups=n_groups),
        grid_spec=grid_spec,
        out_shape=[jax.ShapeDtypeStruct((n_rows, D), f32),
                   jax.ShapeDtypeStruct((n_tok, LANES), f32)],
        compiler_params=pltpu.CompilerParams(
            dimension_semantics=("arbitrary",), vmem_limit_bytes=VMEM_LIMIT),
        name="dispatch",
    )(plan["coff"], plan["rows"], plan["gbase"], plan["tailoff"], plan["tailn"], plan["n_used"],
      hf, idx, plan["coffv"], ltri)


def _expert_kernel(be_ref, nu_ref, x_ref, wgu_ref, bgu_ref, wdn_ref, bdn_ref, pm_ref, y_ref,
                   wgu_b, wdn_b):
    b = pl.program_id(0)

    @pl.when(b < nu_ref[0])
    def _():
        @pl.when((b == 0) | (be_ref[b] != be_ref[jnp.maximum(b - 1, 0)]))
        def _():
            for c in range(2 * D_FF // GU_TILE):
                cols = slice(c * GU_TILE, (c + 1) * GU_TILE)
                wgu_b[:, cols] = _dot(wgu_ref[0, :, cols].astype(bf16), pm_ref[...]).astype(bf16)
            for c in range(D_FF // LANES):
                rows = slice(c * LANES, (c + 1) * LANES)
                wdn_b[rows, :] = wdn_ref[0, rows, :].astype(bf16)

        _expert_block(x_ref, bgu_ref, bdn_ref, y_ref, wgu_b, wdn_b)

    @pl.when(b >= nu_ref[0])
    def _():
        y_ref[...] = jnp.zeros(y_ref.shape, f32)


def _expert_block(x_ref, bgu_ref, bdn_ref, y_ref, wgu_b, wdn_b):
    gu = _dot(x_ref[...].astype(bf16), wgu_b[...]) + bgu_ref[0]
    acts = []
    for c in range(2 * D_FF // GU_TILE):
        g = jnp.minimum(gu[:, c * GU_TILE:c * GU_TILE + LANES], SWIGLU_LIMIT)
        lin = jnp.clip(gu[:, c * GU_TILE + LANES:(c + 1) * GU_TILE], -SWIGLU_LIMIT, SWIGLU_LIMIT)
        acts.append((g * jax.nn.sigmoid(SWIGLU_ALPHA * g) * (lin + 1.0)).astype(bf16))
    act = jnp.concatenate(acts, axis=-1)
    y_ref[...] = _dot(act, wdn_b[...]) + bdn_ref[0]


def _gate_up_perm():
    j = np.arange(GU_TILE)
    p = np.zeros((GU_TILE, GU_TILE), np.float32)
    p[np.where(j < LANES, 2 * j, 2 * (j - LANES) + 1), j] = 1.0
    return jnp.asarray(p, bf16)


def _expert_mlp(plan, xs, w_gu, b_gu, w_dn, b_dn):
    n_rows, D = xs.shape
    tm = EXPERT_TILE
    row_blk = lambda b, be, nu: (jnp.minimum(b, nu[0] - 1), 0)
    per_e = lambda b, be, nu: (be[b], 0, 0)
    grid_spec = pltpu.PrefetchScalarGridSpec(
        num_scalar_prefetch=2,
        grid=(n_rows // tm,),
        in_specs=[pl.BlockSpec((tm, D), row_blk),
                  pl.BlockSpec((1, D, 2 * D_FF), per_e),
                  pl.BlockSpec((1, 1, 2 * D_FF), per_e),
                  pl.BlockSpec((1, D_FF, D), per_e),
                  pl.BlockSpec((1, 1, D), per_e),
                  pl.BlockSpec((GU_TILE, GU_TILE), lambda b, be, nu: (0, 0))],
        out_specs=pl.BlockSpec((tm, D), lambda b, be, nu: (b, 0)),
        scratch_shapes=[pltpu.VMEM((D, 2 * D_FF), bf16), pltpu.VMEM((D_FF, D), bf16)],
    )
    return pl.pallas_call(
        _expert_kernel,
        grid_spec=grid_spec,
        out_shape=jax.ShapeDtypeStruct((n_rows, D), f32),
        compiler_params=pltpu.CompilerParams(
            dimension_semantics=("arbitrary",), vmem_limit_bytes=VMEM_LIMIT),
        name="expert_mlp",
    )(plan["block_e"], plan["n_used"], xs, w_gu, b_gu, w_dn, b_dn, _gate_up_perm())


def _combine_kernel(coff_s, rows_s, gbase_s, x_ref, pos_ref, gate_ref, y_hbm, o_ref, ybuf, sem,
                    *, n_groups):
    g = pl.program_id(0)
    slot = lax.rem(g, 2)

    def cell_copy(sl, i):
        n = pl.multiple_of(rows_s[i], SUBLANES)
        loc = pl.multiple_of(coff_s[i], SUBLANES)
        src = pl.multiple_of(gbase_s[i], SUBLANES)
        return n, pltpu.make_async_copy(y_hbm.at[pl.ds(src, n)], ybuf.at[sl, pl.ds(loc, n)],
                                        sem.at[sl])

    def start_cell(sl, i):
        n, cp = cell_copy(sl, i)
        pl.when(n > 0)(cp.start)

    def wait_cell(sl, i):
        n, cp = cell_copy(sl, i)
        pl.when(n > 0)(cp.wait)

    @pl.when(g == 0)
    def _():
        ybuf[...] = jnp.zeros(ybuf.shape, f32)
        _for_cells(g, functools.partial(start_cell, slot))

    @pl.when(g + 1 < n_groups)
    def _():
        _for_cells(g + 1, functools.partial(start_cell, 1 - slot))

    _for_cells(g, functools.partial(wait_cell, slot))

    riota = lax.broadcasted_iota(jnp.int32, (GROUP, GROUP_ROWS), 1)
    pt = jnp.zeros((GROUP, GROUP_ROWS), f32)
    for kk in range(TOP_K):
        pk = pos_ref[:, kk:kk + 1].astype(jnp.int32)
        pt = jnp.where(riota == pk, gate_ref[:, kk:kk + 1], pt)
    o_ref[...] = x_ref[...] + _dot(pt.astype(bf16), ybuf[slot].astype(bf16))


def _combine(plan, x2, pos, gates, y_rows):
    n_tok, D = x2.shape
    n_groups = n_tok // GROUP
    tok = lambda w: pl.BlockSpec((GROUP, w), lambda g, *_: (g, 0))
    grid_spec = pltpu.PrefetchScalarGridSpec(
        num_scalar_prefetch=3,
        grid=(n_groups,),
        in_specs=[tok(D), tok(LANES), tok(LANES), pl.BlockSpec(memory_space=pl.ANY)],
        out_specs=tok(D),
        scratch_shapes=[pltpu.VMEM((2, GROUP_ROWS, D), f32), pltpu.SemaphoreType.DMA((2,))],
    )
    return pl.pallas_call(
        functools.partial(_combine_kernel, n_groups=n_groups),
        grid_spec=grid_spec,
        out_shape=jax.ShapeDtypeStruct((n_tok, D), f32),
        compiler_params=pltpu.CompilerParams(
            dimension_semantics=("arbitrary",), vmem_limit_bytes=VMEM_LIMIT),
        name="combine",
    )(plan["coff"], plan["rows"], plan["gbase"], x2, pos, gates, y_rows)


def _routing_plan(cnt, n_tok):
    tm = EXPERT_TILE
    n_groups = n_tok // GROUP
    cnt = cnt.reshape(n_groups, LANES)[:, :N_EXPERTS].astype(jnp.int32)
    rows = (cnt + SUBLANES - 1) // SUBLANES * SUBLANES
    coff = jnp.cumsum(rows, axis=1) - rows
    tot = jnp.sum(rows, axis=0)
    padded = (tot + tm - 1) // tm * tm
    pend = jnp.cumsum(padded)
    ebase = pend - padded
    gbase = ebase[None, :] + jnp.cumsum(rows, axis=0) - rows
    n_blocks = _max_rows(n_tok) // tm
    n_used = pend[-1] // tm
    blk = jnp.arange(n_blocks, dtype=jnp.int32)
    first_row = jnp.minimum(blk, n_used - 1) * tm
    block_e = jnp.sum((pend[None, :] <= first_row[:, None]).astype(jnp.int32), axis=1)
    flat = lambda a: a.reshape(-1).astype(jnp.int32)
    return dict(
        coff=flat(coff), rows=flat(rows), gbase=flat(gbase),
        tailoff=flat(ebase + tot), tailn=flat(padded - tot),
        block_e=flat(jnp.minimum(block_e, N_EXPERTS - 1)), n_used=flat(n_used),
        coffv=jnp.zeros((n_groups, 1, LANES), f32).at[:, 0, :N_EXPERTS].set(coff.astype(f32)))


def _max_rows(n_tok):
    tm = EXPERT_TILE
    worst = (n_tok * TOP_K + (n_tok // GROUP) * N_EXPERTS * (SUBLANES - 1)
             + N_EXPERTS * (tm - SUBLANES))
    return (worst + tm - 1) // tm * tm


def _moe(x2, hf, idx, gates, cnt, w_gate_up, b_gate_up, w_down, b_down):
    n_tok, D = x2.shape
    plan = _routing_plan(cnt, n_tok)
    xs, pos = _dispatch(plan, hf, idx, _max_rows(n_tok))
    b_gu = (b_gate_up.reshape(N_EXPERTS, 2 * D_FF // GU_TILE, LANES, 2).transpose(0, 1, 3, 2)
            .reshape(N_EXPERTS, 1, 2 * D_FF))
    y_rows = _expert_mlp(plan, xs, w_gate_up, b_gu, w_down, b_down.reshape(N_EXPERTS, 1, D))
    return _combine(plan, x2, pos, gates, y_rows)


BIAS_ROW = 768


def _band_bias_kernel(r_ref, o_ref):
    x = jnp.broadcast_to(r_ref[0], (Q_TILE, BIAS_ROW))
    x = pltpu.roll(x, BIAS_ROW - (Q_TILE - 1), axis=1, stride=1, stride_axis=0)
    qc = lax.broadcasted_iota(jnp.int32, (Q_TILE, K_BAND), 0) // CHUNK
    kc = lax.broadcasted_iota(jnp.int32, (Q_TILE, K_BAND), 1) // CHUNK
    visible = (kc >= qc) & (kc <= qc + LEFT_CHUNKS)
    o_ref[0] = jnp.where(visible, x[:, :K_BAND], NEG_INF)


def _band_bias(rel_bias):
    H = rel_bias.shape[0]
    d_max = Q_TILE - 1 + LEFT_CHUNKS * CHUNK
    n_far = d_max - MAX_REL + 1
    r = jnp.concatenate(
        [jnp.broadcast_to(rel_bias[:, 2 * MAX_REL:], (H, n_far)),
         rel_bias[:, MAX_REL - (Q_TILE - 1):2 * MAX_REL][:, ::-1],
         jnp.zeros((H, BIAS_ROW - (2 * Q_TILE - 1 + LEFT_CHUNKS * CHUNK)), rel_bias.dtype)], axis=1)
    return pl.pallas_call(
        _band_bias_kernel,
        grid=(H,),
        in_specs=[pl.BlockSpec((1, 1, BIAS_ROW), lambda h: (h, 0, 0))],
        out_specs=pl.BlockSpec((1, Q_TILE, K_BAND), lambda h: (h, 0, 0)),
        out_shape=jax.ShapeDtypeStruct((H, Q_TILE, K_BAND), f32),
        name="band_bias",
    )(r.astype(f32).reshape(H, 1, BIAS_ROW))


def _layer(x, mem, norm_mix_g, w_in, q_norm_g, k_norm_g, rel_bias, conv_w, out_norm_a_g,
           out_norm_b_g, w_out, norm_xattn_g, norm_mem_g, w_xq, w_xkv, xq_norm_g, xk_norm_g,
           w_xo, norm_ffn_g, w_router, b_router, w_gate_up, b_gate_up, w_down, b_down):
    B, S, D = x.shape
    n_tok = B * S
    row = lambda a: a.reshape(1, -1).astype(f32)

    qg = row(jnp.tile(q_norm_g, ATT_HEADS) * (ATT_HEAD_DIM ** -0.5))
    kg = row(jnp.tile(k_norm_g, ATT_HEADS))
    head_ones = jnp.asarray(np.kron(np.eye(ATT_HEADS), np.ones((ATT_HEAD_DIM, ATT_HEAD_DIM))), bf16)
    q, k, v, yb = _mix_in(x, row(norm_mix_g), w_in.astype(bf16), qg, kg, head_ones,
                          conv_w.astype(f32), row(out_norm_b_g))
    x1 = _attn_out(q, k, v, _band_bias(rel_bias), yb, x, w_out.astype(bf16), row(out_norm_a_g))

    kx, vx = _mem_kv(mem, row(norm_mem_g), w_xkv.astype(bf16), row(xk_norm_g))
    wr = jnp.zeros((D, LANES), f32).at[:, :N_EXPERTS].set(w_router)
    wr_hi = wr.astype(bf16)
    wr_lo = (wr - wr_hi.astype(f32)).astype(bf16)
    br = jnp.full((1, LANES), -jnp.inf, f32).at[0, :N_EXPERTS].set(b_router)
    x2, hf, idx, gates, cnt = _xattn_router(
        x1, row(norm_xattn_g), w_xq.astype(bf16), row(xq_norm_g * (X_HEAD_DIM ** -0.5)), kx, vx,
        w_xo.astype(bf16), row(norm_ffn_g), wr_hi, wr_lo, br)

    out = _moe(x2.reshape(n_tok, D), hf.reshape(n_tok, D), idx.reshape(n_tok, LANES),
               gates.reshape(n_tok, LANES), cnt, w_gate_up, b_gate_up, w_down, b_down)
    return out.reshape(B, S, D)


def kernel(x, mem, norm_mix_g, w_in, q_norm_g, k_norm_g, rel_bias, conv_w, out_norm_a_g,
           out_norm_b_g, w_out, norm_xattn_g, norm_mem_g, w_xq, w_xkv, xq_norm_g, xk_norm_g,
           w_xo, norm_ffn_g, w_router, b_router, w_gate_up, b_gate_up, w_down, b_down):
    depth = norm_mix_g.shape[0]
    for l in range(depth):
        x = _layer(x, mem, norm_mix_g[l], w_in[l], q_norm_g[l], k_norm_g[l], rel_bias[l],
                   conv_w[l], out_norm_a_g[l], out_norm_b_g[l], w_out[l], norm_xattn_g[l],
                   norm_mem_g[l], w_xq[l], w_xkv[l], xq_norm_g[l], xk_norm_g[l], w_xo[l],
                   norm_ffn_g[l], w_router[l], b_router[l], w_gate_up[l], b_gate_up[l],
                   w_down[l], b_down[l])
    return x
```

```python
import functools

import jax
import jax.numpy as jnp
import numpy as np
from jax import lax
from jax.experimental import pallas as pl
from jax.experimental.pallas import tpu as pltpu

D_MODEL = 1024
CHUNK = 64
LEFT_CHUNKS = 8
N_MEM = 256
ATT_HEADS = 8
ATT_HEAD_DIM = 64
D_ATT = ATT_HEADS * ATT_HEAD_DIM
D_CONV = 512
MAX_REL = 256
X_HEADS = 4
X_HEAD_DIM = 128
D_X = X_HEADS * X_HEAD_DIM
N_EXPERTS = 32
TOP_K = 4
D_FF = D_MODEL
SWIGLU_LIMIT = 7.0
SWIGLU_ALPHA = 1.702
EPS = 1e-6
NEG_INF = -1e30

LANES = 128
SEQ_TILE = 512
Q_TILE = 128
K_BAND = Q_TILE + LEFT_CHUNKS * CHUNK
EXPERT_TILE = 512
GU_TILE = 2 * LANES
GROUP = 256
SUBLANES = 8
GROUP_ROWS = 1280
VMEM_LIMIT = 56 * 1024 * 1024

bf16 = jnp.bfloat16
f32 = jnp.float32


def _rms(x, g):
    return x * lax.rsqrt(jnp.mean(x * x, axis=-1, keepdims=True) + EPS) * g


def _dot(a, b):
    return jnp.dot(a, b, preferred_element_type=f32)


def _dot_nt(a, b):
    return lax.dot_general(a, b, (((1,), (1,)), ((), ())), preferred_element_type=f32)


def _mix_in_kernel(x_ref, g_ref, w_ref, qg_ref, kg_ref, hm_ref, cw_ref, gb_ref,
                   q_ref, k_ref, v_ref, yb_ref, ub_ref):
    ts = x_ref.shape[1]

    @pl.when(pl.program_id(1) == 0)
    def _():
        ub_ref[0:8, :] = jnp.zeros((8, D_CONV), f32)

    hb = _rms(x_ref[0], g_ref[...]).astype(bf16)

    def head_norm(t, gain):
        ms = _dot((t * t).astype(bf16), hm_ref[...]) * (1.0 / ATT_HEAD_DIM)
        return t * lax.rsqrt(ms + EPS) * gain

    q = _dot(hb, w_ref[:, 0:D_ATT])
    q_ref[0] = head_norm(q, qg_ref[...]).astype(bf16)
    k = _dot(hb, w_ref[:, D_ATT:2 * D_ATT])
    k_ref[0] = head_norm(k, kg_ref[...]).astype(bf16)
    v_ref[0] = _dot(hb, w_ref[:, 2 * D_ATT:3 * D_ATT]).astype(bf16)

    o = 3 * D_ATT
    bg = _dot(hb, w_ref[:, o:o + D_CONV])
    cg = _dot(hb, w_ref[:, o + D_CONV:o + 2 * D_CONV])
    xv = _dot(hb, w_ref[:, o + 2 * D_CONV:o + 3 * D_CONV])
    u = cg * xv
    ub_ref[8:8 + ts, :] = u
    conv = (cw_ref[0:1, :] * ub_ref[6:6 + ts, :] + cw_ref[1:2, :] * ub_ref[7:7 + ts, :]
            + cw_ref[2:3, :] * u)
    ub_ref[0:8, :] = u[ts - 8:ts, :]
    yb_ref[0] = _rms(bg * conv, gb_ref[...]).astype(bf16)


def _mix_in(x, g, w_in, qg, kg, hm, cw, gb):
    B, S, D = x.shape
    ts = SEQ_TILE
    full = lambda shape: pl.BlockSpec(shape, lambda b, j: (0,) * len(shape))
    tile = lambda w: pl.BlockSpec((1, ts, w), lambda b, j: (b, j, 0))
    out = jax.ShapeDtypeStruct((B, S, D_ATT), bf16)
    return pl.pallas_call(
        _mix_in_kernel,
        grid=(B, S // ts),
        in_specs=[tile(D), full((1, D)), full(w_in.shape), full((1, D_ATT)), full((1, D_ATT)),
                  full((D_ATT, D_ATT)), full((3, D_CONV)), full((1, D_CONV))],
        out_specs=[tile(D_ATT)] * 4,
        out_shape=[out] * 4,
        scratch_shapes=[pltpu.VMEM((ts + 8, D_CONV), f32)],
        compiler_params=pltpu.CompilerParams(
            dimension_semantics=("arbitrary", "arbitrary"), vmem_limit_bytes=VMEM_LIMIT),
        name="mix_in",
    )(x, g, w_in, qg, kg, hm, cw, gb)


def _attn_out_kernel(q_ref, k_ref, v_ref, bias_ref, yb_ref, x_ref, w_ref, ga_ref,
                     o_ref, kb_ref, vb_ref):
    tq = q_ref.shape[1]
    n_kc = K_BAND // CHUNK
    c0 = pl.program_id(1) * (tq // CHUNK)
    for kc in range(n_kc):
        src = pl.multiple_of(jnp.maximum(c0 - LEFT_CHUNKS + kc, 0) * CHUNK, CHUNK)
        kb_ref[kc * CHUNK:(kc + 1) * CHUNK, :] = k_ref[0, pl.ds(src, CHUNK), :]
        vb_ref[kc * CHUNK:(kc + 1) * CHUNK, :] = v_ref[0, pl.ds(src, CHUNK), :]

    col_chunk = lax.broadcasted_iota(jnp.int32, (1, K_BAND), 1) // CHUNK
    start_mask = jnp.where(col_chunk >= LEFT_CHUNKS - c0, 0.0, NEG_INF).astype(f32)
    lane = lax.broadcasted_iota(jnp.int32, (1, LANES), 1)
    low = lane < ATT_HEAD_DIM

    pairs = []
    for hp in range(ATT_HEADS // 2):
        sl = slice(hp * LANES, (hp + 1) * LANES)
        qp = q_ref[0, :, sl]
        kp = kb_ref[:, sl]
        vp = vb_ref[:, sl]
        outs = []
        for sub in range(2):
            keep = low if sub == 0 else jnp.logical_not(low)
            qm = jnp.where(keep, qp, jnp.zeros_like(qp))
            s = _dot_nt(qm, kp) + bias_ref[2 * hp + sub] + start_mask
            m = jnp.max(s, axis=-1, keepdims=True)
            p = jnp.exp(s - m)
            l = jnp.sum(p, axis=-1, keepdims=True)
            outs.append(_dot(p.astype(bf16), vp) / l)
        pairs.append(jnp.where(low, outs[0], outs[1]))
    ya = jnp.concatenate(pairs, axis=-1)
    yan = _rms(ya, ga_ref[...]).astype(bf16)
    o_ref[0] = (x_ref[0] + _dot(yan, w_ref[0:D_ATT, :]) + _dot(yb_ref[0], w_ref[D_ATT:, :]))


def _attn_out(q, k, v, bias, yb, x, w_out, ga):
    B, S, D = x.shape
    tq = Q_TILE
    full = lambda shape: pl.BlockSpec(shape, lambda b, j: (0,) * len(shape))
    tile = lambda w: pl.BlockSpec((1, tq, w), lambda b, j: (b, j, 0))
    seq = pl.BlockSpec((1, S, D_ATT), lambda b, j: (b, 0, 0))
    return pl.pallas_call(
        _attn_out_kernel,
        grid=(B, S // tq),
        in_specs=[tile(D_ATT), seq, seq, full(bias.shape), tile(D_CONV), tile(D),
                  full(w_out.shape), full((1, D_ATT))],
        out_specs=tile(D),
        out_shape=jax.ShapeDtypeStruct((B, S, D), f32),
        scratch_shapes=[pltpu.VMEM((K_BAND, D_ATT), bf16), pltpu.VMEM((K_BAND, D_ATT), bf16)],
        compiler_params=pltpu.CompilerParams(
            dimension_semantics=("parallel", "parallel"), vmem_limit_bytes=VMEM_LIMIT),
        name="attn_out",
    )(q, k, v, bias, yb, x, w_out, ga)


def _mem_kv_kernel(m_ref, g_ref, w_ref, kg_ref, k_ref, v_ref):
    hm = _rms(m_ref[0], g_ref[...]).astype(bf16)
    kv = _dot(hm, w_ref[...])
    for h in range(X_HEADS):
        sl = slice(h * X_HEAD_DIM, (h + 1) * X_HEAD_DIM)
        k_ref[0, :, sl] = _rms(kv[:, sl], kg_ref[...]).astype(bf16)
    v_ref[0] = kv[:, D_X:].astype(bf16)


def _mem_kv(mem, g, w_xkv, kg):
    B = mem.shape[0]
    full = lambda shape: pl.BlockSpec(shape, lambda b: (0,) * len(shape))
    out = jax.ShapeDtypeStruct((B, N_MEM, D_X), bf16)
    return pl.pallas_call(
        _mem_kv_kernel,
        grid=(B,),
        in_specs=[pl.BlockSpec((1, N_MEM, D_MODEL), lambda b: (b, 0, 0)), full((1, D_MODEL)),
                  full(w_xkv.shape), full((1, X_HEAD_DIM))],
        out_specs=[pl.BlockSpec((1, N_MEM, D_X), lambda b: (b, 0, 0))] * 2,
        out_shape=[out, out],
        compiler_params=pltpu.CompilerParams(
            dimension_semantics=("parallel",), vmem_limit_bytes=VMEM_LIMIT),
        name="mem_kv",
    )(mem, g, w_xkv, kg)


def _xattn_router_kernel(x_ref, gx_ref, wq_ref, qg_ref, k_ref, v_ref, wo_ref, gf_ref,
                         wrh_ref, wrl_ref, br_ref,
                         x2_ref, hf_ref, idx_ref, gate_ref, cnt_ref):
    x1 = x_ref[0]
    hb = _rms(x1, gx_ref[...]).astype(bf16)
    q = _dot(hb, wq_ref[...])
    heads = []
    for h in range(X_HEADS):
        sl = slice(h * X_HEAD_DIM, (h + 1) * X_HEAD_DIM)
        qh = _rms(q[:, sl], qg_ref[...]).astype(bf16)
        s = _dot_nt(qh, k_ref[0, :, sl])
        m = jnp.max(s, axis=-1, keepdims=True)
        p = jnp.exp(s - m)
        l = jnp.sum(p, axis=-1, keepdims=True)
        heads.append((_dot(p.astype(bf16), v_ref[0, :, sl]) / l).astype(bf16))
    o = jnp.concatenate(heads, axis=-1)
    x2 = x1 + _dot(o, wo_ref[...])
    x2_ref[0] = x2

    hf = _rms(x2, gf_ref[...])
    hf_hi = hf.astype(bf16)
    hf_ref[0] = hf_hi
    hf_lo = (hf - hf_hi.astype(f32)).astype(bf16)
    logits = (_dot(hf_hi, wrh_ref[...]) + _dot(hf_lo, wrh_ref[...]) + _dot(hf_hi, wrl_ref[...])
              + br_ref[...])
    lane = lax.broadcasted_iota(jnp.int32, logits.shape, 1)
    vals, idxs = [], []
    for _ in range(TOP_K):
        m = jnp.max(logits, axis=-1, keepdims=True)
        i = jnp.min(jnp.where(logits == m, lane, LANES), axis=-1, keepdims=True)
        logits = jnp.where(lane == i, -jnp.inf, logits)
        vals.append(m)
        idxs.append(i)
    es = [jnp.exp(v - vals[0]) for v in vals]
    tot = es[0] + es[1] + es[2] + es[3]
    idx_out = jnp.zeros(logits.shape, jnp.int32)
    gate_out = jnp.zeros(logits.shape, f32)
    sel = jnp.zeros(logits.shape, f32)
    for kk in range(TOP_K):
        idx_out = jnp.where(lane == kk, idxs[kk], idx_out)
        gate_out = jnp.where(lane == kk, es[kk] / tot, gate_out)
        sel = jnp.where(lane == idxs[kk], 1.0, sel)
    idx_ref[0] = idx_out
    gate_ref[0] = gate_out
    for j in range(cnt_ref.shape[0]):
        cnt_ref[j] = jnp.sum(sel[j * GROUP:(j + 1) * GROUP], axis=0, keepdims=True)


def _xattn_router(x1, gx, w_xq, qg, kx, vx, w_xo, gf, wr_hi, wr_lo, br):
    B, S, D = x1.shape
    ts = SEQ_TILE
    full = lambda shape: pl.BlockSpec(shape, lambda b, j: (0,) * len(shape))
    tile = lambda w: pl.BlockSpec((1, ts, w), lambda b, j: (b, j, 0))
    mem = pl.BlockSpec((1, N_MEM, D_X), lambda b, j: (b, 0, 0))
    return pl.pallas_call(
        _xattn_router_kernel,
        grid=(B, S // ts),
        in_specs=[tile(D), full((1, D)), full(w_xq.shape), full((1, X_HEAD_DIM)), mem, mem,
                  full(w_xo.shape), full((1, D)), full(wr_hi.shape), full(wr_lo.shape),
                  full((1, LANES))],
        out_specs=[tile(D), tile(D), tile(LANES), tile(LANES),
                   pl.BlockSpec((ts // GROUP, 1, LANES), lambda b, j: (b * (S // ts) + j, 0, 0))],
        out_shape=[jax.ShapeDtypeStruct((B, S, D), f32), jax.ShapeDtypeStruct((B, S, D), bf16),
                   jax.ShapeDtypeStruct((B, S, LANES), jnp.int32),
                   jax.ShapeDtypeStruct((B, S, LANES), f32),
                   jax.ShapeDtypeStruct((B * S // GROUP, 1, LANES), f32)],
        compiler_params=pltpu.CompilerParams(
            dimension_semantics=("parallel", "parallel"), vmem_limit_bytes=VMEM_LIMIT),
        name="xattn_router",
    )(x1, gx, w_xq, qg, kx, vx, w_xo, gf, wr_hi, wr_lo, br)


def _for_cells(g, fn):
    def body(e, carry):
        fn(g * N_EXPERTS + e)
        return carry
    lax.fori_loop(0, N_EXPERTS, body, 0)


def _group_positions(idx_ref, coffv_ref, ltri_ref):
    idxv = idx_ref[...]
    lane = lax.broadcasted_iota(jnp.int32, idxv.shape, 1)
    hots = [lane == idxv[:, kk:kk + 1] for kk in range(TOP_K)]
    multi = jnp.zeros(idxv.shape, f32)
    for h in hots:
        multi = jnp.where(h, 1.0, multi)
    rank = _dot(ltri_ref[...], multi.astype(bf16))
    posmat = coffv_ref[0] + rank
    return [jnp.sum(jnp.where(h, posmat, 0.0), axis=-1, keepdims=True) for h in hots]


def _dispatch_kernel(coff_s, rows_s, gbase_s, toff_s, tn_s, nu_s,
                     hf_ref, idx_ref, coffv_ref, ltri_ref,
                     xs_hbm, pos_ref, xbuf, zbuf, sem, zsem, *, n_groups):
    g = pl.program_id(0)
    slot = lax.rem(g, 2)

    def cell_copy(sl, i):
        n = pl.multiple_of(rows_s[i], SUBLANES)
        src = pl.multiple_of(coff_s[i], SUBLANES)
        dst = pl.multiple_of(gbase_s[i], SUBLANES)
        return n, pltpu.make_async_copy(xbuf.at[sl, pl.ds(src, n)], xs_hbm.at[pl.ds(dst, n)],
                                        sem.at[sl])

    def start_cell(sl, i):
        n, cp = cell_copy(sl, i)
        pl.when(n > 0)(cp.start)

    def wait_cell(sl, i):
        n, cp = cell_copy(sl, i)
        pl.when(n > 0)(cp.wait)

    @pl.when(g >= 2)
    def _():
        _for_cells(g - 2, functools.partial(wait_cell, slot))

    pos = _group_positions(idx_ref, coffv_ref, ltri_ref)
    lane = lax.broadcasted_iota(jnp.int32, (GROUP, LANES), 1)
    riota = lax.broadcasted_iota(jnp.int32, (GROUP, GROUP_ROWS), 1)
    pt = jnp.zeros((GROUP, GROUP_ROWS), f32)
    pos_out = jnp.zeros((GROUP, LANES), f32)
    for kk in range(TOP_K):
        pt = jnp.where(riota == pos[kk].astype(jnp.int32), 1.0, pt)
        pos_out = jnp.where(lane == kk, pos[kk], pos_out)
    pos_ref[...] = pos_out
    xbuf[slot] = lax.dot_general(pt.astype(bf16), hf_ref[...], (((0,), (0,)), ((), ())),
                                 preferred_element_type=f32)
    _for_cells(g, functools.partial(start_cell, slot))

    @pl.when(g == n_groups - 1)
    def _():
        if n_groups >= 2:
            _for_cells(g - 1, functools.partial(wait_cell, 1 - slot))
        _for_cells(g, functools.partial(wait_cell, slot))
        zbuf[...] = jnp.zeros(zbuf.shape, f32)

        def tail_copy(e):
            n = pl.multiple_of(tn_s[e], SUBLANES)
            dst = pl.multiple_of(toff_s[e], SUBLANES)
            return n, pltpu.make_async_copy(zbuf.at[pl.ds(0, n)], xs_hbm.at[pl.ds(dst, n)], zsem)

        def tail(e, carry):
            n, cp = tail_copy(e)

            @pl.when(n > 0)
            def _():
                cp.start()
                cp.wait()
            return carry
        lax.fori_loop(0, N_EXPERTS, tail, 0)

        def unused(b, carry):
            dst = pl.multiple_of(b * EXPERT_TILE, EXPERT_TILE)
            cp = pltpu.make_async_copy(zbuf, xs_hbm.at[pl.ds(dst, EXPERT_TILE)], zsem)
            cp.start()
            cp.wait()
            return carry
        lax.fori_loop(nu_s[0], xs_hbm.shape[0] // EXPERT_TILE, unused, 0)


def _dispatch(plan, hf, idx, n_rows):
    n_tok, D = hf.shape
    n_groups = n_tok // GROUP
    ltri = jnp.asarray(np.tril(np.ones((GROUP, GROUP)), -1), bf16)
    grid_spec = pltpu.PrefetchScalarGridSpec(
        num_scalar_prefetch=6,
        grid=(n_groups,),
        in_specs=[pl.BlockSpec((GROUP, D), lambda g, *_: (g, 0)),
                  pl.BlockSpec((GROUP, LANES), lambda g, *_: (g, 0)),
                  pl.BlockSpec((1, 1, LANES), lambda g, *_: (g, 0, 0)),
                  pl.BlockSpec((GROUP, GROUP), lambda g, *_: (0, 0))],
        out_specs=[pl.BlockSpec(memory_space=pl.ANY),
                   pl.BlockSpec((GROUP, LANES), lambda g, *_: (g, 0))],
        scratch_shapes=[pltpu.VMEM((2, GROUP_ROWS, D), f32), pltpu.VMEM((EXPERT_TILE, D), f32),
                        pltpu.SemaphoreType.DMA((2,)), pltpu.SemaphoreType.DMA(())],
    )
    return pl.pallas_call(
        functools.partial(_dispatch_kernel, n_groups=n_groups),
        grid_spec=grid_spec,
        out_shape=[jax.ShapeDtypeStruct((n_rows, D), f32),
                   jax.ShapeDtypeStruct((n_tok, LANES), f32)],
        compiler_params=pltpu.CompilerParams(
            dimension_semantics=("arbitrary",), vmem_limit_bytes=VMEM_LIMIT),
        name="dispatch",
    )(plan["coff"], plan["rows"], plan["gbase"], plan["tailoff"], plan["tailn"], plan["n_used"],
      hf, idx, plan["coffv"], ltri)


def _expert_kernel(be_ref, nu_ref, x_ref, wgu_ref, bgu_ref, wdn_ref, bdn_ref, pm_ref, y_ref,
                   wgu_b, wdn_b):
    b = pl.program_id(0)

    @pl.when(b < nu_ref[0])
    def _():
        @pl.when((b == 0) | (be_ref[b] != be_ref[jnp.maximum(b - 1, 0)]))
        def _():
            for c in range(2 * D_FF // GU_TILE):
                cols = slice(c * GU_TILE, (c + 1) * GU_TILE)
                wgu_b[:, cols] = _dot(wgu_ref[0, :, cols].astype(bf16), pm_ref[...]).astype(bf16)
            for c in range(D_FF // LANES):
                rows = slice(c * LANES, (c + 1) * LANES)
                wdn_b[rows, :] = wdn_ref[0, rows, :].astype(bf16)

        _expert_block(x_ref, bgu_ref, bdn_ref, y_ref, wgu_b, wdn_b)

    @pl.when(b >= nu_ref[0])
    def _():
        y_ref[...] = jnp.zeros(y_ref.shape, f32)


def _expert_block(x_ref, bgu_ref, bdn_ref, y_ref, wgu_b, wdn_b):
    gu = _dot(x_ref[...].astype(bf16), wgu_b[...]) + bgu_ref[0]
    acts = []
    for c in range(2 * D_FF // GU_TILE):
        g = jnp.minimum(gu[:, c * GU_TILE:c * GU_TILE + LANES], SWIGLU_LIMIT)
        lin = jnp.clip(gu[:, c * GU_TILE + LANES:(c + 1) * GU_TILE], -SWIGLU_LIMIT, SWIGLU_LIMIT)
        acts.append((g * jax.nn.sigmoid(SWIGLU_ALPHA * g) * (lin + 1.0)).astype(bf16))
    act = jnp.concatenate(acts, axis=-1)
    y_ref[...] = _dot(act, wdn_b[...]) + bdn_ref[0]


def _gate_up_perm():
    j = np.arange(GU_TILE)
    p = np.zeros((GU_TILE, GU_TILE), np.float32)
    p[np.where(j < LANES, 2 * j, 2 * (j - LANES) + 1), j] = 1.0
    return jnp.asarray(p, bf16)


def _expert_mlp(plan, xs, w_gu, b_gu, w_dn, b_dn):
    n_rows, D = xs.shape
    tm = EXPERT_TILE
    row_blk = lambda b, be, nu: (jnp.minimum(b, nu[0] - 1), 0)
    per_e = lambda b, be, nu: (be[b], 0, 0)
    grid_spec = pltpu.PrefetchScalarGridSpec(
        num_scalar_prefetch=2,
        grid=(n_rows // tm,),
        in_specs=[pl.BlockSpec((tm, D), row_blk),
                  pl.BlockSpec((1, D, 2 * D_FF), per_e),
                  pl.BlockSpec((1, 1, 2 * D_FF), per_e),
                  pl.BlockSpec((1, D_FF, D), per_e),
                  pl.BlockSpec((1, 1, D), per_e),
                  pl.BlockSpec((GU_TILE, GU_TILE), lambda b, be, nu: (0, 0))],
        out_specs=pl.BlockSpec((tm, D), lambda b, be, nu: (b, 0)),
        scratch_shapes=[pltpu.VMEM((D, 2 * D_FF), bf16), pltpu.VMEM((D_FF, D), bf16)],
    )
    return pl.pallas_call(
        _expert_kernel,
        grid_spec=grid_spec,
        out_shape=jax.ShapeDtypeStruct((n_rows, D), f32),
        compiler_params=pltpu.CompilerParams(
            dimension_semantics=("arbitrary",), vmem_limit_bytes=VMEM_LIMIT),
        name="expert_mlp",
    )(plan["block_e"], plan["n_used"], xs, w_gu, b_gu, w_dn, b_dn, _gate_up_perm())


def _combine_kernel(coff_s, rows_s, gbase_s, x_ref, pos_ref, gate_ref, y_hbm, o_ref, ybuf, sem,
                    *, n_groups):
    g = pl.program_id(0)
    slot = lax.rem(g, 2)

    def cell_copy(sl, i):
        n = pl.multiple_of(rows_s[i], SUBLANES)
        loc = pl.multiple_of(coff_s[i], SUBLANES)
        src = pl.multiple_of(gbase_s[i], SUBLANES)
        return n, pltpu.make_async_copy(y_hbm.at[pl.ds(src, n)], ybuf.at[sl, pl.ds(loc, n)],
                                        sem.at[sl])

    def start_cell(sl, i):
        n, cp = cell_copy(sl, i)
        pl.when(n > 0)(cp.start)

    def wait_cell(sl, i):
        n, cp = cell_copy(sl, i)
        pl.when(n > 0)(cp.wait)

    @pl.when(g == 0)
    def _():
        ybuf[...] = jnp.zeros(ybuf.shape, f32)
        _for_cells(g, functools.partial(start_cell, slot))

    @pl.when(g + 1 < n_groups)
    def _():
        _for_cells(g + 1, functools.partial(start_cell, 1 - slot))

    _for_cells(g, functools.partial(wait_cell, slot))

    riota = lax.broadcasted_iota(jnp.int32, (GROUP, GROUP_ROWS), 1)
    pt = jnp.zeros((GROUP, GROUP_ROWS), f32)
    for kk in range(TOP_K):
        pk = pos_ref[:, kk:kk + 1].astype(jnp.int32)
        pt = jnp.where(riota == pk, gate_ref[:, kk:kk + 1], pt)
    o_ref[...] = x_ref[...] + _dot(pt.astype(bf16), ybuf[slot].astype(bf16))


def _combine(plan, x2, pos, gates, y_rows):
    n_tok, D = x2.shape
    n_groups = n_tok // GROUP
    tok = lambda w: pl.BlockSpec((GROUP, w), lambda g, *_: (g, 0))
    grid_spec = pltpu.PrefetchScalarGridSpec(
        num_scalar_prefetch=3,
        grid=(n_groups,),
        in_specs=[tok(D), tok(LANES), tok(LANES), pl.BlockSpec(memory_space=pl.ANY)],
        out_specs=tok(D),
        scratch_shapes=[pltpu.VMEM((2, GROUP_ROWS, D), f32), pltpu.SemaphoreType.DMA((2,))],
    )
    return pl.pallas_call(
        functools.partial(_combine_kernel, n_groups=n_groups),
        grid_spec=grid_spec,
        out_shape=jax.ShapeDtypeStruct((n_tok, D), f32),
        compiler_params=pltpu.CompilerParams(
            dimension_semantics=("arbitrary",), vmem_limit_bytes=VMEM_LIMIT),
        name="combine",
    )(plan["coff"], plan["rows"], plan["gbase"], x2, pos, gates, y_rows)


def _routing_plan(cnt, n_tok):
    tm = EXPERT_TILE
    n_groups = n_tok // GROUP
    cnt = cnt.reshape(n_groups, LANES)[:, :N_EXPERTS].astype(jnp.int32)
    rows = (cnt + SUBLANES - 1) // SUBLANES * SUBLANES
    coff = jnp.cumsum(rows, axis=1) - rows
    tot = jnp.sum(rows, axis=0)
    padded = (tot + tm - 1) // tm * tm
    pend = jnp.cumsum(padded)
    ebase = pend - padded
    gbase = ebase[None, :] + jnp.cumsum(rows, axis=0) - rows
    n_blocks = _max_rows(n_tok) // tm
    n_used = pend[-1] // tm
    blk = jnp.arange(n_blocks, dtype=jnp.int32)
    first_row = jnp.minimum(blk, n_used - 1) * tm
    block_e = jnp.sum((pend[None, :] <= first_row[:, None]).astype(jnp.int32), axis=1)
    flat = lambda a: a.reshape(-1).astype(jnp.int32)
    return dict(
        coff=flat(coff), rows=flat(rows), gbase=flat(gbase),
        tailoff=flat(ebase + tot), tailn=flat(padded - tot),
        block_e=flat(jnp.minimum(block_e, N_EXPERTS - 1)), n_used=flat(n_used),
        coffv=jnp.zeros((n_groups, 1, LANES), f32).at[:, 0, :N_EXPERTS].set(coff.astype(f32)))


def _max_rows(n_tok):
    tm = EXPERT_TILE
    worst = (n_tok * TOP_K + (n_tok // GROUP) * N_EXPERTS * (SUBLANES - 1)
             + N_EXPERTS * (tm - SUBLANES))
    return (worst + tm - 1) // tm * tm


def _moe(x2, hf, idx, gates, cnt, w_gate_up, b_gate_up, w_down, b_down):
    n_tok, D = x2.shape
    plan = _routing_plan(cnt, n_tok)
    xs, pos = _dispatch(plan, hf, idx, _max_rows(n_tok))
    b_gu = (b_gate_up.reshape(N_EXPERTS, 2 * D_FF // GU_TILE, LANES, 2).transpose(0, 1, 3, 2)
            .reshape(N_EXPERTS, 1, 2 * D_FF))
    y_rows = _expert_mlp(plan, xs, w_gate_up, b_gu, w_down, b_down.reshape(N_EXPERTS, 1, D))
    return _combine(plan, x2, pos, gates, y_rows)


BIAS_ROW = 768


def _band_bias_kernel(r_ref, o_ref):
    x = jnp.broadcast_to(r_ref[0], (Q_TILE, BIAS_ROW))
    x = pltpu.roll(x, BIAS_ROW - (Q_TILE - 1), axis=1, stride=1, stride_axis=0)
    qc = lax.broadcasted_iota(jnp.int32, (Q_TILE, K_BAND), 0) // CHUNK
    kc = lax.broadcasted_iota(jnp.int32, (Q_TILE, K_BAND), 1) // CHUNK
    visible = (kc >= qc) & (kc <= qc + LEFT_CHUNKS)
    o_ref[0] = jnp.where(visible, x[:, :K_BAND], NEG_INF)


def _band_bias(rel_bias):
    H = rel_bias.shape[0]
    d_max = Q_TILE - 1 + LEFT_CHUNKS * CHUNK
    n_far = d_max - MAX_REL + 1
    r = jnp.concatenate(
        [jnp.broadcast_to(rel_bias[:, 2 * MAX_REL:], (H, n_far)),
         rel_bias[:, MAX_REL - (Q_TILE - 1):2 * MAX_REL][:, ::-1],
         jnp.zeros((H, BIAS_ROW - (2 * Q_TILE - 1 + LEFT_CHUNKS * CHUNK)), rel_bias.dtype)], axis=1)
    return pl.pallas_call(
        _band_bias_kernel,
        grid=(H,),
        in_specs=[pl.BlockSpec((1, 1, BIAS_ROW), lambda h: (h, 0, 0))],
        out_specs=pl.BlockSpec((1, Q_TILE, K_BAND), lambda h: (h, 0, 0)),
        out_shape=jax.ShapeDtypeStruct((H, Q_TILE, K_BAND), f32),
        name="band_bias",
    )(r.astype(f32).reshape(H, 1, BIAS_ROW))


def _layer(x, mem, norm_mix_g, w_in, q_norm_g, k_norm_g, rel_bias, conv_w, out_norm_a_g,
           out_norm_b_g, w_out, norm_xattn_g, norm_mem_g, w_xq, w_xkv, xq_norm_g, xk_norm_g,
           w_xo, norm_ffn_g, w_router, b_router, w_gate_up, b_gate_up, w_down, b_down):
    B, S, D = x.shape
    n_tok = B * S
    row = lambda a: a.reshape(1, -1).astype(f32)

    qg = row(jnp.tile(q_norm_g, ATT_HEADS) * (ATT_HEAD_DIM ** -0.5))
    kg = row(jnp.tile(k_norm_g, ATT_HEADS))
    head_ones = jnp.asarray(np.kron(np.eye(ATT_HEADS), np.ones((ATT_HEAD_DIM, ATT_HEAD_DIM))), bf16)
    q, k, v, yb = _mix_in(x, row(norm_mix_g), w_in.astype(bf16), qg, kg, head_ones,
                          conv_w.astype(f32), row(out_norm_b_g))
    x1 = _attn_out(q, k, v, _band_bias(rel_bias), yb, x, w_out.astype(bf16), row(out_norm_a_g))

    kx, vx = _mem_kv(mem, row(norm_mem_g), w_xkv.astype(bf16), row(xk_norm_g))
    wr = jnp.zeros((D, LANES), f32).at[:, :N_EXPERTS].set(w_router)
    wr_hi = wr.astype(bf16)
    wr_lo = (wr - wr_hi.astype(f32)).astype(bf16)
    br = jnp.full((1, LANES), -jnp.inf, f32).at[0, :N_EXPERTS].set(b_router)
    x2, hf, idx, gates, cnt = _xattn_router(
        x1, row(norm_xattn_g), w_xq.astype(bf16), row(xq_norm_g * (X_HEAD_DIM ** -0.5)), kx, vx,
        w_xo.astype(bf16), row(norm_ffn_g), wr_hi, wr_lo, br)

    out = _moe(x2.reshape(n_tok, D), hf.reshape(n_tok, D), idx.reshape(n_tok, LANES),
               gates.reshape(n_tok, LANES), cnt, w_gate_up, b_gate_up, w_down, b_down)
    return out.reshape(B, S, D)


def kernel(x, mem, norm_mix_g, w_in, q_norm_g, k_norm_g, rel_bias, conv_w, out_norm_a_g,
           out_norm_b_g, w_out, norm_xattn_g, norm_mem_g, w_xq, w_xkv, xq_norm_g, xk_norm_g,
           w_xo, norm_ffn_g, w_router, b_router, w_gate_up, b_gate_up, w_down, b_down):
    depth = norm_mix_g.shape[0]
    for l in range(depth):
        x = _layer(x, mem, norm_mix_g[l], w_in[l], q_norm_g[l], k_norm_g[l], rel_bias[l],
                   conv_w[l], out_norm_a_g[l], out_norm_b_g[l], w_out[l], norm_xattn_g[l],
                   norm_mem_g[l], w_xq[l], w_xkv[l], xq_norm_g[l], xk_norm_g[l], w_xo[l],
                   norm_ffn_g[l], w_router[l], b_router[l], w_gate_up[l], b_gate_up[l],
                   w_down[l], b_down[l])
    return x
```

```python
import functools

import jax
import jax.numpy as jnp
import numpy as np
from jax import lax
from jax.experimental import pallas as pl
from jax.experimental.pallas import tpu as pltpu

D_MODEL = 1024
CHUNK = 64
LEFT_CHUNKS = 8
N_MEM = 256
ATT_HEADS = 8
ATT_HEAD_DIM = 64
D_ATT = ATT_HEADS * ATT_HEAD_DIM
D_CONV = 512
MAX_REL = 256
X_HEADS = 4
X_HEAD_DIM = 128
D_X = X_HEADS * X_HEAD_DIM
N_EXPERTS = 32
TOP_K = 4
D_FF = D_MODEL
SWIGLU_LIMIT = 7.0
SWIGLU_ALPHA = 1.702
EPS = 1e-6
NEG_INF = -1e30

LANES = 128
SEQ_TILE = 512
Q_TILE = 256
K_BAND = Q_TILE + LEFT_CHUNKS * CHUNK
START_TILES = LEFT_CHUNKS * CHUNK // Q_TILE
LOG2E = 1.4426950408889634
HEAD_ROUND = 4
EXPERT_TILE = 512
GU_TILE = 2 * LANES
GROUP = 256
SUBLANES = 8
GROUP_ROWS = 1280
VMEM_LIMIT = 56 * 1024 * 1024

bf16 = jnp.bfloat16
f32 = jnp.float32


def _rms(x, g):
    return x * lax.rsqrt(jnp.mean(x * x, axis=-1, keepdims=True) + EPS) * g


def _dot(a, b):
    return jnp.dot(a, b, preferred_element_type=f32)


def _dot_nt(a, b):
    return lax.dot_general(a, b, (((1,), (1,)), ((), ())), preferred_element_type=f32)


def _mix_in_kernel(x_ref, g_ref, w_ref, qg_ref, kg_ref, hm_ref, cw_ref, gb_ref,
                   q_ref, k_ref, v_ref, yb_ref, ub_ref):
    ts = x_ref.shape[1]

    @pl.when(pl.program_id(1) == 0)
    def _():
        ub_ref[0:8, :] = jnp.zeros((8, D_CONV), f32)

    hb = _rms(x_ref[0], g_ref[...]).astype(bf16)

    def head_norm(t, gain):
        ms = _dot((t * t).astype(bf16), hm_ref[...]) * (1.0 / ATT_HEAD_DIM)
        return t * lax.rsqrt(ms + EPS) * gain

    q = _dot(hb, w_ref[:, 0:D_ATT])
    q_ref[0] = head_norm(q, qg_ref[...]).astype(bf16)
    k = _dot(hb, w_ref[:, D_ATT:2 * D_ATT])
    k_ref[0] = head_norm(k, kg_ref[...]).astype(bf16)
    v_ref[0] = _dot(hb, w_ref[:, 2 * D_ATT:3 * D_ATT]).astype(bf16)

    o = 3 * D_ATT
    bg = _dot(hb, w_ref[:, o:o + D_CONV])
    cg = _dot(hb, w_ref[:, o + D_CONV:o + 2 * D_CONV])
    xv = _dot(hb, w_ref[:, o + 2 * D_CONV:o + 3 * D_CONV])
    u = cg * xv
    ub_ref[8:8 + ts, :] = u
    conv = (cw_ref[0:1, :] * ub_ref[6:6 + ts, :] + cw_ref[1:2, :] * ub_ref[7:7 + ts, :]
            + cw_ref[2:3, :] * u)
    ub_ref[0:8, :] = u[ts - 8:ts, :]
    yb_ref[0] = _rms(bg * conv, gb_ref[...]).astype(bf16)


def _mix_in(x, g, w_in, qg, kg, hm, cw, gb):
    B, S, D = x.shape
    ts = SEQ_TILE
    full = lambda shape: pl.BlockSpec(shape, lambda b, j: (0,) * len(shape))
    tile = lambda w: pl.BlockSpec((1, ts, w), lambda b, j: (b, j, 0))
    out = jax.ShapeDtypeStruct((B, S, D_ATT), bf16)
    return pl.pallas_call(
        _mix_in_kernel,
        grid=(B, S // ts),
        in_specs=[tile(D), full((1, D)), full(w_in.shape), full((1, D_ATT)), full((1, D_ATT)),
                  full((D_ATT, D_ATT)), full((3, D_CONV)), full((1, D_CONV))],
        out_specs=[tile(D_ATT)] * 4,
        out_shape=[out] * 4,
        scratch_shapes=[pltpu.VMEM((ts + 8, D_CONV), f32)],
        compiler_params=pltpu.CompilerParams(
            dimension_semantics=("arbitrary", "arbitrary"), vmem_limit_bytes=VMEM_LIMIT),
        name="mix_in",
    )(x, g, w_in, qg, kg, hm, cw, gb)


def _attn_out_kernel(q_ref, k_ref, v_ref, bias_ref, yb_ref, x_ref, w_ref, ga_ref,
                     o_ref, kb_ref, vb_ref):
    tq = q_ref.shape[1]
    n_kc = K_BAND // CHUNK
    c0 = pl.program_id(1) * (tq // CHUNK)
    for kc in range(n_kc):
        src = pl.multiple_of(jnp.maximum(c0 - LEFT_CHUNKS + kc, 0) * CHUNK, CHUNK)
        kb_ref[kc * CHUNK:(kc + 1) * CHUNK, :] = k_ref[0, pl.ds(src, CHUNK), :]
        vb_ref[kc * CHUNK:(kc + 1) * CHUNK, :] = v_ref[0, pl.ds(src, CHUNK), :]

    lane = lax.broadcasted_iota(jnp.int32, (1, LANES), 1)
    low = lane < ATT_HEAD_DIM

    outs = []
    for h0 in range(0, ATT_HEADS, HEAD_ROUND):
        heads = range(h0, h0 + HEAD_ROUND)
        scores, probs, denoms = {}, {}, {}
        for h in heads:
            sl = slice((h // 2) * LANES, (h // 2 + 1) * LANES)
            qp = q_ref[0, :, sl]
            keep = low if h % 2 == 0 else jnp.logical_not(low)
            scores[h] = _dot_nt(jnp.where(keep, qp, jnp.zeros_like(qp)), kb_ref[:, sl])
        for h in heads:
            s = scores[h] + bias_ref[0, h]
            p = jnp.exp2(s - jnp.max(s, axis=-1, keepdims=True))
            denoms[h] = jnp.sum(p, axis=-1, keepdims=True)
            probs[h] = p.astype(bf16)
        for h in heads:
            sl = slice((h // 2) * LANES, (h // 2 + 1) * LANES)
            outs.append(_dot(probs[h], vb_ref[:, sl]) / denoms[h])
    ya = jnp.concatenate([jnp.where(low, outs[2 * hp], outs[2 * hp + 1])
                          for hp in range(ATT_HEADS // 2)], axis=-1)
    yan = _rms(ya, ga_ref[...]).astype(bf16)
    o_ref[0] = (x_ref[0] + _dot(yan, w_ref[0:D_ATT, :]) + _dot(yb_ref[0], w_ref[D_ATT:, :]))


def _attn_out(q, k, v, bias, yb, x, w_out, ga):
    B, S, D = x.shape
    tq = Q_TILE
    full = lambda shape: pl.BlockSpec(shape, lambda b, j: (0,) * len(shape))
    tile = lambda w: pl.BlockSpec((1, tq, w), lambda b, j: (b, j, 0))
    seq = pl.BlockSpec((1, S, D_ATT), lambda b, j: (b, 0, 0))
    return pl.pallas_call(
        _attn_out_kernel,
        grid=(B, S // tq),
        in_specs=[tile(D_ATT), seq, seq,
                  pl.BlockSpec((1,) + bias.shape[1:], lambda b, j: (jnp.minimum(j, START_TILES), 0, 0, 0)),
                  tile(D_CONV), tile(D),
                  full(w_out.shape), full((1, D_ATT))],
        out_specs=tile(D),
        out_shape=jax.ShapeDtypeStruct((B, S, D), f32),
        scratch_shapes=[pltpu.VMEM((K_BAND, D_ATT), bf16), pltpu.VMEM((K_BAND, D_ATT), bf16)],
        compiler_params=pltpu.CompilerParams(
            dimension_semantics=("parallel", "parallel"), vmem_limit_bytes=VMEM_LIMIT),
        name="attn_out",
    )(q, k, v, bias, yb, x, w_out, ga)


def _mem_kv_kernel(m_ref, g_ref, w_ref, kg_ref, k_ref, v_ref):
    hm = _rms(m_ref[0], g_ref[...]).astype(bf16)
    kv = _dot(hm, w_ref[...])
    for h in range(X_HEADS):
        sl = slice(h * X_HEAD_DIM, (h + 1) * X_HEAD_DIM)
        k_ref[0, :, sl] = _rms(kv[:, sl], kg_ref[...]).astype(bf16)
    v_ref[0] = kv[:, D_X:].astype(bf16)


def _mem_kv(mem, g, w_xkv, kg):
    B = mem.shape[0]
    full = lambda shape: pl.BlockSpec(shape, lambda b: (0,) * len(shape))
    out = jax.ShapeDtypeStruct((B, N_MEM, D_X), bf16)
    return pl.pallas_call(
        _mem_kv_kernel,
        grid=(B,),
        in_specs=[pl.BlockSpec((1, N_MEM, D_MODEL), lambda b: (b, 0, 0)), full((1, D_MODEL)),
                  full(w_xkv.shape), full((1, X_HEAD_DIM))],
        out_specs=[pl.BlockSpec((1, N_MEM, D_X), lambda b: (b, 0, 0))] * 2,
        out_shape=[out, out],
        compiler_params=pltpu.CompilerParams(
            dimension_semantics=("parallel",), vmem_limit_bytes=VMEM_LIMIT),
        name="mem_kv",
    )(mem, g, w_xkv, kg)


def _xattn_router_kernel(x_ref, gx_ref, wq_ref, qg_ref, k_ref, v_ref, wo_ref, gf_ref,
                         wrh_ref, wrl_ref, br_ref,
                         x2_ref, hf_ref, idx_ref, gate_ref, cnt_ref):
    x1 = x_ref[0]
    hb = _rms(x1, gx_ref[...]).astype(bf16)
    q = _dot(hb, wq_ref[...])
    heads = []
    for h in range(X_HEADS):
        sl = slice(h * X_HEAD_DIM, (h + 1) * X_HEAD_DIM)
        qh = _rms(q[:, sl], qg_ref[...]).astype(bf16)
        s = _dot_nt(qh, k_ref[0, :, sl])
        m = jnp.max(s, axis=-1, keepdims=True)
        p = jnp.exp(s - m)
        l = jnp.sum(p, axis=-1, keepdims=True)
        heads.append((_dot(p.astype(bf16), v_ref[0, :, sl]) / l).astype(bf16))
    o = jnp.concatenate(heads, axis=-1)
    x2 = x1 + _dot(o, wo_ref[...])
    x2_ref[0] = x2

    hf = _rms(x2, gf_ref[...])
    hf_hi = hf.astype(bf16)
    hf_ref[0] = hf_hi
    hf_lo = (hf - hf_hi.astype(f32)).astype(bf16)
    logits = (_dot(hf_hi, wrh_ref[...]) + _dot(hf_lo, wrh_ref[...]) + _dot(hf_hi, wrl_ref[...])
              + br_ref[...])
    lane = lax.broadcasted_iota(jnp.int32, logits.shape, 1)
    vals, idxs = [], []
    for _ in range(TOP_K):
        m = jnp.max(logits, axis=-1, keepdims=True)
        i = jnp.min(jnp.where(logits == m, lane, LANES), axis=-1, keepdims=True)
        logits = jnp.where(lane == i, -jnp.inf, logits)
        vals.append(m)
        idxs.append(i)
    es = [jnp.exp(v - vals[0]) for v in vals]
    tot = es[0] + es[1] + es[2] + es[3]
    idx_out = jnp.zeros(logits.shape, jnp.int32)
    gate_out = jnp.zeros(logits.shape, f32)
    sel = jnp.zeros(logits.shape, f32)
    for kk in range(TOP_K):
        idx_out = jnp.where(lane == kk, idxs[kk], idx_out)
        gate_out = jnp.where(lane == kk, es[kk] / tot, gate_out)
        sel = jnp.where(lane == idxs[kk], 1.0, sel)
    idx_ref[0] = idx_out
    gate_ref[0] = gate_out
    for j in range(cnt_ref.shape[0]):
        cnt_ref[j] = jnp.sum(sel[j * GROUP:(j + 1) * GROUP], axis=0, keepdims=True)


def _xattn_router(x1, gx, w_xq, qg, kx, vx, w_xo, gf, wr_hi, wr_lo, br):
    B, S, D = x1.shape
    ts = SEQ_TILE
    full = lambda shape: pl.BlockSpec(shape, lambda b, j: (0,) * len(shape))
    tile = lambda w: pl.BlockSpec((1, ts, w), lambda b, j: (b, j, 0))
    mem = pl.BlockSpec((1, N_MEM, D_X), lambda b, j: (b, 0, 0))
    return pl.pallas_call(
        _xattn_router_kernel,
        grid=(B, S // ts),
        in_specs=[tile(D), full((1, D)), full(w_xq.shape), full((1, X_HEAD_DIM)), mem, mem,
                  full(w_xo.shape), full((1, D)), full(wr_hi.shape), full(wr_lo.shape),
                  full((1, LANES))],
        out_specs=[tile(D), tile(D), tile(LANES), tile(LANES),
                   pl.BlockSpec((ts // GROUP, 1, LANES), lambda b, j: (b * (S // ts) + j, 0, 0))],
        out_shape=[jax.ShapeDtypeStruct((B, S, D), f32), jax.ShapeDtypeStruct((B, S, D), bf16),
                   jax.ShapeDtypeStruct((B, S, LANES), jnp.int32),
                   jax.ShapeDtypeStruct((B, S, LANES), f32),
                   jax.ShapeDtypeStruct((B * S // GROUP, 1, LANES), f32)],
        compiler_params=pltpu.CompilerParams(
            dimension_semantics=("parallel", "parallel"), vmem_limit_bytes=VMEM_LIMIT),
        name="xattn_router",
    )(x1, gx, w_xq, qg, kx, vx, w_xo, gf, wr_hi, wr_lo, br)


def _for_cells(g, fn):
    def body(e, carry):
        fn(g * N_EXPERTS + e)
        return carry
    lax.fori_loop(0, N_EXPERTS, body, 0)


def _group_positions(idx_ref, coffv_ref, ltri_ref):
    idxv = idx_ref[...]
    lane = lax.broadcasted_iota(jnp.int32, idxv.shape, 1)
    hots = [lane == idxv[:, kk:kk + 1] for kk in range(TOP_K)]
    multi = jnp.zeros(idxv.shape, f32)
    for h in hots:
        multi = jnp.where(h, 1.0, multi)
    rank = _dot(ltri_ref[...], multi.astype(bf16))
    posmat = coffv_ref[0] + rank
    return [jnp.sum(jnp.where(h, posmat, 0.0), axis=-1, keepdims=True) for h in hots]


def _dispatch_kernel(coff_s, rows_s, gbase_s, toff_s, tn_s, nu_s,
                     hf_ref, idx_ref, coffv_ref, ltri_ref,
                     xs_hbm, pos_ref, xbuf, zbuf, sem, zsem, *, n_groups):
    g = pl.program_id(0)
    slot = lax.rem(g, 2)

    def cell_copy(sl, i):
        n = pl.multiple_of(rows_s[i], SUBLANES)
        src = pl.multiple_of(coff_s[i], SUBLANES)
        dst = pl.multiple_of(gbase_s[i], SUBLANES)
        return n, pltpu.make_async_copy(xbuf.at[sl, pl.ds(src, n)], xs_hbm.at[pl.ds(dst, n)],
                                        sem.at[sl])

    def start_cell(sl, i):
        n, cp = cell_copy(sl, i)
        pl.when(n > 0)(cp.start)

    def wait_cell(sl, i):
        n, cp = cell_copy(sl, i)
        pl.when(n > 0)(cp.wait)

    @pl.when(g >= 2)
    def _():
        _for_cells(g - 2, functools.partial(wait_cell, slot))

    pos = _group_positions(idx_ref, coffv_ref, ltri_ref)
    lane = lax.broadcasted_iota(jnp.int32, (GROUP, LANES), 1)
    riota = lax.broadcasted_iota(jnp.int32, (GROUP, GROUP_ROWS), 1)
    pt = jnp.zeros((GROUP, GROUP_ROWS), f32)
    pos_out = jnp.zeros((GROUP, LANES), f32)
    for kk in range(TOP_K):
        pt = jnp.where(riota == pos[kk].astype(jnp.int32), 1.0, pt)
        pos_out = jnp.where(lane == kk, pos[kk], pos_out)
    pos_ref[...] = pos_out
    xbuf[slot] = lax.dot_general(pt.astype(bf16), hf_ref[...], (((0,), (0,)), ((), ())),
                                 preferred_element_type=f32)
    _for_cells(g, functools.partial(start_cell, slot))

    @pl.when(g == n_groups - 1)
    def _():
        if n_groups >= 2:
            _for_cells(g - 1, functools.partial(wait_cell, 1 - slot))
        _for_cells(g, functools.partial(wait_cell, slot))
        zbuf[...] = jnp.zeros(zbuf.shape, f32)

        def tail_copy(e):
            n = pl.multiple_of(tn_s[e], SUBLANES)
            dst = pl.multiple_of(toff_s[e], SUBLANES)
            return n, pltpu.make_async_copy(zbuf.at[pl.ds(0, n)], xs_hbm.at[pl.ds(dst, n)], zsem)

        def tail(e, carry):
            n, cp = tail_copy(e)

            @pl.when(n > 0)
            def _():
                cp.start()
                cp.wait()
            return carry
        lax.fori_loop(0, N_EXPERTS, tail, 0)

        def unused(b, carry):
            dst = pl.multiple_of(b * EXPERT_TILE, EXPERT_TILE)
            cp = pltpu.make_async_copy(zbuf, xs_hbm.at[pl.ds(dst, EXPERT_TILE)], zsem)
            cp.start()
            cp.wait()
            return carry
        lax.fori_loop(nu_s[0], xs_hbm.shape[0] // EXPERT_TILE, unused, 0)


def _dispatch(plan, hf, idx, n_rows):
    n_tok, D = hf.shape
    n_groups = n_tok // GROUP
    ltri = jnp.asarray(np.tril(np.ones((GROUP, GROUP)), -1), bf16)
    grid_spec = pltpu.PrefetchScalarGridSpec(
        num_scalar_prefetch=6,
        grid=(n_groups,),
        in_specs=[pl.BlockSpec((GROUP, D), lambda g, *_: (g, 0)),
                  pl.BlockSpec((GROUP, LANES), lambda g, *_: (g, 0)),
                  pl.BlockSpec((1, 1, LANES), lambda g, *_: (g, 0, 0)),
                  pl.BlockSpec((GROUP, GROUP), lambda g, *_: (0, 0))],
        out_specs=[pl.BlockSpec(memory_space=pl.ANY),
                   pl.BlockSpec((GROUP, LANES), lambda g, *_: (g, 0))],
        scratch_shapes=[pltpu.VMEM((2, GROUP_ROWS, D), f32), pltpu.VMEM((EXPERT_TILE, D), f32),
                        pltpu.SemaphoreType.DMA((2,)), pltpu.SemaphoreType.DMA(())],
    )
    return pl.pallas_call(
        functools.partial(_dispatch_kernel, n_groups=n_groups),
        grid_spec=grid_spec,
        out_shape=[jax.ShapeDtypeStruct((n_rows, D), f32),
                   jax.ShapeDtypeStruct((n_tok, LANES), f32)],
        compiler_params=pltpu.CompilerParams(
            dimension_semantics=("arbitrary",), vmem_limit_bytes=VMEM_LIMIT),
        name="dispatch",
    )(plan["coff"], plan["rows"], plan["gbase"], plan["tailoff"], plan["tailn"], plan["n_used"],
      hf, idx, plan["coffv"], ltri)


def _expert_kernel(be_ref, nu_ref, x_ref, wgu_ref, bgu_ref, wdn_ref, bdn_ref, pm_ref, y_ref,
                   wgu_b, wdn_b):
    b = pl.program_id(0)

    @pl.when(b < nu_ref[0])
    def _():
        @pl.when((b == 0) | (be_ref[b] != be_ref[jnp.maximum(b - 1, 0)]))
        def _():
            for c in range(2 * D_FF // GU_TILE):
                cols = slice(c * GU_TILE, (c + 1) * GU_TILE)
                wgu_b[:, cols] = _dot(wgu_ref[0, :, cols].astype(bf16), pm_ref[...]).astype(bf16)
            for c in range(D_FF // LANES):
                rows = slice(c * LANES, (c + 1) * LANES)
                wdn_b[rows, :] = wdn_ref[0, rows, :].astype(bf16)

        _expert_block(x_ref, bgu_ref, bdn_ref, y_ref, wgu_b, wdn_b)

    @pl.when(b >= nu_ref[0])
    def _():
        y_ref[...] = jnp.zeros(y_ref.shape, f32)


def _expert_block(x_ref, bgu_ref, bdn_ref, y_ref, wgu_b, wdn_b):
    gu = _dot(x_ref[...].astype(bf16), wgu_b[...]) + bgu_ref[0]
    acts = []
    for c in range(2 * D_FF // GU_TILE):
        g = jnp.minimum(gu[:, c * GU_TILE:c * GU_TILE + LANES], SWIGLU_LIMIT)
        lin = jnp.clip(gu[:, c * GU_TILE + LANES:(c + 1) * GU_TILE], -SWIGLU_LIMIT, SWIGLU_LIMIT)
        acts.append((g * jax.nn.sigmoid(SWIGLU_ALPHA * g) * (lin + 1.0)).astype(bf16))
    act = jnp.concatenate(acts, axis=-1)
    y_ref[...] = _dot(act, wdn_b[...]) + bdn_ref[0]


def _gate_up_perm():
    j = np.arange(GU_TILE)
    p = np.zeros((GU_TILE, GU_TILE), np.float32)
    p[np.where(j < LANES, 2 * j, 2 * (j - LANES) + 1), j] = 1.0
    return jnp.asarray(p, bf16)


def _expert_mlp(plan, xs, w_gu, b_gu, w_dn, b_dn):
    n_rows, D = xs.shape
    tm = EXPERT_TILE
    row_blk = lambda b, be, nu: (jnp.minimum(b, nu[0] - 1), 0)
    per_e = lambda b, be, nu: (be[b], 0, 0)
    grid_spec = pltpu.PrefetchScalarGridSpec(
        num_scalar_prefetch=2,
        grid=(n_rows // tm,),
        in_specs=[pl.BlockSpec((tm, D), row_blk),
                  pl.BlockSpec((1, D, 2 * D_FF), per_e),
                  pl.BlockSpec((1, 1, 2 * D_FF), per_e),
                  pl.BlockSpec((1, D_FF, D), per_e),
                  pl.BlockSpec((1, 1, D), per_e),
                  pl.BlockSpec((GU_TILE, GU_TILE), lambda b, be, nu: (0, 0))],
        out_specs=pl.BlockSpec((tm, D), lambda b, be, nu: (b, 0)),
        scratch_shapes=[pltpu.VMEM((D, 2 * D_FF), bf16), pltpu.VMEM((D_FF, D), bf16)],
    )
    return pl.pallas_call(
        _expert_kernel,
        grid_spec=grid_spec,
        out_shape=jax.ShapeDtypeStruct((n_rows, D), f32),
        compiler_params=pltpu.CompilerParams(
            dimension_semantics=("arbitrary",), vmem_limit_bytes=VMEM_LIMIT),
        name="expert_mlp",
    )(plan["block_e"], plan["n_used"], xs, w_gu, b_gu, w_dn, b_dn, _gate_up_perm())


def _combine_kernel(coff_s, rows_s, gbase_s, x_ref, pos_ref, gate_ref, y_hbm, o_ref, ybuf, sem,
                    *, n_groups):
    g = pl.program_id(0)
    slot = lax.rem(g, 2)

    def cell_copy(sl, i):
        n = pl.multiple_of(rows_s[i], SUBLANES)
        loc = pl.multiple_of(coff_s[i], SUBLANES)
        src = pl.multiple_of(gbase_s[i], SUBLANES)
        return n, pltpu.make_async_copy(y_hbm.at[pl.ds(src, n)], ybuf.at[sl, pl.ds(loc, n)],
                                        sem.at[sl])

    def start_cell(sl, i):
        n, cp = cell_copy(sl, i)
        pl.when(n > 0)(cp.start)

    def wait_cell(sl, i):
        n, cp = cell_copy(sl, i)
        pl.when(n > 0)(cp.wait)

    @pl.when(g == 0)
    def _():
        ybuf[...] = jnp.zeros(ybuf.shape, f32)
        _for_cells(g, functools.partial(start_cell, slot))

    @pl.when(g + 1 < n_groups)
    def _():
        _for_cells(g + 1, functools.partial(start_cell, 1 - slot))

    _for_cells(g, functools.partial(wait_cell, slot))

    riota = lax.broadcasted_iota(jnp.int32, (GROUP, GROUP_ROWS), 1)
    pt = jnp.zeros((GROUP, GROUP_ROWS), f32)
    for kk in range(TOP_K):
        pk = pos_ref[:, kk:kk + 1].astype(jnp.int32)
        pt = jnp.where(riota == pk, gate_ref[:, kk:kk + 1], pt)
    o_ref[...] = x_ref[...] + _dot(pt.astype(bf16), ybuf[slot].astype(bf16))


def _combine(plan, x2, pos, gates, y_rows):
    n_tok, D = x2.shape
    n_groups = n_tok // GROUP
    tok = lambda w: pl.BlockSpec((GROUP, w), lambda g, *_: (g, 0))
    grid_spec = pltpu.PrefetchScalarGridSpec(
        num_scalar_prefetch=3,
        grid=(n_groups,),
        in_specs=[tok(D), tok(LANES), tok(LANES), pl.BlockSpec(memory_space=pl.ANY)],
        out_specs=tok(D),
        scratch_shapes=[pltpu.VMEM((2, GROUP_ROWS, D), f32), pltpu.SemaphoreType.DMA((2,))],
    )
    return pl.pallas_call(
        functools.partial(_combine_kernel, n_groups=n_groups),
        grid_spec=grid_spec,
        out_shape=jax.ShapeDtypeStruct((n_tok, D), f32),
        compiler_params=pltpu.CompilerParams(
            dimension_semantics=("arbitrary",), vmem_limit_bytes=VMEM_LIMIT),
        name="combine",
    )(plan["coff"], plan["rows"], plan["gbase"], x2, pos, gates, y_rows)


def _routing_plan(cnt, n_tok):
    tm = EXPERT_TILE
    n_groups = n_tok // GROUP
    cnt = cnt.reshape(n_groups, LANES)[:, :N_EXPERTS].astype(jnp.int32)
    rows = (cnt + SUBLANES - 1) // SUBLANES * SUBLANES
    coff = jnp.cumsum(rows, axis=1) - rows
    tot = jnp.sum(rows, axis=0)
    padded = (tot + tm - 1) // tm * tm
    pend = jnp.cumsum(padded)
    ebase = pend - padded
    gbase = ebase[None, :] + jnp.cumsum(rows, axis=0) - rows
    n_blocks = _max_rows(n_tok) // tm
    n_used = pend[-1] // tm
    blk = jnp.arange(n_blocks, dtype=jnp.int32)
    first_row = jnp.minimum(blk, n_used - 1) * tm
    block_e = jnp.sum((pend[None, :] <= first_row[:, None]).astype(jnp.int32), axis=1)
    flat = lambda a: a.reshape(-1).astype(jnp.int32)
    return dict(
        coff=flat(coff), rows=flat(rows), gbase=flat(gbase),
        tailoff=flat(ebase + tot), tailn=flat(padded - tot),
        block_e=flat(jnp.minimum(block_e, N_EXPERTS - 1)), n_used=flat(n_used),
        coffv=jnp.zeros((n_groups, 1, LANES), f32).at[:, 0, :N_EXPERTS].set(coff.astype(f32)))


def _max_rows(n_tok):
    tm = EXPERT_TILE
    worst = (n_tok * TOP_K + (n_tok // GROUP) * N_EXPERTS * (SUBLANES - 1)
             + N_EXPERTS * (tm - SUBLANES))
    return (worst + tm - 1) // tm * tm


def _moe(x2, hf, idx, gates, cnt, w_gate_up, b_gate_up, w_down, b_down):
    n_tok, D = x2.shape
    plan = _routing_plan(cnt, n_tok)
    xs, pos = _dispatch(plan, hf, idx, _max_rows(n_tok))
    b_gu = (b_gate_up.reshape(N_EXPERTS, 2 * D_FF // GU_TILE, LANES, 2).transpose(0, 1, 3, 2)
            .reshape(N_EXPERTS, 1, 2 * D_FF))
    y_rows = _expert_mlp(plan, xs, w_gate_up, b_gu, w_down, b_down.reshape(N_EXPERTS, 1, D))
    return _combine(plan, x2, pos, gates, y_rows)


BIAS_ROW = (2 * Q_TILE + LEFT_CHUNKS * CHUNK + LANES - 2) // LANES * LANES


def _band_bias_kernel(r_ref, o_ref):
    x = jnp.broadcast_to(r_ref[0], (Q_TILE, BIAS_ROW))
    x = pltpu.roll(x, BIAS_ROW - (Q_TILE - 1), axis=1, stride=1, stride_axis=0)
    qc = lax.broadcasted_iota(jnp.int32, (Q_TILE, K_BAND), 0) // CHUNK
    kc = lax.broadcasted_iota(jnp.int32, (Q_TILE, K_BAND), 1) // CHUNK
    first_chunk = LEFT_CHUNKS - pl.program_id(0) * (Q_TILE // CHUNK)
    visible = (kc >= qc) & (kc <= qc + LEFT_CHUNKS) & (kc >= first_chunk)
    o_ref[0, 0] = jnp.where(visible, x[:, :K_BAND] * LOG2E, NEG_INF)


def _band_bias(rel_bias):
    H = rel_bias.shape[0]
    d_max = Q_TILE - 1 + LEFT_CHUNKS * CHUNK
    n_far = d_max - MAX_REL + 1
    r = jnp.concatenate(
        [jnp.broadcast_to(rel_bias[:, 2 * MAX_REL:], (H, n_far)),
         rel_bias[:, MAX_REL - (Q_TILE - 1):2 * MAX_REL][:, ::-1],
         jnp.zeros((H, BIAS_ROW - (2 * Q_TILE - 1 + LEFT_CHUNKS * CHUNK)), rel_bias.dtype)], axis=1)
    return pl.pallas_call(
        _band_bias_kernel,
        grid=(START_TILES + 1, H),
        in_specs=[pl.BlockSpec((1, 1, BIAS_ROW), lambda t, h: (h, 0, 0))],
        out_specs=pl.BlockSpec((1, 1, Q_TILE, K_BAND), lambda t, h: (t, h, 0, 0)),
        out_shape=jax.ShapeDtypeStruct((START_TILES + 1, H, Q_TILE, K_BAND), f32),
        name="band_bias",
    )(r.astype(f32).reshape(H, 1, BIAS_ROW))


def _layer(x, mem, norm_mix_g, w_in, q_norm_g, k_norm_g, rel_bias, conv_w, out_norm_a_g,
           out_norm_b_g, w_out, norm_xattn_g, norm_mem_g, w_xq, w_xkv, xq_norm_g, xk_norm_g,
           w_xo, norm_ffn_g, w_router, b_router, w_gate_up, b_gate_up, w_down, b_down):
    B, S, D = x.shape
    n_tok = B * S
    row = lambda a: a.reshape(1, -1).astype(f32)

    qg = row(jnp.tile(q_norm_g, ATT_HEADS) * (ATT_HEAD_DIM ** -0.5 * LOG2E))
    kg = row(jnp.tile(k_norm_g, ATT_HEADS))
    head_ones = jnp.asarray(np.kron(np.eye(ATT_HEADS), np.ones((ATT_HEAD_DIM, ATT_HEAD_DIM))), bf16)
    q, k, v, yb = _mix_in(x, row(norm_mix_g), w_in.astype(bf16), qg, kg, head_ones,
                          conv_w.astype(f32), row(out_norm_b_g))
    x1 = _attn_out(q, k, v, _band_bias(rel_bias), yb, x, w_out.astype(bf16), row(out_norm_a_g))

    kx, vx = _mem_kv(mem, row(norm_mem_g), w_xkv.astype(bf16), row(xk_norm_g))
    wr = jnp.zeros((D, LANES), f32).at[:, :N_EXPERTS].set(w_router)
    wr_hi = wr.astype(bf16)
    wr_lo = (wr - wr_hi.astype(f32)).astype(bf16)
    br = jnp.full((1, LANES), -jnp.inf, f32).at[0, :N_EXPERTS].set(b_router)
    x2, hf, idx, gates, cnt = _xattn_router(
        x1, row(norm_xattn_g), w_xq.astype(bf16), row(xq_norm_g * (X_HEAD_DIM ** -0.5)), kx, vx,
        w_xo.astype(bf16), row(norm_ffn_g), wr_hi, wr_lo, br)

    out = _moe(x2.reshape(n_tok, D), hf.reshape(n_tok, D), idx.reshape(n_tok, LANES),
               gates.reshape(n_tok, LANES), cnt, w_gate_up, b_gate_up, w_down, b_down)
    return out.reshape(B, S, D)


def kernel(x, mem, norm_mix_g, w_in, q_norm_g, k_norm_g, rel_bias, conv_w, out_norm_a_g,
           out_norm_b_g, w_out, norm_xattn_g, norm_mem_g, w_xq, w_xkv, xq_norm_g, xk_norm_g,
           w_xo, norm_ffn_g, w_router, b_router, w_gate_up, b_gate_up, w_down, b_down):
    depth = norm_mix_g.shape[0]
    for l in range(depth):
        x = _layer(x, mem, norm_mix_g[l], w_in[l], q_norm_g[l], k_norm_g[l], rel_bias[l],
                   conv_w[l], out_norm_a_g[l], out_norm_b_g[l], w_out[l], norm_xattn_g[l],
                   norm_mem_g[l], w_xq[l], w_xkv[l], xq_norm_g[l], xk_norm_g[l], w_xo[l],
                   norm_ffn_g[l], w_router[l], b_router[l], w_gate_up[l], b_gate_up[l],
                   w_down[l], b_down[l])
    return x
```

```python
import functools

import jax
import jax.numpy as jnp
import numpy as np
from jax import lax
from jax.experimental import pallas as pl
from jax.experimental.pallas import tpu as pltpu

D_MODEL = 1024
CHUNK = 64
LEFT_CHUNKS = 8
N_MEM = 256
ATT_HEADS = 8
ATT_HEAD_DIM = 64
D_ATT = ATT_HEADS * ATT_HEAD_DIM
D_CONV = 512
MAX_REL = 256
X_HEADS = 4
X_HEAD_DIM = 128
D_X = X_HEADS * X_HEAD_DIM
N_EXPERTS = 32
TOP_K = 4
D_FF = D_MODEL
SWIGLU_LIMIT = 7.0
SWIGLU_ALPHA = 1.702
EPS = 1e-6
NEG_INF = -1e30

LANES = 128
SEQ_TILE = 512
Q_TILE = 256
K_BAND = Q_TILE + LEFT_CHUNKS * CHUNK
START_TILES = LEFT_CHUNKS * CHUNK // Q_TILE
LOG2E = 1.4426950408889634
HEAD_ROUND = 4
EXPERT_TILE = 512
GU_TILE = 2 * LANES
GROUP = 256
SUBLANES = 8
GROUP_ROWS = 1280
VMEM_LIMIT = 56 * 1024 * 1024

bf16 = jnp.bfloat16
f32 = jnp.float32


def _rms(x, g):
    return x * lax.rsqrt(jnp.mean(x * x, axis=-1, keepdims=True) + EPS) * g


def _dot(a, b):
    return jnp.dot(a, b, preferred_element_type=f32)


def _dot_nt(a, b):
    return lax.dot_general(a, b, (((1,), (1,)), ((), ())), preferred_element_type=f32)


def _mix_in_kernel(x_ref, g_ref, w_ref, qg_ref, kg_ref, hm_ref, cw_ref, gb_ref,
                   q_ref, k_ref, v_ref, yb_ref, ub_ref):
    ts = x_ref.shape[1]

    @pl.when(pl.program_id(1) == 0)
    def _():
        ub_ref[0:8, :] = jnp.zeros((8, D_CONV), f32)

    hb = _rms(x_ref[0], g_ref[...]).astype(bf16)

    def head_norm(t, gain):
        ms = _dot((t * t).astype(bf16), hm_ref[...]) * (1.0 / ATT_HEAD_DIM)
        return t * lax.rsqrt(ms + EPS) * gain

    q = _dot(hb, w_ref[:, 0:D_ATT])
    q_ref[0] = head_norm(q, qg_ref[...]).astype(bf16)
    k = _dot(hb, w_ref[:, D_ATT:2 * D_ATT])
    k_ref[0] = head_norm(k, kg_ref[...]).astype(bf16)
    v_ref[0] = _dot(hb, w_ref[:, 2 * D_ATT:3 * D_ATT]).astype(bf16)

    o = 3 * D_ATT
    bg = _dot(hb, w_ref[:, o:o + D_CONV])
    cg = _dot(hb, w_ref[:, o + D_CONV:o + 2 * D_CONV])
    xv = _dot(hb, w_ref[:, o + 2 * D_CONV:o + 3 * D_CONV])
    u = cg * xv
    ub_ref[8:8 + ts, :] = u
    conv = (cw_ref[0:1, :] * ub_ref[6:6 + ts, :] + cw_ref[1:2, :] * ub_ref[7:7 + ts, :]
            + cw_ref[2:3, :] * u)
    ub_ref[0:8, :] = u[ts - 8:ts, :]
    yb_ref[0] = _rms(bg * conv, gb_ref[...]).astype(bf16)


def _mix_in(x, g, w_in, qg, kg, hm, cw, gb):
    B, S, D = x.shape
    ts = SEQ_TILE
    full = lambda shape: pl.BlockSpec(shape, lambda b, j: (0,) * len(shape))
    tile = lambda w: pl.BlockSpec((1, ts, w), lambda b, j: (b, j, 0))
    out = jax.ShapeDtypeStruct((B, S, D_ATT), bf16)
    return pl.pallas_call(
        _mix_in_kernel,
        grid=(B, S // ts),
        in_specs=[tile(D), full((1, D)), full(w_in.shape), full((1, D_ATT)), full((1, D_ATT)),
                  full((D_ATT, D_ATT)), full((3, D_CONV)), full((1, D_CONV))],
        out_specs=[tile(D_ATT)] * 4,
        out_shape=[out] * 4,
        scratch_shapes=[pltpu.VMEM((ts + 8, D_CONV), f32)],
        compiler_params=pltpu.CompilerParams(
            dimension_semantics=("arbitrary", "arbitrary"), vmem_limit_bytes=VMEM_LIMIT),
        name="mix_in",
    )(x, g, w_in, qg, kg, hm, cw, gb)


def _attn_out_kernel(q_ref, k_ref, v_ref, bias_ref, yb_ref, x_ref, w_ref, ga_ref,
                     o_ref, kb_ref, vb_ref):
    tq = q_ref.shape[1]
    n_kc = K_BAND // CHUNK
    c0 = pl.program_id(1) * (tq // CHUNK)
    for kc in range(n_kc):
        src = pl.multiple_of(jnp.maximum(c0 - LEFT_CHUNKS + kc, 0) * CHUNK, CHUNK)
        kb_ref[kc * CHUNK:(kc + 1) * CHUNK, :] = k_ref[0, pl.ds(src, CHUNK), :]
        vb_ref[kc * CHUNK:(kc + 1) * CHUNK, :] = v_ref[0, pl.ds(src, CHUNK), :]

    lane = lax.broadcasted_iota(jnp.int32, (1, LANES), 1)
    low = lane < ATT_HEAD_DIM

    outs = []
    for h0 in range(0, ATT_HEADS, HEAD_ROUND):
        heads = range(h0, h0 + HEAD_ROUND)
        scores, probs, denoms = {}, {}, {}
        for h in heads:
            sl = slice((h // 2) * LANES, (h // 2 + 1) * LANES)
            qp = q_ref[0, :, sl]
            keep = low if h % 2 == 0 else jnp.logical_not(low)
            scores[h] = _dot_nt(jnp.where(keep, qp, jnp.zeros_like(qp)), kb_ref[:, sl])
        for h in heads:
            s = scores[h] + bias_ref[0, h]
            p = jnp.exp2(s - jnp.max(s, axis=-1, keepdims=True))
            denoms[h] = jnp.sum(p, axis=-1, keepdims=True)
            probs[h] = p.astype(bf16)
        for h in heads:
            sl = slice((h // 2) * LANES, (h // 2 + 1) * LANES)
            outs.append(_dot(probs[h], vb_ref[:, sl]) / denoms[h])
    ya = jnp.concatenate([jnp.where(low, outs[2 * hp], outs[2 * hp + 1])
                          for hp in range(ATT_HEADS // 2)], axis=-1)
    yan = _rms(ya, ga_ref[...]).astype(bf16)
    o_ref[0] = (x_ref[0] + _dot(yan, w_ref[0:D_ATT, :]) + _dot(yb_ref[0], w_ref[D_ATT:, :]))


def _attn_out(q, k, v, bias, yb, x, w_out, ga):
    B, S, D = x.shape
    tq = Q_TILE
    full = lambda shape: pl.BlockSpec(shape, lambda b, j: (0,) * len(shape))
    tile = lambda w: pl.BlockSpec((1, tq, w), lambda b, j: (b, j, 0))
    seq = pl.BlockSpec((1, S, D_ATT), lambda b, j: (b, 0, 0))
    return pl.pallas_call(
        _attn_out_kernel,
        grid=(B, S // tq),
        in_specs=[tile(D_ATT), seq, seq,
                  pl.BlockSpec((1,) + bias.shape[1:], lambda b, j: (jnp.minimum(j, START_TILES), 0, 0, 0)),
                  tile(D_CONV), tile(D),
                  full(w_out.shape), full((1, D_ATT))],
        out_specs=tile(D),
        out_shape=jax.ShapeDtypeStruct((B, S, D), f32),
        scratch_shapes=[pltpu.VMEM((K_BAND, D_ATT), bf16), pltpu.VMEM((K_BAND, D_ATT), bf16)],
        compiler_params=pltpu.CompilerParams(
            dimension_semantics=("parallel", "parallel"), vmem_limit_bytes=VMEM_LIMIT),
        name="attn_out",
    )(q, k, v, bias, yb, x, w_out, ga)


def _mem_kv_kernel(m_ref, g_ref, w_ref, kg_ref, k_ref, v_ref):
    hm = _rms(m_ref[0], g_ref[...]).astype(bf16)
    kv = _dot(hm, w_ref[...])
    for h in range(X_HEADS):
        sl = slice(h * X_HEAD_DIM, (h + 1) * X_HEAD_DIM)
        k_ref[0, :, sl] = _rms(kv[:, sl], kg_ref[...]).astype(bf16)
    v_ref[0] = kv[:, D_X:].astype(bf16)


def _mem_kv(mem, g, w_xkv, kg):
    B = mem.shape[0]
    full = lambda shape: pl.BlockSpec(shape, lambda b: (0,) * len(shape))
    out = jax.ShapeDtypeStruct((B, N_MEM, D_X), bf16)
    return pl.pallas_call(
        _mem_kv_kernel,
        grid=(B,),
        in_specs=[pl.BlockSpec((1, N_MEM, D_MODEL), lambda b: (b, 0, 0)), full((1, D_MODEL)),
                  full(w_xkv.shape), full((1, X_HEAD_DIM))],
        out_specs=[pl.BlockSpec((1, N_MEM, D_X), lambda b: (b, 0, 0))] * 2,
        out_shape=[out, out],
        compiler_params=pltpu.CompilerParams(
            dimension_semantics=("parallel",), vmem_limit_bytes=VMEM_LIMIT),
        name="mem_kv",
    )(mem, g, w_xkv, kg)


def _xattn_router_kernel(x_ref, gx_ref, wq_ref, qg_ref, k_ref, v_ref, wo_ref, gf_ref,
                         wrh_ref, wrl_ref, br_ref,
                         x2_ref, hf_ref, idx_ref, gate_ref, cnt_ref):
    x1 = x_ref[0]
    hb = _rms(x1, gx_ref[...]).astype(bf16)
    q = _dot(hb, wq_ref[...])
    heads = []
    for h in range(X_HEADS):
        sl = slice(h * X_HEAD_DIM, (h + 1) * X_HEAD_DIM)
        qh = _rms(q[:, sl], qg_ref[...]).astype(bf16)
        s = _dot_nt(qh, k_ref[0, :, sl])
        m = jnp.max(s, axis=-1, keepdims=True)
        p = jnp.exp(s - m)
        l = jnp.sum(p, axis=-1, keepdims=True)
        heads.append((_dot(p.astype(bf16), v_ref[0, :, sl]) / l).astype(bf16))
    o = jnp.concatenate(heads, axis=-1)
    x2 = x1 + _dot(o, wo_ref[...])
    x2_ref[0] = x2

    hf = _rms(x2, gf_ref[...])
    hf_hi = hf.astype(bf16)
    hf_ref[0] = hf_hi
    hf_lo = (hf - hf_hi.astype(f32)).astype(bf16)
    logits = (_dot(hf_hi, wrh_ref[...]) + _dot(hf_lo, wrh_ref[...]) + _dot(hf_hi, wrl_ref[...])
              + br_ref[...])
    lane = lax.broadcasted_iota(jnp.int32, logits.shape, 1)
    vals, idxs = [], []
    for _ in range(TOP_K):
        m = jnp.max(logits, axis=-1, keepdims=True)
        i = jnp.min(jnp.where(logits == m, lane, LANES), axis=-1, keepdims=True)
        logits = jnp.where(lane == i, -jnp.inf, logits)
        vals.append(m)
        idxs.append(i)
    es = [jnp.exp(v - vals[0]) for v in vals]
    tot = es[0] + es[1] + es[2] + es[3]
    idx_out = jnp.zeros(logits.shape, jnp.int32)
    gate_out = jnp.zeros(logits.shape, f32)
    sel = jnp.zeros(logits.shape, f32)
    for kk in range(TOP_K):
        idx_out = jnp.where(lane == kk, idxs[kk], idx_out)
        gate_out = jnp.where(lane == kk, es[kk] / tot, gate_out)
        sel = jnp.where(lane == idxs[kk], 1.0, sel)
    idx_ref[0] = idx_out
    gate_ref[0] = gate_out
    for j in range(cnt_ref.shape[0]):
        cnt_ref[j] = jnp.sum(sel[j * GROUP:(j + 1) * GROUP], axis=0, keepdims=True)


def _xattn_router(x1, gx, w_xq, qg, kx, vx, w_xo, gf, wr_hi, wr_lo, br):
    B, S, D = x1.shape
    ts = SEQ_TILE
    full = lambda shape: pl.BlockSpec(shape, lambda b, j: (0,) * len(shape))
    tile = lambda w: pl.BlockSpec((1, ts, w), lambda b, j: (b, j, 0))
    mem = pl.BlockSpec((1, N_MEM, D_X), lambda b, j: (b, 0, 0))
    return pl.pallas_call(
        _xattn_router_kernel,
        grid=(B, S // ts),
        in_specs=[tile(D), full((1, D)), full(w_xq.shape), full((1, X_HEAD_DIM)), mem, mem,
                  full(w_xo.shape), full((1, D)), full(wr_hi.shape), full(wr_lo.shape),
                  full((1, LANES))],
        out_specs=[tile(D), tile(D), tile(LANES), tile(LANES),
                   pl.BlockSpec((ts // GROUP, 1, LANES), lambda b, j: (b * (S // ts) + j, 0, 0))],
        out_shape=[jax.ShapeDtypeStruct((B, S, D), f32), jax.ShapeDtypeStruct((B, S, D), bf16),
                   jax.ShapeDtypeStruct((B, S, LANES), jnp.int32),
                   jax.ShapeDtypeStruct((B, S, LANES), f32),
                   jax.ShapeDtypeStruct((B * S // GROUP, 1, LANES), f32)],
        compiler_params=pltpu.CompilerParams(
            dimension_semantics=("parallel", "parallel"), vmem_limit_bytes=VMEM_LIMIT),
        name="xattn_router",
    )(x1, gx, w_xq, qg, kx, vx, w_xo, gf, wr_hi, wr_lo, br)


u32 = jnp.uint32
HIGH_HALF = 0xFFFF0000


def _pack_pairs(x):
    w = x.shape[1] // 2
    lo = pltpu.bitcast(x[:, :w], u32) >> 16
    hi = pltpu.bitcast(x[:, w:], u32) & u32(HIGH_HALF)
    return lo | hi


def _unpack_pairs(u):
    lo = pltpu.bitcast(u << 16, f32)
    hi = pltpu.bitcast(u & u32(HIGH_HALF), f32)
    return jnp.concatenate([lo, hi], axis=-1).astype(bf16)


def _for_cells(g, fn):
    def body(e, carry):
        fn(g * N_EXPERTS + e)
        return carry
    lax.fori_loop(0, N_EXPERTS, body, 0)


def _group_positions(idx_ref, coffv_ref, ltri_ref):
    idxv = idx_ref[...]
    lane = lax.broadcasted_iota(jnp.int32, idxv.shape, 1)
    hots = [lane == idxv[:, kk:kk + 1] for kk in range(TOP_K)]
    multi = jnp.zeros(idxv.shape, f32)
    for h in hots:
        multi = jnp.where(h, 1.0, multi)
    rank = _dot(ltri_ref[...], multi.astype(bf16))
    posmat = coffv_ref[0] + rank
    return [jnp.sum(jnp.where(h, posmat, 0.0), axis=-1, keepdims=True) for h in hots]


def _dispatch_kernel(coff_s, rows_s, gbase_s, toff_s, tn_s, nu_s,
                     hf_ref, idx_ref, coffv_ref, ltri_ref,
                     xs_hbm, pos_ref, xbuf, zbuf, sem, zsem, *, n_groups):
    g = pl.program_id(0)
    slot = lax.rem(g, 2)

    def cell_copy(sl, i):
        n = pl.multiple_of(rows_s[i], SUBLANES)
        src = pl.multiple_of(coff_s[i], SUBLANES)
        dst = pl.multiple_of(gbase_s[i], SUBLANES)
        return n, pltpu.make_async_copy(xbuf.at[sl, pl.ds(src, n)], xs_hbm.at[pl.ds(dst, n)],
                                        sem.at[sl])

    def start_cell(sl, i):
        n, cp = cell_copy(sl, i)
        pl.when(n > 0)(cp.start)

    def wait_cell(sl, i):
        n, cp = cell_copy(sl, i)
        pl.when(n > 0)(cp.wait)

    @pl.when(g >= 2)
    def _():
        _for_cells(g - 2, functools.partial(wait_cell, slot))

    pos = _group_positions(idx_ref, coffv_ref, ltri_ref)
    lane = lax.broadcasted_iota(jnp.int32, (GROUP, LANES), 1)
    riota = lax.broadcasted_iota(jnp.int32, (GROUP, GROUP_ROWS), 1)
    pt = jnp.zeros((GROUP, GROUP_ROWS), f32)
    pos_out = jnp.zeros((GROUP, LANES), f32)
    for kk in range(TOP_K):
        pt = jnp.where(riota == pos[kk].astype(jnp.int32), 1.0, pt)
        pos_out = jnp.where(lane == kk, pos[kk], pos_out)
    pos_ref[...] = pos_out
    xbuf[slot] = _pack_pairs(lax.dot_general(pt.astype(bf16), hf_ref[...], (((0,), (0,)), ((), ())),
                                             preferred_element_type=f32))
    _for_cells(g, functools.partial(start_cell, slot))

    @pl.when(g == n_groups - 1)
    def _():
        if n_groups >= 2:
            _for_cells(g - 1, functools.partial(wait_cell, 1 - slot))
        _for_cells(g, functools.partial(wait_cell, slot))
        zbuf[...] = jnp.zeros(zbuf.shape, u32)

        def tail_copy(e):
            n = pl.multiple_of(tn_s[e], SUBLANES)
            dst = pl.multiple_of(toff_s[e], SUBLANES)
            return n, pltpu.make_async_copy(zbuf.at[pl.ds(0, n)], xs_hbm.at[pl.ds(dst, n)], zsem)

        def tail(e, carry):
            n, cp = tail_copy(e)

            @pl.when(n > 0)
            def _():
                cp.start()
                cp.wait()
            return carry
        lax.fori_loop(0, N_EXPERTS, tail, 0)

        def unused(b, carry):
            dst = pl.multiple_of(b * EXPERT_TILE, EXPERT_TILE)
            cp = pltpu.make_async_copy(zbuf, xs_hbm.at[pl.ds(dst, EXPERT_TILE)], zsem)
            cp.start()
            cp.wait()
            return carry
        lax.fori_loop(nu_s[0], xs_hbm.shape[0] // EXPERT_TILE, unused, 0)


def _dispatch(plan, hf, idx, n_rows):
    n_tok, D = hf.shape
    n_groups = n_tok // GROUP
    ltri = jnp.asarray(np.tril(np.ones((GROUP, GROUP)), -1), bf16)
    grid_spec = pltpu.PrefetchScalarGridSpec(
        num_scalar_prefetch=6,
        grid=(n_groups,),
        in_specs=[pl.BlockSpec((GROUP, D), lambda g, *_: (g, 0)),
                  pl.BlockSpec((GROUP, LANES), lambda g, *_: (g, 0)),
                  pl.BlockSpec((1, 1, LANES), lambda g, *_: (g, 0, 0)),
                  pl.BlockSpec((GROUP, GROUP), lambda g, *_: (0, 0))],
        out_specs=[pl.BlockSpec(memory_space=pl.ANY),
                   pl.BlockSpec((GROUP, LANES), lambda g, *_: (g, 0))],
        scratch_shapes=[pltpu.VMEM((2, GROUP_ROWS, D // 2), u32),
                        pltpu.VMEM((EXPERT_TILE, D // 2), u32),
                        pltpu.SemaphoreType.DMA((2,)), pltpu.SemaphoreType.DMA(())],
    )
    return pl.pallas_call(
        functools.partial(_dispatch_kernel, n_groups=n_groups),
        grid_spec=grid_spec,
        out_shape=[jax.ShapeDtypeStruct((n_rows, D // 2), u32),
                   jax.ShapeDtypeStruct((n_tok, LANES), f32)],
        compiler_params=pltpu.CompilerParams(
            dimension_semantics=("arbitrary",), vmem_limit_bytes=VMEM_LIMIT),
        name="dispatch",
    )(plan["coff"], plan["rows"], plan["gbase"], plan["tailoff"], plan["tailn"], plan["n_used"],
      hf, idx, plan["coffv"], ltri)


def _expert_kernel(be_ref, nu_ref, x_ref, wgu_ref, bgu_ref, wdn_ref, bdn_ref, pm_ref, y_ref,
                   wgu_b, wdn_b):
    b = pl.program_id(0)

    @pl.when(b < nu_ref[0])
    def _():
        @pl.when((b == 0) | (be_ref[b] != be_ref[jnp.maximum(b - 1, 0)]))
        def _():
            for c in range(2 * D_FF // GU_TILE):
                cols = slice(c * GU_TILE, (c + 1) * GU_TILE)
                wgu_b[:, cols] = _dot(wgu_ref[0, :, cols].astype(bf16), pm_ref[...]).astype(bf16)
            for c in range(D_FF // LANES):
                rows = slice(c * LANES, (c + 1) * LANES)
                wdn_b[rows, :] = wdn_ref[0, rows, :].astype(bf16)

        _expert_block(x_ref, bgu_ref, bdn_ref, y_ref, wgu_b, wdn_b)

    @pl.when(b >= nu_ref[0])
    def _():
        y_ref[...] = jnp.zeros(y_ref.shape, u32)


def _expert_block(x_ref, bgu_ref, bdn_ref, y_ref, wgu_b, wdn_b):
    gu = _dot(_unpack_pairs(x_ref[...]), wgu_b[...]) + bgu_ref[0]
    acts = []
    for c in range(2 * D_FF // GU_TILE):
        g = jnp.minimum(gu[:, c * GU_TILE:c * GU_TILE + LANES], SWIGLU_LIMIT)
        lin = jnp.clip(gu[:, c * GU_TILE + LANES:(c + 1) * GU_TILE], -SWIGLU_LIMIT, SWIGLU_LIMIT)
        acts.append((g * jax.nn.sigmoid(SWIGLU_ALPHA * g) * (lin + 1.0)).astype(bf16))
    act = jnp.concatenate(acts, axis=-1)
    y = _dot(act, wdn_b[...]) + bdn_ref[0]
    y_ref[...] = _pack_pairs(y.astype(bf16).astype(f32))


def _gate_up_perm():
    j = np.arange(GU_TILE)
    p = np.zeros((GU_TILE, GU_TILE), np.float32)
    p[np.where(j < LANES, 2 * j, 2 * (j - LANES) + 1), j] = 1.0
    return jnp.asarray(p, bf16)


def _expert_mlp(plan, xs, w_gu, b_gu, w_dn, b_dn):
    n_rows = xs.shape[0]
    D = D_MODEL
    tm = EXPERT_TILE
    row_blk = lambda b, be, nu: (jnp.minimum(b, nu[0] - 1), 0)
    per_e = lambda b, be, nu: (be[b], 0, 0)
    grid_spec = pltpu.PrefetchScalarGridSpec(
        num_scalar_prefetch=2,
        grid=(n_rows // tm,),
        in_specs=[pl.BlockSpec((tm, D // 2), row_blk),
                  pl.BlockSpec((1, D, 2 * D_FF), per_e),
                  pl.BlockSpec((1, 1, 2 * D_FF), per_e),
                  pl.BlockSpec((1, D_FF, D), per_e),
                  pl.BlockSpec((1, 1, D), per_e),
                  pl.BlockSpec((GU_TILE, GU_TILE), lambda b, be, nu: (0, 0))],
        out_specs=pl.BlockSpec((tm, D // 2), lambda b, be, nu: (b, 0)),
        scratch_shapes=[pltpu.VMEM((D, 2 * D_FF), bf16), pltpu.VMEM((D_FF, D), bf16)],
    )
    return pl.pallas_call(
        _expert_kernel,
        grid_spec=grid_spec,
        out_shape=jax.ShapeDtypeStruct((n_rows, D // 2), u32),
        compiler_params=pltpu.CompilerParams(
            dimension_semantics=("arbitrary",), vmem_limit_bytes=VMEM_LIMIT),
        name="expert_mlp",
    )(plan["block_e"], plan["n_used"], xs, w_gu, b_gu, w_dn, b_dn, _gate_up_perm())


def _combine_kernel(coff_s, rows_s, gbase_s, x_ref, pos_ref, gate_ref, y_hbm, o_ref, ybuf, sem,
                    *, n_groups):
    g = pl.program_id(0)
    slot = lax.rem(g, 2)

    def cell_copy(sl, i):
        n = pl.multiple_of(rows_s[i], SUBLANES)
        loc = pl.multiple_of(coff_s[i], SUBLANES)
        src = pl.multiple_of(gbase_s[i], SUBLANES)
        return n, pltpu.make_async_copy(y_hbm.at[pl.ds(src, n)], ybuf.at[sl, pl.ds(loc, n)],
                                        sem.at[sl])

    def start_cell(sl, i):
        n, cp = cell_copy(sl, i)
        pl.when(n > 0)(cp.start)

    def wait_cell(sl, i):
        n, cp = cell_copy(sl, i)
        pl.when(n > 0)(cp.wait)

    @pl.when(g == 0)
    def _():
        ybuf[...] = jnp.zeros(ybuf.shape, u32)
        _for_cells(g, functools.partial(start_cell, slot))

    @pl.when(g + 1 < n_groups)
    def _():
        _for_cells(g + 1, functools.partial(start_cell, 1 - slot))

    _for_cells(g, functools.partial(wait_cell, slot))

    riota = lax.broadcasted_iota(jnp.int32, (GROUP, GROUP_ROWS), 1)
    pt = jnp.zeros((GROUP, GROUP_ROWS), f32)
    for kk in range(TOP_K):
        pk = pos_ref[:, kk:kk + 1].astype(jnp.int32)
        pt = jnp.where(riota == pk, gate_ref[:, kk:kk + 1], pt)
    o_ref[...] = x_ref[...] + _dot(pt.astype(bf16), _unpack_pairs(ybuf[slot]))


def _combine(plan, x2, pos, gates, y_rows):
    n_tok, D = x2.shape
    n_groups = n_tok // GROUP
    tok = lambda w: pl.BlockSpec((GROUP, w), lambda g, *_: (g, 0))
    grid_spec = pltpu.PrefetchScalarGridSpec(
        num_scalar_prefetch=3,
        grid=(n_groups,),
        in_specs=[tok(D), tok(LANES), tok(LANES), pl.BlockSpec(memory_space=pl.ANY)],
        out_specs=tok(D),
        scratch_shapes=[pltpu.VMEM((2, GROUP_ROWS, D // 2), u32), pltpu.SemaphoreType.DMA((2,))],
    )
    return pl.pallas_call(
        functools.partial(_combine_kernel, n_groups=n_groups),
        grid_spec=grid_spec,
        out_shape=jax.ShapeDtypeStruct((n_tok, D), f32),
        compiler_params=pltpu.CompilerParams(
            dimension_semantics=("arbitrary",), vmem_limit_bytes=VMEM_LIMIT),
        name="combine",
    )(plan["coff"], plan["rows"], plan["gbase"], x2, pos, gates, y_rows)


def _routing_plan(cnt, n_tok):
    tm = EXPERT_TILE
    n_groups = n_tok // GROUP
    cnt = cnt.reshape(n_groups, LANES)[:, :N_EXPERTS].astype(jnp.int32)
    rows = (cnt + SUBLANES - 1) // SUBLANES * SUBLANES
    coff = jnp.cumsum(rows, axis=1) - rows
    tot = jnp.sum(rows, axis=0)
    padded = (tot + tm - 1) // tm * tm
    pend = jnp.cumsum(padded)
    ebase = pend - padded
    gbase = ebase[None, :] + jnp.cumsum(rows, axis=0) - rows
    n_blocks = _max_rows(n_tok) // tm
    n_used = pend[-1] // tm
    blk = jnp.arange(n_blocks, dtype=jnp.int32)
    first_row = jnp.minimum(blk, n_used - 1) * tm
    block_e = jnp.sum((pend[None, :] <= first_row[:, None]).astype(jnp.int32), axis=1)
    flat = lambda a: a.reshape(-1).astype(jnp.int32)
    return dict(
        coff=flat(coff), rows=flat(rows), gbase=flat(gbase),
        tailoff=flat(ebase + tot), tailn=flat(padded - tot),
        block_e=flat(jnp.minimum(block_e, N_EXPERTS - 1)), n_used=flat(n_used),
        coffv=jnp.zeros((n_groups, 1, LANES), f32).at[:, 0, :N_EXPERTS].set(coff.astype(f32)))


def _max_rows(n_tok):
    tm = EXPERT_TILE
    worst = (n_tok * TOP_K + (n_tok // GROUP) * N_EXPERTS * (SUBLANES - 1)
             + N_EXPERTS * (tm - SUBLANES))
    return (worst + tm - 1) // tm * tm


def _moe(x2, hf, idx, gates, cnt, w_gate_up, b_gate_up, w_down, b_down):
    n_tok, D = x2.shape
    plan = _routing_plan(cnt, n_tok)
    xs, pos = _dispatch(plan, hf, idx, _max_rows(n_tok))
    b_gu = (b_gate_up.reshape(N_EXPERTS, 2 * D_FF // GU_TILE, LANES, 2).transpose(0, 1, 3, 2)
            .reshape(N_EXPERTS, 1, 2 * D_FF))
    y_rows = _expert_mlp(plan, xs, w_gate_up, b_gu, w_down, b_down.reshape(N_EXPERTS, 1, D))
    return _combine(plan, x2, pos, gates, y_rows)


BIAS_ROW = (2 * Q_TILE + LEFT_CHUNKS * CHUNK + LANES - 2) // LANES * LANES


def _band_bias_kernel(r_ref, o_ref):
    x = jnp.broadcast_to(r_ref[0], (Q_TILE, BIAS_ROW))
    x = pltpu.roll(x, BIAS_ROW - (Q_TILE - 1), axis=1, stride=1, stride_axis=0)
    qc = lax.broadcasted_iota(jnp.int32, (Q_TILE, K_BAND), 0) // CHUNK
    kc = lax.broadcasted_iota(jnp.int32, (Q_TILE, K_BAND), 1) // CHUNK
    first_chunk = LEFT_CHUNKS - pl.program_id(0) * (Q_TILE // CHUNK)
    visible = (kc >= qc) & (kc <= qc + LEFT_CHUNKS) & (kc >= first_chunk)
    o_ref[0, 0] = jnp.where(visible, x[:, :K_BAND] * LOG2E, NEG_INF)


def _band_bias(rel_bias):
    H = rel_bias.shape[0]
    d_max = Q_TILE - 1 + LEFT_CHUNKS * CHUNK
    n_far = d_max - MAX_REL + 1
    r = jnp.concatenate(
        [jnp.broadcast_to(rel_bias[:, 2 * MAX_REL:], (H, n_far)),
         rel_bias[:, MAX_REL - (Q_TILE - 1):2 * MAX_REL][:, ::-1],
         jnp.zeros((H, BIAS_ROW - (2 * Q_TILE - 1 + LEFT_CHUNKS * CHUNK)), rel_bias.dtype)], axis=1)
    return pl.pallas_call(
        _band_bias_kernel,
        grid=(START_TILES + 1, H),
        in_specs=[pl.BlockSpec((1, 1, BIAS_ROW), lambda t, h: (h, 0, 0))],
        out_specs=pl.BlockSpec((1, 1, Q_TILE, K_BAND), lambda t, h: (t, h, 0, 0)),
        out_shape=jax.ShapeDtypeStruct((START_TILES + 1, H, Q_TILE, K_BAND), f32),
        name="band_bias",
    )(r.astype(f32).reshape(H, 1, BIAS_ROW))


def _layer(x, mem, norm_mix_g, w_in, q_norm_g, k_norm_g, rel_bias, conv_w, out_norm_a_g,
           out_norm_b_g, w_out, norm_xattn_g, norm_mem_g, w_xq, w_xkv, xq_norm_g, xk_norm_g,
           w_xo, norm_ffn_g, w_router, b_router, w_gate_up, b_gate_up, w_down, b_down):
    B, S, D = x.shape
    n_tok = B * S
    row = lambda a: a.reshape(1, -1).astype(f32)

    qg = row(jnp.tile(q_norm_g, ATT_HEADS) * (ATT_HEAD_DIM ** -0.5 * LOG2E))
    kg = row(jnp.tile(k_norm_g, ATT_HEADS))
    head_ones = jnp.asarray(np.kron(np.eye(ATT_HEADS), np.ones((ATT_HEAD_DIM, ATT_HEAD_DIM))), bf16)
    q, k, v, yb = _mix_in(x, row(norm_mix_g), w_in.astype(bf16), qg, kg, head_ones,
                          conv_w.astype(f32), row(out_norm_b_g))
    x1 = _attn_out(q, k, v, _band_bias(rel_bias), yb, x, w_out.astype(bf16), row(out_norm_a_g))

    kx, vx = _mem_kv(mem, row(norm_mem_g), w_xkv.astype(bf16), row(xk_norm_g))
    wr = jnp.zeros((D, LANES), f32).at[:, :N_EXPERTS].set(w_router)
    wr_hi = wr.astype(bf16)
    wr_lo = (wr - wr_hi.astype(f32)).astype(bf16)
    br = jnp.full((1, LANES), -jnp.inf, f32).at[0, :N_EXPERTS].set(b_router)
    x2, hf, idx, gates, cnt = _xattn_router(
        x1, row(norm_xattn_g), w_xq.astype(bf16), row(xq_norm_g * (X_HEAD_DIM ** -0.5)), kx, vx,
        w_xo.astype(bf16), row(norm_ffn_g), wr_hi, wr_lo, br)

    out = _moe(x2.reshape(n_tok, D), hf.reshape(n_tok, D), idx.reshape(n_tok, LANES),
               gates.reshape(n_tok, LANES), cnt, w_gate_up, b_gate_up, w_down, b_down)
    return out.reshape(B, S, D)


def kernel(x, mem, norm_mix_g, w_in, q_norm_g, k_norm_g, rel_bias, conv_w, out_norm_a_g,
           out_norm_b_g, w_out, norm_xattn_g, norm_mem_g, w_xq, w_xkv, xq_norm_g, xk_norm_g,
           w_xo, norm_ffn_g, w_router, b_router, w_gate_up, b_gate_up, w_down, b_down):
    depth = norm_mix_g.shape[0]
    for l in range(depth):
        x = _layer(x, mem, norm_mix_g[l], w_in[l], q_norm_g[l], k_norm_g[l], rel_bias[l],
                   conv_w[l], out_norm_a_g[l], out_norm_b_g[l], w_out[l], norm_xattn_g[l],
                   norm_mem_g[l], w_xq[l], w_xkv[l], xq_norm_g[l], xk_norm_g[l], w_xo[l],
                   norm_ffn_g[l], w_router[l], b_router[l], w_gate_up[l], b_gate_up[l],
                   w_down[l], b_down[l])
    return x
```

```python
import functools

import jax
import jax.numpy as jnp
import numpy as np
from jax import lax
from jax.experimental import pallas as pl
from jax.experimental.pallas import tpu as pltpu

D_MODEL = 1024
CHUNK = 64
LEFT_CHUNKS = 8
N_MEM = 256
ATT_HEADS = 8
ATT_HEAD_DIM = 64
D_ATT = ATT_HEADS * ATT_HEAD_DIM
D_CONV = 512
MAX_REL = 256
X_HEADS = 4
X_HEAD_DIM = 128
D_X = X_HEADS * X_HEAD_DIM
N_EXPERTS = 32
TOP_K = 4
D_FF = D_MODEL
SWIGLU_LIMIT = 7.0
SWIGLU_ALPHA = 1.702
EPS = 1e-6
NEG_INF = -1e30

LANES = 128
SEQ_TILE = 1024
Q_TILE = 256
K_BAND = Q_TILE + LEFT_CHUNKS * CHUNK
START_TILES = LEFT_CHUNKS * CHUNK // Q_TILE
LOG2E = 1.4426950408889634
HEAD_ROUND = 4
EXPERT_TILE = 512
GU_TILE = 2 * LANES
GROUP = 256
SUBLANES = 8
GROUP_ROWS = 1280
VMEM_LIMIT = 56 * 1024 * 1024

bf16 = jnp.bfloat16
f32 = jnp.float32


def _rms(x, g):
    return x * lax.rsqrt(jnp.mean(x * x, axis=-1, keepdims=True) + EPS) * g


def _dot(a, b):
    return jnp.dot(a, b, preferred_element_type=f32)


def _dot_nt(a, b):
    return lax.dot_general(a, b, (((1,), (1,)), ((), ())), preferred_element_type=f32)


def _mix_in_kernel(x_ref, g_ref, w_ref, qg_ref, kg_ref, hm_ref, cw_ref, gb_ref,
                   q_ref, k_ref, v_ref, yb_ref, ub_ref):
    ts = x_ref.shape[1]

    @pl.when(pl.program_id(1) == 0)
    def _():
        ub_ref[0:8, :] = jnp.zeros((8, D_CONV), f32)

    hb = _rms(x_ref[0], g_ref[...]).astype(bf16)

    def head_norm(t, gain):
        ms = _dot((t * t).astype(bf16), hm_ref[...]) * (1.0 / ATT_HEAD_DIM)
        return t * lax.rsqrt(ms + EPS) * gain

    q = _dot(hb, w_ref[:, 0:D_ATT])
    q_ref[0] = head_norm(q, qg_ref[...]).astype(bf16)
    k = _dot(hb, w_ref[:, D_ATT:2 * D_ATT])
    k_ref[0] = head_norm(k, kg_ref[...]).astype(bf16)
    v_ref[0] = _dot(hb, w_ref[:, 2 * D_ATT:3 * D_ATT]).astype(bf16)

    o = 3 * D_ATT
    bg = _dot(hb, w_ref[:, o:o + D_CONV])
    cg = _dot(hb, w_ref[:, o + D_CONV:o + 2 * D_CONV])
    xv = _dot(hb, w_ref[:, o + 2 * D_CONV:o + 3 * D_CONV])
    u = cg * xv
    ub_ref[8:8 + ts, :] = u
    conv = (cw_ref[0:1, :] * ub_ref[6:6 + ts, :] + cw_ref[1:2, :] * ub_ref[7:7 + ts, :]
            + cw_ref[2:3, :] * u)
    ub_ref[0:8, :] = u[ts - 8:ts, :]
    yb_ref[0] = _rms(bg * conv, gb_ref[...]).astype(bf16)


def _mix_in(x, g, w_in, qg, kg, hm, cw, gb):
    B, S, D = x.shape
    ts = SEQ_TILE
    full = lambda shape: pl.BlockSpec(shape, lambda b, j: (0,) * len(shape))
    tile = lambda w: pl.BlockSpec((1, ts, w), lambda b, j: (b, j, 0))
    out = jax.ShapeDtypeStruct((B, S, D_ATT), bf16)
    return pl.pallas_call(
        _mix_in_kernel,
        grid=(B, S // ts),
        in_specs=[tile(D), full((1, D)), full(w_in.shape), full((1, D_ATT)), full((1, D_ATT)),
                  full((D_ATT, D_ATT)), full((3, D_CONV)), full((1, D_CONV))],
        out_specs=[tile(D_ATT)] * 4,
        out_shape=[out] * 4,
        scratch_shapes=[pltpu.VMEM((ts + 8, D_CONV), f32)],
        compiler_params=pltpu.CompilerParams(
            dimension_semantics=("arbitrary", "arbitrary"), vmem_limit_bytes=VMEM_LIMIT),
        name="mix_in",
    )(x, g, w_in, qg, kg, hm, cw, gb)


def _attn_out_kernel(q_ref, k_ref, v_ref, bias_ref, yb_ref, x_ref, w_ref, ga_ref,
                     o_ref, kb_ref, vb_ref):
    tq = q_ref.shape[1]
    n_kc = K_BAND // CHUNK
    c0 = pl.program_id(1) * (tq // CHUNK)
    for kc in range(n_kc):
        src = pl.multiple_of(jnp.maximum(c0 - LEFT_CHUNKS + kc, 0) * CHUNK, CHUNK)
        kb_ref[kc * CHUNK:(kc + 1) * CHUNK, :] = k_ref[0, pl.ds(src, CHUNK), :]
        vb_ref[kc * CHUNK:(kc + 1) * CHUNK, :] = v_ref[0, pl.ds(src, CHUNK), :]

    lane = lax.broadcasted_iota(jnp.int32, (1, LANES), 1)
    low = lane < ATT_HEAD_DIM

    outs = []
    for h0 in range(0, ATT_HEADS, HEAD_ROUND):
        heads = range(h0, h0 + HEAD_ROUND)
        scores, probs, denoms = {}, {}, {}
        for h in heads:
            sl = slice((h // 2) * LANES, (h // 2 + 1) * LANES)
            qp = q_ref[0, :, sl]
            keep = low if h % 2 == 0 else jnp.logical_not(low)
            scores[h] = _dot_nt(jnp.where(keep, qp, jnp.zeros_like(qp)), kb_ref[:, sl])
        for h in heads:
            s = scores[h] + bias_ref[0, h]
            p = jnp.exp2(s - jnp.max(s, axis=-1, keepdims=True))
            denoms[h] = jnp.sum(p, axis=-1, keepdims=True)
            probs[h] = p.astype(bf16)
        for h in heads:
            sl = slice((h // 2) * LANES, (h // 2 + 1) * LANES)
            outs.append(_dot(probs[h], vb_ref[:, sl]) / denoms[h])
    ya = jnp.concatenate([jnp.where(low, outs[2 * hp], outs[2 * hp + 1])
                          for hp in range(ATT_HEADS // 2)], axis=-1)
    yan = _rms(ya, ga_ref[...]).astype(bf16)
    o_ref[0] = (x_ref[0] + _dot(yan, w_ref[0:D_ATT, :]) + _dot(yb_ref[0], w_ref[D_ATT:, :]))


def _attn_out(q, k, v, bias, yb, x, w_out, ga):
    B, S, D = x.shape
    tq = Q_TILE
    full = lambda shape: pl.BlockSpec(shape, lambda b, j: (0,) * len(shape))
    tile = lambda w: pl.BlockSpec((1, tq, w), lambda b, j: (b, j, 0))
    seq = pl.BlockSpec((1, S, D_ATT), lambda b, j: (b, 0, 0))
    return pl.pallas_call(
        _attn_out_kernel,
        grid=(B, S // tq),
        in_specs=[tile(D_ATT), seq, seq,
                  pl.BlockSpec((1,) + bias.shape[1:], lambda b, j: (jnp.minimum(j, START_TILES), 0, 0, 0)),
                  tile(D_CONV), tile(D),
                  full(w_out.shape), full((1, D_ATT))],
        out_specs=tile(D),
        out_shape=jax.ShapeDtypeStruct((B, S, D), f32),
        scratch_shapes=[pltpu.VMEM((K_BAND, D_ATT), bf16), pltpu.VMEM((K_BAND, D_ATT), bf16)],
        compiler_params=pltpu.CompilerParams(
            dimension_semantics=("parallel", "parallel"), vmem_limit_bytes=VMEM_LIMIT),
        name="attn_out",
    )(q, k, v, bias, yb, x, w_out, ga)


def _mem_kv_kernel(m_ref, g_ref, w_ref, kg_ref, k_ref, v_ref):
    hm = _rms(m_ref[0], g_ref[...]).astype(bf16)
    kv = _dot(hm, w_ref[...])
    for h in range(X_HEADS):
        sl = slice(h * X_HEAD_DIM, (h + 1) * X_HEAD_DIM)
        k_ref[0, :, sl] = _rms(kv[:, sl], kg_ref[...]).astype(bf16)
    v_ref[0] = kv[:, D_X:].astype(bf16)


def _mem_kv(mem, g, w_xkv, kg):
    B = mem.shape[0]
    full = lambda shape: pl.BlockSpec(shape, lambda b: (0,) * len(shape))
    out = jax.ShapeDtypeStruct((B, N_MEM, D_X), bf16)
    return pl.pallas_call(
        _mem_kv_kernel,
        grid=(B,),
        in_specs=[pl.BlockSpec((1, N_MEM, D_MODEL), lambda b: (b, 0, 0)), full((1, D_MODEL)),
                  full(w_xkv.shape), full((1, X_HEAD_DIM))],
        out_specs=[pl.BlockSpec((1, N_MEM, D_X), lambda b: (b, 0, 0))] * 2,
        out_shape=[out, out],
        compiler_params=pltpu.CompilerParams(
            dimension_semantics=("parallel",), vmem_limit_bytes=VMEM_LIMIT),
        name="mem_kv",
    )(mem, g, w_xkv, kg)


def _xattn_router_kernel(x_ref, gx_ref, wq_ref, qg_ref, k_ref, v_ref, wo_ref, gf_ref,
                         wrh_ref, wrl_ref, br_ref,
                         x2_ref, hf_ref, idx_ref, gate_ref, cnt_ref):
    x1 = x_ref[0]
    hb = _rms(x1, gx_ref[...]).astype(bf16)
    q = _dot(hb, wq_ref[...])
    hsl = [slice(h * X_HEAD_DIM, (h + 1) * X_HEAD_DIM) for h in range(X_HEADS)]
    scores = [_dot_nt(_rms(q[:, sl], qg_ref[...]).astype(bf16), k_ref[0, :, sl]) for sl in hsl]
    probs, denoms = [], []
    for s in scores:
        p = jnp.exp2(s - jnp.max(s, axis=-1, keepdims=True))
        denoms.append(jnp.sum(p, axis=-1, keepdims=True))
        probs.append(p.astype(bf16))
    heads = [(_dot(p, v_ref[0, :, sl]) / l).astype(bf16) for p, l, sl in zip(probs, denoms, hsl)]
    o = jnp.concatenate(heads, axis=-1)
    x2 = x1 + _dot(o, wo_ref[...])
    x2_ref[0] = x2

    hf = _rms(x2, gf_ref[...])
    hf_hi = hf.astype(bf16)
    hf_ref[0] = hf_hi
    hf_lo = (hf - hf_hi.astype(f32)).astype(bf16)
    logits = (_dot(hf_hi, wrh_ref[...]) + _dot(hf_lo, wrh_ref[...]) + _dot(hf_hi, wrl_ref[...])
              + br_ref[...])
    lane = lax.broadcasted_iota(jnp.int32, logits.shape, 1)
    vals, idxs = [], []
    for _ in range(TOP_K):
        m = jnp.max(logits, axis=-1, keepdims=True)
        i = jnp.min(jnp.where(logits == m, lane, LANES), axis=-1, keepdims=True)
        logits = jnp.where(lane == i, -jnp.inf, logits)
        vals.append(m)
        idxs.append(i)
    es = [jnp.exp(v - vals[0]) for v in vals]
    tot = es[0] + es[1] + es[2] + es[3]
    idx_out = jnp.zeros(logits.shape, jnp.int32)
    gate_out = jnp.zeros(logits.shape, f32)
    sel = jnp.zeros(logits.shape, f32)
    for kk in range(TOP_K):
        idx_out = jnp.where(lane == kk, idxs[kk], idx_out)
        gate_out = jnp.where(lane == kk, es[kk] / tot, gate_out)
        sel = jnp.where(lane == idxs[kk], 1.0, sel)
    idx_ref[0] = idx_out
    gate_ref[0] = gate_out
    for j in range(cnt_ref.shape[0]):
        cnt_ref[j] = jnp.sum(sel[j * GROUP:(j + 1) * GROUP], axis=0, keepdims=True)


def _xattn_router(x1, gx, w_xq, qg, kx, vx, w_xo, gf, wr_hi, wr_lo, br):
    B, S, D = x1.shape
    ts = SEQ_TILE
    full = lambda shape: pl.BlockSpec(shape, lambda b, j: (0,) * len(shape))
    tile = lambda w: pl.BlockSpec((1, ts, w), lambda b, j: (b, j, 0))
    mem = pl.BlockSpec((1, N_MEM, D_X), lambda b, j: (b, 0, 0))
    return pl.pallas_call(
        _xattn_router_kernel,
        grid=(B, S // ts),
        in_specs=[tile(D), full((1, D)), full(w_xq.shape), full((1, X_HEAD_DIM)), mem, mem,
                  full(w_xo.shape), full((1, D)), full(wr_hi.shape), full(wr_lo.shape),
                  full((1, LANES))],
        out_specs=[tile(D), tile(D), tile(LANES), tile(LANES),
                   pl.BlockSpec((ts // GROUP, 1, LANES), lambda b, j: (b * (S // ts) + j, 0, 0))],
        out_shape=[jax.ShapeDtypeStruct((B, S, D), f32), jax.ShapeDtypeStruct((B, S, D), bf16),
                   jax.ShapeDtypeStruct((B, S, LANES), jnp.int32),
                   jax.ShapeDtypeStruct((B, S, LANES), f32),
                   jax.ShapeDtypeStruct((B * S // GROUP, 1, LANES), f32)],
        compiler_params=pltpu.CompilerParams(
            dimension_semantics=("parallel", "parallel"), vmem_limit_bytes=VMEM_LIMIT),
        name="xattn_router",
    )(x1, gx, w_xq, qg, kx, vx, w_xo, gf, wr_hi, wr_lo, br)


u32 = jnp.uint32
HIGH_HALF = 0xFFFF0000


def _pack_pairs(x):
    w = x.shape[1] // 2
    lo = pltpu.bitcast(x[:, :w], u32) >> 16
    hi = pltpu.bitcast(x[:, w:], u32) & u32(HIGH_HALF)
    return lo | hi


def _unpack_pairs(u):
    lo = pltpu.bitcast(u << 16, f32)
    hi = pltpu.bitcast(u & u32(HIGH_HALF), f32)
    return jnp.concatenate([lo, hi], axis=-1).astype(bf16)


def _for_cells(g, fn):
    def body(e, carry):
        fn(g * N_EXPERTS + e)
        return carry
    lax.fori_loop(0, N_EXPERTS, body, 0)


def _group_positions(idx_ref, coffv_ref, ltri_ref):
    idxv = idx_ref[...]
    lane = lax.broadcasted_iota(jnp.int32, idxv.shape, 1)
    hots = [lane == idxv[:, kk:kk + 1] for kk in range(TOP_K)]
    multi = jnp.zeros(idxv.shape, f32)
    for h in hots:
        multi = jnp.where(h, 1.0, multi)
    rank = _dot(ltri_ref[...], multi.astype(bf16))
    posmat = coffv_ref[0] + rank
    return [jnp.sum(jnp.where(h, posmat, 0.0), axis=-1, keepdims=True) for h in hots]


def _dispatch_kernel(coff_s, rows_s, gbase_s, gtot_s, toff_s, tn_s, nu_s,
                     hf_ref, idx_ref, coffv_ref, ltri_ref,
                     xs_hbm, pos_ref, xbuf, zbuf, sem, zsem, *, n_groups):
    g = pl.program_id(0)
    slot = lax.rem(g, 2)

    def cell_copy(sl, i):
        n = pl.multiple_of(rows_s[i], SUBLANES)
        src = pl.multiple_of(coff_s[i], SUBLANES)
        dst = pl.multiple_of(gbase_s[i], SUBLANES)
        return n, pltpu.make_async_copy(xbuf.at[sl, pl.ds(src, n)], xs_hbm.at[pl.ds(dst, n)],
                                        sem.at[sl])

    def start_cell(sl, i):
        n, cp = cell_copy(sl, i)
        pl.when(n > 0)(cp.start)

    def wait_group(sl, gg):
        n = pl.multiple_of(gtot_s[gg], SUBLANES)
        pltpu.make_async_copy(xbuf.at[sl, pl.ds(0, n)], xs_hbm.at[pl.ds(0, n)], sem.at[sl]).wait()

    @pl.when(g >= 2)
    def _():
        wait_group(slot, g - 2)

    pos = _group_positions(idx_ref, coffv_ref, ltri_ref)
    lane = lax.broadcasted_iota(jnp.int32, (GROUP, LANES), 1)
    riota = lax.broadcasted_iota(jnp.int32, (GROUP, GROUP_ROWS), 1)
    pt = jnp.zeros((GROUP, GROUP_ROWS), f32)
    pos_out = jnp.zeros((GROUP, LANES), f32)
    for kk in range(TOP_K):
        pt = jnp.where(riota == pos[kk].astype(jnp.int32), 1.0, pt)
        pos_out = jnp.where(lane == kk, pos[kk], pos_out)
    pos_ref[...] = pos_out
    xbuf[slot] = _pack_pairs(lax.dot_general(pt.astype(bf16), hf_ref[...], (((0,), (0,)), ((), ())),
                                             preferred_element_type=f32))
    _for_cells(g, functools.partial(start_cell, slot))

    @pl.when(g == n_groups - 1)
    def _():
        if n_groups >= 2:
            wait_group(1 - slot, g - 1)
        wait_group(slot, g)
        zbuf[...] = jnp.zeros(zbuf.shape, u32)

        def tail_copy(e):
            n = pl.multiple_of(tn_s[e], SUBLANES)
            dst = pl.multiple_of(toff_s[e], SUBLANES)
            return n, pltpu.make_async_copy(zbuf.at[pl.ds(0, n)], xs_hbm.at[pl.ds(dst, n)], zsem)

        def tail(e, carry):
            n, cp = tail_copy(e)

            @pl.when(n > 0)
            def _():
                cp.start()
                cp.wait()
            return carry
        lax.fori_loop(0, N_EXPERTS, tail, 0)

        def unused(b, carry):
            dst = pl.multiple_of(b * EXPERT_TILE, EXPERT_TILE)
            cp = pltpu.make_async_copy(zbuf, xs_hbm.at[pl.ds(dst, EXPERT_TILE)], zsem)
            cp.start()
            cp.wait()
            return carry
        lax.fori_loop(nu_s[0], xs_hbm.shape[0] // EXPERT_TILE, unused, 0)


def _dispatch(plan, hf, idx, n_rows):
    n_tok, D = hf.shape
    n_groups = n_tok // GROUP
    ltri = jnp.asarray(np.tril(np.ones((GROUP, GROUP)), -1), bf16)
    grid_spec = pltpu.PrefetchScalarGridSpec(
        num_scalar_prefetch=7,
        grid=(n_groups,),
        in_specs=[pl.BlockSpec((GROUP, D), lambda g, *_: (g, 0)),
                  pl.BlockSpec((GROUP, LANES), lambda g, *_: (g, 0)),
                  pl.BlockSpec((1, 1, LANES), lambda g, *_: (g, 0, 0)),
                  pl.BlockSpec((GROUP, GROUP), lambda g, *_: (0, 0))],
        out_specs=[pl.BlockSpec(memory_space=pl.ANY),
                   pl.BlockSpec((GROUP, LANES), lambda g, *_: (g, 0))],
        scratch_shapes=[pltpu.VMEM((2, GROUP_ROWS, D // 2), u32),
                        pltpu.VMEM((EXPERT_TILE, D // 2), u32),
                        pltpu.SemaphoreType.DMA((2,)), pltpu.SemaphoreType.DMA(())],
    )
    return pl.pallas_call(
        functools.partial(_dispatch_kernel, n_groups=n_groups),
        grid_spec=grid_spec,
        out_shape=[jax.ShapeDtypeStruct((n_rows, D // 2), u32),
                   jax.ShapeDtypeStruct((n_tok, LANES), f32)],
        compiler_params=pltpu.CompilerParams(
            dimension_semantics=("arbitrary",), vmem_limit_bytes=VMEM_LIMIT),
        name="dispatch",
    )(plan["coff"], plan["rows"], plan["gbase"], plan["gtot"], plan["tailoff"], plan["tailn"],
      plan["n_used"], hf, idx, plan["coffv"], ltri)


def _expert_kernel(be_ref, nu_ref, x_ref, wgu_ref, bgu_ref, wdn_ref, bdn_ref, pm_ref, y_ref,
                   wgu_b, wdn_b):
    b = pl.program_id(0)

    @pl.when(b < nu_ref[0])
    def _():
        @pl.when((b == 0) | (be_ref[b] != be_ref[jnp.maximum(b - 1, 0)]))
        def _():
            for c in range(2 * D_FF // GU_TILE):
                cols = slice(c * GU_TILE, (c + 1) * GU_TILE)
                wgu_b[:, cols] = _dot(wgu_ref[0, :, cols].astype(bf16), pm_ref[...]).astype(bf16)
            for c in range(D_FF // LANES):
                rows = slice(c * LANES, (c + 1) * LANES)
                wdn_b[rows, :] = wdn_ref[0, rows, :].astype(bf16)

        _expert_block(x_ref, bgu_ref, bdn_ref, y_ref, wgu_b, wdn_b)

    @pl.when(b >= nu_ref[0])
    def _():
        y_ref[...] = jnp.zeros(y_ref.shape, u32)


def _expert_block(x_ref, bgu_ref, bdn_ref, y_ref, wgu_b, wdn_b):
    gu = _dot(_unpack_pairs(x_ref[...]), wgu_b[...]) + bgu_ref[0]
    acts = []
    for c in range(2 * D_FF // GU_TILE):
        g = jnp.minimum(gu[:, c * GU_TILE:c * GU_TILE + LANES], SWIGLU_LIMIT)
        lin = jnp.clip(gu[:, c * GU_TILE + LANES:(c + 1) * GU_TILE], -SWIGLU_LIMIT, SWIGLU_LIMIT)
        acts.append((g * jax.nn.sigmoid(SWIGLU_ALPHA * g) * (lin + 1.0)).astype(bf16))
    act = jnp.concatenate(acts, axis=-1)
    y = _dot(act, wdn_b[...]) + bdn_ref[0]
    y_ref[...] = _pack_pairs(y.astype(bf16).astype(f32))


def _gate_up_perm():
    j = np.arange(GU_TILE)
    p = np.zeros((GU_TILE, GU_TILE), np.float32)
    p[np.where(j < LANES, 2 * j, 2 * (j - LANES) + 1), j] = 1.0
    return jnp.asarray(p, bf16)


def _expert_mlp(plan, xs, w_gu, b_gu, w_dn, b_dn):
    n_rows = xs.shape[0]
    D = D_MODEL
    tm = EXPERT_TILE
    row_blk = lambda b, be, nu: (jnp.minimum(b, nu[0] - 1), 0)
    per_e = lambda b, be, nu: (be[b], 0, 0)
    grid_spec = pltpu.PrefetchScalarGridSpec(
        num_scalar_prefetch=2,
        grid=(n_rows // tm,),
        in_specs=[pl.BlockSpec((tm, D // 2), row_blk),
                  pl.BlockSpec((1, D, 2 * D_FF), per_e),
                  pl.BlockSpec((1, 1, 2 * D_FF), per_e),
                  pl.BlockSpec((1, D_FF, D), per_e),
                  pl.BlockSpec((1, 1, D), per_e),
                  pl.BlockSpec((GU_TILE, GU_TILE), lambda b, be, nu: (0, 0))],
        out_specs=pl.BlockSpec((tm, D // 2), lambda b, be, nu: (b, 0)),
        scratch_shapes=[pltpu.VMEM((D, 2 * D_FF), bf16), pltpu.VMEM((D_FF, D), bf16)],
    )
    return pl.pallas_call(
        _expert_kernel,
        grid_spec=grid_spec,
        out_shape=jax.ShapeDtypeStruct((n_rows, D // 2), u32),
        compiler_params=pltpu.CompilerParams(
            dimension_semantics=("arbitrary",), vmem_limit_bytes=VMEM_LIMIT),
        name="expert_mlp",
    )(plan["block_e"], plan["n_used"], xs, w_gu, b_gu, w_dn, b_dn, _gate_up_perm())


def _combine_kernel(coff_s, rows_s, gbase_s, gtot_s, x_ref, pos_ref, gate_ref, y_hbm, o_ref, ybuf, sem,
                    *, n_groups):
    g = pl.program_id(0)
    slot = lax.rem(g, 2)

    def cell_copy(sl, i):
        n = pl.multiple_of(rows_s[i], SUBLANES)
        loc = pl.multiple_of(coff_s[i], SUBLANES)
        src = pl.multiple_of(gbase_s[i], SUBLANES)
        return n, pltpu.make_async_copy(y_hbm.at[pl.ds(src, n)], ybuf.at[sl, pl.ds(loc, n)],
                                        sem.at[sl])

    def start_cell(sl, i):
        n, cp = cell_copy(sl, i)
        pl.when(n > 0)(cp.start)

    def wait_group(sl, gg):
        n = pl.multiple_of(gtot_s[gg], SUBLANES)
        pltpu.make_async_copy(y_hbm.at[pl.ds(0, n)], ybuf.at[sl, pl.ds(0, n)], sem.at[sl]).wait()

    @pl.when(g == 0)
    def _():
        ybuf[...] = jnp.zeros(ybuf.shape, u32)
        _for_cells(g, functools.partial(start_cell, slot))

    @pl.when(g + 1 < n_groups)
    def _():
        _for_cells(g + 1, functools.partial(start_cell, 1 - slot))

    wait_group(slot, g)

    riota = lax.broadcasted_iota(jnp.int32, (GROUP, GROUP_ROWS), 1)
    pt = jnp.zeros((GROUP, GROUP_ROWS), f32)
    for kk in range(TOP_K):
        pk = pos_ref[:, kk:kk + 1].astype(jnp.int32)
        pt = jnp.where(riota == pk, gate_ref[:, kk:kk + 1], pt)
    o_ref[...] = x_ref[...] + _dot(pt.astype(bf16), _unpack_pairs(ybuf[slot]))


def _combine(plan, x2, pos, gates, y_rows):
    n_tok, D = x2.shape
    n_groups = n_tok // GROUP
    tok = lambda w: pl.BlockSpec((GROUP, w), lambda g, *_: (g, 0))
    grid_spec = pltpu.PrefetchScalarGridSpec(
        num_scalar_prefetch=4,
        grid=(n_groups,),
        in_specs=[tok(D), tok(LANES), tok(LANES), pl.BlockSpec(memory_space=pl.ANY)],
        out_specs=tok(D),
        scratch_shapes=[pltpu.VMEM((2, GROUP_ROWS, D // 2), u32), pltpu.SemaphoreType.DMA((2,))],
    )
    return pl.pallas_call(
        functools.partial(_combine_kernel, n_groups=n_groups),
        grid_spec=grid_spec,
        out_shape=jax.ShapeDtypeStruct((n_tok, D), f32),
        compiler_params=pltpu.CompilerParams(
            dimension_semantics=("arbitrary",), vmem_limit_bytes=VMEM_LIMIT),
        name="combine",
    )(plan["coff"], plan["rows"], plan["gbase"], plan["gtot"], x2, pos, gates, y_rows)


def _routing_plan(cnt, n_tok):
    tm = EXPERT_TILE
    n_groups = n_tok // GROUP
    cnt = cnt.reshape(n_groups, LANES)[:, :N_EXPERTS].astype(jnp.int32)
    rows = (cnt + SUBLANES - 1) // SUBLANES * SUBLANES
    coff = jnp.cumsum(rows, axis=1) - rows
    tot = jnp.sum(rows, axis=0)
    padded = (tot + tm - 1) // tm * tm
    pend = jnp.cumsum(padded)
    ebase = pend - padded
    gbase = ebase[None, :] + jnp.cumsum(rows, axis=0) - rows
    n_blocks = _max_rows(n_tok) // tm
    n_used = pend[-1] // tm
    blk = jnp.arange(n_blocks, dtype=jnp.int32)
    first_row = jnp.minimum(blk, n_used - 1) * tm
    block_e = jnp.sum((pend[None, :] <= first_row[:, None]).astype(jnp.int32), axis=1)
    flat = lambda a: a.reshape(-1).astype(jnp.int32)
    return dict(
        coff=flat(coff), rows=flat(rows), gbase=flat(gbase), gtot=flat(jnp.sum(rows, axis=1)),
        tailoff=flat(ebase + tot), tailn=flat(padded - tot),
        block_e=flat(jnp.minimum(block_e, N_EXPERTS - 1)), n_used=flat(n_used),
        coffv=jnp.zeros((n_groups, 1, LANES), f32).at[:, 0, :N_EXPERTS].set(coff.astype(f32)))


def _max_rows(n_tok):
    tm = EXPERT_TILE
    worst = (n_tok * TOP_K + (n_tok // GROUP) * N_EXPERTS * (SUBLANES - 1)
             + N_EXPERTS * (tm - SUBLANES))
    return (worst + tm - 1) // tm * tm


def _moe(x2, hf, idx, gates, cnt, w_gate_up, b_gate_up, w_down, b_down):
    n_tok, D = x2.shape
    plan = _routing_plan(cnt, n_tok)
    xs, pos = _dispatch(plan, hf, idx, _max_rows(n_tok))
    b_gu = (b_gate_up.reshape(N_EXPERTS, 2 * D_FF // GU_TILE, LANES, 2).transpose(0, 1, 3, 2)
            .reshape(N_EXPERTS, 1, 2 * D_FF))
    y_rows = _expert_mlp(plan, xs, w_gate_up, b_gu, w_down, b_down.reshape(N_EXPERTS, 1, D))
    return _combine(plan, x2, pos, gates, y_rows)


BIAS_ROW = (2 * Q_TILE + LEFT_CHUNKS * CHUNK + LANES - 2) // LANES * LANES


def _band_bias_kernel(r_ref, o_ref):
    x = jnp.broadcast_to(r_ref[0], (Q_TILE, BIAS_ROW))
    x = pltpu.roll(x, BIAS_ROW - (Q_TILE - 1), axis=1, stride=1, stride_axis=0)
    qc = lax.broadcasted_iota(jnp.int32, (Q_TILE, K_BAND), 0) // CHUNK
    kc = lax.broadcasted_iota(jnp.int32, (Q_TILE, K_BAND), 1) // CHUNK
    first_chunk = LEFT_CHUNKS - pl.program_id(0) * (Q_TILE // CHUNK)
    visible = (kc >= qc) & (kc <= qc + LEFT_CHUNKS) & (kc >= first_chunk)
    o_ref[0, 0] = jnp.where(visible, x[:, :K_BAND] * LOG2E, NEG_INF)


def _band_bias(rel_bias):
    H = rel_bias.shape[0]
    d_max = Q_TILE - 1 + LEFT_CHUNKS * CHUNK
    n_far = d_max - MAX_REL + 1
    r = jnp.concatenate(
        [jnp.broadcast_to(rel_bias[:, 2 * MAX_REL:], (H, n_far)),
         rel_bias[:, MAX_REL - (Q_TILE - 1):2 * MAX_REL][:, ::-1],
         jnp.zeros((H, BIAS_ROW - (2 * Q_TILE - 1 + LEFT_CHUNKS * CHUNK)), rel_bias.dtype)], axis=1)
    return pl.pallas_call(
        _band_bias_kernel,
        grid=(START_TILES + 1, H),
        in_specs=[pl.BlockSpec((1, 1, BIAS_ROW), lambda t, h: (h, 0, 0))],
        out_specs=pl.BlockSpec((1, 1, Q_TILE, K_BAND), lambda t, h: (t, h, 0, 0)),
        out_shape=jax.ShapeDtypeStruct((START_TILES + 1, H, Q_TILE, K_BAND), f32),
        name="band_bias",
    )(r.astype(f32).reshape(H, 1, BIAS_ROW))


def _layer(x, mem, norm_mix_g, w_in, q_norm_g, k_norm_g, rel_bias, conv_w, out_norm_a_g,
           out_norm_b_g, w_out, norm_xattn_g, norm_mem_g, w_xq, w_xkv, xq_norm_g, xk_norm_g,
           w_xo, norm_ffn_g, w_router, b_router, w_gate_up, b_gate_up, w_down, b_down):
    B, S, D = x.shape
    n_tok = B * S
    row = lambda a: a.reshape(1, -1).astype(f32)

    qg = row(jnp.tile(q_norm_g, ATT_HEADS) * (ATT_HEAD_DIM ** -0.5 * LOG2E))
    kg = row(jnp.tile(k_norm_g, ATT_HEADS))
    head_ones = jnp.asarray(np.kron(np.eye(ATT_HEADS), np.ones((ATT_HEAD_DIM, ATT_HEAD_DIM))), bf16)
    q, k, v, yb = _mix_in(x, row(norm_mix_g), w_in.astype(bf16), qg, kg, head_ones,
                          conv_w.astype(f32), row(out_norm_b_g))
    x1 = _attn_out(q, k, v, _band_bias(rel_bias), yb, x, w_out.astype(bf16), row(out_norm_a_g))

    kx, vx = _mem_kv(mem, row(norm_mem_g), w_xkv.astype(bf16), row(xk_norm_g))
    wr = jnp.zeros((D, LANES), f32).at[:, :N_EXPERTS].set(w_router)
    wr_hi = wr.astype(bf16)
    wr_lo = (wr - wr_hi.astype(f32)).astype(bf16)
    br = jnp.full((1, LANES), -jnp.inf, f32).at[0, :N_EXPERTS].set(b_router)
    x2, hf, idx, gates, cnt = _xattn_router(
        x1, row(norm_xattn_g), w_xq.astype(bf16), row(xq_norm_g * (X_HEAD_DIM ** -0.5 * LOG2E)),
        kx, vx,
        w_xo.astype(bf16), row(norm_ffn_g), wr_hi, wr_lo, br)

    out = _moe(x2.reshape(n_tok, D), hf.reshape(n_tok, D), idx.reshape(n_tok, LANES),
               gates.reshape(n_tok, LANES), cnt, w_gate_up, b_gate_up, w_down, b_down)
    return out.reshape(B, S, D)


def kernel(x, mem, norm_mix_g, w_in, q_norm_g, k_norm_g, rel_bias, conv_w, out_norm_a_g,
           out_norm_b_g, w_out, norm_xattn_g, norm_mem_g, w_xq, w_xkv, xq_norm_g, xk_norm_g,
           w_xo, norm_ffn_g, w_router, b_router, w_gate_up, b_gate_up, w_down, b_down):
    depth = norm_mix_g.shape[0]
    for l in range(depth):
        x = _layer(x, mem, norm_mix_g[l], w_in[l], q_norm_g[l], k_norm_g[l], rel_bias[l],
                   conv_w[l], out_norm_a_g[l], out_norm_b_g[l], w_out[l], norm_xattn_g[l],
                   norm_mem_g[l], w_xq[l], w_xkv[l], xq_norm_g[l], xk_norm_g[l], w_xo[l],
                   norm_ffn_g[l], w_router[l], b_router[l], w_gate_up[l], b_gate_up[l],
                   w_down[l], b_down[l])
    return x
```

```python
import functools

import jax
import jax.numpy as jnp
import numpy as np
from jax import lax
from jax.experimental import pallas as pl
from jax.experimental.pallas import tpu as pltpu

D_MODEL = 1024
CHUNK = 64
LEFT_CHUNKS = 8
N_MEM = 256
ATT_HEADS = 8
ATT_HEAD_DIM = 64
D_ATT = ATT_HEADS * ATT_HEAD_DIM
D_CONV = 512
MAX_REL = 256
X_HEADS = 4
X_HEAD_DIM = 128
D_X = X_HEADS * X_HEAD_DIM
N_EXPERTS = 32
TOP_K = 4
D_FF = D_MODEL
SWIGLU_LIMIT = 7.0
SWIGLU_ALPHA = 1.702
EPS = 1e-6
NEG_INF = -1e30

LANES = 128
SEQ_TILE = 1024
Q_TILE = 256
K_BAND = Q_TILE + LEFT_CHUNKS * CHUNK
START_TILES = LEFT_CHUNKS * CHUNK // Q_TILE
LOG2E = 1.4426950408889634
HEAD_ROUND = 4
EXPERT_TILE = 512
GU_TILE = 2 * LANES
GROUP = 256
SUBLANES = 8
GROUP_ROWS = 1280
VMEM_LIMIT = 56 * 1024 * 1024

bf16 = jnp.bfloat16
f32 = jnp.float32


def _rms(x, g):
    return x * lax.rsqrt(jnp.mean(x * x, axis=-1, keepdims=True) + EPS) * g


def _dot(a, b):
    return jnp.dot(a, b, preferred_element_type=f32)


def _dot_nt(a, b):
    return lax.dot_general(a, b, (((1,), (1,)), ((), ())), preferred_element_type=f32)


def _mix_in_kernel(x_ref, g_ref, w_ref, qg_ref, kg_ref, hm_ref, cw_ref, gb_ref,
                   q_ref, k_ref, v_ref, yb_ref, ub_ref):
    ts = x_ref.shape[1]

    @pl.when(pl.program_id(1) == 0)
    def _():
        ub_ref[0:8, :] = jnp.zeros((8, D_CONV), f32)

    hb = _rms(x_ref[0], g_ref[...]).astype(bf16)

    def head_norm(t, gain):
        ms = _dot((t * t).astype(bf16), hm_ref[...]) * (1.0 / ATT_HEAD_DIM)
        return t * lax.rsqrt(ms + EPS) * gain

    q = _dot(hb, w_ref[:, 0:D_ATT])
    q_ref[0] = head_norm(q, qg_ref[...]).astype(bf16)
    k = _dot(hb, w_ref[:, D_ATT:2 * D_ATT])
    k_ref[0] = head_norm(k, kg_ref[...]).astype(bf16)
    v_ref[0] = _dot(hb, w_ref[:, 2 * D_ATT:3 * D_ATT]).astype(bf16)

    o = 3 * D_ATT
    bg = _dot(hb, w_ref[:, o:o + D_CONV])
    cg = _dot(hb, w_ref[:, o + D_CONV:o + 2 * D_CONV])
    xv = _dot(hb, w_ref[:, o + 2 * D_CONV:o + 3 * D_CONV])
    u = cg * xv
    ub_ref[8:8 + ts, :] = u
    conv = (cw_ref[0:1, :] * ub_ref[6:6 + ts, :] + cw_ref[1:2, :] * ub_ref[7:7 + ts, :]
            + cw_ref[2:3, :] * u)
    ub_ref[0:8, :] = u[ts - 8:ts, :]
    yb_ref[0] = _rms(bg * conv, gb_ref[...]).astype(bf16)


def _mix_in(x, g, w_in, qg, kg, hm, cw, gb):
    B, S, D = x.shape
    ts = SEQ_TILE
    full = lambda shape: pl.BlockSpec(shape, lambda b, j: (0,) * len(shape))
    tile = lambda w: pl.BlockSpec((1, ts, w), lambda b, j: (b, j, 0))
    out = jax.ShapeDtypeStruct((B, S, D_ATT), bf16)
    return pl.pallas_call(
        _mix_in_kernel,
        grid=(B, S // ts),
        in_specs=[tile(D), full((1, D)), full(w_in.shape), full((1, D_ATT)), full((1, D_ATT)),
                  full((D_ATT, D_ATT)), full((3, D_CONV)), full((1, D_CONV))],
        out_specs=[tile(D_ATT)] * 4,
        out_shape=[out] * 4,
        scratch_shapes=[pltpu.VMEM((ts + 8, D_CONV), f32)],
        compiler_params=pltpu.CompilerParams(
            dimension_semantics=("arbitrary", "arbitrary"), vmem_limit_bytes=VMEM_LIMIT),
        name="mix_in",
    )(x, g, w_in, qg, kg, hm, cw, gb)


def _attn_out_kernel(q_ref, k_ref, v_ref, bias_ref, yb_ref, x_ref, w_ref, ga_ref,
                     o_ref, kb_ref, vb_ref):
    tq = q_ref.shape[1]
    n_kc = K_BAND // CHUNK
    c0 = pl.program_id(1) * (tq // CHUNK)
    for kc in range(n_kc):
        src = pl.multiple_of(jnp.maximum(c0 - LEFT_CHUNKS + kc, 0) * CHUNK, CHUNK)
        kb_ref[kc * CHUNK:(kc + 1) * CHUNK, :] = k_ref[0, pl.ds(src, CHUNK), :]
        vb_ref[kc * CHUNK:(kc + 1) * CHUNK, :] = v_ref[0, pl.ds(src, CHUNK), :]

    lane = lax.broadcasted_iota(jnp.int32, (1, LANES), 1)
    low = lane < ATT_HEAD_DIM

    outs = []
    for h0 in range(0, ATT_HEADS, HEAD_ROUND):
        heads = range(h0, h0 + HEAD_ROUND)
        scores, probs, denoms = {}, {}, {}
        for h in heads:
            sl = slice((h // 2) * LANES, (h // 2 + 1) * LANES)
            qp = q_ref[0, :, sl]
            keep = low if h % 2 == 0 else jnp.logical_not(low)
            scores[h] = _dot_nt(jnp.where(keep, qp, jnp.zeros_like(qp)), kb_ref[:, sl])
        for h in heads:
            s = scores[h] + bias_ref[0, h]
            p = jnp.exp2(s - jnp.max(s, axis=-1, keepdims=True))
            denoms[h] = jnp.sum(p, axis=-1, keepdims=True)
            probs[h] = p.astype(bf16)
        for h in heads:
            sl = slice((h // 2) * LANES, (h // 2 + 1) * LANES)
            outs.append(_dot(probs[h], vb_ref[:, sl]) / denoms[h])
    ya = jnp.concatenate([jnp.where(low, outs[2 * hp], outs[2 * hp + 1])
                          for hp in range(ATT_HEADS // 2)], axis=-1)
    yan = _rms(ya, ga_ref[...]).astype(bf16)
    o_ref[0] = (x_ref[0] + _dot(yan, w_ref[0:D_ATT, :]) + _dot(yb_ref[0], w_ref[D_ATT:, :]))


def _attn_out(q, k, v, bias, yb, x, w_out, ga):
    B, S, D = x.shape
    tq = Q_TILE
    full = lambda shape: pl.BlockSpec(shape, lambda b, j: (0,) * len(shape))
    tile = lambda w: pl.BlockSpec((1, tq, w), lambda b, j: (b, j, 0))
    seq = pl.BlockSpec((1, S, D_ATT), lambda b, j: (b, 0, 0))
    return pl.pallas_call(
        _attn_out_kernel,
        grid=(B, S // tq),
        in_specs=[tile(D_ATT), seq, seq,
                  pl.BlockSpec((1,) + bias.shape[1:], lambda b, j: (jnp.minimum(j, START_TILES), 0, 0, 0)),
                  tile(D_CONV), tile(D),
                  full(w_out.shape), full((1, D_ATT))],
        out_specs=tile(D),
        out_shape=jax.ShapeDtypeStruct((B, S, D), f32),
        scratch_shapes=[pltpu.VMEM((K_BAND, D_ATT), bf16), pltpu.VMEM((K_BAND, D_ATT), bf16)],
        compiler_params=pltpu.CompilerParams(
            dimension_semantics=("parallel", "parallel"), vmem_limit_bytes=VMEM_LIMIT),
        name="attn_out",
    )(q, k, v, bias, yb, x, w_out, ga)


def _mem_kv_kernel(m_ref, g_ref, w_ref, kg_ref, k_ref, v_ref):
    hm = _rms(m_ref[0], g_ref[...]).astype(bf16)
    kv = _dot(hm, w_ref[...])
    for h in range(X_HEADS):
        sl = slice(h * X_HEAD_DIM, (h + 1) * X_HEAD_DIM)
        k_ref[0, :, sl] = _rms(kv[:, sl], kg_ref[...]).astype(bf16)
    v_ref[0] = kv[:, D_X:].astype(bf16)


def _mem_kv(mem, g, w_xkv, kg):
    B = mem.shape[0]
    full = lambda shape: pl.BlockSpec(shape, lambda b: (0,) * len(shape))
    out = jax.ShapeDtypeStruct((B, N_MEM, D_X), bf16)
    return pl.pallas_call(
        _mem_kv_kernel,
        grid=(B,),
        in_specs=[pl.BlockSpec((1, N_MEM, D_MODEL), lambda b: (b, 0, 0)), full((1, D_MODEL)),
                  full(w_xkv.shape), full((1, X_HEAD_DIM))],
        out_specs=[pl.BlockSpec((1, N_MEM, D_X), lambda b: (b, 0, 0))] * 2,
        out_shape=[out, out],
        compiler_params=pltpu.CompilerParams(
            dimension_semantics=("parallel",), vmem_limit_bytes=VMEM_LIMIT),
        name="mem_kv",
    )(mem, g, w_xkv, kg)


def _xattn_router_kernel(x_ref, gx_ref, wq_ref, qg_ref, k_ref, v_ref, wo_ref, gf_ref,
                         wrh_ref, wrl_ref, br_ref,
                         x2_ref, hf_ref, idx_ref, gate_ref, cnt_ref):
    x1 = x_ref[0]
    hb = _rms(x1, gx_ref[...]).astype(bf16)
    q = _dot(hb, wq_ref[...])
    hsl = [slice(h * X_HEAD_DIM, (h + 1) * X_HEAD_DIM) for h in range(X_HEADS)]
    scores = [_dot_nt(_rms(q[:, sl], qg_ref[...]).astype(bf16), k_ref[0, :, sl]) for sl in hsl]
    probs, denoms = [], []
    for s in scores:
        p = jnp.exp2(s - jnp.max(s, axis=-1, keepdims=True))
        denoms.append(jnp.sum(p, axis=-1, keepdims=True))
        probs.append(p.astype(bf16))
    heads = [(_dot(p, v_ref[0, :, sl]) / l).astype(bf16) for p, l, sl in zip(probs, denoms, hsl)]
    o = jnp.concatenate(heads, axis=-1)
    x2 = x1 + _dot(o, wo_ref[...])
    x2_ref[0] = x2

    hf = _rms(x2, gf_ref[...])
    hf_hi = hf.astype(bf16)
    hf_ref[0] = hf_hi
    hf_lo = (hf - hf_hi.astype(f32)).astype(bf16)
    logits = (_dot(hf_hi, wrh_ref[...]) + _dot(hf_lo, wrh_ref[...]) + _dot(hf_hi, wrl_ref[...])
              + br_ref[...])
    lane = lax.broadcasted_iota(jnp.int32, logits.shape, 1)
    vals, idxs = [], []
    for _ in range(TOP_K):
        m = jnp.max(logits, axis=-1, keepdims=True)
        i = jnp.min(jnp.where(logits == m, lane, LANES), axis=-1, keepdims=True)
        logits = jnp.where(lane == i, -jnp.inf, logits)
        vals.append(m)
        idxs.append(i)
    es = [jnp.exp(v - vals[0]) for v in vals]
    tot = es[0] + es[1] + es[2] + es[3]
    idx_out = jnp.zeros(logits.shape, jnp.int32)
    gate_out = jnp.zeros(logits.shape, f32)
    sel = jnp.zeros(logits.shape, f32)
    for kk in range(TOP_K):
        idx_out = jnp.where(lane == kk, idxs[kk], idx_out)
        gate_out = jnp.where(lane == kk, es[kk] / tot, gate_out)
        sel = jnp.where(lane == idxs[kk], 1.0, sel)
    idx_ref[0] = idx_out
    gate_ref[0] = gate_out
    for j in range(cnt_ref.shape[0]):
        cnt_ref[j] = jnp.sum(sel[j * GROUP:(j + 1) * GROUP], axis=0, keepdims=True)


def _xattn_router(x1, gx, w_xq, qg, kx, vx, w_xo, gf, wr_hi, wr_lo, br):
    B, S, D = x1.shape
    ts = SEQ_TILE
    full = lambda shape: pl.BlockSpec(shape, lambda b, j: (0,) * len(shape))
    tile = lambda w: pl.BlockSpec((1, ts, w), lambda b, j: (b, j, 0))
    mem = pl.BlockSpec((1, N_MEM, D_X), lambda b, j: (b, 0, 0))
    return pl.pallas_call(
        _xattn_router_kernel,
        grid=(B, S // ts),
        in_specs=[tile(D), full((1, D)), full(w_xq.shape), full((1, X_HEAD_DIM)), mem, mem,
                  full(w_xo.shape), full((1, D)), full(wr_hi.shape), full(wr_lo.shape),
                  full((1, LANES))],
        out_specs=[tile(D), tile(D), tile(LANES), tile(LANES),
                   pl.BlockSpec((ts // GROUP, 1, LANES), lambda b, j: (b * (S // ts) + j, 0, 0))],
        out_shape=[jax.ShapeDtypeStruct((B, S, D), f32), jax.ShapeDtypeStruct((B, S, D), bf16),
                   jax.ShapeDtypeStruct((B, S, LANES), jnp.int32),
                   jax.ShapeDtypeStruct((B, S, LANES), f32),
                   jax.ShapeDtypeStruct((B * S // GROUP, 1, LANES), f32)],
        compiler_params=pltpu.CompilerParams(
            dimension_semantics=("parallel", "parallel"), vmem_limit_bytes=VMEM_LIMIT),
        name="xattn_router",
    )(x1, gx, w_xq, qg, kx, vx, w_xo, gf, wr_hi, wr_lo, br)


u32 = jnp.uint32
HIGH_HALF = 0xFFFF0000


def _pack_pairs(x):
    w = x.shape[1] // 2
    lo = pltpu.bitcast(x[:, :w], u32) >> 16
    hi = pltpu.bitcast(x[:, w:], u32) & u32(HIGH_HALF)
    return lo | hi


def _unpack_pairs(u):
    lo = pltpu.bitcast(u << 16, f32)
    hi = pltpu.bitcast(u & u32(HIGH_HALF), f32)
    return jnp.concatenate([lo, hi], axis=-1).astype(bf16)


def _for_cells(g, fn):
    def body(e, carry):
        fn(g * N_EXPERTS + e)
        return carry
    lax.fori_loop(0, N_EXPERTS, body, 0)


def _group_positions(idx_ref, coffv_ref, ltri_ref):
    idxv = idx_ref[...]
    lane = lax.broadcasted_iota(jnp.int32, idxv.shape, 1)
    hots = [lane == idxv[:, kk:kk + 1] for kk in range(TOP_K)]
    multi = jnp.zeros(idxv.shape, f32)
    for h in hots:
        multi = jnp.where(h, 1.0, multi)
    rank = _dot(ltri_ref[...], multi.astype(bf16))
    posmat = coffv_ref[0] + rank
    return [jnp.sum(jnp.where(h, posmat, 0.0), axis=-1, keepdims=True) for h in hots]


def _dispatch_kernel(coff_s, rows_s, gbase_s, gtot_s, toff_s, tn_s, nu_s,
                     hf_ref, idx_ref, coffv_ref, ltri_ref,
                     xs_hbm, pos_ref, xbuf, zbuf, sem, zsem, *, n_groups):
    g = pl.program_id(0)
    slot = lax.rem(g, 2)

    def cell_copy(sl, i):
        n = pl.multiple_of(rows_s[i], SUBLANES)
        src = pl.multiple_of(coff_s[i], SUBLANES)
        dst = pl.multiple_of(gbase_s[i], SUBLANES)
        return n, pltpu.make_async_copy(xbuf.at[sl, pl.ds(src, n)], xs_hbm.at[pl.ds(dst, n)],
                                        sem.at[sl])

    def start_cell(sl, i):
        n, cp = cell_copy(sl, i)
        pl.when(n > 0)(cp.start)

    def wait_group(sl, gg):
        n = pl.multiple_of(gtot_s[gg], SUBLANES)
        pltpu.make_async_copy(xbuf.at[sl, pl.ds(0, n)], xs_hbm.at[pl.ds(0, n)], sem.at[sl]).wait()

    @pl.when(g >= 2)
    def _():
        wait_group(slot, g - 2)

    pos = _group_positions(idx_ref, coffv_ref, ltri_ref)
    lane = lax.broadcasted_iota(jnp.int32, (GROUP, LANES), 1)
    riota = lax.broadcasted_iota(jnp.int32, (GROUP, GROUP_ROWS), 1)
    pt = jnp.zeros((GROUP, GROUP_ROWS), f32)
    pos_out = jnp.zeros((GROUP, LANES), f32)
    for kk in range(TOP_K):
        pt = jnp.where(riota == pos[kk].astype(jnp.int32), 1.0, pt)
        pos_out = jnp.where(lane == kk, pos[kk], pos_out)
    pos_ref[...] = pos_out
    xbuf[slot] = _pack_pairs(lax.dot_general(pt.astype(bf16), hf_ref[...], (((0,), (0,)), ((), ())),
                                             preferred_element_type=f32))
    _for_cells(g, functools.partial(start_cell, slot))

    @pl.when(g == n_groups - 1)
    def _():
        if n_groups >= 2:
            wait_group(1 - slot, g - 1)
        wait_group(slot, g)
        zbuf[...] = jnp.zeros(zbuf.shape, u32)

        def tail_copy(e):
            n = pl.multiple_of(tn_s[e], SUBLANES)
            dst = pl.multiple_of(toff_s[e], SUBLANES)
            return n, pltpu.make_async_copy(zbuf.at[pl.ds(0, n)], xs_hbm.at[pl.ds(dst, n)], zsem)

        def tail(e, carry):
            n, cp = tail_copy(e)

            @pl.when(n > 0)
            def _():
                cp.start()
                cp.wait()
            return carry
        lax.fori_loop(0, N_EXPERTS, tail, 0)

        def unused(b, carry):
            dst = pl.multiple_of(b * EXPERT_TILE, EXPERT_TILE)
            cp = pltpu.make_async_copy(zbuf, xs_hbm.at[pl.ds(dst, EXPERT_TILE)], zsem)
            cp.start()
            cp.wait()
            return carry
        lax.fori_loop(nu_s[0], xs_hbm.shape[0] // EXPERT_TILE, unused, 0)


def _dispatch(plan, hf, idx, n_rows):
    n_tok, D = hf.shape
    n_groups = n_tok // GROUP
    ltri = jnp.asarray(np.tril(np.ones((GROUP, GROUP)), -1), bf16)
    grid_spec = pltpu.PrefetchScalarGridSpec(
        num_scalar_prefetch=7,
        grid=(n_groups,),
        in_specs=[pl.BlockSpec((GROUP, D), lambda g, *_: (g, 0)),
                  pl.BlockSpec((GROUP, LANES), lambda g, *_: (g, 0)),
                  pl.BlockSpec((1, 1, LANES), lambda g, *_: (g, 0, 0)),
                  pl.BlockSpec((GROUP, GROUP), lambda g, *_: (0, 0))],
        out_specs=[pl.BlockSpec(memory_space=pl.ANY),
                   pl.BlockSpec((GROUP, LANES), lambda g, *_: (g, 0))],
        scratch_shapes=[pltpu.VMEM((2, GROUP_ROWS, D // 2), u32),
                        pltpu.VMEM((EXPERT_TILE, D // 2), u32),
                        pltpu.SemaphoreType.DMA((2,)), pltpu.SemaphoreType.DMA(())],
    )
    return pl.pallas_call(
        functools.partial(_dispatch_kernel, n_groups=n_groups),
        grid_spec=grid_spec,
        out_shape=[jax.ShapeDtypeStruct((n_rows, D // 2), u32),
                   jax.ShapeDtypeStruct((n_tok, LANES), f32)],
        compiler_params=pltpu.CompilerParams(
            dimension_semantics=("arbitrary",), vmem_limit_bytes=VMEM_LIMIT),
        name="dispatch",
    )(plan["coff"], plan["rows"], plan["gbase"], plan["gtot"], plan["tailoff"], plan["tailn"],
      plan["n_used"], hf, idx, plan["coffv"], ltri)


def _expert_kernel(eb_ref, nb_ref, nu_ref, wgu_ref, bgu_ref, wdn_ref, bdn_ref, pm_ref, xs_hbm,
                   y_hbm, wgu_b, wdn_b, xin, yout, in_sem, out_sem):
    e = pl.program_id(0)
    tm = xin.shape[1]
    n_used = nu_ref[0]

    def in_copy(b):
        sl = lax.rem(b, 2)
        rows = pl.ds(pl.multiple_of(b * tm, tm), tm)
        return pltpu.make_async_copy(xs_hbm.at[rows], xin.at[sl], in_sem.at[sl])

    def out_copy(b):
        sl = lax.rem(b, 2)
        rows = pl.ds(pl.multiple_of(b * tm, tm), tm)
        return pltpu.make_async_copy(yout.at[sl], y_hbm.at[rows], out_sem.at[sl])

    @pl.when(e == 0)
    def _():
        in_copy(0).start()

    @pl.when(nb_ref[e] > 0)
    def _():
        for c in range(2 * D_FF // GU_TILE):
            cols = slice(c * GU_TILE, (c + 1) * GU_TILE)
            wgu_b[:, cols] = _dot(wgu_ref[0, :, cols].astype(bf16), pm_ref[...]).astype(bf16)
        for c in range(D_FF // LANES):
            rows = slice(c * LANES, (c + 1) * LANES)
            wdn_b[rows, :] = wdn_ref[0, rows, :].astype(bf16)

        def block(b, carry):
            sl = lax.rem(b, 2)
            in_copy(b).wait()

            @pl.when(b + 1 < n_used)
            def _():
                in_copy(b + 1).start()

            @pl.when(b >= 2)
            def _():
                out_copy(b - 2).wait()

            _expert_block(xin.at[sl], bgu_ref, bdn_ref, yout.at[sl], wgu_b, wdn_b)
            out_copy(b).start()
            return carry
        lax.fori_loop(eb_ref[e], eb_ref[e] + nb_ref[e], block, 0)

    @pl.when(e == pl.num_programs(0) - 1)
    def _():
        @pl.when(n_used >= 2)
        def _():
            out_copy(n_used - 2).wait()
        out_copy(n_used - 1).wait()
        yout[0] = jnp.zeros(yout.shape[1:], u32)

        def unused(b, carry):
            rows = pl.ds(pl.multiple_of(b * tm, tm), tm)
            cp = pltpu.make_async_copy(yout.at[0], y_hbm.at[rows], out_sem.at[0])
            cp.start()
            cp.wait()
            return carry
        lax.fori_loop(n_used, y_hbm.shape[0] // tm, unused, 0)


def _expert_block(x_ref, bgu_ref, bdn_ref, y_ref, wgu_b, wdn_b):
    gu = _dot(_unpack_pairs(x_ref[...]), wgu_b[...]) + bgu_ref[0]
    acts = []
    for c in range(2 * D_FF // GU_TILE):
        g = jnp.minimum(gu[:, c * GU_TILE:c * GU_TILE + LANES], SWIGLU_LIMIT)
        lin = jnp.clip(gu[:, c * GU_TILE + LANES:(c + 1) * GU_TILE], -SWIGLU_LIMIT, SWIGLU_LIMIT)
        acts.append((g * jax.nn.sigmoid(SWIGLU_ALPHA * g) * (lin + 1.0)).astype(bf16))
    act = jnp.concatenate(acts, axis=-1)
    y = _dot(act, wdn_b[...]) + bdn_ref[0]
    y_ref[...] = _pack_pairs(y.astype(bf16).astype(f32))


def _gate_up_perm():
    j = np.arange(GU_TILE)
    p = np.zeros((GU_TILE, GU_TILE), np.float32)
    p[np.where(j < LANES, 2 * j, 2 * (j - LANES) + 1), j] = 1.0
    return jnp.asarray(p, bf16)


def _expert_mlp(plan, xs, w_gu, b_gu, w_dn, b_dn):
    n_rows = xs.shape[0]
    D = D_MODEL
    tm = EXPERT_TILE
    per_e = lambda e, *_: (e, 0, 0)
    grid_spec = pltpu.PrefetchScalarGridSpec(
        num_scalar_prefetch=3,
        grid=(N_EXPERTS,),
        in_specs=[pl.BlockSpec((1, D, 2 * D_FF), per_e),
                  pl.BlockSpec((1, 1, 2 * D_FF), per_e),
                  pl.BlockSpec((1, D_FF, D), per_e),
                  pl.BlockSpec((1, 1, D), per_e),
                  pl.BlockSpec((GU_TILE, GU_TILE), lambda e, *_: (0, 0)),
                  pl.BlockSpec(memory_space=pl.ANY)],
        out_specs=pl.BlockSpec(memory_space=pl.ANY),
        scratch_shapes=[pltpu.VMEM((D, 2 * D_FF), bf16), pltpu.VMEM((D_FF, D), bf16),
                        pltpu.VMEM((2, tm, D // 2), u32), pltpu.VMEM((2, tm, D // 2), u32),
                        pltpu.SemaphoreType.DMA((2,)), pltpu.SemaphoreType.DMA((2,))],
    )
    return pl.pallas_call(
        _expert_kernel,
        grid_spec=grid_spec,
        out_shape=jax.ShapeDtypeStruct((n_rows, D // 2), u32),
        compiler_params=pltpu.CompilerParams(
            dimension_semantics=("arbitrary",), vmem_limit_bytes=VMEM_LIMIT),
        name="expert_mlp",
    )(plan["eblk"], plan["nblk"], plan["n_used"], w_gu, b_gu, w_dn, b_dn, _gate_up_perm(), xs)


def _combine_kernel(coff_s, rows_s, gbase_s, gtot_s, x_ref, pos_ref, gate_ref, y_hbm, o_ref, ybuf, sem,
                    *, n_groups):
    g = pl.program_id(0)
    slot = lax.rem(g, 2)

    def cell_copy(sl, i):
        n = pl.multiple_of(rows_s[i], SUBLANES)
        loc = pl.multiple_of(coff_s[i], SUBLANES)
        src = pl.multiple_of(gbase_s[i], SUBLANES)
        return n, pltpu.make_async_copy(y_hbm.at[pl.ds(src, n)], ybuf.at[sl, pl.ds(loc, n)],
                                        sem.at[sl])

    def start_cell(sl, i):
        n, cp = cell_copy(sl, i)
        pl.when(n > 0)(cp.start)

    def wait_group(sl, gg):
        n = pl.multiple_of(gtot_s[gg], SUBLANES)
        pltpu.make_async_copy(y_hbm.at[pl.ds(0, n)], ybuf.at[sl, pl.ds(0, n)], sem.at[sl]).wait()

    @pl.when(g == 0)
    def _():
        ybuf[...] = jnp.zeros(ybuf.shape, u32)
        _for_cells(g, functools.partial(start_cell, slot))

    @pl.when(g + 1 < n_groups)
    def _():
        _for_cells(g + 1, functools.partial(start_cell, 1 - slot))

    wait_group(slot, g)

    riota = lax.broadcasted_iota(jnp.int32, (GROUP, GROUP_ROWS), 1)
    pt = jnp.zeros((GROUP, GROUP_ROWS), f32)
    for kk in range(TOP_K):
        pk = pos_ref[:, kk:kk + 1].astype(jnp.int32)
        pt = jnp.where(riota == pk, gate_ref[:, kk:kk + 1], pt)
    o_ref[...] = x_ref[...] + _dot(pt.astype(bf16), _unpack_pairs(ybuf[slot]))


def _combine(plan, x2, pos, gates, y_rows):
    n_tok, D = x2.shape
    n_groups = n_tok // GROUP
    tok = lambda w: pl.BlockSpec((GROUP, w), lambda g, *_: (g, 0))
    grid_spec = pltpu.PrefetchScalarGridSpec(
        num_scalar_prefetch=4,
        grid=(n_groups,),
        in_specs=[tok(D), tok(LANES), tok(LANES), pl.BlockSpec(memory_space=pl.ANY)],
        out_specs=tok(D),
        scratch_shapes=[pltpu.VMEM((2, GROUP_ROWS, D // 2), u32), pltpu.SemaphoreType.DMA((2,))],
    )
    return pl.pallas_call(
        functools.partial(_combine_kernel, n_groups=n_groups),
        grid_spec=grid_spec,
        out_shape=jax.ShapeDtypeStruct((n_tok, D), f32),
        compiler_params=pltpu.CompilerParams(
            dimension_semantics=("arbitrary",), vmem_limit_bytes=VMEM_LIMIT),
        name="combine",
    )(plan["coff"], plan["rows"], plan["gbase"], plan["gtot"], x2, pos, gates, y_rows)


def _routing_plan(cnt, n_tok):
    tm = EXPERT_TILE
    n_groups = n_tok // GROUP
    cnt = cnt.reshape(n_groups, LANES)[:, :N_EXPERTS].astype(jnp.int32)
    rows = (cnt + SUBLANES - 1) // SUBLANES * SUBLANES
    coff = jnp.cumsum(rows, axis=1) - rows
    tot = jnp.sum(rows, axis=0)
    padded = (tot + tm - 1) // tm * tm
    pend = jnp.cumsum(padded)
    ebase = pend - padded
    gbase = ebase[None, :] + jnp.cumsum(rows, axis=0) - rows
    flat = lambda a: a.reshape(-1).astype(jnp.int32)
    return dict(
        coff=flat(coff), rows=flat(rows), gbase=flat(gbase), gtot=flat(jnp.sum(rows, axis=1)),
        tailoff=flat(ebase + tot), tailn=flat(padded - tot),
        eblk=flat(ebase // tm), nblk=flat(padded // tm), n_used=flat(pend[-1] // tm),
        coffv=jnp.zeros((n_groups, 1, LANES), f32).at[:, 0, :N_EXPERTS].set(coff.astype(f32)))


def _max_rows(n_tok):
    tm = EXPERT_TILE
    worst = (n_tok * TOP_K + (n_tok // GROUP) * N_EXPERTS * (SUBLANES - 1)
             + N_EXPERTS * (tm - SUBLANES))
    return (worst + tm - 1) // tm * tm


def _moe(x2, hf, idx, gates, cnt, w_gate_up, b_gate_up, w_down, b_down):
    n_tok, D = x2.shape
    plan = _routing_plan(cnt, n_tok)
    xs, pos = _dispatch(plan, hf, idx, _max_rows(n_tok))
    b_gu = (b_gate_up.reshape(N_EXPERTS, 2 * D_FF // GU_TILE, LANES, 2).transpose(0, 1, 3, 2)
            .reshape(N_EXPERTS, 1, 2 * D_FF))
    y_rows = _expert_mlp(plan, xs, w_gate_up, b_gu, w_down, b_down.reshape(N_EXPERTS, 1, D))
    return _combine(plan, x2, pos, gates, y_rows)


BIAS_ROW = (2 * Q_TILE + LEFT_CHUNKS * CHUNK + LANES - 2) // LANES * LANES


def _band_bias_kernel(r_ref, o_ref):
    x = jnp.broadcast_to(r_ref[0], (Q_TILE, BIAS_ROW))
    x = pltpu.roll(x, BIAS_ROW - (Q_TILE - 1), axis=1, stride=1, stride_axis=0)
    qc = lax.broadcasted_iota(jnp.int32, (Q_TILE, K_BAND), 0) // CHUNK
    kc = lax.broadcasted_iota(jnp.int32, (Q_TILE, K_BAND), 1) // CHUNK
    first_chunk = LEFT_CHUNKS - pl.program_id(0) * (Q_TILE // CHUNK)
    visible = (kc >= qc) & (kc <= qc + LEFT_CHUNKS) & (kc >= first_chunk)
    o_ref[0, 0] = jnp.where(visible, x[:, :K_BAND] * LOG2E, NEG_INF)


def _band_bias(rel_bias):
    H = rel_bias.shape[0]
    d_max = Q_TILE - 1 + LEFT_CHUNKS * CHUNK
    n_far = d_max - MAX_REL + 1
    r = jnp.concatenate(
        [jnp.broadcast_to(rel_bias[:, 2 * MAX_REL:], (H, n_far)),
         rel_bias[:, MAX_REL - (Q_TILE - 1):2 * MAX_REL][:, ::-1],
         jnp.zeros((H, BIAS_ROW - (2 * Q_TILE - 1 + LEFT_CHUNKS * CHUNK)), rel_bias.dtype)], axis=1)
    return pl.pallas_call(
        _band_bias_kernel,
        grid=(START_TILES + 1, H),
        in_specs=[pl.BlockSpec((1, 1, BIAS_ROW), lambda t, h: (h, 0, 0))],
        out_specs=pl.BlockSpec((1, 1, Q_TILE, K_BAND), lambda t, h: (t, h, 0, 0)),
        out_shape=jax.ShapeDtypeStruct((START_TILES + 1, H, Q_TILE, K_BAND), f32),
        name="band_bias",
    )(r.astype(f32).reshape(H, 1, BIAS_ROW))


def _layer(x, mem, norm_mix_g, w_in, q_norm_g, k_norm_g, rel_bias, conv_w, out_norm_a_g,
           out_norm_b_g, w_out, norm_xattn_g, norm_mem_g, w_xq, w_xkv, xq_norm_g, xk_norm_g,
           w_xo, norm_ffn_g, w_router, b_router, w_gate_up, b_gate_up, w_down, b_down):
    B, S, D = x.shape
    n_tok = B * S
    row = lambda a: a.reshape(1, -1).astype(f32)

    qg = row(jnp.tile(q_norm_g, ATT_HEADS) * (ATT_HEAD_DIM ** -0.5 * LOG2E))
    kg = row(jnp.tile(k_norm_g, ATT_HEADS))
    head_ones = jnp.asarray(np.kron(np.eye(ATT_HEADS), np.ones((ATT_HEAD_DIM, ATT_HEAD_DIM))), bf16)
    q, k, v, yb = _mix_in(x, row(norm_mix_g), w_in.astype(bf16), qg, kg, head_ones,
                          conv_w.astype(f32), row(out_norm_b_g))
    x1 = _attn_out(q, k, v, _band_bias(rel_bias), yb, x, w_out.astype(bf16), row(out_norm_a_g))

    kx, vx = _mem_kv(mem, row(norm_mem_g), w_xkv.astype(bf16), row(xk_norm_g))
    wr = jnp.zeros((D, LANES), f32).at[:, :N_EXPERTS].set(w_router)
    wr_hi = wr.astype(bf16)
    wr_lo = (wr - wr_hi.astype(f32)).astype(bf16)
    br = jnp.full((1, LANES), -jnp.inf, f32).at[0, :N_EXPERTS].set(b_router)
    x2, hf, idx, gates, cnt = _xattn_router(
        x1, row(norm_xattn_g), w_xq.astype(bf16), row(xq_norm_g * (X_HEAD_DIM ** -0.5 * LOG2E)),
        kx, vx,
        w_xo.astype(bf16), row(norm_ffn_g), wr_hi, wr_lo, br)

    out = _moe(x2.reshape(n_tok, D), hf.reshape(n_tok, D), idx.reshape(n_tok, LANES),
               gates.reshape(n_tok, LANES), cnt, w_gate_up, b_gate_up, w_down, b_down)
    return out.reshape(B, S, D)


def kernel(x, mem, norm_mix_g, w_in, q_norm_g, k_norm_g, rel_bias, conv_w, out_norm_a_g,
           out_norm_b_g, w_out, norm_xattn_g, norm_mem_g, w_xq, w_xkv, xq_norm_g, xk_norm_g,
           w_xo, norm_ffn_g, w_router, b_router, w_gate_up, b_gate_up, w_down, b_down):
    depth = norm_mix_g.shape[0]
    for l in range(depth):
        x = _layer(x, mem, norm_mix_g[l], w_in[l], q_norm_g[l], k_norm_g[l], rel_bias[l],
                   conv_w[l], out_norm_a_g[l], out_norm_b_g[l], w_out[l], norm_xattn_g[l],
                   norm_mem_g[l], w_xq[l], w_xkv[l], xq_norm_g[l], xk_norm_g[l], w_xo[l],
                   norm_ffn_g[l], w_router[l], b_router[l], w_gate_up[l], b_gate_up[l],
                   w_down[l], b_down[l])
    return x
```

```python
import functools

import jax
import jax.numpy as jnp
import numpy as np
from jax import lax
from jax.experimental import pallas as pl
from jax.experimental.pallas import tpu as pltpu

D_MODEL = 1024
CHUNK = 64
LEFT_CHUNKS = 8
N_MEM = 256
ATT_HEADS = 8
ATT_HEAD_DIM = 64
D_ATT = ATT_HEADS * ATT_HEAD_DIM
D_CONV = 512
MAX_REL = 256
X_HEADS = 4
X_HEAD_DIM = 128
D_X = X_HEADS * X_HEAD_DIM
N_EXPERTS = 32
TOP_K = 4
D_FF = D_MODEL
SWIGLU_LIMIT = 7.0
SWIGLU_ALPHA = 1.702
EPS = 1e-6
NEG_INF = -1e30

LANES = 128
SEQ_TILE = 1024
Q_TILE = 256
K_BAND = Q_TILE + LEFT_CHUNKS * CHUNK
START_TILES = LEFT_CHUNKS * CHUNK // Q_TILE
LOG2E = 1.4426950408889634
HEAD_ROUND = 4
EXPERT_TILE = 512
GU_TILE = 2 * LANES
GROUP = 256
STEP_GROUPS = 2
SUBLANES = 8
GROUP_ROWS = 1280
VMEM_LIMIT = 56 * 1024 * 1024

bf16 = jnp.bfloat16
f32 = jnp.float32


def _rms(x, g):
    return x * lax.rsqrt(jnp.mean(x * x, axis=-1, keepdims=True) + EPS) * g


def _dot(a, b):
    return jnp.dot(a, b, preferred_element_type=f32)


def _dot_nt(a, b):
    return lax.dot_general(a, b, (((1,), (1,)), ((), ())), preferred_element_type=f32)


def _mix_in_kernel(x_ref, g_ref, w_ref, qg_ref, kg_ref, hm_ref, cw_ref, gb_ref,
                   q_ref, k_ref, v_ref, yb_ref, ub_ref):
    ts = x_ref.shape[1]

    @pl.when(pl.program_id(1) == 0)
    def _():
        ub_ref[0:8, :] = jnp.zeros((8, D_CONV), f32)

    hb = _rms(x_ref[0], g_ref[...]).astype(bf16)

    def head_norm(t, gain):
        ms = _dot((t * t).astype(bf16), hm_ref[...]) * (1.0 / ATT_HEAD_DIM)
        return t * lax.rsqrt(ms + EPS) * gain

    q = _dot(hb, w_ref[:, 0:D_ATT])
    q_ref[0] = head_norm(q, qg_ref[...]).astype(bf16)
    k = _dot(hb, w_ref[:, D_ATT:2 * D_ATT])
    k_ref[0] = head_norm(k, kg_ref[...]).astype(bf16)
    v_ref[0] = _dot(hb, w_ref[:, 2 * D_ATT:3 * D_ATT]).astype(bf16)

    o = 3 * D_ATT
    bg = _dot(hb, w_ref[:, o:o + D_CONV])
    cg = _dot(hb, w_ref[:, o + D_CONV:o + 2 * D_CONV])
    xv = _dot(hb, w_ref[:, o + 2 * D_CONV:o + 3 * D_CONV])
    u = cg * xv
    ub_ref[8:8 + ts, :] = u
    conv = (cw_ref[0:1, :] * ub_ref[6:6 + ts, :] + cw_ref[1:2, :] * ub_ref[7:7 + ts, :]
            + cw_ref[2:3, :] * u)
    ub_ref[0:8, :] = u[ts - 8:ts, :]
    yb_ref[0] = _rms(bg * conv, gb_ref[...]).astype(bf16)


def _mix_in(x, g, w_in, qg, kg, hm, cw, gb):
    B, S, D = x.shape
    ts = SEQ_TILE
    full = lambda shape: pl.BlockSpec(shape, lambda b, j: (0,) * len(shape))
    tile = lambda w: pl.BlockSpec((1, ts, w), lambda b, j: (b, j, 0))
    out = jax.ShapeDtypeStruct((B, S, D_ATT), bf16)
    return pl.pallas_call(
        _mix_in_kernel,
        grid=(B, S // ts),
        in_specs=[tile(D), full((1, D)), full(w_in.shape), full((1, D_ATT)), full((1, D_ATT)),
                  full((D_ATT, D_ATT)), full((3, D_CONV)), full((1, D_CONV))],
        out_specs=[tile(D_ATT)] * 4,
        out_shape=[out] * 4,
        scratch_shapes=[pltpu.VMEM((ts + 8, D_CONV), f32)],
        compiler_params=pltpu.CompilerParams(
            dimension_semantics=("arbitrary", "arbitrary"), vmem_limit_bytes=VMEM_LIMIT),
        name="mix_in",
    )(x, g, w_in, qg, kg, hm, cw, gb)


def _attn_out_kernel(q_ref, k_ref, v_ref, bias_ref, yb_ref, x_ref, w_ref, ga_ref,
                     o_ref, kb_ref, vb_ref):
    tq = q_ref.shape[1]
    n_kc = K_BAND // CHUNK
    c0 = pl.program_id(1) * (tq // CHUNK)
    for kc in range(n_kc):
        src = pl.multiple_of(jnp.maximum(c0 - LEFT_CHUNKS + kc, 0) * CHUNK, CHUNK)
        kb_ref[kc * CHUNK:(kc + 1) * CHUNK, :] = k_ref[0, pl.ds(src, CHUNK), :]
        vb_ref[kc * CHUNK:(kc + 1) * CHUNK, :] = v_ref[0, pl.ds(src, CHUNK), :]

    lane = lax.broadcasted_iota(jnp.int32, (1, LANES), 1)
    low = lane < ATT_HEAD_DIM

    outs = []
    for h0 in range(0, ATT_HEADS, HEAD_ROUND):
        heads = range(h0, h0 + HEAD_ROUND)
        scores, probs, denoms = {}, {}, {}
        for h in heads:
            sl = slice((h // 2) * LANES, (h // 2 + 1) * LANES)
            qp = q_ref[0, :, sl]
            keep = low if h % 2 == 0 else jnp.logical_not(low)
            scores[h] = _dot_nt(jnp.where(keep, qp, jnp.zeros_like(qp)), kb_ref[:, sl])
        for h in heads:
            s = scores[h] + bias_ref[0, h]
            p = jnp.exp2(s - jnp.max(s, axis=-1, keepdims=True))
            denoms[h] = jnp.sum(p, axis=-1, keepdims=True)
            probs[h] = p.astype(bf16)
        for h in heads:
            sl = slice((h // 2) * LANES, (h // 2 + 1) * LANES)
            outs.append(_dot(probs[h], vb_ref[:, sl]) / denoms[h])
    ya = jnp.concatenate([jnp.where(low, outs[2 * hp], outs[2 * hp + 1])
                          for hp in range(ATT_HEADS // 2)], axis=-1)
    yan = _rms(ya, ga_ref[...]).astype(bf16)
    o_ref[0] = (x_ref[0] + _dot(yan, w_ref[0:D_ATT, :]) + _dot(yb_ref[0], w_ref[D_ATT:, :]))


def _attn_out(q, k, v, bias, yb, x, w_out, ga):
    B, S, D = x.shape
    tq = Q_TILE
    full = lambda shape: pl.BlockSpec(shape, lambda b, j: (0,) * len(shape))
    tile = lambda w: pl.BlockSpec((1, tq, w), lambda b, j: (b, j, 0))
    seq = pl.BlockSpec((1, S, D_ATT), lambda b, j: (b, 0, 0))
    return pl.pallas_call(
        _attn_out_kernel,
        grid=(B, S // tq),
        in_specs=[tile(D_ATT), seq, seq,
                  pl.BlockSpec((1,) + bias.shape[1:], lambda b, j: (jnp.minimum(j, START_TILES), 0, 0, 0)),
                  tile(D_CONV), tile(D),
                  full(w_out.shape), full((1, D_ATT))],
        out_specs=tile(D),
        out_shape=jax.ShapeDtypeStruct((B, S, D), f32),
        scratch_shapes=[pltpu.VMEM((K_BAND, D_ATT), bf16), pltpu.VMEM((K_BAND, D_ATT), bf16)],
        compiler_params=pltpu.CompilerParams(
            dimension_semantics=("parallel", "parallel"), vmem_limit_bytes=VMEM_LIMIT),
        name="attn_out",
    )(q, k, v, bias, yb, x, w_out, ga)


def _mem_kv_kernel(m_ref, g_ref, w_ref, kg_ref, k_ref, v_ref):
    hm = _rms(m_ref[0], g_ref[...]).astype(bf16)
    kv = _dot(hm, w_ref[...])
    for h in range(X_HEADS):
        sl = slice(h * X_HEAD_DIM, (h + 1) * X_HEAD_DIM)
        k_ref[0, :, sl] = _rms(kv[:, sl], kg_ref[...]).astype(bf16)
    v_ref[0] = kv[:, D_X:].astype(bf16)


def _mem_kv(mem, g, w_xkv, kg):
    B = mem.shape[0]
    full = lambda shape: pl.BlockSpec(shape, lambda b: (0,) * len(shape))
    out = jax.ShapeDtypeStruct((B, N_MEM, D_X), bf16)
    return pl.pallas_call(
        _mem_kv_kernel,
        grid=(B,),
        in_specs=[pl.BlockSpec((1, N_MEM, D_MODEL), lambda b: (b, 0, 0)), full((1, D_MODEL)),
                  full(w_xkv.shape), full((1, X_HEAD_DIM))],
        out_specs=[pl.BlockSpec((1, N_MEM, D_X), lambda b: (b, 0, 0))] * 2,
        out_shape=[out, out],
        compiler_params=pltpu.CompilerParams(
            dimension_semantics=("parallel",), vmem_limit_bytes=VMEM_LIMIT),
        name="mem_kv",
    )(mem, g, w_xkv, kg)


def _xattn_router_kernel(x_ref, gx_ref, wq_ref, qg_ref, k_ref, v_ref, wo_ref, gf_ref,
                         wrh_ref, wrl_ref, br_ref,
                         x2_ref, hf_ref, idx_ref, gate_ref, cnt_ref):
    x1 = x_ref[0]
    hb = _rms(x1, gx_ref[...]).astype(bf16)
    q = _dot(hb, wq_ref[...])
    hsl = [slice(h * X_HEAD_DIM, (h + 1) * X_HEAD_DIM) for h in range(X_HEADS)]
    scores = [_dot_nt(_rms(q[:, sl], qg_ref[...]).astype(bf16), k_ref[0, :, sl]) for sl in hsl]
    probs, denoms = [], []
    for s in scores:
        p = jnp.exp2(s - jnp.max(s, axis=-1, keepdims=True))
        denoms.append(jnp.sum(p, axis=-1, keepdims=True))
        probs.append(p.astype(bf16))
    heads = [(_dot(p, v_ref[0, :, sl]) / l).astype(bf16) for p, l, sl in zip(probs, denoms, hsl)]
    o = jnp.concatenate(heads, axis=-1)
    x2 = x1 + _dot(o, wo_ref[...])
    x2_ref[0] = x2

    hf = _rms(x2, gf_ref[...])
    hf_hi = hf.astype(bf16)
    hf_ref[0] = hf_hi
    hf_lo = (hf - hf_hi.astype(f32)).astype(bf16)
    logits = (_dot(hf_hi, wrh_ref[...]) + _dot(hf_lo, wrh_ref[...]) + _dot(hf_hi, wrl_ref[...])
              + br_ref[...])
    lane = lax.broadcasted_iota(jnp.int32, logits.shape, 1)
    vals, idxs = [], []
    for _ in range(TOP_K):
        m = jnp.max(logits, axis=-1, keepdims=True)
        i = jnp.min(jnp.where(logits == m, lane, LANES), axis=-1, keepdims=True)
        logits = jnp.where(lane == i, -jnp.inf, logits)
        vals.append(m)
        idxs.append(i)
    es = [jnp.exp(v - vals[0]) for v in vals]
    tot = es[0] + es[1] + es[2] + es[3]
    idx_out = jnp.zeros(logits.shape, jnp.int32)
    gate_out = jnp.zeros(logits.shape, f32)
    sel = jnp.zeros(logits.shape, f32)
    for kk in range(TOP_K):
        idx_out = jnp.where(lane == kk, idxs[kk], idx_out)
        gate_out = jnp.where(lane == kk, es[kk] / tot, gate_out)
        sel = jnp.where(lane == idxs[kk], 1.0, sel)
    idx_ref[0] = idx_out
    gate_ref[0] = gate_out
    for j in range(cnt_ref.shape[0]):
        cnt_ref[j] = jnp.sum(sel[j * GROUP:(j + 1) * GROUP], axis=0, keepdims=True)


def _xattn_router(x1, gx, w_xq, qg, kx, vx, w_xo, gf, wr_hi, wr_lo, br):
    B, S, D = x1.shape
    ts = SEQ_TILE
    full = lambda shape: pl.BlockSpec(shape, lambda b, j: (0,) * len(shape))
    tile = lambda w: pl.BlockSpec((1, ts, w), lambda b, j: (b, j, 0))
    mem = pl.BlockSpec((1, N_MEM, D_X), lambda b, j: (b, 0, 0))
    return pl.pallas_call(
        _xattn_router_kernel,
        grid=(B, S // ts),
        in_specs=[tile(D), full((1, D)), full(w_xq.shape), full((1, X_HEAD_DIM)), mem, mem,
                  full(w_xo.shape), full((1, D)), full(wr_hi.shape), full(wr_lo.shape),
                  full((1, LANES))],
        out_specs=[tile(D), tile(D), tile(LANES), tile(LANES),
                   pl.BlockSpec((ts // GROUP, 1, LANES), lambda b, j: (b * (S // ts) + j, 0, 0))],
        out_shape=[jax.ShapeDtypeStruct((B, S, D), f32), jax.ShapeDtypeStruct((B, S, D), bf16),
                   jax.ShapeDtypeStruct((B, S, LANES), jnp.int32),
                   jax.ShapeDtypeStruct((B, S, LANES), f32),
                   jax.ShapeDtypeStruct((B * S // GROUP, 1, LANES), f32)],
        compiler_params=pltpu.CompilerParams(
            dimension_semantics=("parallel", "parallel"), vmem_limit_bytes=VMEM_LIMIT),
        name="xattn_router",
    )(x1, gx, w_xq, qg, kx, vx, w_xo, gf, wr_hi, wr_lo, br)


u32 = jnp.uint32
HIGH_HALF = 0xFFFF0000


def _pack_pairs(x):
    w = x.shape[1] // 2
    lo = pltpu.bitcast(x[:, :w], u32) >> 16
    hi = pltpu.bitcast(x[:, w:], u32) & u32(HIGH_HALF)
    return lo | hi


def _unpack_pairs(u):
    lo = pltpu.bitcast(u << 16, f32)
    hi = pltpu.bitcast(u & u32(HIGH_HALF), f32)
    return jnp.concatenate([lo, hi], axis=-1).astype(bf16)


def _for_cells(g, fn):
    def body(e, carry):
        fn(g * N_EXPERTS + e)
        return carry
    lax.fori_loop(0, N_EXPERTS, body, 0)


def _group_positions(idxv, coffv, ltri):
    lane = lax.broadcasted_iota(jnp.int32, idxv.shape, 1)
    hots = [lane == idxv[:, kk:kk + 1] for kk in range(TOP_K)]
    multi = jnp.zeros(idxv.shape, f32)
    for h in hots:
        multi = jnp.where(h, 1.0, multi)
    rank = _dot(ltri, multi.astype(bf16))
    posmat = coffv + rank
    return [jnp.sum(jnp.where(h, posmat, 0.0), axis=-1, keepdims=True) for h in hots]


def _dispatch_kernel(coff_s, rows_s, gbase_s, gtot_s, toff_s, tn_s, nu_s,
                     hf_ref, idx_ref, coffv_ref, ltri_ref,
                     xs_hbm, pos_ref, xbuf, zbuf, sem, zsem, *, n_steps):
    step = pl.program_id(0)
    places = range(STEP_GROUPS)
    group = lambda i: step * STEP_GROUPS + i
    tok = lambda i: slice(i * GROUP, (i + 1) * GROUP)

    def cell_copy(sl, i):
        n = pl.multiple_of(rows_s[i], SUBLANES)
        src = pl.multiple_of(coff_s[i], SUBLANES)
        dst = pl.multiple_of(gbase_s[i], SUBLANES)
        return n, pltpu.make_async_copy(xbuf.at[sl, pl.ds(src, n)], xs_hbm.at[pl.ds(dst, n)],
                                        sem.at[sl])

    def start_cell(sl, i):
        n, cp = cell_copy(sl, i)
        pl.when(n > 0)(cp.start)

    def wait_group(sl, gg):
        n = pl.multiple_of(gtot_s[gg], SUBLANES)
        pltpu.make_async_copy(xbuf.at[sl, pl.ds(0, n)], xs_hbm.at[pl.ds(0, n)], sem.at[sl]).wait()

    @pl.when(step >= 1)
    def _():
        for i in places:
            wait_group(i, group(i) - STEP_GROUPS)

    pos = [_group_positions(idx_ref[tok(i), :], coffv_ref[i], ltri_ref[...]) for i in places]
    lane = lax.broadcasted_iota(jnp.int32, (GROUP, LANES), 1)
    riota = lax.broadcasted_iota(jnp.int32, (GROUP, GROUP_ROWS), 1)
    pts = []
    for i in places:
        pt = jnp.zeros((GROUP, GROUP_ROWS), f32)
        pos_out = jnp.zeros((GROUP, LANES), f32)
        for kk in range(TOP_K):
            pt = jnp.where(riota == pos[i][kk].astype(jnp.int32), 1.0, pt)
            pos_out = jnp.where(lane == kk, pos[i][kk], pos_out)
        pos_ref[tok(i), :] = pos_out
        pts.append(pt.astype(bf16))
    rows = [lax.dot_general(pts[i], hf_ref[tok(i), :], (((0,), (0,)), ((), ())),
                            preferred_element_type=f32) for i in places]
    for i in places:
        xbuf[i] = _pack_pairs(rows[i])
    for i in places:
        _for_cells(group(i), functools.partial(start_cell, i))

    @pl.when(step == n_steps - 1)
    def _():
        for i in places:
            wait_group(i, group(i))
        zbuf[...] = jnp.zeros(zbuf.shape, u32)

        def tail_copy(e):
            n = pl.multiple_of(tn_s[e], SUBLANES)
            dst = pl.multiple_of(toff_s[e], SUBLANES)
            return n, pltpu.make_async_copy(zbuf.at[pl.ds(0, n)], xs_hbm.at[pl.ds(dst, n)], zsem)

        def tail(e, carry):
            n, cp = tail_copy(e)

            @pl.when(n > 0)
            def _():
                cp.start()
                cp.wait()
            return carry
        lax.fori_loop(0, N_EXPERTS, tail, 0)

        def unused(b, carry):
            dst = pl.multiple_of(b * EXPERT_TILE, EXPERT_TILE)
            cp = pltpu.make_async_copy(zbuf, xs_hbm.at[pl.ds(dst, EXPERT_TILE)], zsem)
            cp.start()
            cp.wait()
            return carry
        lax.fori_loop(nu_s[0], xs_hbm.shape[0] // EXPERT_TILE, unused, 0)


def _dispatch(plan, hf, idx, n_rows):
    n_tok, D = hf.shape
    n_steps = n_tok // (GROUP * STEP_GROUPS)
    assert n_steps * GROUP * STEP_GROUPS == n_tok
    ltri = jnp.asarray(np.tril(np.ones((GROUP, GROUP)), -1), bf16)
    tok = lambda w: pl.BlockSpec((STEP_GROUPS * GROUP, w), lambda s, *_: (s, 0))
    grid_spec = pltpu.PrefetchScalarGridSpec(
        num_scalar_prefetch=7,
        grid=(n_steps,),
        in_specs=[tok(D), tok(LANES),
                  pl.BlockSpec((STEP_GROUPS, 1, LANES), lambda s, *_: (s, 0, 0)),
                  pl.BlockSpec((GROUP, GROUP), lambda s, *_: (0, 0))],
        out_specs=[pl.BlockSpec(memory_space=pl.ANY), tok(LANES)],
        scratch_shapes=[pltpu.VMEM((STEP_GROUPS, GROUP_ROWS, D // 2), u32),
                        pltpu.VMEM((EXPERT_TILE, D // 2), u32),
                        pltpu.SemaphoreType.DMA((STEP_GROUPS,)), pltpu.SemaphoreType.DMA(())],
    )
    return pl.pallas_call(
        functools.partial(_dispatch_kernel, n_steps=n_steps),
        grid_spec=grid_spec,
        out_shape=[jax.ShapeDtypeStruct((n_rows, D // 2), u32),
                   jax.ShapeDtypeStruct((n_tok, LANES), f32)],
        compiler_params=pltpu.CompilerParams(
            dimension_semantics=("arbitrary",), vmem_limit_bytes=VMEM_LIMIT),
        name="dispatch",
    )(plan["coff"], plan["rows"], plan["gbase"], plan["gtot"], plan["tailoff"], plan["tailn"],
      plan["n_used"], hf, idx, plan["coffv"], ltri)


def _expert_kernel(eb_ref, nb_ref, nu_ref, wgu_ref, bgu_ref, wdn_ref, bdn_ref, pm_ref, xs_hbm,
                   y_hbm, wgu_b, wdn_b, xin, yout, in_sem, out_sem):
    e = pl.program_id(0)
    tm = xin.shape[1]
    n_used = nu_ref[0]

    def in_copy(b):
        sl = lax.rem(b, 2)
        rows = pl.ds(pl.multiple_of(b * tm, tm), tm)
        return pltpu.make_async_copy(xs_hbm.at[rows], xin.at[sl], in_sem.at[sl])

    def out_copy(b):
        sl = lax.rem(b, 2)
        rows = pl.ds(pl.multiple_of(b * tm, tm), tm)
        return pltpu.make_async_copy(yout.at[sl], y_hbm.at[rows], out_sem.at[sl])

    @pl.when(e == 0)
    def _():
        in_copy(0).start()

    @pl.when(nb_ref[e] > 0)
    def _():
        for c in range(2 * D_FF // GU_TILE):
            cols = slice(c * GU_TILE, (c + 1) * GU_TILE)
            wgu_b[:, cols] = _dot(wgu_ref[0, :, cols].astype(bf16), pm_ref[...]).astype(bf16)
        for c in range(D_FF // LANES):
            rows = slice(c * LANES, (c + 1) * LANES)
            wdn_b[rows, :] = wdn_ref[0, rows, :].astype(bf16)

        def block(b, carry):
            sl = lax.rem(b, 2)
            in_copy(b).wait()

            @pl.when(b + 1 < n_used)
            def _():
                in_copy(b + 1).start()

            @pl.when(b >= 2)
            def _():
                out_copy(b - 2).wait()

            _expert_block(xin.at[sl], bgu_ref, bdn_ref, yout.at[sl], wgu_b, wdn_b)
            out_copy(b).start()
            return carry
        lax.fori_loop(eb_ref[e], eb_ref[e] + nb_ref[e], block, 0)

    @pl.when(e == pl.num_programs(0) - 1)
    def _():
        @pl.when(n_used >= 2)
        def _():
            out_copy(n_used - 2).wait()
        out_copy(n_used - 1).wait()
        yout[0] = jnp.zeros(yout.shape[1:], u32)

        def unused(b, carry):
            rows = pl.ds(pl.multiple_of(b * tm, tm), tm)
            cp = pltpu.make_async_copy(yout.at[0], y_hbm.at[rows], out_sem.at[0])
            cp.start()
            cp.wait()
            return carry
        lax.fori_loop(n_used, y_hbm.shape[0] // tm, unused, 0)


def _expert_block(x_ref, bgu_ref, bdn_ref, y_ref, wgu_b, wdn_b):
    gu = _dot(_unpack_pairs(x_ref[...]), wgu_b[...]) + bgu_ref[0]
    acts = []
    for c in range(2 * D_FF // GU_TILE):
        g = jnp.minimum(gu[:, c * GU_TILE:c * GU_TILE + LANES], SWIGLU_LIMIT)
        lin = jnp.clip(gu[:, c * GU_TILE + LANES:(c + 1) * GU_TILE], -SWIGLU_LIMIT, SWIGLU_LIMIT)
        acts.append((g * jax.nn.sigmoid(SWIGLU_ALPHA * g) * (lin + 1.0)).astype(bf16))
    act = jnp.concatenate(acts, axis=-1)
    y = _dot(act, wdn_b[...]) + bdn_ref[0]
    y_ref[...] = _pack_pairs(y.astype(bf16).astype(f32))


def _gate_up_perm():
    j = np.arange(GU_TILE)
    p = np.zeros((GU_TILE, GU_TILE), np.float32)
    p[np.where(j < LANES, 2 * j, 2 * (j - LANES) + 1), j] = 1.0
    return jnp.asarray(p, bf16)


def _expert_mlp(plan, xs, w_gu, b_gu, w_dn, b_dn):
    n_rows = xs.shape[0]
    D = D_MODEL
    tm = EXPERT_TILE
    per_e = lambda e, *_: (e, 0, 0)
    grid_spec = pltpu.PrefetchScalarGridSpec(
        num_scalar_prefetch=3,
        grid=(N_EXPERTS,),
        in_specs=[pl.BlockSpec((1, D, 2 * D_FF), per_e),
                  pl.BlockSpec((1, 1, 2 * D_FF), per_e),
                  pl.BlockSpec((1, D_FF, D), per_e),
                  pl.BlockSpec((1, 1, D), per_e),
                  pl.BlockSpec((GU_TILE, GU_TILE), lambda e, *_: (0, 0)),
                  pl.BlockSpec(memory_space=pl.ANY)],
        out_specs=pl.BlockSpec(memory_space=pl.ANY),
        scratch_shapes=[pltpu.VMEM((D, 2 * D_FF), bf16), pltpu.VMEM((D_FF, D), bf16),
                        pltpu.VMEM((2, tm, D // 2), u32), pltpu.VMEM((2, tm, D // 2), u32),
                        pltpu.SemaphoreType.DMA((2,)), pltpu.SemaphoreType.DMA((2,))],
    )
    return pl.pallas_call(
        _expert_kernel,
        grid_spec=grid_spec,
        out_shape=jax.ShapeDtypeStruct((n_rows, D // 2), u32),
        compiler_params=pltpu.CompilerParams(
            dimension_semantics=("arbitrary",), vmem_limit_bytes=VMEM_LIMIT),
        name="expert_mlp",
    )(plan["eblk"], plan["nblk"], plan["n_used"], w_gu, b_gu, w_dn, b_dn, _gate_up_perm(), xs)


def _combine_kernel(coff_s, rows_s, gbase_s, gtot_s, x_ref, pos_ref, gate_ref, y_hbm, o_ref, ybuf, sem,
                    *, n_steps):
    step = pl.program_id(0)
    places = range(STEP_GROUPS)
    group = lambda s, i: s * STEP_GROUPS + i
    slot = lambda s, i: lax.rem(s, 2) * STEP_GROUPS + i
    tok = lambda i: slice(i * GROUP, (i + 1) * GROUP)

    def cell_copy(sl, i):
        n = pl.multiple_of(rows_s[i], SUBLANES)
        loc = pl.multiple_of(coff_s[i], SUBLANES)
        src = pl.multiple_of(gbase_s[i], SUBLANES)
        return n, pltpu.make_async_copy(y_hbm.at[pl.ds(src, n)], ybuf.at[sl, pl.ds(loc, n)],
                                        sem.at[sl])

    def start_cell(sl, i):
        n, cp = cell_copy(sl, i)
        pl.when(n > 0)(cp.start)

    def wait_group(sl, gg):
        n = pl.multiple_of(gtot_s[gg], SUBLANES)
        pltpu.make_async_copy(y_hbm.at[pl.ds(0, n)], ybuf.at[sl, pl.ds(0, n)], sem.at[sl]).wait()

    @pl.when(step == 0)
    def _():
        ybuf[...] = jnp.zeros(ybuf.shape, u32)
        for i in places:
            _for_cells(group(step, i), functools.partial(start_cell, slot(step, i)))

    @pl.when(step + 1 < n_steps)
    def _():
        for i in places:
            _for_cells(group(step + 1, i), functools.partial(start_cell, slot(step + 1, i)))

    for i in places:
        wait_group(slot(step, i), group(step, i))

    riota = lax.broadcasted_iota(jnp.int32, (GROUP, GROUP_ROWS), 1)
    pts = []
    for i in places:
        pt = jnp.zeros((GROUP, GROUP_ROWS), f32)
        for kk in range(TOP_K):
            pk = pos_ref[tok(i), kk:kk + 1].astype(jnp.int32)
            pt = jnp.where(riota == pk, gate_ref[tok(i), kk:kk + 1], pt)
        pts.append(pt.astype(bf16))
    ys = [_unpack_pairs(ybuf[slot(step, i)]) for i in places]
    outs = [_dot(pts[i], ys[i]) for i in places]
    for i in places:
        o_ref[tok(i), :] = x_ref[tok(i), :] + outs[i]


def _combine(plan, x2, pos, gates, y_rows):
    n_tok, D = x2.shape
    n_steps = n_tok // (GROUP * STEP_GROUPS)
    assert n_steps * GROUP * STEP_GROUPS == n_tok
    tok = lambda w: pl.BlockSpec((STEP_GROUPS * GROUP, w), lambda s, *_: (s, 0))
    grid_spec = pltpu.PrefetchScalarGridSpec(
        num_scalar_prefetch=4,
        grid=(n_steps,),
        in_specs=[tok(D), tok(LANES), tok(LANES), pl.BlockSpec(memory_space=pl.ANY)],
        out_specs=tok(D),
        scratch_shapes=[pltpu.VMEM((2 * STEP_GROUPS, GROUP_ROWS, D // 2), u32),
                        pltpu.SemaphoreType.DMA((2 * STEP_GROUPS,))],
    )
    return pl.pallas_call(
        functools.partial(_combine_kernel, n_steps=n_steps),
        grid_spec=grid_spec,
        out_shape=jax.ShapeDtypeStruct((n_tok, D), f32),
        compiler_params=pltpu.CompilerParams(
            dimension_semantics=("arbitrary",), vmem_limit_bytes=VMEM_LIMIT),
        name="combine",
    )(plan["coff"], plan["rows"], plan["gbase"], plan["gtot"], x2, pos, gates, y_rows)


def _routing_plan(cnt, n_tok):
    tm = EXPERT_TILE
    n_groups = n_tok // GROUP
    cnt = cnt.reshape(n_groups, LANES)[:, :N_EXPERTS].astype(jnp.int32)
    rows = (cnt + SUBLANES - 1) // SUBLANES * SUBLANES
    coff = jnp.cumsum(rows, axis=1) - rows
    tot = jnp.sum(rows, axis=0)
    padded = (tot + tm - 1) // tm * tm
    pend = jnp.cumsum(padded)
    ebase = pend - padded
    gbase = ebase[None, :] + jnp.cumsum(rows, axis=0) - rows
    flat = lambda a: a.reshape(-1).astype(jnp.int32)
    return dict(
        coff=flat(coff), rows=flat(rows), gbase=flat(gbase), gtot=flat(jnp.sum(rows, axis=1)),
        tailoff=flat(ebase + tot), tailn=flat(padded - tot),
        eblk=flat(ebase // tm), nblk=flat(padded // tm), n_used=flat(pend[-1] // tm),
        coffv=jnp.zeros((n_groups, 1, LANES), f32).at[:, 0, :N_EXPERTS].set(coff.astype(f32)))


def _max_rows(n_tok):
    tm = EXPERT_TILE
    worst = (n_tok * TOP_K + (n_tok // GROUP) * N_EXPERTS * (SUBLANES - 1)
             + N_EXPERTS * (tm - SUBLANES))
    return (worst + tm - 1) // tm * tm


def _moe(x2, hf, idx, gates, cnt, w_gate_up, b_gate_up, w_down, b_down):
    n_tok, D = x2.shape
    plan = _routing_plan(cnt, n_tok)
    xs, pos = _dispatch(plan, hf, idx, _max_rows(n_tok))
    b_gu = (b_gate_up.reshape(N_EXPERTS, 2 * D_FF // GU_TILE, LANES, 2).transpose(0, 1, 3, 2)
            .reshape(N_EXPERTS, 1, 2 * D_FF))
    y_rows = _expert_mlp(plan, xs, w_gate_up, b_gu, w_down, b_down.reshape(N_EXPERTS, 1, D))
    return _combine(plan, x2, pos, gates, y_rows)


BIAS_ROW = (2 * Q_TILE + LEFT_CHUNKS * CHUNK + LANES - 2) // LANES * LANES


def _band_bias_kernel(r_ref, o_ref):
    x = jnp.broadcast_to(r_ref[0], (Q_TILE, BIAS_ROW))
    x = pltpu.roll(x, BIAS_ROW - (Q_TILE - 1), axis=1, stride=1, stride_axis=0)
    qc = lax.broadcasted_iota(jnp.int32, (Q_TILE, K_BAND), 0) // CHUNK
    kc = lax.broadcasted_iota(jnp.int32, (Q_TILE, K_BAND), 1) // CHUNK
    first_chunk = LEFT_CHUNKS - pl.program_id(0) * (Q_TILE // CHUNK)
    visible = (kc >= qc) & (kc <= qc + LEFT_CHUNKS) & (kc >= first_chunk)
    o_ref[0, 0] = jnp.where(visible, x[:, :K_BAND] * LOG2E, NEG_INF)


def _band_bias(rel_bias):
    H = rel_bias.shape[0]
    d_max = Q_TILE - 1 + LEFT_CHUNKS * CHUNK
    n_far = d_max - MAX_REL + 1
    r = jnp.concatenate(
        [jnp.broadcast_to(rel_bias[:, 2 * MAX_REL:], (H, n_far)),
         rel_bias[:, MAX_REL - (Q_TILE - 1):2 * MAX_REL][:, ::-1],
         jnp.zeros((H, BIAS_ROW - (2 * Q_TILE - 1 + LEFT_CHUNKS * CHUNK)), rel_bias.dtype)], axis=1)
    return pl.pallas_call(
        _band_bias_kernel,
        grid=(START_TILES + 1, H),
        in_specs=[pl.BlockSpec((1, 1, BIAS_ROW), lambda t, h: (h, 0, 0))],
        out_specs=pl.BlockSpec((1, 1, Q_TILE, K_BAND), lambda t, h: (t, h, 0, 0)),
        out_shape=jax.ShapeDtypeStruct((START_TILES + 1, H, Q_TILE, K_BAND), f32),
        name="band_bias",
    )(r.astype(f32).reshape(H, 1, BIAS_ROW))


def _layer(x, mem, norm_mix_g, w_in, q_norm_g, k_norm_g, rel_bias, conv_w, out_norm_a_g,
           out_norm_b_g, w_out, norm_xattn_g, norm_mem_g, w_xq, w_xkv, xq_norm_g, xk_norm_g,
           w_xo, norm_ffn_g, w_router, b_router, w_gate_up, b_gate_up, w_down, b_down):
    B, S, D = x.shape
    n_tok = B * S
    row = lambda a: a.reshape(1, -1).astype(f32)

    qg = row(jnp.tile(q_norm_g, ATT_HEADS) * (ATT_HEAD_DIM ** -0.5 * LOG2E))
    kg = row(jnp.tile(k_norm_g, ATT_HEADS))
    head_ones = jnp.asarray(np.kron(np.eye(ATT_HEADS), np.ones((ATT_HEAD_DIM, ATT_HEAD_DIM))), bf16)
    q, k, v, yb = _mix_in(x, row(norm_mix_g), w_in.astype(bf16), qg, kg, head_ones,
                          conv_w.astype(f32), row(out_norm_b_g))
    x1 = _attn_out(q, k, v, _band_bias(rel_bias), yb, x, w_out.astype(bf16), row(out_norm_a_g))

    kx, vx = _mem_kv(mem, row(norm_mem_g), w_xkv.astype(bf16), row(xk_norm_g))
    wr = jnp.zeros((D, LANES), f32).at[:, :N_EXPERTS].set(w_router)
    wr_hi = wr.astype(bf16)
    wr_lo = (wr - wr_hi.astype(f32)).astype(bf16)
    br = jnp.full((1, LANES), -jnp.inf, f32).at[0, :N_EXPERTS].set(b_router)
    x2, hf, idx, gates, cnt = _xattn_router(
        x1, row(norm_xattn_g), w_xq.astype(bf16), row(xq_norm_g * (X_HEAD_DIM ** -0.5 * LOG2E)),
        kx, vx,
        w_xo.astype(bf16), row(norm_ffn_g), wr_hi, wr_lo, br)

    out = _moe(x2.reshape(n_tok, D), hf.reshape(n_tok, D), idx.reshape(n_tok, LANES),
               gates.reshape(n_tok, LANES), cnt, w_gate_up, b_gate_up, w_down, b_down)
    return out.reshape(B, S, D)


def kernel(x, mem, norm_mix_g, w_in, q_norm_g, k_norm_g, rel_bias, conv_w, out_norm_a_g,
           out_norm_b_g, w_out, norm_xattn_g, norm_mem_g, w_xq, w_xkv, xq_norm_g, xk_norm_g,
           w_xo, norm_ffn_g, w_router, b_router, w_gate_up, b_gate_up, w_down, b_down):
    depth = norm_mix_g.shape[0]
    for l in range(depth):
        x = _layer(x, mem, norm_mix_g[l], w_in[l], q_norm_g[l], k_norm_g[l], rel_bias[l],
                   conv_w[l], out_norm_a_g[l], out_norm_b_g[l], w_out[l], norm_xattn_g[l],
                   norm_mem_g[l], w_xq[l], w_xkv[l], xq_norm_g[l], xk_norm_g[l], w_xo[l],
                   norm_ffn_g[l], w_router[l], b_router[l], w_gate_up[l], b_gate_up[l],
                   w_down[l], b_down[l])
    return x
```

```python
import functools

import jax
import jax.numpy as jnp
import numpy as np
from jax import lax
from jax.experimental import pallas as pl
from jax.experimental.pallas import tpu as pltpu

D_MODEL = 1024
CHUNK = 64
LEFT_CHUNKS = 8
N_MEM = 256
ATT_HEADS = 8
ATT_HEAD_DIM = 64
D_ATT = ATT_HEADS * ATT_HEAD_DIM
D_CONV = 512
MAX_REL = 256
X_HEADS = 4
X_HEAD_DIM = 128
D_X = X_HEADS * X_HEAD_DIM
N_EXPERTS = 32
TOP_K = 4
D_FF = D_MODEL
SWIGLU_LIMIT = 7.0
SWIGLU_ALPHA = 1.702
EPS = 1e-6
NEG_INF = -1e30

LANES = 128
SEQ_TILE = 1024
Q_TILE = 256
K_BAND = Q_TILE + LEFT_CHUNKS * CHUNK
START_TILES = LEFT_CHUNKS * CHUNK // Q_TILE
LOG2E = 1.4426950408889634
HEAD_ROUND = 4
EXPERT_TILE = 512
GU_TILE = 2 * LANES
GROUP = 256
STEP_GROUPS = 2
SUBLANES = 8
GROUP_ROWS = 1280
VMEM_LIMIT = 56 * 1024 * 1024

bf16 = jnp.bfloat16
f32 = jnp.float32


def _rms(x, g):
    return x * lax.rsqrt(jnp.mean(x * x, axis=-1, keepdims=True) + EPS) * g


def _dot(a, b):
    return jnp.dot(a, b, preferred_element_type=f32)


def _dot_nt(a, b):
    return lax.dot_general(a, b, (((1,), (1,)), ((), ())), preferred_element_type=f32)


def _mix_in_kernel(x_ref, g_ref, w_ref, qg_ref, kg_ref, hm_ref, cw_ref, gb_ref,
                   q_ref, k_ref, v_ref, yb_ref, ub_ref):
    ts = x_ref.shape[1]

    @pl.when(pl.program_id(1) == 0)
    def _():
        ub_ref[0:8, :] = jnp.zeros((8, D_CONV), f32)

    hb = _rms(x_ref[0], g_ref[...]).astype(bf16)

    def head_norm(t, gain):
        ms = _dot((t * t).astype(bf16), hm_ref[...]) * (1.0 / ATT_HEAD_DIM)
        return t * lax.rsqrt(ms + EPS) * gain

    q = _dot(hb, w_ref[:, 0:D_ATT])
    q_ref[0] = head_norm(q, qg_ref[...]).astype(bf16)
    k = _dot(hb, w_ref[:, D_ATT:2 * D_ATT])
    k_ref[0] = head_norm(k, kg_ref[...]).astype(bf16)
    v_ref[0] = _dot(hb, w_ref[:, 2 * D_ATT:3 * D_ATT]).astype(bf16)

    o = 3 * D_ATT
    bg = _dot(hb, w_ref[:, o:o + D_CONV])
    cg = _dot(hb, w_ref[:, o + D_CONV:o + 2 * D_CONV])
    xv = _dot(hb, w_ref[:, o + 2 * D_CONV:o + 3 * D_CONV])
    u = cg * xv
    ub_ref[8:8 + ts, :] = u
    conv = (cw_ref[0:1, :] * ub_ref[6:6 + ts, :] + cw_ref[1:2, :] * ub_ref[7:7 + ts, :]
            + cw_ref[2:3, :] * u)
    ub_ref[0:8, :] = u[ts - 8:ts, :]
    yb_ref[0] = _rms(bg * conv, gb_ref[...]).astype(bf16)


def _mix_in(x, g, w_in, qg, kg, hm, cw, gb):
    B, S, D = x.shape
    ts = SEQ_TILE
    full = lambda shape: pl.BlockSpec(shape, lambda b, j: (0,) * len(shape))
    tile = lambda w: pl.BlockSpec((1, ts, w), lambda b, j: (b, j, 0))
    out = jax.ShapeDtypeStruct((B, S, D_ATT), bf16)
    return pl.pallas_call(
        _mix_in_kernel,
        grid=(B, S // ts),
        in_specs=[tile(D), full((1, D)), full(w_in.shape), full((1, D_ATT)), full((1, D_ATT)),
                  full((D_ATT, D_ATT)), full((3, D_CONV)), full((1, D_CONV))],
        out_specs=[tile(D_ATT)] * 4,
        out_shape=[out] * 4,
        scratch_shapes=[pltpu.VMEM((ts + 8, D_CONV), f32)],
        compiler_params=pltpu.CompilerParams(
            dimension_semantics=("arbitrary", "arbitrary"), vmem_limit_bytes=VMEM_LIMIT),
        name="mix_in",
    )(x, g, w_in, qg, kg, hm, cw, gb)


def _attn_out_kernel(q_ref, k_ref, v_ref, bias_ref, yb_ref, x_ref, w_ref, ga_ref,
                     o_ref, kb_ref, vb_ref):
    tq = q_ref.shape[1]
    n_kc = K_BAND // CHUNK
    c0 = pl.program_id(1) * (tq // CHUNK)
    for kc in range(n_kc):
        src = pl.multiple_of(jnp.maximum(c0 - LEFT_CHUNKS + kc, 0) * CHUNK, CHUNK)
        kb_ref[kc * CHUNK:(kc + 1) * CHUNK, :] = k_ref[0, pl.ds(src, CHUNK), :]
        vb_ref[kc * CHUNK:(kc + 1) * CHUNK, :] = v_ref[0, pl.ds(src, CHUNK), :]

    lane = lax.broadcasted_iota(jnp.int32, (1, LANES), 1)
    low = lane < ATT_HEAD_DIM

    outs = []
    for h0 in range(0, ATT_HEADS, HEAD_ROUND):
        heads = range(h0, h0 + HEAD_ROUND)
        scores, probs, denoms = {}, {}, {}
        for h in heads:
            sl = slice((h // 2) * LANES, (h // 2 + 1) * LANES)
            qp = q_ref[0, :, sl]
            keep = low if h % 2 == 0 else jnp.logical_not(low)
            scores[h] = _dot_nt(jnp.where(keep, qp, jnp.zeros_like(qp)), kb_ref[:, sl])
        for h in heads:
            s = scores[h] + bias_ref[0, h]
            p = jnp.exp2(s - jnp.max(s, axis=-1, keepdims=True))
            denoms[h] = jnp.sum(p, axis=-1, keepdims=True)
            probs[h] = p.astype(bf16)
        for h in heads:
            sl = slice((h // 2) * LANES, (h // 2 + 1) * LANES)
            outs.append(_dot(probs[h], vb_ref[:, sl]) / denoms[h])
    ya = jnp.concatenate([jnp.where(low, outs[2 * hp], outs[2 * hp + 1])
                          for hp in range(ATT_HEADS // 2)], axis=-1)
    yan = _rms(ya, ga_ref[...]).astype(bf16)
    o_ref[0] = (x_ref[0] + _dot(yan, w_ref[0:D_ATT, :]) + _dot(yb_ref[0], w_ref[D_ATT:, :]))


def _attn_out(q, k, v, bias, yb, x, w_out, ga):
    B, S, D = x.shape
    tq = Q_TILE
    full = lambda shape: pl.BlockSpec(shape, lambda b, j: (0,) * len(shape))
    tile = lambda w: pl.BlockSpec((1, tq, w), lambda b, j: (b, j, 0))
    seq = pl.BlockSpec((1, S, D_ATT), lambda b, j: (b, 0, 0))
    return pl.pallas_call(
        _attn_out_kernel,
        grid=(B, S // tq),
        in_specs=[tile(D_ATT), seq, seq,
                  pl.BlockSpec((1,) + bias.shape[1:], lambda b, j: (jnp.minimum(j, START_TILES), 0, 0, 0)),
                  tile(D_CONV), tile(D),
                  full(w_out.shape), full((1, D_ATT))],
        out_specs=tile(D),
        out_shape=jax.ShapeDtypeStruct((B, S, D), f32),
        scratch_shapes=[pltpu.VMEM((K_BAND, D_ATT), bf16), pltpu.VMEM((K_BAND, D_ATT), bf16)],
        compiler_params=pltpu.CompilerParams(
            dimension_semantics=("parallel", "parallel"), vmem_limit_bytes=VMEM_LIMIT),
        name="attn_out",
    )(q, k, v, bias, yb, x, w_out, ga)


def _mem_kv_kernel(m_ref, g_ref, w_ref, kg_ref, k_ref, v_ref):
    hm = _rms(m_ref[0], g_ref[...]).astype(bf16)
    kv = _dot(hm, w_ref[...])
    for h in range(X_HEADS):
        sl = slice(h * X_HEAD_DIM, (h + 1) * X_HEAD_DIM)
        k_ref[0, :, sl] = _rms(kv[:, sl], kg_ref[...]).astype(bf16)
    v_ref[0] = kv[:, D_X:].astype(bf16)


def _mem_kv(mem, g, w_xkv, kg):
    B = mem.shape[0]
    full = lambda shape: pl.BlockSpec(shape, lambda b: (0,) * len(shape))
    out = jax.ShapeDtypeStruct((B, N_MEM, D_X), bf16)
    return pl.pallas_call(
        _mem_kv_kernel,
        grid=(B,),
        in_specs=[pl.BlockSpec((1, N_MEM, D_MODEL), lambda b: (b, 0, 0)), full((1, D_MODEL)),
                  full(w_xkv.shape), full((1, X_HEAD_DIM))],
        out_specs=[pl.BlockSpec((1, N_MEM, D_X), lambda b: (b, 0, 0))] * 2,
        out_shape=[out, out],
        compiler_params=pltpu.CompilerParams(
            dimension_semantics=("parallel",), vmem_limit_bytes=VMEM_LIMIT),
        name="mem_kv",
    )(mem, g, w_xkv, kg)


def _xattn_router_kernel(x_ref, gx_ref, wq_ref, qg_ref, k_ref, v_ref, wo_ref, gf_ref,
                         wrh_ref, wrl_ref, br_ref,
                         x2_ref, hf_ref, idx_ref, gate_ref, cnt_ref):
    x1 = x_ref[0]
    hb = _rms(x1, gx_ref[...]).astype(bf16)
    q = _dot(hb, wq_ref[...])
    hsl = [slice(h * X_HEAD_DIM, (h + 1) * X_HEAD_DIM) for h in range(X_HEADS)]
    scores = [_dot_nt(_rms(q[:, sl], qg_ref[...]).astype(bf16), k_ref[0, :, sl]) for sl in hsl]
    probs, denoms = [], []
    for s in scores:
        p = jnp.exp2(s - jnp.max(s, axis=-1, keepdims=True))
        denoms.append(jnp.sum(p, axis=-1, keepdims=True))
        probs.append(p.astype(bf16))
    heads = [(_dot(p, v_ref[0, :, sl]) / l).astype(bf16) for p, l, sl in zip(probs, denoms, hsl)]
    o = jnp.concatenate(heads, axis=-1)
    x2 = x1 + _dot(o, wo_ref[...])
    x2_ref[0] = x2

    hf = _rms(x2, gf_ref[...])
    hf_hi = hf.astype(bf16)
    hf_ref[0] = hf_hi
    hf_lo = (hf - hf_hi.astype(f32)).astype(bf16)
    logits = (_dot(hf_hi, wrh_ref[...]) + _dot(hf_lo, wrh_ref[...]) + _dot(hf_hi, wrl_ref[...])
              + br_ref[...])
    lane = lax.broadcasted_iota(jnp.int32, logits.shape, 1)
    vals, idxs = [], []
    for _ in range(TOP_K):
        m = jnp.max(logits, axis=-1, keepdims=True)
        i = jnp.min(jnp.where(logits == m, lane, LANES), axis=-1, keepdims=True)
        logits = jnp.where(lane == i, -jnp.inf, logits)
        vals.append(m)
        idxs.append(i)
    es = [jnp.exp(v - vals[0]) for v in vals]
    tot = es[0] + es[1] + es[2] + es[3]
    idx_out = jnp.zeros(logits.shape, jnp.int32)
    gate_out = jnp.zeros(logits.shape, f32)
    sel = jnp.zeros(logits.shape, f32)
    for kk in range(TOP_K):
        idx_out = jnp.where(lane == kk, idxs[kk], idx_out)
        gate_out = jnp.where(lane == kk, es[kk] / tot, gate_out)
        sel = jnp.where(lane == idxs[kk], 1.0, sel)
    idx_ref[0] = idx_out
    gate_ref[0] = gate_out
    for j in range(cnt_ref.shape[0]):
        cnt_ref[j] = jnp.sum(sel[j * GROUP:(j + 1) * GROUP], axis=0, keepdims=True)


def _xattn_router(x1, gx, w_xq, qg, kx, vx, w_xo, gf, wr_hi, wr_lo, br):
    B, S, D = x1.shape
    ts = SEQ_TILE
    full = lambda shape: pl.BlockSpec(shape, lambda b, j: (0,) * len(shape))
    tile = lambda w: pl.BlockSpec((1, ts, w), lambda b, j: (b, j, 0))
    mem = pl.BlockSpec((1, N_MEM, D_X), lambda b, j: (b, 0, 0))
    return pl.pallas_call(
        _xattn_router_kernel,
        grid=(B, S // ts),
        in_specs=[tile(D), full((1, D)), full(w_xq.shape), full((1, X_HEAD_DIM)), mem, mem,
                  full(w_xo.shape), full((1, D)), full(wr_hi.shape), full(wr_lo.shape),
                  full((1, LANES))],
        out_specs=[tile(D), tile(D), tile(LANES), tile(LANES),
                   pl.BlockSpec((ts // GROUP, 1, LANES), lambda b, j: (b * (S // ts) + j, 0, 0))],
        out_shape=[jax.ShapeDtypeStruct((B, S, D), f32), jax.ShapeDtypeStruct((B, S, D), bf16),
                   jax.ShapeDtypeStruct((B, S, LANES), jnp.int32),
                   jax.ShapeDtypeStruct((B, S, LANES), f32),
                   jax.ShapeDtypeStruct((B * S // GROUP, 1, LANES), f32)],
        compiler_params=pltpu.CompilerParams(
            dimension_semantics=("parallel", "parallel"), vmem_limit_bytes=VMEM_LIMIT),
        name="xattn_router",
    )(x1, gx, w_xq, qg, kx, vx, w_xo, gf, wr_hi, wr_lo, br)


u32 = jnp.uint32
HIGH_HALF = 0xFFFF0000


def _pack_pairs(x):
    w = x.shape[1] // 2
    lo = pltpu.bitcast(x[:, :w], u32) >> 16
    hi = pltpu.bitcast(x[:, w:], u32) & u32(HIGH_HALF)
    return lo | hi


def _unpack_pairs(u):
    lo = pltpu.bitcast(u << 16, f32)
    hi = pltpu.bitcast(u & u32(HIGH_HALF), f32)
    return jnp.concatenate([lo, hi], axis=-1).astype(bf16)


def _for_cells(g, fn):
    def body(e, carry):
        fn(g * N_EXPERTS + e)
        return carry
    lax.fori_loop(0, N_EXPERTS, body, 0)


def _group_positions(idxv, coffv, ltri):
    lane = lax.broadcasted_iota(jnp.int32, idxv.shape, 1)
    hots = [lane == idxv[:, kk:kk + 1] for kk in range(TOP_K)]
    multi = jnp.zeros(idxv.shape, f32)
    for h in hots:
        multi = jnp.where(h, 1.0, multi)
    rank = _dot(ltri, multi.astype(bf16))
    posmat = coffv + rank
    return [jnp.sum(jnp.where(h, posmat, 0.0), axis=-1, keepdims=True) for h in hots]


def _dispatch_kernel(coff_s, rows_s, gbase_s, gtot_s, toff_s, tn_s, nu_s,
                     hf_ref, idx_ref, coffv_ref, ltri_ref,
                     xs_hbm, pos_ref, xbuf, zbuf, sem, zsem, *, n_steps):
    step = pl.program_id(0)
    places = range(STEP_GROUPS)
    group = lambda i: step * STEP_GROUPS + i
    slot = lambda i: lax.rem(step, 2) * STEP_GROUPS + i
    tok = lambda i: slice(i * GROUP, (i + 1) * GROUP)

    def cell_copy(sl, i):
        n = pl.multiple_of(rows_s[i], SUBLANES)
        src = pl.multiple_of(coff_s[i], SUBLANES)
        dst = pl.multiple_of(gbase_s[i], SUBLANES)
        return n, pltpu.make_async_copy(xbuf.at[sl, pl.ds(src, n)], xs_hbm.at[pl.ds(dst, n)],
                                        sem.at[sl])

    def start_cell(sl, i):
        n, cp = cell_copy(sl, i)
        pl.when(n > 0)(cp.start)

    def wait_group(sl, gg):
        n = pl.multiple_of(gtot_s[gg], SUBLANES)
        pltpu.make_async_copy(xbuf.at[sl, pl.ds(0, n)], xs_hbm.at[pl.ds(0, n)], sem.at[sl]).wait()

    @pl.when(step >= 2)
    def _():
        for i in places:
            wait_group(slot(i), group(i) - 2 * STEP_GROUPS)

    pos = [_group_positions(idx_ref[tok(i), :], coffv_ref[i], ltri_ref[...]) for i in places]
    lane = lax.broadcasted_iota(jnp.int32, (GROUP, LANES), 1)
    riota = lax.broadcasted_iota(jnp.int32, (GROUP, GROUP_ROWS), 1)
    pts = []
    for i in places:
        pt = jnp.zeros((GROUP, GROUP_ROWS), f32)
        pos_out = jnp.zeros((GROUP, LANES), f32)
        for kk in range(TOP_K):
            pt = jnp.where(riota == pos[i][kk].astype(jnp.int32), 1.0, pt)
            pos_out = jnp.where(lane == kk, pos[i][kk], pos_out)
        pos_ref[tok(i), :] = pos_out
        pts.append(pt.astype(bf16))
    rows = [lax.dot_general(pts[i], hf_ref[tok(i), :], (((0,), (0,)), ((), ())),
                            preferred_element_type=f32) for i in places]
    for i in places:
        xbuf[slot(i)] = _pack_pairs(rows[i])
    for i in places:
        _for_cells(group(i), functools.partial(start_cell, slot(i)))

    @pl.when(step == n_steps - 1)
    def _():
        if n_steps >= 2:
            for i in places:
                prev_slot = (1 - lax.rem(step, 2)) * STEP_GROUPS + i
                wait_group(prev_slot, group(i) - STEP_GROUPS)
        for i in places:
            wait_group(slot(i), group(i))
        zbuf[...] = jnp.zeros(zbuf.shape, u32)

        def tail_copy(e):
            n = pl.multiple_of(tn_s[e], SUBLANES)
            dst = pl.multiple_of(toff_s[e], SUBLANES)
            return n, pltpu.make_async_copy(zbuf.at[pl.ds(0, n)], xs_hbm.at[pl.ds(dst, n)], zsem)

        def tail(e, carry):
            n, cp = tail_copy(e)

            @pl.when(n > 0)
            def _():
                cp.start()
                cp.wait()
            return carry
        lax.fori_loop(0, N_EXPERTS, tail, 0)

        def unused(b, carry):
            dst = pl.multiple_of(b * EXPERT_TILE, EXPERT_TILE)
            cp = pltpu.make_async_copy(zbuf, xs_hbm.at[pl.ds(dst, EXPERT_TILE)], zsem)
            cp.start()
            cp.wait()
            return carry
        lax.fori_loop(nu_s[0], xs_hbm.shape[0] // EXPERT_TILE, unused, 0)


def _dispatch(plan, hf, idx, n_rows):
    n_tok, D = hf.shape
    n_steps = n_tok // (GROUP * STEP_GROUPS)
    assert n_steps * GROUP * STEP_GROUPS == n_tok
    ltri = jnp.asarray(np.tril(np.ones((GROUP, GROUP)), -1), bf16)
    tok = lambda w: pl.BlockSpec((STEP_GROUPS * GROUP, w), lambda s, *_: (s, 0))
    grid_spec = pltpu.PrefetchScalarGridSpec(
        num_scalar_prefetch=7,
        grid=(n_steps,),
        in_specs=[tok(D), tok(LANES),
                  pl.BlockSpec((STEP_GROUPS, 1, LANES), lambda s, *_: (s, 0, 0)),
                  pl.BlockSpec((GROUP, GROUP), lambda s, *_: (0, 0))],
        out_specs=[pl.BlockSpec(memory_space=pl.ANY), tok(LANES)],
        scratch_shapes=[pltpu.VMEM((2 * STEP_GROUPS, GROUP_ROWS, D // 2), u32),
                        pltpu.VMEM((EXPERT_TILE, D // 2), u32),
                        pltpu.SemaphoreType.DMA((2 * STEP_GROUPS,)), pltpu.SemaphoreType.DMA(())],
    )
    return pl.pallas_call(
        functools.partial(_dispatch_kernel, n_steps=n_steps),
        grid_spec=grid_spec,
        out_shape=[jax.ShapeDtypeStruct((n_rows, D // 2), u32),
                   jax.ShapeDtypeStruct((n_tok, LANES), f32)],
        compiler_params=pltpu.CompilerParams(
            dimension_semantics=("arbitrary",), vmem_limit_bytes=VMEM_LIMIT),
        name="dispatch",
    )(plan["coff"], plan["rows"], plan["gbase"], plan["gtot"], plan["tailoff"], plan["tailn"],
      plan["n_used"], hf, idx, plan["coffv"], ltri)


def _expert_kernel(eb_ref, nb_ref, nu_ref, wgu_ref, bgu_ref, wdn_ref, bdn_ref, pm_ref, xs_hbm,
                   y_hbm, wgu_b, wdn_b, xin, yout, in_sem, out_sem):
    e = pl.program_id(0)
    tm = xin.shape[1]
    n_used = nu_ref[0]

    def in_copy(b):
        sl = lax.rem(b, 2)
        rows = pl.ds(pl.multiple_of(b * tm, tm), tm)
        return pltpu.make_async_copy(xs_hbm.at[rows], xin.at[sl], in_sem.at[sl])

    def out_copy(b):
        sl = lax.rem(b, 2)
        rows = pl.ds(pl.multiple_of(b * tm, tm), tm)
        return pltpu.make_async_copy(yout.at[sl], y_hbm.at[rows], out_sem.at[sl])

    @pl.when(e == 0)
    def _():
        in_copy(0).start()

    @pl.when(nb_ref[e] > 0)
    def _():
        for c in range(2 * D_FF // GU_TILE):
            cols = slice(c * GU_TILE, (c + 1) * GU_TILE)
            wgu_b[:, cols] = _dot(wgu_ref[0, :, cols].astype(bf16), pm_ref[...]).astype(bf16)
        for c in range(D_FF // LANES):
            rows = slice(c * LANES, (c + 1) * LANES)
            wdn_b[rows, :] = wdn_ref[0, rows, :].astype(bf16)

        def block(b, carry):
            sl = lax.rem(b, 2)
            in_copy(b).wait()

            @pl.when(b + 1 < n_used)
            def _():
                in_copy(b + 1).start()

            @pl.when(b >= 2)
            def _():
                out_copy(b - 2).wait()

            _expert_block(xin.at[sl], bgu_ref, bdn_ref, yout.at[sl], wgu_b, wdn_b)
            out_copy(b).start()
            return carry
        lax.fori_loop(eb_ref[e], eb_ref[e] + nb_ref[e], block, 0)

    @pl.when(e == pl.num_programs(0) - 1)
    def _():
        @pl.when(n_used >= 2)
        def _():
            out_copy(n_used - 2).wait()
        out_copy(n_used - 1).wait()
        yout[0] = jnp.zeros(yout.shape[1:], u32)

        def unused(b, carry):
            rows = pl.ds(pl.multiple_of(b * tm, tm), tm)
            cp = pltpu.make_async_copy(yout.at[0], y_hbm.at[rows], out_sem.at[0])
            cp.start()
            cp.wait()
            return carry
        lax.fori_loop(n_used, y_hbm.shape[0] // tm, unused, 0)


def _expert_block(x_ref, bgu_ref, bdn_ref, y_ref, wgu_b, wdn_b):
    gu = _dot(_unpack_pairs(x_ref[...]), wgu_b[...]) + bgu_ref[0]
    acts = []
    for c in range(2 * D_FF // GU_TILE):
        g = jnp.minimum(gu[:, c * GU_TILE:c * GU_TILE + LANES], SWIGLU_LIMIT)
        lin = jnp.clip(gu[:, c * GU_TILE + LANES:(c + 1) * GU_TILE], -SWIGLU_LIMIT, SWIGLU_LIMIT)
        acts.append((g * jax.nn.sigmoid(SWIGLU_ALPHA * g) * (lin + 1.0)).astype(bf16))
    act = jnp.concatenate(acts, axis=-1)
    y = _dot(act, wdn_b[...]) + bdn_ref[0]
    y_ref[...] = _pack_pairs(y.astype(bf16).astype(f32))


def _gate_up_perm():
    j = np.arange(GU_TILE)
    p = np.zeros((GU_TILE, GU_TILE), np.float32)
    p[np.where(j < LANES, 2 * j, 2 * (j - LANES) + 1), j] = 1.0
    return jnp.asarray(p, bf16)


def _expert_mlp(plan, xs, w_gu, b_gu, w_dn, b_dn):
    n_rows = xs.shape[0]
    D = D_MODEL
    tm = EXPERT_TILE
    per_e = lambda e, *_: (e, 0, 0)
    grid_spec = pltpu.PrefetchScalarGridSpec(
        num_scalar_prefetch=3,
        grid=(N_EXPERTS,),
        in_specs=[pl.BlockSpec((1, D, 2 * D_FF), per_e),
                  pl.BlockSpec((1, 1, 2 * D_FF), per_e),
                  pl.BlockSpec((1, D_FF, D), per_e),
                  pl.BlockSpec((1, 1, D), per_e),
                  pl.BlockSpec((GU_TILE, GU_TILE), lambda e, *_: (0, 0)),
                  pl.BlockSpec(memory_space=pl.ANY)],
        out_specs=pl.BlockSpec(memory_space=pl.ANY),
        scratch_shapes=[pltpu.VMEM((D, 2 * D_FF), bf16), pltpu.VMEM((D_FF, D), bf16),
                        pltpu.VMEM((2, tm, D // 2), u32), pltpu.VMEM((2, tm, D // 2), u32),
                        pltpu.SemaphoreType.DMA((2,)), pltpu.SemaphoreType.DMA((2,))],
    )
    return pl.pallas_call(
        _expert_kernel,
        grid_spec=grid_spec,
        out_shape=jax.ShapeDtypeStruct((n_rows, D // 2), u32),
        compiler_params=pltpu.CompilerParams(
            dimension_semantics=("arbitrary",), vmem_limit_bytes=VMEM_LIMIT),
        name="expert_mlp",
    )(plan["eblk"], plan["nblk"], plan["n_used"], w_gu, b_gu, w_dn, b_dn, _gate_up_perm(), xs)


def _combine_kernel(coff_s, rows_s, gbase_s, gtot_s, x_ref, pos_ref, gate_ref, y_hbm, o_ref, ybuf, sem,
                    *, n_steps):
    step = pl.program_id(0)
    places = range(STEP_GROUPS)
    group = lambda s, i: s * STEP_GROUPS + i
    slot = lambda s, i: lax.rem(s, 2) * STEP_GROUPS + i
    tok = lambda i: slice(i * GROUP, (i + 1) * GROUP)

    def cell_copy(sl, i):
        n = pl.multiple_of(rows_s[i], SUBLANES)
        loc = pl.multiple_of(coff_s[i], SUBLANES)
        src = pl.multiple_of(gbase_s[i], SUBLANES)
        return n, pltpu.make_async_copy(y_hbm.at[pl.ds(src, n)], ybuf.at[sl, pl.ds(loc, n)],
                                        sem.at[sl])

    def start_cell(sl, i):
        n, cp = cell_copy(sl, i)
        pl.when(n > 0)(cp.start)

    def wait_group(sl, gg):
        n = pl.multiple_of(gtot_s[gg], SUBLANES)
        pltpu.make_async_copy(y_hbm.at[pl.ds(0, n)], ybuf.at[sl, pl.ds(0, n)], sem.at[sl]).wait()

    @pl.when(step == 0)
    def _():
        ybuf[...] = jnp.zeros(ybuf.shape, u32)
        for i in places:
            _for_cells(group(step, i), functools.partial(start_cell, slot(step, i)))

    @pl.when(step + 1 < n_steps)
    def _():
        for i in places:
            _for_cells(group(step + 1, i), functools.partial(start_cell, slot(step + 1, i)))

    for i in places:
        wait_group(slot(step, i), group(step, i))

    riota = lax.broadcasted_iota(jnp.int32, (GROUP, GROUP_ROWS), 1)
    pts = []
    for i in places:
        pt = jnp.zeros((GROUP, GROUP_ROWS), f32)
        for kk in range(TOP_K):
            pk = pos_ref[tok(i), kk:kk + 1].astype(jnp.int32)
            pt = jnp.where(riota == pk, gate_ref[tok(i), kk:kk + 1], pt)
        pts.append(pt.astype(bf16))
    ys = [_unpack_pairs(ybuf[slot(step, i)]) for i in places]
    outs = [_dot(pts[i], ys[i]) for i in places]
    for i in places:
        o_ref[tok(i), :] = x_ref[tok(i), :] + outs[i]


def _combine(plan, x2, pos, gates, y_rows):
    n_tok, D = x2.shape
    n_steps = n_tok // (GROUP * STEP_GROUPS)
    assert n_steps * GROUP * STEP_GROUPS == n_tok
    tok = lambda w: pl.BlockSpec((STEP_GROUPS * GROUP, w), lambda s, *_: (s, 0))
    grid_spec = pltpu.PrefetchScalarGridSpec(
        num_scalar_prefetch=4,
        grid=(n_steps,),
        in_specs=[tok(D), tok(LANES), tok(LANES), pl.BlockSpec(memory_space=pl.ANY)],
        out_specs=tok(D),
        scratch_shapes=[pltpu.VMEM((2 * STEP_GROUPS, GROUP_ROWS, D // 2), u32),
                        pltpu.SemaphoreType.DMA((2 * STEP_GROUPS,))],
    )
    return pl.pallas_call(
        functools.partial(_combine_kernel, n_steps=n_steps),
        grid_spec=grid_spec,
        out_shape=jax.ShapeDtypeStruct((n_tok, D), f32),
        compiler_params=pltpu.CompilerParams(
            dimension_semantics=("arbitrary",), vmem_limit_bytes=VMEM_LIMIT),
        name="combine",
    )(plan["coff"], plan["rows"], plan["gbase"], plan["gtot"], x2, pos, gates, y_rows)


def _routing_plan(cnt, n_tok):
    tm = EXPERT_TILE
    n_groups = n_tok // GROUP
    cnt = cnt.reshape(n_groups, LANES)[:, :N_EXPERTS].astype(jnp.int32)
    rows = (cnt + SUBLANES - 1) // SUBLANES * SUBLANES
    coff = jnp.cumsum(rows, axis=1) - rows
    tot = jnp.sum(rows, axis=0)
    padded = (tot + tm - 1) // tm * tm
    pend = jnp.cumsum(padded)
    ebase = pend - padded
    gbase = ebase[None, :] + jnp.cumsum(rows, axis=0) - rows
    flat = lambda a: a.reshape(-1).astype(jnp.int32)
    return dict(
        coff=flat(coff), rows=flat(rows), gbase=flat(gbase), gtot=flat(jnp.sum(rows, axis=1)),
        tailoff=flat(ebase + tot), tailn=flat(padded - tot),
        eblk=flat(ebase // tm), nblk=flat(padded // tm), n_used=flat(pend[-1] // tm),
        coffv=jnp.zeros((n_groups, 1, LANES), f32).at[:, 0, :N_EXPERTS].set(coff.astype(f32)))


def _max_rows(n_tok):
    tm = EXPERT_TILE
    worst = (n_tok * TOP_K + (n_tok // GROUP) * N_EXPERTS * (SUBLANES - 1)
             + N_EXPERTS * (tm - SUBLANES))
    return (worst + tm - 1) // tm * tm


def _moe(x2, hf, idx, gates, cnt, w_gate_up, b_gate_up, w_down, b_down):
    n_tok, D = x2.shape
    plan = _routing_plan(cnt, n_tok)
    xs, pos = _dispatch(plan, hf, idx, _max_rows(n_tok))
    b_gu = (b_gate_up.reshape(N_EXPERTS, 2 * D_FF // GU_TILE, LANES, 2).transpose(0, 1, 3, 2)
            .reshape(N_EXPERTS, 1, 2 * D_FF))
    y_rows = _expert_mlp(plan, xs, w_gate_up, b_gu, w_down, b_down.reshape(N_EXPERTS, 1, D))
    return _combine(plan, x2, pos, gates, y_rows)


BIAS_ROW = (2 * Q_TILE + LEFT_CHUNKS * CHUNK + LANES - 2) // LANES * LANES


def _band_bias_kernel(r_ref, o_ref):
    x = jnp.broadcast_to(r_ref[0], (Q_TILE, BIAS_ROW))
    x = pltpu.roll(x, BIAS_ROW - (Q_TILE - 1), axis=1, stride=1, stride_axis=0)
    qc = lax.broadcasted_iota(jnp.int32, (Q_TILE, K_BAND), 0) // CHUNK
    kc = lax.broadcasted_iota(jnp.int32, (Q_TILE, K_BAND), 1) // CHUNK
    first_chunk = LEFT_CHUNKS - pl.program_id(0) * (Q_TILE // CHUNK)
    visible = (kc >= qc) & (kc <= qc + LEFT_CHUNKS) & (kc >= first_chunk)
    o_ref[0, 0] = jnp.where(visible, x[:, :K_BAND] * LOG2E, NEG_INF)


def _band_bias(rel_bias):
    H = rel_bias.shape[0]
    d_max = Q_TILE - 1 + LEFT_CHUNKS * CHUNK
    n_far = d_max - MAX_REL + 1
    r = jnp.concatenate(
        [jnp.broadcast_to(rel_bias[:, 2 * MAX_REL:], (H, n_far)),
         rel_bias[:, MAX_REL - (Q_TILE - 1):2 * MAX_REL][:, ::-1],
         jnp.zeros((H, BIAS_ROW - (2 * Q_TILE - 1 + LEFT_CHUNKS * CHUNK)), rel_bias.dtype)], axis=1)
    return pl.pallas_call(
        _band_bias_kernel,
        grid=(START_TILES + 1, H),
        in_specs=[pl.BlockSpec((1, 1, BIAS_ROW), lambda t, h: (h, 0, 0))],
        out_specs=pl.BlockSpec((1, 1, Q_TILE, K_BAND), lambda t, h: (t, h, 0, 0)),
        out_shape=jax.ShapeDtypeStruct((START_TILES + 1, H, Q_TILE, K_BAND), f32),
        name="band_bias",
    )(r.astype(f32).reshape(H, 1, BIAS_ROW))


def _layer(x, mem, norm_mix_g, w_in, q_norm_g, k_norm_g, rel_bias, conv_w, out_norm_a_g,
           out_norm_b_g, w_out, norm_xattn_g, norm_mem_g, w_xq, w_xkv, xq_norm_g, xk_norm_g,
           w_xo, norm_ffn_g, w_router, b_router, w_gate_up, b_gate_up, w_down, b_down):
    B, S, D = x.shape
    n_tok = B * S
    row = lambda a: a.reshape(1, -1).astype(f32)

    qg = row(jnp.tile(q_norm_g, ATT_HEADS) * (ATT_HEAD_DIM ** -0.5 * LOG2E))
    kg = row(jnp.tile(k_norm_g, ATT_HEADS))
    head_ones = jnp.asarray(np.kron(np.eye(ATT_HEADS), np.ones((ATT_HEAD_DIM, ATT_HEAD_DIM))), bf16)
    q, k, v, yb = _mix_in(x, row(norm_mix_g), w_in.astype(bf16), qg, kg, head_ones,
                          conv_w.astype(f32), row(out_norm_b_g))
    x1 = _attn_out(q, k, v, _band_bias(rel_bias), yb, x, w_out.astype(bf16), row(out_norm_a_g))

    kx, vx = _mem_kv(mem, row(norm_mem_g), w_xkv.astype(bf16), row(xk_norm_g))
    wr = jnp.zeros((D, LANES), f32).at[:, :N_EXPERTS].set(w_router)
    wr_hi = wr.astype(bf16)
    wr_lo = (wr - wr_hi.astype(f32)).astype(bf16)
    br = jnp.full((1, LANES), -jnp.inf, f32).at[0, :N_EXPERTS].set(b_router)
    x2, hf, idx, gates, cnt = _xattn_router(
        x1, row(norm_xattn_g), w_xq.astype(bf16), row(xq_norm_g * (X_HEAD_DIM ** -0.5 * LOG2E)),
        kx, vx,
        w_xo.astype(bf16), row(norm_ffn_g), wr_hi, wr_lo, br)

    out = _moe(x2.reshape(n_tok, D), hf.reshape(n_tok, D), idx.reshape(n_tok, LANES),
               gates.reshape(n_tok, LANES), cnt, w_gate_up, b_gate_up, w_down, b_down)
    return out.reshape(B, S, D)


def kernel(x, mem, norm_mix_g, w_in, q_norm_g, k_norm_g, rel_bias, conv_w, out_norm_a_g,
           out_norm_b_g, w_out, norm_xattn_g, norm_mem_g, w_xq, w_xkv, xq_norm_g, xk_norm_g,
           w_xo, norm_ffn_g, w_router, b_router, w_gate_up, b_gate_up, w_down, b_down):
    depth = norm_mix_g.shape[0]
    for l in range(depth):
        x = _layer(x, mem, norm_mix_g[l], w_in[l], q_norm_g[l], k_norm_g[l], rel_bias[l],
                   conv_w[l], out_norm_a_g[l], out_norm_b_g[l], w_out[l], norm_xattn_g[l],
                   norm_mem_g[l], w_xq[l], w_xkv[l], xq_norm_g[l], xk_norm_g[l], w_xo[l],
                   norm_ffn_g[l], w_router[l], b_router[l], w_gate_up[l], b_gate_up[l],
                   w_down[l], b_down[l])
    return x
```

```python
import functools

import jax
import jax.numpy as jnp
import numpy as np
from jax import lax
from jax.experimental import pallas as pl
from jax.experimental.pallas import tpu as pltpu

D_MODEL = 1024
CHUNK = 64
LEFT_CHUNKS = 8
N_MEM = 256
ATT_HEADS = 8
ATT_HEAD_DIM = 64
D_ATT = ATT_HEADS * ATT_HEAD_DIM
D_CONV = 512
MAX_REL = 256
X_HEADS = 4
X_HEAD_DIM = 128
D_X = X_HEADS * X_HEAD_DIM
N_EXPERTS = 32
TOP_K = 4
D_FF = D_MODEL
SWIGLU_LIMIT = 7.0
SWIGLU_ALPHA = 1.702
EPS = 1e-6
NEG_INF = -1e30

LANES = 128
SEQ_TILE = 1024
Q_TILE = 256
K_BAND = Q_TILE + LEFT_CHUNKS * CHUNK
START_TILES = LEFT_CHUNKS * CHUNK // Q_TILE
LOG2E = 1.4426950408889634
HEAD_ROUND = 4
EXPERT_TILE = 512
GU_TILE = 2 * LANES
GROUP = 256
STEP_GROUPS = 2
SUBLANES = 8
GROUP_ROWS = 1280
VMEM_LIMIT = 56 * 1024 * 1024

bf16 = jnp.bfloat16
f32 = jnp.float32


def _rms(x, g):
    return x * lax.rsqrt(jnp.mean(x * x, axis=-1, keepdims=True) + EPS) * g


def _dot(a, b):
    return jnp.dot(a, b, preferred_element_type=f32)


def _dot_nt(a, b):
    return lax.dot_general(a, b, (((1,), (1,)), ((), ())), preferred_element_type=f32)


def _mix_in_kernel(x_ref, g_ref, w_ref, qg_ref, kg_ref, hm_ref, cw_ref, gb_ref,
                   q_ref, k_ref, v_ref, yb_ref, ub_ref):
    ts = x_ref.shape[1]

    @pl.when(pl.program_id(1) == 0)
    def _():
        ub_ref[0:8, :] = jnp.zeros((8, D_CONV), f32)

    hb = _rms(x_ref[0], g_ref[...]).astype(bf16)

    def head_norm(t, gain):
        ms = _dot((t * t).astype(bf16), hm_ref[...]) * (1.0 / ATT_HEAD_DIM)
        return t * lax.rsqrt(ms + EPS) * gain

    q = _dot(hb, w_ref[:, 0:D_ATT])
    q_ref[0] = head_norm(q, qg_ref[...]).astype(bf16)
    k = _dot(hb, w_ref[:, D_ATT:2 * D_ATT])
    k_ref[0] = head_norm(k, kg_ref[...]).astype(bf16)
    v_ref[0] = _dot(hb, w_ref[:, 2 * D_ATT:3 * D_ATT]).astype(bf16)

    o = 3 * D_ATT
    bg = _dot(hb, w_ref[:, o:o + D_CONV])
    cg = _dot(hb, w_ref[:, o + D_CONV:o + 2 * D_CONV])
    xv = _dot(hb, w_ref[:, o + 2 * D_CONV:o + 3 * D_CONV])
    u = cg * xv
    ub_ref[8:8 + ts, :] = u
    conv = (cw_ref[0:1, :] * ub_ref[6:6 + ts, :] + cw_ref[1:2, :] * ub_ref[7:7 + ts, :]
            + cw_ref[2:3, :] * u)
    ub_ref[0:8, :] = u[ts - 8:ts, :]
    yb_ref[0] = _rms(bg * conv, gb_ref[...]).astype(bf16)


def _mix_in(x, g, w_in, qg, kg, hm, cw, gb):
    B, S, D = x.shape
    ts = SEQ_TILE
    full = lambda shape: pl.BlockSpec(shape, lambda b, j: (0,) * len(shape))
    tile = lambda w: pl.BlockSpec((1, ts, w), lambda b, j: (b, j, 0))
    out = jax.ShapeDtypeStruct((B, S, D_ATT), bf16)
    return pl.pallas_call(
        _mix_in_kernel,
        grid=(B, S // ts),
        in_specs=[tile(D), full((1, D)), full(w_in.shape), full((1, D_ATT)), full((1, D_ATT)),
                  full((D_ATT, D_ATT)), full((3, D_CONV)), full((1, D_CONV))],
        out_specs=[tile(D_ATT)] * 4,
        out_shape=[out] * 4,
        scratch_shapes=[pltpu.VMEM((ts + 8, D_CONV), f32)],
        compiler_params=pltpu.CompilerParams(
            dimension_semantics=("arbitrary", "arbitrary"), vmem_limit_bytes=VMEM_LIMIT),
        name="mix_in",
    )(x, g, w_in, qg, kg, hm, cw, gb)


def _attn_out_kernel(q_ref, k_ref, v_ref, bias_ref, yb_ref, x_ref, w_ref, ga_ref,
                     o_ref, kb_ref, vb_ref):
    tq = q_ref.shape[1]
    n_kc = K_BAND // CHUNK
    c0 = pl.program_id(1) * (tq // CHUNK)
    for kc in range(n_kc):
        src = pl.multiple_of(jnp.maximum(c0 - LEFT_CHUNKS + kc, 0) * CHUNK, CHUNK)
        kb_ref[kc * CHUNK:(kc + 1) * CHUNK, :] = k_ref[0, pl.ds(src, CHUNK), :]
        vb_ref[kc * CHUNK:(kc + 1) * CHUNK, :] = v_ref[0, pl.ds(src, CHUNK), :]

    lane = lax.broadcasted_iota(jnp.int32, (1, LANES), 1)
    low = lane < ATT_HEAD_DIM

    outs = []
    for h0 in range(0, ATT_HEADS, HEAD_ROUND):
        heads = range(h0, h0 + HEAD_ROUND)
        scores, probs, denoms = {}, {}, {}
        for h in heads:
            sl = slice((h // 2) * LANES, (h // 2 + 1) * LANES)
            qp = q_ref[0, :, sl]
            keep = low if h % 2 == 0 else jnp.logical_not(low)
            scores[h] = _dot_nt(jnp.where(keep, qp, jnp.zeros_like(qp)), kb_ref[:, sl])
        for h in heads:
            s = scores[h] + bias_ref[0, h]
            p = jnp.exp2(s - jnp.max(s, axis=-1, keepdims=True))
            denoms[h] = jnp.sum(p, axis=-1, keepdims=True)
            probs[h] = p.astype(bf16)
        for h in heads:
            sl = slice((h // 2) * LANES, (h // 2 + 1) * LANES)
            outs.append(_dot(probs[h], vb_ref[:, sl]) / denoms[h])
    ya = jnp.concatenate([jnp.where(low, outs[2 * hp], outs[2 * hp + 1])
                          for hp in range(ATT_HEADS // 2)], axis=-1)
    yan = _rms(ya, ga_ref[...]).astype(bf16)
    o_ref[0] = (x_ref[0] + _dot(yan, w_ref[0:D_ATT, :]) + _dot(yb_ref[0], w_ref[D_ATT:, :]))


def _attn_out(q, k, v, bias, yb, x, w_out, ga):
    B, S, D = x.shape
    tq = Q_TILE
    full = lambda shape: pl.BlockSpec(shape, lambda b, j: (0,) * len(shape))
    tile = lambda w: pl.BlockSpec((1, tq, w), lambda b, j: (b, j, 0))
    seq = pl.BlockSpec((1, S, D_ATT), lambda b, j: (b, 0, 0))
    return pl.pallas_call(
        _attn_out_kernel,
        grid=(B, S // tq),
        in_specs=[tile(D_ATT), seq, seq,
                  pl.BlockSpec((1,) + bias.shape[1:], lambda b, j: (jnp.minimum(j, START_TILES), 0, 0, 0)),
                  tile(D_CONV), tile(D),
                  full(w_out.shape), full((1, D_ATT))],
        out_specs=tile(D),
        out_shape=jax.ShapeDtypeStruct((B, S, D), f32),
        scratch_shapes=[pltpu.VMEM((K_BAND, D_ATT), bf16), pltpu.VMEM((K_BAND, D_ATT), bf16)],
        compiler_params=pltpu.CompilerParams(
            dimension_semantics=("parallel", "parallel"), vmem_limit_bytes=VMEM_LIMIT),
        name="attn_out",
    )(q, k, v, bias, yb, x, w_out, ga)


def _mem_kv_kernel(m_ref, g_ref, w_ref, kg_ref, k_ref, v_ref):
    hm = _rms(m_ref[0], g_ref[...]).astype(bf16)
    kv = _dot(hm, w_ref[...])
    for h in range(X_HEADS):
        sl = slice(h * X_HEAD_DIM, (h + 1) * X_HEAD_DIM)
        k_ref[0, :, sl] = _rms(kv[:, sl], kg_ref[...]).astype(bf16)
    v_ref[0] = kv[:, D_X:].astype(bf16)


def _mem_kv(mem, g, w_xkv, kg):
    B = mem.shape[0]
    full = lambda shape: pl.BlockSpec(shape, lambda b: (0,) * len(shape))
    out = jax.ShapeDtypeStruct((B, N_MEM, D_X), bf16)
    return pl.pallas_call(
        _mem_kv_kernel,
        grid=(B,),
        in_specs=[pl.BlockSpec((1, N_MEM, D_MODEL), lambda b: (b, 0, 0)), full((1, D_MODEL)),
                  full(w_xkv.shape), full((1, X_HEAD_DIM))],
        out_specs=[pl.BlockSpec((1, N_MEM, D_X), lambda b: (b, 0, 0))] * 2,
        out_shape=[out, out],
        compiler_params=pltpu.CompilerParams(
            dimension_semantics=("parallel",), vmem_limit_bytes=VMEM_LIMIT),
        name="mem_kv",
    )(mem, g, w_xkv, kg)


def _xattn_router_kernel(x_ref, gx_ref, wq_ref, qg_ref, k_ref, v_ref, wo_ref, gf_ref,
                         wrh_ref, wrl_ref, br_ref,
                         x2_ref, hf_ref, idx_ref, gate_ref, cnt_ref):
    x1 = x_ref[0]
    hb = _rms(x1, gx_ref[...]).astype(bf16)
    q = _dot(hb, wq_ref[...])
    hsl = [slice(h * X_HEAD_DIM, (h + 1) * X_HEAD_DIM) for h in range(X_HEADS)]
    scores = [_dot_nt(_rms(q[:, sl], qg_ref[...]).astype(bf16), k_ref[0, :, sl]) for sl in hsl]
    probs, denoms = [], []
    for s in scores:
        p = jnp.exp2(s - jnp.max(s, axis=-1, keepdims=True))
        denoms.append(jnp.sum(p, axis=-1, keepdims=True))
        probs.append(p.astype(bf16))
    heads = [(_dot(p, v_ref[0, :, sl]) / l).astype(bf16) for p, l, sl in zip(probs, denoms, hsl)]
    o = jnp.concatenate(heads, axis=-1)
    x2 = x1 + _dot(o, wo_ref[...])
    x2_ref[0] = x2

    hf = _rms(x2, gf_ref[...])
    hf_hi = hf.astype(bf16)
    hf_ref[0] = hf_hi
    hf_lo = (hf - hf_hi.astype(f32)).astype(bf16)
    logits = (_dot(hf_hi, wrh_ref[...]) + _dot(hf_lo, wrh_ref[...]) + _dot(hf_hi, wrl_ref[...])
              + br_ref[...])
    lane = lax.broadcasted_iota(jnp.int32, logits.shape, 1)
    vals, idxs = [], []
    for _ in range(TOP_K):
        m = jnp.max(logits, axis=-1, keepdims=True)
        i = jnp.min(jnp.where(logits == m, lane, LANES), axis=-1, keepdims=True)
        logits = jnp.where(lane == i, -jnp.inf, logits)
        vals.append(m)
        idxs.append(i)
    es = [jnp.exp(v - vals[0]) for v in vals]
    tot = es[0] + es[1] + es[2] + es[3]
    idx_out = jnp.zeros(logits.shape, jnp.int32)
    gate_out = jnp.zeros(logits.shape, f32)
    sel = jnp.zeros(logits.shape, f32)
    for kk in range(TOP_K):
        idx_out = jnp.where(lane == kk, idxs[kk], idx_out)
        gate_out = jnp.where(lane == kk, es[kk] / tot, gate_out)
        sel = jnp.where(lane == idxs[kk], 1.0, sel)
    idx_ref[0] = idx_out
    gate_ref[0] = gate_out
    for j in range(cnt_ref.shape[0]):
        cnt_ref[j] = jnp.sum(sel[j * GROUP:(j + 1) * GROUP], axis=0, keepdims=True)


def _xattn_router(x1, gx, w_xq, qg, kx, vx, w_xo, gf, wr_hi, wr_lo, br):
    B, S, D = x1.shape
    ts = SEQ_TILE
    full = lambda shape: pl.BlockSpec(shape, lambda b, j: (0,) * len(shape))
    tile = lambda w: pl.BlockSpec((1, ts, w), lambda b, j: (b, j, 0))
    mem = pl.BlockSpec((1, N_MEM, D_X), lambda b, j: (b, 0, 0))
    return pl.pallas_call(
        _xattn_router_kernel,
        grid=(B, S // ts),
        in_specs=[tile(D), full((1, D)), full(w_xq.shape), full((1, X_HEAD_DIM)), mem, mem,
                  full(w_xo.shape), full((1, D)), full(wr_hi.shape), full(wr_lo.shape),
                  full((1, LANES))],
        out_specs=[tile(D), tile(D), tile(LANES), tile(LANES),
                   pl.BlockSpec((ts // GROUP, 1, LANES), lambda b, j: (b * (S // ts) + j, 0, 0))],
        out_shape=[jax.ShapeDtypeStruct((B, S, D), f32), jax.ShapeDtypeStruct((B, S, D), bf16),
                   jax.ShapeDtypeStruct((B, S, LANES), jnp.int32),
                   jax.ShapeDtypeStruct((B, S, LANES), f32),
                   jax.ShapeDtypeStruct((B * S // GROUP, 1, LANES), f32)],
        compiler_params=pltpu.CompilerParams(
            dimension_semantics=("parallel", "parallel"), vmem_limit_bytes=VMEM_LIMIT),
        name="xattn_router",
    )(x1, gx, w_xq, qg, kx, vx, w_xo, gf, wr_hi, wr_lo, br)


u32 = jnp.uint32
HIGH_HALF = 0xFFFF0000


def _pack_pairs(x):
    w = x.shape[1] // 2
    lo = pltpu.bitcast(x[:, :w], u32) >> 16
    hi = pltpu.bitcast(x[:, w:], u32) & u32(HIGH_HALF)
    return lo | hi


def _unpack_pairs(u):
    lo = pltpu.bitcast(u << 16, f32)
    hi = pltpu.bitcast(u & u32(HIGH_HALF), f32)
    return jnp.concatenate([lo, hi], axis=-1).astype(bf16)


def _for_cells(g, fn):
    def body(e, carry):
        fn(g * N_EXPERTS + e)
        return carry
    lax.fori_loop(0, N_EXPERTS, body, 0)


def _group_positions(idxv, coffv, ltri):
    lane = lax.broadcasted_iota(jnp.int32, idxv.shape, 1)
    hots = [lane == idxv[:, kk:kk + 1] for kk in range(TOP_K)]
    multi = jnp.zeros(idxv.shape, f32)
    for h in hots:
        multi = jnp.where(h, 1.0, multi)
    rank = _dot(ltri, multi.astype(bf16))
    posmat = coffv + rank
    return [jnp.sum(jnp.where(h, posmat, 0.0), axis=-1, keepdims=True) for h in hots]


def _dispatch_kernel(coff_s, rows_s, gbase_s, gtot_s, toff_s, tn_s, nu_s,
                     hf_ref, idx_ref, coffv_ref, ltri_ref,
                     xs_hbm, pos_ref, xbuf, zbuf, sem, zsem, *, n_steps):
    step = pl.program_id(0)
    places = range(STEP_GROUPS)
    group = lambda i: step * STEP_GROUPS + i
    slot = lambda i: lax.rem(step, 2) * STEP_GROUPS + i
    tok = lambda i: slice(i * GROUP, (i + 1) * GROUP)

    def cell_copy(sl, i):
        n = pl.multiple_of(rows_s[i], SUBLANES)
        src = pl.multiple_of(coff_s[i], SUBLANES)
        dst = pl.multiple_of(gbase_s[i], SUBLANES)
        return n, pltpu.make_async_copy(xbuf.at[sl, pl.ds(src, n)], xs_hbm.at[pl.ds(dst, n)],
                                        sem.at[sl])

    def start_cell(sl, i):
        n, cp = cell_copy(sl, i)
        pl.when(n > 0)(cp.start)

    def wait_group(sl, gg):
        n = pl.multiple_of(gtot_s[gg], SUBLANES)
        pltpu.make_async_copy(xbuf.at[sl, pl.ds(0, n)], xs_hbm.at[pl.ds(0, n)], sem.at[sl]).wait()

    @pl.when(step >= 2)
    def _():
        for i in places:
            wait_group(slot(i), group(i) - 2 * STEP_GROUPS)

    pos = [_group_positions(idx_ref[tok(i), :], coffv_ref[i], ltri_ref[...]) for i in places]
    lane = lax.broadcasted_iota(jnp.int32, (GROUP, LANES), 1)
    riota = lax.broadcasted_iota(jnp.int32, (GROUP, GROUP_ROWS), 1)
    pts = []
    for i in places:
        pt = jnp.zeros((GROUP, GROUP_ROWS), f32)
        pos_out = jnp.zeros((GROUP, LANES), f32)
        for kk in range(TOP_K):
            pt = jnp.where(riota == pos[i][kk].astype(jnp.int32), 1.0, pt)
            pos_out = jnp.where(lane == kk, pos[i][kk], pos_out)
        pos_ref[tok(i), :] = pos_out
        pts.append(pt.astype(bf16))
    rows = [lax.dot_general(pts[i], hf_ref[tok(i), :], (((0,), (0,)), ((), ())),
                            preferred_element_type=f32) for i in places]
    for i in places:
        xbuf[slot(i)] = _pack_pairs(rows[i])
    for i in places:
        _for_cells(group(i), functools.partial(start_cell, slot(i)))

    @pl.when(step == n_steps - 1)
    def _():
        if n_steps >= 2:
            for i in places:
                prev_slot = (1 - lax.rem(step, 2)) * STEP_GROUPS + i
                wait_group(prev_slot, group(i) - STEP_GROUPS)
        for i in places:
            wait_group(slot(i), group(i))
        zbuf[...] = jnp.zeros(zbuf.shape, u32)

        def tail_copy(e):
            n = pl.multiple_of(tn_s[e], SUBLANES)
            dst = pl.multiple_of(toff_s[e], SUBLANES)
            return n, pltpu.make_async_copy(zbuf.at[pl.ds(0, n)], xs_hbm.at[pl.ds(dst, n)], zsem)

        def unused_copy(b):
            dst = pl.multiple_of(b * EXPERT_TILE, EXPERT_TILE)
            return pltpu.make_async_copy(zbuf, xs_hbm.at[pl.ds(dst, EXPERT_TILE)], zsem)

        def run(method):
            def tail(e, carry):
                n, cp = tail_copy(e)
                pl.when(n > 0)(getattr(cp, method))
                return carry
            lax.fori_loop(0, N_EXPERTS, tail, 0)

            def unused(b, carry):
                getattr(unused_copy(b), method)()
                return carry
            lax.fori_loop(nu_s[0], xs_hbm.shape[0] // EXPERT_TILE, unused, 0)
        run("start")
        run("wait")


def _dispatch(plan, hf, idx, n_rows):
    n_tok, D = hf.shape
    n_steps = n_tok // (GROUP * STEP_GROUPS)
    assert n_steps * GROUP * STEP_GROUPS == n_tok
    ltri = jnp.asarray(np.tril(np.ones((GROUP, GROUP)), -1), bf16)
    tok = lambda w: pl.BlockSpec((STEP_GROUPS * GROUP, w), lambda s, *_: (s, 0))
    grid_spec = pltpu.PrefetchScalarGridSpec(
        num_scalar_prefetch=7,
        grid=(n_steps,),
        in_specs=[tok(D), tok(LANES),
                  pl.BlockSpec((STEP_GROUPS, 1, LANES), lambda s, *_: (s, 0, 0)),
                  pl.BlockSpec((GROUP, GROUP), lambda s, *_: (0, 0))],
        out_specs=[pl.BlockSpec(memory_space=pl.ANY), tok(LANES)],
        scratch_shapes=[pltpu.VMEM((2 * STEP_GROUPS, GROUP_ROWS, D // 2), u32),
                        pltpu.VMEM((EXPERT_TILE, D // 2), u32),
                        pltpu.SemaphoreType.DMA((2 * STEP_GROUPS,)), pltpu.SemaphoreType.DMA(())],
    )
    return pl.pallas_call(
        functools.partial(_dispatch_kernel, n_steps=n_steps),
        grid_spec=grid_spec,
        out_shape=[jax.ShapeDtypeStruct((n_rows, D // 2), u32),
                   jax.ShapeDtypeStruct((n_tok, LANES), f32)],
        compiler_params=pltpu.CompilerParams(
            dimension_semantics=("arbitrary",), vmem_limit_bytes=VMEM_LIMIT),
        name="dispatch",
    )(plan["coff"], plan["rows"], plan["gbase"], plan["gtot"], plan["tailoff"], plan["tailn"],
      plan["n_used"], hf, idx, plan["coffv"], ltri)


def _expert_kernel(eb_ref, nb_ref, nu_ref, wgu_ref, bgu_ref, wdn_ref, bdn_ref, pm_ref, xs_hbm,
                   y_hbm, wgu_b, wdn_b, xin, yout, in_sem, out_sem):
    e = pl.program_id(0)
    tm = xin.shape[1]
    n_used = nu_ref[0]

    def in_copy(b):
        sl = lax.rem(b, 2)
        rows = pl.ds(pl.multiple_of(b * tm, tm), tm)
        return pltpu.make_async_copy(xs_hbm.at[rows], xin.at[sl], in_sem.at[sl])

    def out_copy(b):
        sl = lax.rem(b, 2)
        rows = pl.ds(pl.multiple_of(b * tm, tm), tm)
        return pltpu.make_async_copy(yout.at[sl], y_hbm.at[rows], out_sem.at[sl])

    @pl.when(e == 0)
    def _():
        in_copy(0).start()

    @pl.when(nb_ref[e] > 0)
    def _():
        for c in range(2 * D_FF // GU_TILE):
            cols = slice(c * GU_TILE, (c + 1) * GU_TILE)
            wgu_b[:, cols] = _dot(wgu_ref[0, :, cols].astype(bf16), pm_ref[...]).astype(bf16)
        for c in range(D_FF // LANES):
            rows = slice(c * LANES, (c + 1) * LANES)
            wdn_b[rows, :] = wdn_ref[0, rows, :].astype(bf16)

        def block(b, carry):
            sl = lax.rem(b, 2)
            in_copy(b).wait()

            @pl.when(b + 1 < n_used)
            def _():
                in_copy(b + 1).start()

            @pl.when(b >= 2)
            def _():
                out_copy(b - 2).wait()

            _expert_block(xin.at[sl], bgu_ref, bdn_ref, yout.at[sl], wgu_b, wdn_b)
            out_copy(b).start()
            return carry
        lax.fori_loop(eb_ref[e], eb_ref[e] + nb_ref[e], block, 0)

    @pl.when(e == pl.num_programs(0) - 1)
    def _():
        @pl.when(n_used >= 2)
        def _():
            out_copy(n_used - 2).wait()
        out_copy(n_used - 1).wait()
        yout[0] = jnp.zeros(yout.shape[1:], u32)

        def run(method):
            def unused(b, carry):
                rows = pl.ds(pl.multiple_of(b * tm, tm), tm)
                getattr(pltpu.make_async_copy(yout.at[0], y_hbm.at[rows], out_sem.at[0]), method)()
                return carry
            lax.fori_loop(n_used, y_hbm.shape[0] // tm, unused, 0)
        run("start")
        run("wait")


def _expert_block(x_ref, bgu_ref, bdn_ref, y_ref, wgu_b, wdn_b):
    gu = _dot(_unpack_pairs(x_ref[...]), wgu_b[...]) + bgu_ref[0]
    acts = []
    for c in range(2 * D_FF // GU_TILE):
        g = jnp.minimum(gu[:, c * GU_TILE:c * GU_TILE + LANES], SWIGLU_LIMIT)
        lin = jnp.clip(gu[:, c * GU_TILE + LANES:(c + 1) * GU_TILE], -SWIGLU_LIMIT, SWIGLU_LIMIT)
        acts.append((g * jax.nn.sigmoid(SWIGLU_ALPHA * g) * (lin + 1.0)).astype(bf16))
    act = jnp.concatenate(acts, axis=-1)
    y = _dot(act, wdn_b[...]) + bdn_ref[0]
    y_ref[...] = _pack_pairs(y.astype(bf16).astype(f32))


def _gate_up_perm():
    j = np.arange(GU_TILE)
    p = np.zeros((GU_TILE, GU_TILE), np.float32)
    p[np.where(j < LANES, 2 * j, 2 * (j - LANES) + 1), j] = 1.0
    return jnp.asarray(p, bf16)


def _expert_mlp(plan, xs, w_gu, b_gu, w_dn, b_dn):
    n_rows = xs.shape[0]
    D = D_MODEL
    tm = EXPERT_TILE
    per_e = lambda e, *_: (e, 0, 0)
    grid_spec = pltpu.PrefetchScalarGridSpec(
        num_scalar_prefetch=3,
        grid=(N_EXPERTS,),
        in_specs=[pl.BlockSpec((1, D, 2 * D_FF), per_e),
                  pl.BlockSpec((1, 1, 2 * D_FF), per_e),
                  pl.BlockSpec((1, D_FF, D), per_e),
                  pl.BlockSpec((1, 1, D), per_e),
                  pl.BlockSpec((GU_TILE, GU_TILE), lambda e, *_: (0, 0)),
                  pl.BlockSpec(memory_space=pl.ANY)],
        out_specs=pl.BlockSpec(memory_space=pl.ANY),
        scratch_shapes=[pltpu.VMEM((D, 2 * D_FF), bf16), pltpu.VMEM((D_FF, D), bf16),
                        pltpu.VMEM((2, tm, D // 2), u32), pltpu.VMEM((2, tm, D // 2), u32),
                        pltpu.SemaphoreType.DMA((2,)), pltpu.SemaphoreType.DMA((2,))],
    )
    return pl.pallas_call(
        _expert_kernel,
        grid_spec=grid_spec,
        out_shape=jax.ShapeDtypeStruct((n_rows, D // 2), u32),
        compiler_params=pltpu.CompilerParams(
            dimension_semantics=("arbitrary",), vmem_limit_bytes=VMEM_LIMIT),
        name="expert_mlp",
    )(plan["eblk"], plan["nblk"], plan["n_used"], w_gu, b_gu, w_dn, b_dn, _gate_up_perm(), xs)


def _combine_kernel(coff_s, rows_s, gbase_s, gtot_s, x_ref, pos_ref, gate_ref, y_hbm, o_ref, ybuf, sem,
                    *, n_steps):
    step = pl.program_id(0)
    places = range(STEP_GROUPS)
    group = lambda s, i: s * STEP_GROUPS + i
    slot = lambda s, i: lax.rem(s, 2) * STEP_GROUPS + i
    tok = lambda i: slice(i * GROUP, (i + 1) * GROUP)

    def cell_copy(sl, i):
        n = pl.multiple_of(rows_s[i], SUBLANES)
        loc = pl.multiple_of(coff_s[i], SUBLANES)
        src = pl.multiple_of(gbase_s[i], SUBLANES)
        return n, pltpu.make_async_copy(y_hbm.at[pl.ds(src, n)], ybuf.at[sl, pl.ds(loc, n)],
                                        sem.at[sl])

    def start_cell(sl, i):
        n, cp = cell_copy(sl, i)
        pl.when(n > 0)(cp.start)

    def wait_group(sl, gg):
        n = pl.multiple_of(gtot_s[gg], SUBLANES)
        pltpu.make_async_copy(y_hbm.at[pl.ds(0, n)], ybuf.at[sl, pl.ds(0, n)], sem.at[sl]).wait()

    @pl.when(step == 0)
    def _():
        ybuf[...] = jnp.zeros(ybuf.shape, u32)
        for i in places:
            _for_cells(group(step, i), functools.partial(start_cell, slot(step, i)))

    @pl.when(step + 1 < n_steps)
    def _():
        for i in places:
            _for_cells(group(step + 1, i), functools.partial(start_cell, slot(step + 1, i)))

    for i in places:
        wait_group(slot(step, i), group(step, i))

    riota = lax.broadcasted_iota(jnp.int32, (GROUP, GROUP_ROWS), 1)
    pts = []
    for i in places:
        pt = jnp.zeros((GROUP, GROUP_ROWS), f32)
        for kk in range(TOP_K):
            pk = pos_ref[tok(i), kk:kk + 1].astype(jnp.int32)
            pt = jnp.where(riota == pk, gate_ref[tok(i), kk:kk + 1], pt)
        pts.append(pt.astype(bf16))
    ys = [_unpack_pairs(ybuf[slot(step, i)]) for i in places]
    outs = [_dot(pts[i], ys[i]) for i in places]
    for i in places:
        o_ref[tok(i), :] = x_ref[tok(i), :] + outs[i]


def _combine(plan, x2, pos, gates, y_rows):
    n_tok, D = x2.shape
    n_steps = n_tok // (GROUP * STEP_GROUPS)
    assert n_steps * GROUP * STEP_GROUPS == n_tok
    tok = lambda w: pl.BlockSpec((STEP_GROUPS * GROUP, w), lambda s, *_: (s, 0))
    grid_spec = pltpu.PrefetchScalarGridSpec(
        num_scalar_prefetch=4,
        grid=(n_steps,),
        in_specs=[tok(D), tok(LANES), tok(LANES), pl.BlockSpec(memory_space=pl.ANY)],
        out_specs=tok(D),
        scratch_shapes=[pltpu.VMEM((2 * STEP_GROUPS, GROUP_ROWS, D // 2), u32),
                        pltpu.SemaphoreType.DMA((2 * STEP_GROUPS,))],
    )
    return pl.pallas_call(
        functools.partial(_combine_kernel, n_steps=n_steps),
        grid_spec=grid_spec,
        out_shape=jax.ShapeDtypeStruct((n_tok, D), f32),
        compiler_params=pltpu.CompilerParams(
            dimension_semantics=("arbitrary",), vmem_limit_bytes=VMEM_LIMIT),
        name="combine",
    )(plan["coff"], plan["rows"], plan["gbase"], plan["gtot"], x2, pos, gates, y_rows)


def _routing_plan(cnt, n_tok):
    tm = EXPERT_TILE
    n_groups = n_tok // GROUP
    cnt = cnt.reshape(n_groups, LANES)[:, :N_EXPERTS].astype(jnp.int32)
    rows = (cnt + SUBLANES - 1) // SUBLANES * SUBLANES
    coff = jnp.cumsum(rows, axis=1) - rows
    tot = jnp.sum(rows, axis=0)
    padded = (tot + tm - 1) // tm * tm
    pend = jnp.cumsum(padded)
    ebase = pend - padded
    gbase = ebase[None, :] + jnp.cumsum(rows, axis=0) - rows
    flat = lambda a: a.reshape(-1).astype(jnp.int32)
    return dict(
        coff=flat(coff), rows=flat(rows), gbase=flat(gbase), gtot=flat(jnp.sum(rows, axis=1)),
        tailoff=flat(ebase + tot), tailn=flat(padded - tot),
        eblk=flat(ebase // tm), nblk=flat(padded // tm), n_used=flat(pend[-1] // tm),
        coffv=jnp.zeros((n_groups, 1, LANES), f32).at[:, 0, :N_EXPERTS].set(coff.astype(f32)))


def _max_rows(n_tok):
    tm = EXPERT_TILE
    worst = (n_tok * TOP_K + (n_tok // GROUP) * N_EXPERTS * (SUBLANES - 1)
             + N_EXPERTS * (tm - SUBLANES))
    return (worst + tm - 1) // tm * tm


def _moe(x2, hf, idx, gates, cnt, w_gate_up, b_gate_up, w_down, b_down):
    n_tok, D = x2.shape
    plan = _routing_plan(cnt, n_tok)
    xs, pos = _dispatch(plan, hf, idx, _max_rows(n_tok))
    b_gu = (b_gate_up.reshape(N_EXPERTS, 2 * D_FF // GU_TILE, LANES, 2).transpose(0, 1, 3, 2)
            .reshape(N_EXPERTS, 1, 2 * D_FF))
    y_rows = _expert_mlp(plan, xs, w_gate_up, b_gu, w_down, b_down.reshape(N_EXPERTS, 1, D))
    return _combine(plan, x2, pos, gates, y_rows)


BIAS_ROW = (2 * Q_TILE + LEFT_CHUNKS * CHUNK + LANES - 2) // LANES * LANES


def _band_bias_kernel(r_ref, o_ref):
    x = jnp.broadcast_to(r_ref[0], (Q_TILE, BIAS_ROW))
    x = pltpu.roll(x, BIAS_ROW - (Q_TILE - 1), axis=1, stride=1, stride_axis=0)
    qc = lax.broadcasted_iota(jnp.int32, (Q_TILE, K_BAND), 0) // CHUNK
    kc = lax.broadcasted_iota(jnp.int32, (Q_TILE, K_BAND), 1) // CHUNK
    first_chunk = LEFT_CHUNKS - pl.program_id(0) * (Q_TILE // CHUNK)
    visible = (kc >= qc) & (kc <= qc + LEFT_CHUNKS) & (kc >= first_chunk)
    o_ref[0, 0] = jnp.where(visible, x[:, :K_BAND] * LOG2E, NEG_INF)


def _band_bias(rel_bias):
    H = rel_bias.shape[0]
    d_max = Q_TILE - 1 + LEFT_CHUNKS * CHUNK
    n_far = d_max - MAX_REL + 1
    r = jnp.concatenate(
        [jnp.broadcast_to(rel_bias[:, 2 * MAX_REL:], (H, n_far)),
         rel_bias[:, MAX_REL - (Q_TILE - 1):2 * MAX_REL][:, ::-1],
         jnp.zeros((H, BIAS_ROW - (2 * Q_TILE - 1 + LEFT_CHUNKS * CHUNK)), rel_bias.dtype)], axis=1)
    return pl.pallas_call(
        _band_bias_kernel,
        grid=(START_TILES + 1, H),
        in_specs=[pl.BlockSpec((1, 1, BIAS_ROW), lambda t, h: (h, 0, 0))],
        out_specs=pl.BlockSpec((1, 1, Q_TILE, K_BAND), lambda t, h: (t, h, 0, 0)),
        out_shape=jax.ShapeDtypeStruct((START_TILES + 1, H, Q_TILE, K_BAND), f32),
        name="band_bias",
    )(r.astype(f32).reshape(H, 1, BIAS_ROW))


def _layer(x, mem, norm_mix_g, w_in, q_norm_g, k_norm_g, rel_bias, conv_w, out_norm_a_g,
           out_norm_b_g, w_out, norm_xattn_g, norm_mem_g, w_xq, w_xkv, xq_norm_g, xk_norm_g,
           w_xo, norm_ffn_g, w_router, b_router, w_gate_up, b_gate_up, w_down, b_down):
    B, S, D = x.shape
    n_tok = B * S
    row = lambda a: a.reshape(1, -1).astype(f32)

    qg = row(jnp.tile(q_norm_g, ATT_HEADS) * (ATT_HEAD_DIM ** -0.5 * LOG2E))
    kg = row(jnp.tile(k_norm_g, ATT_HEADS))
    head_ones = jnp.asarray(np.kron(np.eye(ATT_HEADS), np.ones((ATT_HEAD_DIM, ATT_HEAD_DIM))), bf16)
    q, k, v, yb = _mix_in(x, row(norm_mix_g), w_in.astype(bf16), qg, kg, head_ones,
                          conv_w.astype(f32), row(out_norm_b_g))
    x1 = _attn_out(q, k, v, _band_bias(rel_bias), yb, x, w_out.astype(bf16), row(out_norm_a_g))

    kx, vx = _mem_kv(mem, row(norm_mem_g), w_xkv.astype(bf16), row(xk_norm_g))
    wr = jnp.zeros((D, LANES), f32).at[:, :N_EXPERTS].set(w_router)
    wr_hi = wr.astype(bf16)
    wr_lo = (wr - wr_hi.astype(f32)).astype(bf16)
    br = jnp.full((1, LANES), -jnp.inf, f32).at[0, :N_EXPERTS].set(b_router)
    x2, hf, idx, gates, cnt = _xattn_router(
        x1, row(norm_xattn_g), w_xq.astype(bf16), row(xq_norm_g * (X_HEAD_DIM ** -0.5 * LOG2E)),
        kx, vx,
        w_xo.astype(bf16), row(norm_ffn_g), wr_hi, wr_lo, br)

    out = _moe(x2.reshape(n_tok, D), hf.reshape(n_tok, D), idx.reshape(n_tok, LANES),
               gates.reshape(n_tok, LANES), cnt, w_gate_up, b_gate_up, w_down, b_down)
    return out.reshape(B, S, D)


def kernel(x, mem, norm_mix_g, w_in, q_norm_g, k_norm_g, rel_bias, conv_w, out_norm_a_g,
           out_norm_b_g, w_out, norm_xattn_g, norm_mem_g, w_xq, w_xkv, xq_norm_g, xk_norm_g,
           w_xo, norm_ffn_g, w_router, b_router, w_gate_up, b_gate_up, w_down, b_down):
    depth = norm_mix_g.shape[0]
    for l in range(depth):
        x = _layer(x, mem, norm_mix_g[l], w_in[l], q_norm_g[l], k_norm_g[l], rel_bias[l],
                   conv_w[l], out_norm_a_g[l], out_norm_b_g[l], w_out[l], norm_xattn_g[l],
                   norm_mem_g[l], w_xq[l], w_xkv[l], xq_norm_g[l], xk_norm_g[l], w_xo[l],
                   norm_ffn_g[l], w_router[l], b_router[l], w_gate_up[l], b_gate_up[l],
                   w_down[l], b_down[l])
    return x
```

```python
import functools

import jax
import jax.numpy as jnp
import numpy as np
from jax import lax
from jax.experimental import pallas as pl
from jax.experimental.pallas import tpu as pltpu

D_MODEL = 1024
CHUNK = 64
LEFT_CHUNKS = 8
N_MEM = 256
ATT_HEADS = 8
ATT_HEAD_DIM = 64
D_ATT = ATT_HEADS * ATT_HEAD_DIM
D_CONV = 512
MAX_REL = 256
X_HEADS = 4
X_HEAD_DIM = 128
D_X = X_HEADS * X_HEAD_DIM
N_EXPERTS = 32
TOP_K = 4
D_FF = D_MODEL
SWIGLU_LIMIT = 7.0
SWIGLU_ALPHA = 1.702
EPS = 1e-6
NEG_INF = -1e30

LANES = 128
SEQ_TILE = 1024
Q_TILE = 256
K_BAND = Q_TILE + LEFT_CHUNKS * CHUNK
START_TILES = LEFT_CHUNKS * CHUNK // Q_TILE
LOG2E = 1.4426950408889634
HEAD_ROUND = 4
EXPERT_TILE = 512
GU_TILE = 2 * LANES
GROUP = 256
STEP_GROUPS = 2
SUBLANES = 8
GROUP_ROWS = 1280
VMEM_LIMIT = 56 * 1024 * 1024

bf16 = jnp.bfloat16
f32 = jnp.float32


def _rms(x, g):
    return x * lax.rsqrt(jnp.mean(x * x, axis=-1, keepdims=True) + EPS) * g


def _dot(a, b):
    return jnp.dot(a, b, preferred_element_type=f32)


def _dot_nt(a, b):
    return lax.dot_general(a, b, (((1,), (1,)), ((), ())), preferred_element_type=f32)


def _mix_in_kernel(x_ref, g_ref, w_ref, qg_ref, kg_ref, hm_ref, cw_ref, gb_ref,
                   q_ref, k_ref, v_ref, yb_ref, ub_ref):
    ts = x_ref.shape[1]

    @pl.when(pl.program_id(1) == 0)
    def _():
        ub_ref[0:8, :] = jnp.zeros((8, D_CONV), f32)

    hb = _rms(x_ref[0], g_ref[...]).astype(bf16)

    def head_norm(t, gain):
        ms = _dot((t * t).astype(bf16), hm_ref[...]) * (1.0 / ATT_HEAD_DIM)
        return t * lax.rsqrt(ms + EPS) * gain

    q = _dot(hb, w_ref[:, 0:D_ATT])
    q_ref[0] = head_norm(q, qg_ref[...]).astype(bf16)
    k = _dot(hb, w_ref[:, D_ATT:2 * D_ATT])
    k_ref[0] = head_norm(k, kg_ref[...]).astype(bf16)
    v_ref[0] = _dot(hb, w_ref[:, 2 * D_ATT:3 * D_ATT]).astype(bf16)

    o = 3 * D_ATT
    bg = _dot(hb, w_ref[:, o:o + D_CONV])
    cg = _dot(hb, w_ref[:, o + D_CONV:o + 2 * D_CONV])
    xv = _dot(hb, w_ref[:, o + 2 * D_CONV:o + 3 * D_CONV])
    u = cg * xv
    ub_ref[8:8 + ts, :] = u
    conv = (cw_ref[0:1, :] * ub_ref[6:6 + ts, :] + cw_ref[1:2, :] * ub_ref[7:7 + ts, :]
            + cw_ref[2:3, :] * u)
    ub_ref[0:8, :] = u[ts - 8:ts, :]
    yb_ref[0] = _rms(bg * conv, gb_ref[...]).astype(bf16)


def _mix_in(x, g, w_in, qg, kg, hm, cw, gb):
    B, S, D = x.shape
    ts = SEQ_TILE
    full = lambda shape: pl.BlockSpec(shape, lambda b, j: (0,) * len(shape))
    tile = lambda w: pl.BlockSpec((1, ts, w), lambda b, j: (b, j, 0))
    out = jax.ShapeDtypeStruct((B, S, D_ATT), bf16)
    return pl.pallas_call(
        _mix_in_kernel,
        grid=(B, S // ts),
        in_specs=[tile(D), full((1, D)), full(w_in.shape), full((1, D_ATT)), full((1, D_ATT)),
                  full((D_ATT, D_ATT)), full((3, D_CONV)), full((1, D_CONV))],
        out_specs=[tile(D_ATT)] * 4,
        out_shape=[out] * 4,
        scratch_shapes=[pltpu.VMEM((ts + 8, D_CONV), f32)],
        compiler_params=pltpu.CompilerParams(
            dimension_semantics=("arbitrary", "arbitrary"), vmem_limit_bytes=VMEM_LIMIT),
        name="mix_in",
    )(x, g, w_in, qg, kg, hm, cw, gb)


def _attn_out_kernel(q_ref, k_ref, v_ref, bias_ref, yb_ref, x_ref, w_ref, ga_ref,
                     o_ref, kb_ref, vb_ref):
    tq = q_ref.shape[1]
    n_kc = K_BAND // CHUNK
    c0 = pl.program_id(1) * (tq // CHUNK)
    for kc in range(n_kc):
        src = pl.multiple_of(jnp.maximum(c0 - LEFT_CHUNKS + kc, 0) * CHUNK, CHUNK)
        kb_ref[kc * CHUNK:(kc + 1) * CHUNK, :] = k_ref[0, pl.ds(src, CHUNK), :]
        vb_ref[kc * CHUNK:(kc + 1) * CHUNK, :] = v_ref[0, pl.ds(src, CHUNK), :]

    lane = lax.broadcasted_iota(jnp.int32, (1, LANES), 1)
    low = lane < ATT_HEAD_DIM

    outs = []
    for h0 in range(0, ATT_HEADS, HEAD_ROUND):
        heads = range(h0, h0 + HEAD_ROUND)
        scores, probs, denoms = {}, {}, {}
        for h in heads:
            sl = slice((h // 2) * LANES, (h // 2 + 1) * LANES)
            qp = q_ref[0, :, sl]
            keep = low if h % 2 == 0 else jnp.logical_not(low)
            scores[h] = _dot_nt(jnp.where(keep, qp, jnp.zeros_like(qp)), kb_ref[:, sl])
        for h in heads:
            s = scores[h] + bias_ref[0, h]
            p = jnp.exp2(s - jnp.max(s, axis=-1, keepdims=True))
            denoms[h] = jnp.sum(p, axis=-1, keepdims=True)
            probs[h] = p.astype(bf16)
        for h in heads:
            sl = slice((h // 2) * LANES, (h // 2 + 1) * LANES)
            outs.append(_dot(probs[h], vb_ref[:, sl]) / denoms[h])
    ya = jnp.concatenate([jnp.where(low, outs[2 * hp], outs[2 * hp + 1])
                          for hp in range(ATT_HEADS // 2)], axis=-1)
    yan = _rms(ya, ga_ref[...]).astype(bf16)
    o_ref[0] = (x_ref[0] + _dot(yan, w_ref[0:D_ATT, :]) + _dot(yb_ref[0], w_ref[D_ATT:, :]))


def _attn_out(q, k, v, bias, yb, x, w_out, ga):
    B, S, D = x.shape
    tq = Q_TILE
    full = lambda shape: pl.BlockSpec(shape, lambda b, j: (0,) * len(shape))
    tile = lambda w: pl.BlockSpec((1, tq, w), lambda b, j: (b, j, 0))
    seq = pl.BlockSpec((1, S, D_ATT), lambda b, j: (b, 0, 0))
    return pl.pallas_call(
        _attn_out_kernel,
        grid=(B, S // tq),
        in_specs=[tile(D_ATT), seq, seq,
                  pl.BlockSpec((1,) + bias.shape[1:], lambda b, j: (jnp.minimum(j, START_TILES), 0, 0, 0)),
                  tile(D_CONV), tile(D),
                  full(w_out.shape), full((1, D_ATT))],
        out_specs=tile(D),
        out_shape=jax.ShapeDtypeStruct((B, S, D), f32),
        scratch_shapes=[pltpu.VMEM((K_BAND, D_ATT), bf16), pltpu.VMEM((K_BAND, D_ATT), bf16)],
        compiler_params=pltpu.CompilerParams(
            dimension_semantics=("parallel", "parallel"), vmem_limit_bytes=VMEM_LIMIT),
        name="attn_out",
    )(q, k, v, bias, yb, x, w_out, ga)


def _mem_kv_kernel(m_ref, g_ref, w_ref, kg_ref, k_ref, v_ref):
    hm = _rms(m_ref[0], g_ref[...]).astype(bf16)
    kv = _dot(hm, w_ref[...])
    for h in range(X_HEADS):
        sl = slice(h * X_HEAD_DIM, (h + 1) * X_HEAD_DIM)
        k_ref[0, :, sl] = _rms(kv[:, sl], kg_ref[...]).astype(bf16)
    v_ref[0] = kv[:, D_X:].astype(bf16)


def _mem_kv(mem, g, w_xkv, kg):
    B = mem.shape[0]
    full = lambda shape: pl.BlockSpec(shape, lambda b: (0,) * len(shape))
    out = jax.ShapeDtypeStruct((B, N_MEM, D_X), bf16)
    return pl.pallas_call(
        _mem_kv_kernel,
        grid=(B,),
        in_specs=[pl.BlockSpec((1, N_MEM, D_MODEL), lambda b: (b, 0, 0)), full((1, D_MODEL)),
                  full(w_xkv.shape), full((1, X_HEAD_DIM))],
        out_specs=[pl.BlockSpec((1, N_MEM, D_X), lambda b: (b, 0, 0))] * 2,
        out_shape=[out, out],
        compiler_params=pltpu.CompilerParams(
            dimension_semantics=("parallel",), vmem_limit_bytes=VMEM_LIMIT),
        name="mem_kv",
    )(mem, g, w_xkv, kg)


def _xattn_router_kernel(x_ref, gx_ref, wq_ref, qg_ref, k_ref, v_ref, wo_ref, gf_ref,
                         wr_ref, br_ref,
                         x2_ref, hf_ref, route_ref, cnt_ref):
    x1 = x_ref[0]
    hb = _rms(x1, gx_ref[...]).astype(bf16)
    q = _dot(hb, wq_ref[...])
    hsl = [slice(h * X_HEAD_DIM, (h + 1) * X_HEAD_DIM) for h in range(X_HEADS)]
    scores = [_dot_nt(_rms(q[:, sl], qg_ref[...]).astype(bf16), k_ref[0, :, sl]) for sl in hsl]
    probs, denoms = [], []
    for s in scores:
        p = jnp.exp2(s - jnp.max(s, axis=-1, keepdims=True))
        denoms.append(jnp.sum(p, axis=-1, keepdims=True))
        probs.append(p.astype(bf16))
    heads = [(_dot(p, v_ref[0, :, sl]) / l).astype(bf16) for p, l, sl in zip(probs, denoms, hsl)]
    o = jnp.concatenate(heads, axis=-1)
    x2 = x1 + _dot(o, wo_ref[...])
    x2_ref[0] = x2

    hf = _rms(x2, gf_ref[...])
    hf_hi = hf.astype(bf16)
    hf_ref[0] = hf_hi
    hf_lo = (hf - hf_hi.astype(f32)).astype(bf16)
    hi_prod = _dot(hf_hi, wr_ref[...])
    logits = (hi_prod[:, :LANES] + hi_prod[:, LANES:] + _dot(hf_lo, wr_ref[:, :LANES])
              + br_ref[...])
    lt = logits.T[:N_EXPERTS]
    eidx = lax.broadcasted_iota(jnp.int32, lt.shape, 0)
    vals, idxs = [], []
    for _ in range(TOP_K):
        m = jnp.max(lt, axis=0, keepdims=True)
        i = jnp.min(jnp.where(lt == m, eidx, N_EXPERTS), axis=0, keepdims=True)
        lt = jnp.where(eidx == i, -jnp.inf, lt)
        vals.append(m)
        idxs.append(i)
    es = [jnp.exp(v - vals[0]) for v in vals]
    tot = es[0] + es[1] + es[2] + es[3]
    packed = jnp.concatenate([i.astype(f32) for i in idxs] + [e / tot for e in es]
                             + [jnp.zeros((LANES - 2 * TOP_K, lt.shape[1]), f32)], axis=0)
    route_ref[0] = packed.T
    sel = jnp.zeros(lt.shape, f32)
    for i in idxs:
        sel = jnp.where(eidx == i, 1.0, sel)
    lane = lax.broadcasted_iota(jnp.int32, (N_EXPERTS, LANES), 1)
    cnt = jnp.zeros((N_EXPERTS, LANES), f32)
    for j in range(lt.shape[1] // GROUP):
        cnt = jnp.where(lane == j, jnp.sum(sel[:, j * GROUP:(j + 1) * GROUP], axis=1, keepdims=True), cnt)
    cnt_ref[0] = cnt


def _xattn_router(x1, gx, w_xq, qg, kx, vx, w_xo, gf, wr, br):
    B, S, D = x1.shape
    ts = SEQ_TILE
    full = lambda shape: pl.BlockSpec(shape, lambda b, j: (0,) * len(shape))
    tile = lambda w: pl.BlockSpec((1, ts, w), lambda b, j: (b, j, 0))
    mem = pl.BlockSpec((1, N_MEM, D_X), lambda b, j: (b, 0, 0))
    return pl.pallas_call(
        _xattn_router_kernel,
        grid=(B, S // ts),
        in_specs=[tile(D), full((1, D)), full(w_xq.shape), full((1, X_HEAD_DIM)), mem, mem,
                  full(w_xo.shape), full((1, D)), full(wr.shape),
                  full((1, LANES))],
        out_specs=[tile(D), tile(D), tile(LANES),
                   pl.BlockSpec((1, N_EXPERTS, LANES), lambda b, j: (b * (S // ts) + j, 0, 0))],
        out_shape=[jax.ShapeDtypeStruct((B, S, D), f32), jax.ShapeDtypeStruct((B, S, D), bf16),
                   jax.ShapeDtypeStruct((B, S, LANES), f32),
                   jax.ShapeDtypeStruct((B * S // ts, N_EXPERTS, LANES), f32)],
        compiler_params=pltpu.CompilerParams(
            dimension_semantics=("parallel", "parallel"), vmem_limit_bytes=VMEM_LIMIT),
        name="xattn_router",
    )(x1, gx, w_xq, qg, kx, vx, w_xo, gf, wr, br)


u32 = jnp.uint32
HIGH_HALF = 0xFFFF0000


def _pack_pairs(x):
    w = x.shape[1] // 2
    lo = pltpu.bitcast(x[:, :w], u32) >> 16
    hi = pltpu.bitcast(x[:, w:], u32) & u32(HIGH_HALF)
    return lo | hi


def _unpack_pairs(u):
    lo = pltpu.bitcast(u << 16, f32)
    hi = pltpu.bitcast(u & u32(HIGH_HALF), f32)
    return jnp.concatenate([lo, hi], axis=-1).astype(bf16)


def _for_cells(g, fn):
    def body(e, carry):
        fn(g * N_EXPERTS + e)
        return carry
    lax.fori_loop(0, N_EXPERTS, body, 0)


def _group_positions(route, coffv, ltri):
    lane = lax.broadcasted_iota(jnp.int32, route.shape, 1).astype(f32)
    hots = [lane == route[:, kk:kk + 1] for kk in range(TOP_K)]
    multi = jnp.zeros(route.shape, f32)
    for h in hots:
        multi = jnp.where(h, 1.0, multi)
    rank = _dot(ltri, multi.astype(bf16))
    posmat = coffv + rank
    return [jnp.sum(jnp.where(h, posmat, 0.0), axis=-1, keepdims=True) for h in hots]


def _dispatch_kernel(coff_s, rows_s, gbase_s, gtot_s, toff_s, tn_s, nu_s,
                     hf_ref, route_ref, coffv_ref, ltri_ref,
                     xs_hbm, pos_ref, xbuf, zbuf, sem, zsem, *, n_steps):
    step = pl.program_id(0)
    places = range(STEP_GROUPS)
    group = lambda i: step * STEP_GROUPS + i
    slot = lambda i: lax.rem(step, 2) * STEP_GROUPS + i
    tok = lambda i: slice(i * GROUP, (i + 1) * GROUP)

    def cell_copy(sl, i):
        n = pl.multiple_of(rows_s[i], SUBLANES)
        src = pl.multiple_of(coff_s[i], SUBLANES)
        dst = pl.multiple_of(gbase_s[i], SUBLANES)
        return n, pltpu.make_async_copy(xbuf.at[sl, pl.ds(src, n)], xs_hbm.at[pl.ds(dst, n)],
                                        sem.at[sl])

    def start_cell(sl, i):
        n, cp = cell_copy(sl, i)
        pl.when(n > 0)(cp.start)

    def wait_group(sl, gg):
        n = pl.multiple_of(gtot_s[gg], SUBLANES)
        pltpu.make_async_copy(xbuf.at[sl, pl.ds(0, n)], xs_hbm.at[pl.ds(0, n)], sem.at[sl]).wait()

    @pl.when(step >= 2)
    def _():
        for i in places:
            wait_group(slot(i), group(i) - 2 * STEP_GROUPS)

    pos = [_group_positions(route_ref[tok(i), :], coffv_ref[i], ltri_ref[...]) for i in places]
    lane = lax.broadcasted_iota(jnp.int32, (GROUP, LANES), 1)
    riota = lax.broadcasted_iota(jnp.int32, (GROUP, GROUP_ROWS), 1)
    pts = []
    for i in places:
        pt = jnp.zeros((GROUP, GROUP_ROWS), f32)
        pos_out = jnp.zeros((GROUP, LANES), f32)
        for kk in range(TOP_K):
            pt = jnp.where(riota == pos[i][kk].astype(jnp.int32), 1.0, pt)
            pos_out = jnp.where(lane == kk, pos[i][kk], pos_out)
        pos_ref[tok(i), :] = pos_out
        pts.append(pt.astype(bf16))
    rows = [lax.dot_general(pts[i], hf_ref[tok(i), :], (((0,), (0,)), ((), ())),
                            preferred_element_type=f32) for i in places]
    for i in places:
        xbuf[slot(i)] = _pack_pairs(rows[i])
    for i in places:
        _for_cells(group(i), functools.partial(start_cell, slot(i)))

    @pl.when(step == n_steps - 1)
    def _():
        if n_steps >= 2:
            for i in places:
                prev_slot = (1 - lax.rem(step, 2)) * STEP_GROUPS + i
                wait_group(prev_slot, group(i) - STEP_GROUPS)
        for i in places:
            wait_group(slot(i), group(i))
        zbuf[...] = jnp.zeros(zbuf.shape, u32)

        def tail_copy(e):
            n = pl.multiple_of(tn_s[e], SUBLANES)
            dst = pl.multiple_of(toff_s[e], SUBLANES)
            return n, pltpu.make_async_copy(zbuf.at[pl.ds(0, n)], xs_hbm.at[pl.ds(dst, n)], zsem)

        def unused_copy(b):
            dst = pl.multiple_of(b * EXPERT_TILE, EXPERT_TILE)
            return pltpu.make_async_copy(zbuf, xs_hbm.at[pl.ds(dst, EXPERT_TILE)], zsem)

        def run(method):
            def tail(e, carry):
                n, cp = tail_copy(e)
                pl.when(n > 0)(getattr(cp, method))
                return carry
            lax.fori_loop(0, N_EXPERTS, tail, 0)

            def unused(b, carry):
                getattr(unused_copy(b), method)()
                return carry
            lax.fori_loop(nu_s[0], xs_hbm.shape[0] // EXPERT_TILE, unused, 0)
        run("start")
        run("wait")


def _dispatch(plan, hf, route, n_rows):
    n_tok, D = hf.shape
    n_steps = n_tok // (GROUP * STEP_GROUPS)
    assert n_steps * GROUP * STEP_GROUPS == n_tok
    ltri = jnp.asarray(np.tril(np.ones((GROUP, GROUP)), -1), bf16)
    tok = lambda w: pl.BlockSpec((STEP_GROUPS * GROUP, w), lambda s, *_: (s, 0))
    grid_spec = pltpu.PrefetchScalarGridSpec(
        num_scalar_prefetch=7,
        grid=(n_steps,),
        in_specs=[tok(D), tok(LANES),
                  pl.BlockSpec((STEP_GROUPS, 1, LANES), lambda s, *_: (s, 0, 0)),
                  pl.BlockSpec((GROUP, GROUP), lambda s, *_: (0, 0))],
        out_specs=[pl.BlockSpec(memory_space=pl.ANY), tok(LANES)],
        scratch_shapes=[pltpu.VMEM((2 * STEP_GROUPS, GROUP_ROWS, D // 2), u32),
                        pltpu.VMEM((EXPERT_TILE, D // 2), u32),
                        pltpu.SemaphoreType.DMA((2 * STEP_GROUPS,)), pltpu.SemaphoreType.DMA(())],
    )
    return pl.pallas_call(
        functools.partial(_dispatch_kernel, n_steps=n_steps),
        grid_spec=grid_spec,
        out_shape=[jax.ShapeDtypeStruct((n_rows, D // 2), u32),
                   jax.ShapeDtypeStruct((n_tok, LANES), f32)],
        compiler_params=pltpu.CompilerParams(
            dimension_semantics=("arbitrary",), vmem_limit_bytes=VMEM_LIMIT),
        name="dispatch",
    )(plan["coff"], plan["rows"], plan["gbase"], plan["gtot"], plan["tailoff"], plan["tailn"],
      plan["n_used"], hf, route, plan["coffv"], ltri)


def _expert_kernel(eb_ref, nb_ref, nu_ref, wgu_ref, bgu_ref, wdn_ref, bdn_ref, pm_ref, xs_hbm,
                   y_hbm, wgu_b, wdn_b, xin, yout, in_sem, out_sem):
    e = pl.program_id(0)
    tm = xin.shape[1]
    n_used = nu_ref[0]

    def in_copy(b):
        sl = lax.rem(b, 2)
        rows = pl.ds(pl.multiple_of(b * tm, tm), tm)
        return pltpu.make_async_copy(xs_hbm.at[rows], xin.at[sl], in_sem.at[sl])

    def out_copy(b):
        sl = lax.rem(b, 2)
        rows = pl.ds(pl.multiple_of(b * tm, tm), tm)
        return pltpu.make_async_copy(yout.at[sl], y_hbm.at[rows], out_sem.at[sl])

    @pl.when(e == 0)
    def _():
        in_copy(0).start()

    @pl.when(nb_ref[e] > 0)
    def _():
        for c in range(2 * D_FF // GU_TILE):
            cols = slice(c * GU_TILE, (c + 1) * GU_TILE)
            wgu_b[:, cols] = _dot(wgu_ref[0, :, cols].astype(bf16), pm_ref[...]).astype(bf16)
        for c in range(D_FF // LANES):
            rows = slice(c * LANES, (c + 1) * LANES)
            wdn_b[rows, :] = wdn_ref[0, rows, :].astype(bf16)

        def block(b, carry):
            sl = lax.rem(b, 2)
            in_copy(b).wait()

            @pl.when(b + 1 < n_used)
            def _():
                in_copy(b + 1).start()

            @pl.when(b >= 2)
            def _():
                out_copy(b - 2).wait()

            _expert_block(xin.at[sl], bgu_ref, bdn_ref, yout.at[sl], wgu_b, wdn_b)
            out_copy(b).start()
            return carry
        lax.fori_loop(eb_ref[e], eb_ref[e] + nb_ref[e], block, 0)

    @pl.when(e == pl.num_programs(0) - 1)
    def _():
        @pl.when(n_used >= 2)
        def _():
            out_copy(n_used - 2).wait()
        out_copy(n_used - 1).wait()
        yout[0] = jnp.zeros(yout.shape[1:], u32)

        def run(method):
            def unused(b, carry):
                rows = pl.ds(pl.multiple_of(b * tm, tm), tm)
                getattr(pltpu.make_async_copy(yout.at[0], y_hbm.at[rows], out_sem.at[0]), method)()
                return carry
            lax.fori_loop(n_used, y_hbm.shape[0] // tm, unused, 0)
        run("start")
        run("wait")


def _expert_block(x_ref, bgu_ref, bdn_ref, y_ref, wgu_b, wdn_b):
    gu = _dot(_unpack_pairs(x_ref[...]), wgu_b[...]) + bgu_ref[0]
    acts = []
    for c in range(2 * D_FF // GU_TILE):
        g = jnp.minimum(gu[:, c * GU_TILE:c * GU_TILE + LANES], SWIGLU_LIMIT)
        lin = jnp.clip(gu[:, c * GU_TILE + LANES:(c + 1) * GU_TILE], -SWIGLU_LIMIT, SWIGLU_LIMIT)
        acts.append((g * jax.nn.sigmoid(SWIGLU_ALPHA * g) * (lin + 1.0)).astype(bf16))
    act = jnp.concatenate(acts, axis=-1)
    y = _dot(act, wdn_b[...]) + bdn_ref[0]
    y_ref[...] = _pack_pairs(y.astype(bf16).astype(f32))


def _gate_up_perm():
    j = np.arange(GU_TILE)
    p = np.zeros((GU_TILE, GU_TILE), np.float32)
    p[np.where(j < LANES, 2 * j, 2 * (j - LANES) + 1), j] = 1.0
    return jnp.asarray(p, bf16)


def _expert_mlp(plan, xs, w_gu, b_gu, w_dn, b_dn):
    n_rows = xs.shape[0]
    D = D_MODEL
    tm = EXPERT_TILE
    per_e = lambda e, *_: (e, 0, 0)
    grid_spec = pltpu.PrefetchScalarGridSpec(
        num_scalar_prefetch=3,
        grid=(N_EXPERTS,),
        in_specs=[pl.BlockSpec((1, D, 2 * D_FF), per_e),
                  pl.BlockSpec((1, 1, 2 * D_FF), per_e),
                  pl.BlockSpec((1, D_FF, D), per_e),
                  pl.BlockSpec((1, 1, D), per_e),
                  pl.BlockSpec((GU_TILE, GU_TILE), lambda e, *_: (0, 0)),
                  pl.BlockSpec(memory_space=pl.ANY)],
        out_specs=pl.BlockSpec(memory_space=pl.ANY),
        scratch_shapes=[pltpu.VMEM((D, 2 * D_FF), bf16), pltpu.VMEM((D_FF, D), bf16),
                        pltpu.VMEM((2, tm, D // 2), u32), pltpu.VMEM((2, tm, D // 2), u32),
                        pltpu.SemaphoreType.DMA((2,)), pltpu.SemaphoreType.DMA((2,))],
    )
    return pl.pallas_call(
        _expert_kernel,
        grid_spec=grid_spec,
        out_shape=jax.ShapeDtypeStruct((n_rows, D // 2), u32),
        compiler_params=pltpu.CompilerParams(
            dimension_semantics=("arbitrary",), vmem_limit_bytes=VMEM_LIMIT),
        name="expert_mlp",
    )(plan["eblk"], plan["nblk"], plan["n_used"], w_gu, b_gu, w_dn, b_dn, _gate_up_perm(), xs)


def _combine_kernel(coff_s, rows_s, gbase_s, gtot_s, x_ref, pos_ref, route_ref, y_hbm, o_ref, ybuf, sem,
                    *, n_steps):
    step = pl.program_id(0)
    places = range(STEP_GROUPS)
    group = lambda s, i: s * STEP_GROUPS + i
    slot = lambda s, i: lax.rem(s, 2) * STEP_GROUPS + i
    tok = lambda i: slice(i * GROUP, (i + 1) * GROUP)

    def cell_copy(sl, i):
        n = pl.multiple_of(rows_s[i], SUBLANES)
        loc = pl.multiple_of(coff_s[i], SUBLANES)
        src = pl.multiple_of(gbase_s[i], SUBLANES)
        return n, pltpu.make_async_copy(y_hbm.at[pl.ds(src, n)], ybuf.at[sl, pl.ds(loc, n)],
                                        sem.at[sl])

    def start_cell(sl, i):
        n, cp = cell_copy(sl, i)
        pl.when(n > 0)(cp.start)

    def wait_group(sl, gg):
        n = pl.multiple_of(gtot_s[gg], SUBLANES)
        pltpu.make_async_copy(y_hbm.at[pl.ds(0, n)], ybuf.at[sl, pl.ds(0, n)], sem.at[sl]).wait()

    @pl.when(step == 0)
    def _():
        ybuf[...] = jnp.zeros(ybuf.shape, u32)
        for i in places:
            _for_cells(group(step, i), functools.partial(start_cell, slot(step, i)))

    @pl.when(step + 1 < n_steps)
    def _():
        for i in places:
            _for_cells(group(step + 1, i), functools.partial(start_cell, slot(step + 1, i)))

    for i in places:
        wait_group(slot(step, i), group(step, i))

    riota = lax.broadcasted_iota(jnp.int32, (GROUP, GROUP_ROWS), 1)
    pts = []
    for i in places:
        pt = jnp.zeros((GROUP, GROUP_ROWS), f32)
        for kk in range(TOP_K):
            pk = pos_ref[tok(i), kk:kk + 1].astype(jnp.int32)
            pt = jnp.where(riota == pk, route_ref[tok(i), TOP_K + kk:TOP_K + kk + 1], pt)
        pts.append(pt.astype(bf16))
    ys = [_unpack_pairs(ybuf[slot(step, i)]) for i in places]
    outs = [_dot(pts[i], ys[i]) for i in places]
    for i in places:
        o_ref[tok(i), :] = x_ref[tok(i), :] + outs[i]


def _combine(plan, x2, pos, route, y_rows):
    n_tok, D = x2.shape
    n_steps = n_tok // (GROUP * STEP_GROUPS)
    assert n_steps * GROUP * STEP_GROUPS == n_tok
    tok = lambda w: pl.BlockSpec((STEP_GROUPS * GROUP, w), lambda s, *_: (s, 0))
    grid_spec = pltpu.PrefetchScalarGridSpec(
        num_scalar_prefetch=4,
        grid=(n_steps,),
        in_specs=[tok(D), tok(LANES), tok(LANES), pl.BlockSpec(memory_space=pl.ANY)],
        out_specs=tok(D),
        scratch_shapes=[pltpu.VMEM((2 * STEP_GROUPS, GROUP_ROWS, D // 2), u32),
                        pltpu.SemaphoreType.DMA((2 * STEP_GROUPS,))],
    )
    return pl.pallas_call(
        functools.partial(_combine_kernel, n_steps=n_steps),
        grid_spec=grid_spec,
        out_shape=jax.ShapeDtypeStruct((n_tok, D), f32),
        compiler_params=pltpu.CompilerParams(
            dimension_semantics=("arbitrary",), vmem_limit_bytes=VMEM_LIMIT),
        name="combine",
    )(plan["coff"], plan["rows"], plan["gbase"], plan["gtot"], x2, pos, route, y_rows)


def _routing_plan(cnt, n_tok):
    tm = EXPERT_TILE
    n_groups = n_tok // GROUP
    per_tile = SEQ_TILE // GROUP
    cnt = cnt[:, :, :per_tile].transpose(0, 2, 1).reshape(n_groups, N_EXPERTS).astype(jnp.int32)
    rows = (cnt + SUBLANES - 1) // SUBLANES * SUBLANES
    coff = jnp.cumsum(rows, axis=1) - rows
    tot = jnp.sum(rows, axis=0)
    padded = (tot + tm - 1) // tm * tm
    pend = jnp.cumsum(padded)
    ebase = pend - padded
    gbase = ebase[None, :] + jnp.cumsum(rows, axis=0) - rows
    flat = lambda a: a.reshape(-1).astype(jnp.int32)
    return dict(
        coff=flat(coff), rows=flat(rows), gbase=flat(gbase), gtot=flat(jnp.sum(rows, axis=1)),
        tailoff=flat(ebase + tot), tailn=flat(padded - tot),
        eblk=flat(ebase // tm), nblk=flat(padded // tm), n_used=flat(pend[-1] // tm),
        coffv=jnp.zeros((n_groups, 1, LANES), f32).at[:, 0, :N_EXPERTS].set(coff.astype(f32)))


def _max_rows(n_tok):
    tm = EXPERT_TILE
    worst = (n_tok * TOP_K + (n_tok // GROUP) * N_EXPERTS * (SUBLANES - 1)
             + N_EXPERTS * (tm - SUBLANES))
    return (worst + tm - 1) // tm * tm


def _moe(x2, hf, route, cnt, w_gate_up, b_gate_up, w_down, b_down):
    n_tok, D = x2.shape
    plan = _routing_plan(cnt, n_tok)
    xs, pos = _dispatch(plan, hf, route, _max_rows(n_tok))
    b_gu = (b_gate_up.reshape(N_EXPERTS, 2 * D_FF // GU_TILE, LANES, 2).transpose(0, 1, 3, 2)
            .reshape(N_EXPERTS, 1, 2 * D_FF))
    y_rows = _expert_mlp(plan, xs, w_gate_up, b_gu, w_down, b_down.reshape(N_EXPERTS, 1, D))
    return _combine(plan, x2, pos, route, y_rows)


BIAS_ROW = (2 * Q_TILE + LEFT_CHUNKS * CHUNK + LANES - 2) // LANES * LANES


def _band_bias_kernel(r_ref, o_ref):
    x = jnp.broadcast_to(r_ref[0], (Q_TILE, BIAS_ROW))
    x = pltpu.roll(x, BIAS_ROW - (Q_TILE - 1), axis=1, stride=1, stride_axis=0)
    qc = lax.broadcasted_iota(jnp.int32, (Q_TILE, K_BAND), 0) // CHUNK
    kc = lax.broadcasted_iota(jnp.int32, (Q_TILE, K_BAND), 1) // CHUNK
    first_chunk = LEFT_CHUNKS - pl.program_id(0) * (Q_TILE // CHUNK)
    visible = (kc >= qc) & (kc <= qc + LEFT_CHUNKS) & (kc >= first_chunk)
    o_ref[0, 0] = jnp.where(visible, x[:, :K_BAND] * LOG2E, NEG_INF)


def _band_bias(rel_bias):
    H = rel_bias.shape[0]
    d_max = Q_TILE - 1 + LEFT_CHUNKS * CHUNK
    n_far = d_max - MAX_REL + 1
    r = jnp.concatenate(
        [jnp.broadcast_to(rel_bias[:, 2 * MAX_REL:], (H, n_far)),
         rel_bias[:, MAX_REL - (Q_TILE - 1):2 * MAX_REL][:, ::-1],
         jnp.zeros((H, BIAS_ROW - (2 * Q_TILE - 1 + LEFT_CHUNKS * CHUNK)), rel_bias.dtype)], axis=1)
    return pl.pallas_call(
        _band_bias_kernel,
        grid=(START_TILES + 1, H),
        in_specs=[pl.BlockSpec((1, 1, BIAS_ROW), lambda t, h: (h, 0, 0))],
        out_specs=pl.BlockSpec((1, 1, Q_TILE, K_BAND), lambda t, h: (t, h, 0, 0)),
        out_shape=jax.ShapeDtypeStruct((START_TILES + 1, H, Q_TILE, K_BAND), f32),
        name="band_bias",
    )(r.astype(f32).reshape(H, 1, BIAS_ROW))


def _layer(x, mem, norm_mix_g, w_in, q_norm_g, k_norm_g, rel_bias, conv_w, out_norm_a_g,
           out_norm_b_g, w_out, norm_xattn_g, norm_mem_g, w_xq, w_xkv, xq_norm_g, xk_norm_g,
           w_xo, norm_ffn_g, w_router, b_router, w_gate_up, b_gate_up, w_down, b_down):
    B, S, D = x.shape
    n_tok = B * S
    row = lambda a: a.reshape(1, -1).astype(f32)

    qg = row(jnp.tile(q_norm_g, ATT_HEADS) * (ATT_HEAD_DIM ** -0.5 * LOG2E))
    kg = row(jnp.tile(k_norm_g, ATT_HEADS))
    head_ones = jnp.asarray(np.kron(np.eye(ATT_HEADS), np.ones((ATT_HEAD_DIM, ATT_HEAD_DIM))), bf16)
    q, k, v, yb = _mix_in(x, row(norm_mix_g), w_in.astype(bf16), qg, kg, head_ones,
                          conv_w.astype(f32), row(out_norm_b_g))
    x1 = _attn_out(q, k, v, _band_bias(rel_bias), yb, x, w_out.astype(bf16), row(out_norm_a_g))

    kx, vx = _mem_kv(mem, row(norm_mem_g), w_xkv.astype(bf16), row(xk_norm_g))
    wr = jnp.zeros((D, LANES), f32).at[:, :N_EXPERTS].set(w_router)
    wr_hi = wr.astype(bf16)
    wr_lo = (wr - wr_hi.astype(f32)).astype(bf16)
    br = jnp.full((1, LANES), -jnp.inf, f32).at[0, :N_EXPERTS].set(b_router)
    x2, hf, route, cnt = _xattn_router(
        x1, row(norm_xattn_g), w_xq.astype(bf16), row(xq_norm_g * (X_HEAD_DIM ** -0.5 * LOG2E)),
        kx, vx, w_xo.astype(bf16), row(norm_ffn_g), jnp.concatenate([wr_hi, wr_lo], axis=1), br)

    out = _moe(x2.reshape(n_tok, D), hf.reshape(n_tok, D), route.reshape(n_tok, LANES), cnt,
               w_gate_up, b_gate_up, w_down, b_down)
    return out.reshape(B, S, D)


def kernel(x, mem, norm_mix_g, w_in, q_norm_g, k_norm_g, rel_bias, conv_w, out_norm_a_g,
           out_norm_b_g, w_out, norm_xattn_g, norm_mem_g, w_xq, w_xkv, xq_norm_g, xk_norm_g,
           w_xo, norm_ffn_g, w_router, b_router, w_gate_up, b_gate_up, w_down, b_down):
    depth = norm_mix_g.shape[0]
    for l in range(depth):
        x = _layer(x, mem, norm_mix_g[l], w_in[l], q_norm_g[l], k_norm_g[l], rel_bias[l],
                   conv_w[l], out_norm_a_g[l], out_norm_b_g[l], w_out[l], norm_xattn_g[l],
                   norm_mem_g[l], w_xq[l], w_xkv[l], xq_norm_g[l], xk_norm_g[l], w_xo[l],
                   norm_ffn_g[l], w_router[l], b_router[l], w_gate_up[l], b_gate_up[l],
                   w_down[l], b_down[l])
    return x
```

```python
import functools

import jax
import jax.numpy as jnp
import numpy as np
from jax import lax
from jax.experimental import pallas as pl
from jax.experimental.pallas import tpu as pltpu

D_MODEL = 1024
CHUNK = 64
LEFT_CHUNKS = 8
N_MEM = 256
ATT_HEADS = 8
ATT_HEAD_DIM = 64
D_ATT = ATT_HEADS * ATT_HEAD_DIM
D_CONV = 512
MAX_REL = 256
X_HEADS = 4
X_HEAD_DIM = 128
D_X = X_HEADS * X_HEAD_DIM
N_EXPERTS = 32
TOP_K = 4
D_FF = D_MODEL
SWIGLU_LIMIT = 7.0
SWIGLU_ALPHA = 1.702
EPS = 1e-6
NEG_INF = -1e30

LANES = 128
SEQ_TILE = 1024
Q_TILE = 256
K_BAND = Q_TILE + LEFT_CHUNKS * CHUNK
START_TILES = LEFT_CHUNKS * CHUNK // Q_TILE
LOG2E = 1.4426950408889634
HEAD_ROUND = 4
EXPERT_TILE = 512
GU_TILE = 2 * LANES
GROUP = 256
CELL_UNROLL = 4
STEP_GROUPS = 2
SUBLANES = 8
GROUP_ROWS = 1280
VMEM_LIMIT = 56 * 1024 * 1024

bf16 = jnp.bfloat16
f32 = jnp.float32


def _rms(x, g):
    return x * lax.rsqrt(jnp.mean(x * x, axis=-1, keepdims=True) + EPS) * g


def _dot(a, b):
    return jnp.dot(a, b, preferred_element_type=f32)


def _dot_nt(a, b):
    return lax.dot_general(a, b, (((1,), (1,)), ((), ())), preferred_element_type=f32)


def _mix_in_kernel(x_ref, g_ref, w_ref, qg_ref, kg_ref, hm_ref, cw_ref, gb_ref,
                   q_ref, k_ref, v_ref, yb_ref, ub_ref):
    ts = x_ref.shape[1]

    @pl.when(pl.program_id(1) == 0)
    def _():
        ub_ref[0:8, :] = jnp.zeros((8, D_CONV), f32)

    hb = _rms(x_ref[0], g_ref[...]).astype(bf16)

    def head_norm(t, gain):
        ms = _dot((t * t).astype(bf16), hm_ref[...]) * (1.0 / ATT_HEAD_DIM)
        return t * lax.rsqrt(ms + EPS) * gain

    o = 3 * D_ATT
    bg = _dot(hb, w_ref[:, o:o + D_CONV])
    cg = _dot(hb, w_ref[:, o + D_CONV:o + 2 * D_CONV])
    xv = _dot(hb, w_ref[:, o + 2 * D_CONV:o + 3 * D_CONV])
    u = cg * xv
    ub_ref[8:8 + ts, :] = u
    conv = (cw_ref[0:1, :] * ub_ref[6:6 + ts, :] + cw_ref[1:2, :] * ub_ref[7:7 + ts, :]
            + cw_ref[2:3, :] * u)
    ub_ref[0:8, :] = u[ts - 8:ts, :]
    yb_ref[0] = _rms(bg * conv, gb_ref[...]).astype(bf16)

    q = _dot(hb, w_ref[:, 0:D_ATT])
    q_ref[0] = head_norm(q, qg_ref[...]).astype(bf16)
    k = _dot(hb, w_ref[:, D_ATT:2 * D_ATT])
    k_ref[0] = head_norm(k, kg_ref[...]).astype(bf16)
    v_ref[0] = _dot(hb, w_ref[:, 2 * D_ATT:3 * D_ATT]).astype(bf16)


def _mix_in(x, g, w_in, qg, kg, hm, cw, gb):
    B, S, D = x.shape
    ts = SEQ_TILE
    full = lambda shape: pl.BlockSpec(shape, lambda b, j: (0,) * len(shape))
    tile = lambda w: pl.BlockSpec((1, ts, w), lambda b, j: (b, j, 0))
    out = jax.ShapeDtypeStruct((B, S, D_ATT), bf16)
    return pl.pallas_call(
        _mix_in_kernel,
        grid=(B, S // ts),
        in_specs=[tile(D), full((1, D)), full(w_in.shape), full((1, D_ATT)), full((1, D_ATT)),
                  full((D_ATT, D_ATT)), full((3, D_CONV)), full((1, D_CONV))],
        out_specs=[tile(D_ATT)] * 4,
        out_shape=[out] * 4,
        scratch_shapes=[pltpu.VMEM((ts + 8, D_CONV), f32)],
        compiler_params=pltpu.CompilerParams(
            dimension_semantics=("arbitrary", "arbitrary"), vmem_limit_bytes=VMEM_LIMIT),
        name="mix_in",
    )(x, g, w_in, qg, kg, hm, cw, gb)


def _attn_out_kernel(q_ref, k_ref, v_ref, bias_ref, yb_ref, x_ref, w_ref, ga_ref,
                     o_ref, kb_ref, vb_ref):
    tq = q_ref.shape[1]
    n_kc = K_BAND // CHUNK
    c0 = pl.program_id(1) * (tq // CHUNK)
    for kc in range(n_kc):
        src = pl.multiple_of(jnp.maximum(c0 - LEFT_CHUNKS + kc, 0) * CHUNK, CHUNK)
        kb_ref[kc * CHUNK:(kc + 1) * CHUNK, :] = k_ref[0, pl.ds(src, CHUNK), :]
        vb_ref[kc * CHUNK:(kc + 1) * CHUNK, :] = v_ref[0, pl.ds(src, CHUNK), :]

    lane = lax.broadcasted_iota(jnp.int32, (1, LANES), 1)
    low = lane < ATT_HEAD_DIM

    outs = []
    for h0 in range(0, ATT_HEADS, HEAD_ROUND):
        heads = range(h0, h0 + HEAD_ROUND)
        scores, probs, denoms = {}, {}, {}
        for h in heads:
            sl = slice((h // 2) * LANES, (h // 2 + 1) * LANES)
            qp = q_ref[0, :, sl]
            keep = low if h % 2 == 0 else jnp.logical_not(low)
            scores[h] = _dot_nt(jnp.where(keep, qp, jnp.zeros_like(qp)), kb_ref[:, sl])
        for h in heads:
            s = scores[h] + bias_ref[0, h]
            p = jnp.exp2(s - jnp.max(s, axis=-1, keepdims=True))
            denoms[h] = jnp.sum(p, axis=-1, keepdims=True)
            probs[h] = p.astype(bf16)
        for h in heads:
            sl = slice((h // 2) * LANES, (h // 2 + 1) * LANES)
            outs.append(_dot(probs[h], vb_ref[:, sl]) / denoms[h])
    ya = jnp.concatenate([jnp.where(low, outs[2 * hp], outs[2 * hp + 1])
                          for hp in range(ATT_HEADS // 2)], axis=-1)
    yan = _rms(ya, ga_ref[...]).astype(bf16)
    o_ref[0] = (x_ref[0] + _dot(yan, w_ref[0:D_ATT, :]) + _dot(yb_ref[0], w_ref[D_ATT:, :]))


def _attn_out(q, k, v, bias, yb, x, w_out, ga):
    B, S, D = x.shape
    tq = Q_TILE
    full = lambda shape: pl.BlockSpec(shape, lambda b, j: (0,) * len(shape))
    tile = lambda w: pl.BlockSpec((1, tq, w), lambda b, j: (b, j, 0))
    seq = pl.BlockSpec((1, S, D_ATT), lambda b, j: (b, 0, 0))
    return pl.pallas_call(
        _attn_out_kernel,
        grid=(B, S // tq),
        in_specs=[tile(D_ATT), seq, seq,
                  pl.BlockSpec((1,) + bias.shape[1:], lambda b, j: (jnp.minimum(j, START_TILES), 0, 0, 0)),
                  tile(D_CONV), tile(D),
                  full(w_out.shape), full((1, D_ATT))],
        out_specs=tile(D),
        out_shape=jax.ShapeDtypeStruct((B, S, D), f32),
        scratch_shapes=[pltpu.VMEM((K_BAND, D_ATT), bf16), pltpu.VMEM((K_BAND, D_ATT), bf16)],
        compiler_params=pltpu.CompilerParams(
            dimension_semantics=("parallel", "parallel"), vmem_limit_bytes=VMEM_LIMIT),
        name="attn_out",
    )(q, k, v, bias, yb, x, w_out, ga)


def _mem_kv_kernel(m_ref, g_ref, w_ref, kg_ref, k_ref, v_ref):
    hm = _rms(m_ref[...], g_ref[...]).astype(bf16)
    kv = _dot(hm, w_ref[...])
    for h in range(X_HEADS):
        sl = slice(h * X_HEAD_DIM, (h + 1) * X_HEAD_DIM)
        k_ref[:, sl] = _rms(kv[:, sl], kg_ref[...]).astype(bf16)
    v_ref[...] = kv[:, D_X:].astype(bf16)


def _mem_kv(mem, g, w_xkv, kg):
    B = mem.shape[0]
    rows = B * N_MEM
    tile = min(SEQ_TILE, rows)
    full = lambda shape: pl.BlockSpec(shape, lambda i: (0,) * len(shape))
    out = jax.ShapeDtypeStruct((rows, D_X), bf16)
    kx, vx = pl.pallas_call(
        _mem_kv_kernel,
        grid=(rows // tile,),
        in_specs=[pl.BlockSpec((tile, D_MODEL), lambda i: (i, 0)), full((1, D_MODEL)),
                  full(w_xkv.shape), full((1, X_HEAD_DIM))],
        out_specs=[pl.BlockSpec((tile, D_X), lambda i: (i, 0))] * 2,
        out_shape=[out, out],
        compiler_params=pltpu.CompilerParams(
            dimension_semantics=("parallel",), vmem_limit_bytes=VMEM_LIMIT),
        name="mem_kv",
    )(mem.reshape(rows, D_MODEL), g, w_xkv, kg)
    return kx.reshape(B, N_MEM, D_X), vx.reshape(B, N_MEM, D_X)


def _xattn_router_kernel(x_ref, gx_ref, wq_ref, qg_ref, k_ref, v_ref, wo_ref, gf_ref,
                         wr_ref, br_ref,
                         x2_ref, hf_ref, route_ref, cnt_ref):
    x1 = x_ref[0]
    hb = _rms(x1, gx_ref[...]).astype(bf16)
    q = _dot(hb, wq_ref[...])
    hsl = [slice(h * X_HEAD_DIM, (h + 1) * X_HEAD_DIM) for h in range(X_HEADS)]
    scores = [_dot_nt(_rms(q[:, sl], qg_ref[...]).astype(bf16), k_ref[0, :, sl]) for sl in hsl]
    probs, denoms = [], []
    for s in scores:
        p = jnp.exp2(s - jnp.max(s, axis=-1, keepdims=True))
        denoms.append(jnp.sum(p, axis=-1, keepdims=True))
        probs.append(p.astype(bf16))
    heads = [(_dot(p, v_ref[0, :, sl]) / l).astype(bf16) for p, l, sl in zip(probs, denoms, hsl)]
    o = jnp.concatenate(heads, axis=-1)
    x2 = x1 + _dot(o, wo_ref[...])
    x2_ref[0] = x2

    hf = _rms(x2, gf_ref[...])
    hf_hi = hf.astype(bf16)
    hf_ref[0] = hf_hi
    hf_lo = (hf - hf_hi.astype(f32)).astype(bf16)
    hi_prod = _dot(hf_hi, wr_ref[...])
    logits = (hi_prod[:, :LANES] + hi_prod[:, LANES:] + _dot(hf_lo, wr_ref[:, :LANES])
              + br_ref[...])
    lt = logits.T[:N_EXPERTS]
    eidx = lax.broadcasted_iota(jnp.int32, lt.shape, 0)
    vals, idxs = [], []
    for _ in range(TOP_K):
        m = jnp.max(lt, axis=0, keepdims=True)
        i = jnp.min(jnp.where(lt == m, eidx, N_EXPERTS), axis=0, keepdims=True)
        lt = jnp.where(eidx == i, -jnp.inf, lt)
        vals.append(m)
        idxs.append(i)
    es = [jnp.exp(v - vals[0]) for v in vals]
    tot = es[0] + es[1] + es[2] + es[3]
    packed = jnp.concatenate([i.astype(f32) for i in idxs] + [e / tot for e in es]
                             + [jnp.zeros((LANES - 2 * TOP_K, lt.shape[1]), f32)], axis=0)
    route_ref[0] = packed.T
    sel = jnp.zeros(lt.shape, f32)
    for i in idxs:
        sel = jnp.where(eidx == i, 1.0, sel)
    lane = lax.broadcasted_iota(jnp.int32, (N_EXPERTS, LANES), 1)
    cnt = jnp.zeros((N_EXPERTS, LANES), f32)
    for j in range(lt.shape[1] // GROUP):
        cnt = jnp.where(lane == j, jnp.sum(sel[:, j * GROUP:(j + 1) * GROUP], axis=1, keepdims=True), cnt)
    cnt_ref[0] = cnt


def _xattn_router(x1, gx, w_xq, qg, kx, vx, w_xo, gf, wr, br):
    B, S, D = x1.shape
    ts = SEQ_TILE
    full = lambda shape: pl.BlockSpec(shape, lambda b, j: (0,) * len(shape))
    tile = lambda w: pl.BlockSpec((1, ts, w), lambda b, j: (b, j, 0))
    mem = pl.BlockSpec((1, N_MEM, D_X), lambda b, j: (b, 0, 0))
    return pl.pallas_call(
        _xattn_router_kernel,
        grid=(B, S // ts),
        in_specs=[tile(D), full((1, D)), full(w_xq.shape), full((1, X_HEAD_DIM)), mem, mem,
                  full(w_xo.shape), full((1, D)), full(wr.shape),
                  full((1, LANES))],
        out_specs=[tile(D), tile(D), tile(LANES),
                   pl.BlockSpec((1, N_EXPERTS, LANES), lambda b, j: (b * (S // ts) + j, 0, 0))],
        out_shape=[jax.ShapeDtypeStruct((B, S, D), f32), jax.ShapeDtypeStruct((B, S, D), bf16),
                   jax.ShapeDtypeStruct((B, S, LANES), f32),
                   jax.ShapeDtypeStruct((B * S // ts, N_EXPERTS, LANES), f32)],
        compiler_params=pltpu.CompilerParams(
            dimension_semantics=("parallel", "parallel"), vmem_limit_bytes=VMEM_LIMIT),
        name="xattn_router",
    )(x1, gx, w_xq, qg, kx, vx, w_xo, gf, wr, br)


u32 = jnp.uint32
HIGH_HALF = 0xFFFF0000


def _pack_pairs(x):
    w = x.shape[1] // 2
    lo = pltpu.bitcast(x[:, :w], u32) >> 16
    hi = pltpu.bitcast(x[:, w:], u32) & u32(HIGH_HALF)
    return lo | hi


def _unpack_pairs(u):
    lo = pltpu.bitcast(u << 16, f32)
    hi = pltpu.bitcast(u & u32(HIGH_HALF), f32)
    return jnp.concatenate([lo, hi], axis=-1).astype(bf16)


def _for_cells(g, fn):
    def body(j, carry):
        for u in range(CELL_UNROLL):
            fn(g * N_EXPERTS + j * CELL_UNROLL + u)
        return carry
    lax.fori_loop(0, N_EXPERTS // CELL_UNROLL, body, 0)


def _group_positions(route, coffv, ltri):
    lane = lax.broadcasted_iota(jnp.int32, route.shape, 1).astype(f32)
    hots = [lane == route[:, kk:kk + 1] for kk in range(TOP_K)]
    multi = jnp.zeros(route.shape, f32)
    for h in hots:
        multi = jnp.where(h, 1.0, multi)
    rank = _dot(ltri, multi.astype(bf16))
    posmat = coffv + rank
    return [jnp.sum(jnp.where(h, posmat, 0.0), axis=-1, keepdims=True) for h in hots]


def _dispatch_kernel(coff_s, rows_s, gbase_s, gtot_s, toff_s, tn_s, nu_s,
                     hf_ref, route_ref, coffv_ref, ltri_ref,
                     xs_hbm, pos_ref, xbuf, zbuf, sem, zsem, *, n_steps):
    step = pl.program_id(0)
    places = range(STEP_GROUPS)
    group = lambda i: step * STEP_GROUPS + i
    slot = lambda i: lax.rem(step, 2) * STEP_GROUPS + i
    tok = lambda i: slice(i * GROUP, (i + 1) * GROUP)

    def cell_copy(sl, i):
        n = pl.multiple_of(rows_s[i], SUBLANES)
        src = pl.multiple_of(coff_s[i], SUBLANES)
        dst = pl.multiple_of(gbase_s[i], SUBLANES)
        return n, pltpu.make_async_copy(xbuf.at[sl, pl.ds(src, n)], xs_hbm.at[pl.ds(dst, n)],
                                        sem.at[sl])

    def start_cell(sl, i):
        n, cp = cell_copy(sl, i)
        pl.when(n > 0)(cp.start)

    def wait_group(sl, gg):
        n = pl.multiple_of(gtot_s[gg], SUBLANES)
        pltpu.make_async_copy(xbuf.at[sl, pl.ds(0, n)], xs_hbm.at[pl.ds(0, n)], sem.at[sl]).wait()

    @pl.when(step >= 2)
    def _():
        for i in places:
            wait_group(slot(i), group(i) - 2 * STEP_GROUPS)

    pos = [_group_positions(route_ref[tok(i), :], coffv_ref[i], ltri_ref[...]) for i in places]
    lane = lax.broadcasted_iota(jnp.int32, (GROUP, LANES), 1)
    riota = lax.broadcasted_iota(jnp.int32, (GROUP, GROUP_ROWS), 1)
    pts = []
    for i in places:
        pt = jnp.zeros((GROUP, GROUP_ROWS), f32)
        pos_out = jnp.zeros((GROUP, LANES), f32)
        for kk in range(TOP_K):
            pt = jnp.where(riota == pos[i][kk].astype(jnp.int32), 1.0, pt)
            pos_out = jnp.where(lane == kk, pos[i][kk], pos_out)
        pos_ref[tok(i), :] = pos_out
        pts.append(pt.astype(bf16))
    rows = [lax.dot_general(pts[i], hf_ref[tok(i), :], (((0,), (0,)), ((), ())),
                            preferred_element_type=f32) for i in places]
    for i in places:
        xbuf[slot(i)] = _pack_pairs(rows[i])
    for i in places:
        _for_cells(group(i), functools.partial(start_cell, slot(i)))

    @pl.when(step == n_steps - 1)
    def _():
        if n_steps >= 2:
            for i in places:
                prev_slot = (1 - lax.rem(step, 2)) * STEP_GROUPS + i
                wait_group(prev_slot, group(i) - STEP_GROUPS)
        for i in places:
            wait_group(slot(i), group(i))
        zbuf[...] = jnp.zeros(zbuf.shape, u32)

        def tail_copy(e):
            n = pl.multiple_of(tn_s[e], SUBLANES)
            dst = pl.multiple_of(toff_s[e], SUBLANES)
            return n, pltpu.make_async_copy(zbuf.at[pl.ds(0, n)], xs_hbm.at[pl.ds(dst, n)], zsem)

        def unused_copy(b):
            dst = pl.multiple_of(b * EXPERT_TILE, EXPERT_TILE)
            return pltpu.make_async_copy(zbuf, xs_hbm.at[pl.ds(dst, EXPERT_TILE)], zsem)

        def run(method):
            def tail(e, carry):
                n, cp = tail_copy(e)
                pl.when(n > 0)(getattr(cp, method))
                return carry
            lax.fori_loop(0, N_EXPERTS, tail, 0)

            def unused(b, carry):
                getattr(unused_copy(b), method)()
                return carry
            lax.fori_loop(nu_s[0], xs_hbm.shape[0] // EXPERT_TILE, unused, 0)
        run("start")
        run("wait")


def _dispatch(plan, hf, route, n_rows):
    n_tok, D = hf.shape
    n_steps = n_tok // (GROUP * STEP_GROUPS)
    assert n_steps * GROUP * STEP_GROUPS == n_tok
    ltri = jnp.asarray(np.tril(np.ones((GROUP, GROUP)), -1), bf16)
    tok = lambda w: pl.BlockSpec((STEP_GROUPS * GROUP, w), lambda s, *_: (s, 0))
    grid_spec = pltpu.PrefetchScalarGridSpec(
        num_scalar_prefetch=7,
        grid=(n_steps,),
        in_specs=[tok(D), tok(LANES),
                  pl.BlockSpec((STEP_GROUPS, 1, LANES), lambda s, *_: (s, 0, 0)),
                  pl.BlockSpec((GROUP, GROUP), lambda s, *_: (0, 0))],
        out_specs=[pl.BlockSpec(memory_space=pl.ANY), tok(LANES)],
        scratch_shapes=[pltpu.VMEM((2 * STEP_GROUPS, GROUP_ROWS, D // 2), u32),
                        pltpu.VMEM((EXPERT_TILE, D // 2), u32),
                        pltpu.SemaphoreType.DMA((2 * STEP_GROUPS,)), pltpu.SemaphoreType.DMA(())],
    )
    return pl.pallas_call(
        functools.partial(_dispatch_kernel, n_steps=n_steps),
        grid_spec=grid_spec,
        out_shape=[jax.ShapeDtypeStruct((n_rows, D // 2), u32),
                   jax.ShapeDtypeStruct((n_tok, LANES), f32)],
        compiler_params=pltpu.CompilerParams(
            dimension_semantics=("arbitrary",), vmem_limit_bytes=VMEM_LIMIT),
        name="dispatch",
    )(plan["coff"], plan["rows"], plan["gbase"], plan["gtot"], plan["tailoff"], plan["tailn"],
      plan["n_used"], hf, route, plan["coffv"], ltri)


def _expert_kernel(eb_ref, nb_ref, nu_ref, wgu_ref, bgu_ref, wdn_ref, bdn_ref, pm_ref, xs_hbm,
                   y_hbm, wgu_b, wdn_b, xin, yout, in_sem, out_sem):
    e = pl.program_id(0)
    tm = xin.shape[1]
    n_used = nu_ref[0]

    def in_copy(b):
        sl = lax.rem(b, 2)
        rows = pl.ds(pl.multiple_of(b * tm, tm), tm)
        return pltpu.make_async_copy(xs_hbm.at[rows], xin.at[sl], in_sem.at[sl])

    def out_copy(b):
        sl = lax.rem(b, 2)
        rows = pl.ds(pl.multiple_of(b * tm, tm), tm)
        return pltpu.make_async_copy(yout.at[sl], y_hbm.at[rows], out_sem.at[sl])

    @pl.when(e == 0)
    def _():
        in_copy(0).start()

    @pl.when(nb_ref[e] > 0)
    def _():
        for c in range(2 * D_FF // GU_TILE):
            cols = slice(c * GU_TILE, (c + 1) * GU_TILE)
            wgu_b[:, cols] = _dot(wgu_ref[0, :, cols].astype(bf16), pm_ref[...]).astype(bf16)
        for c in range(D_FF // LANES):
            rows = slice(c * LANES, (c + 1) * LANES)
            wdn_b[rows, :] = wdn_ref[0, rows, :].astype(bf16)

        def block(b, carry):
            sl = lax.rem(b, 2)
            in_copy(b).wait()

            @pl.when(b + 1 < n_used)
            def _():
                in_copy(b + 1).start()

            @pl.when(b >= 2)
            def _():
                out_copy(b - 2).wait()

            _expert_block(xin.at[sl], bgu_ref, bdn_ref, yout.at[sl], wgu_b, wdn_b)
            out_copy(b).start()
            return carry
        lax.fori_loop(eb_ref[e], eb_ref[e] + nb_ref[e], block, 0)

    @pl.when(e == pl.num_programs(0) - 1)
    def _():
        @pl.when(n_used >= 2)
        def _():
            out_copy(n_used - 2).wait()
        out_copy(n_used - 1).wait()
        yout[0] = jnp.zeros(yout.shape[1:], u32)

        def run(method):
            def unused(b, carry):
                rows = pl.ds(pl.multiple_of(b * tm, tm), tm)
                getattr(pltpu.make_async_copy(yout.at[0], y_hbm.at[rows], out_sem.at[0]), method)()
                return carry
            lax.fori_loop(n_used, y_hbm.shape[0] // tm, unused, 0)
        run("start")
        run("wait")


def _expert_block(x_ref, bgu_ref, bdn_ref, y_ref, wgu_b, wdn_b):
    gu = _dot(_unpack_pairs(x_ref[...]), wgu_b[...]) + bgu_ref[0]
    acts = []
    for c in range(2 * D_FF // GU_TILE):
        g = jnp.minimum(gu[:, c * GU_TILE:c * GU_TILE + LANES], SWIGLU_LIMIT)
        lin = jnp.clip(gu[:, c * GU_TILE + LANES:(c + 1) * GU_TILE], -SWIGLU_LIMIT, SWIGLU_LIMIT)
        acts.append((g * jax.nn.sigmoid(SWIGLU_ALPHA * g) * (lin + 1.0)).astype(bf16))
    act = jnp.concatenate(acts, axis=-1)
    y = _dot(act, wdn_b[...]) + bdn_ref[0]
    y_ref[...] = _pack_pairs(y.astype(bf16).astype(f32))


def _gate_up_perm():
    j = np.arange(GU_TILE)
    p = np.zeros((GU_TILE, GU_TILE), np.float32)
    p[np.where(j < LANES, 2 * j, 2 * (j - LANES) + 1), j] = 1.0
    return jnp.asarray(p, bf16)


def _expert_mlp(plan, xs, w_gu, b_gu, w_dn, b_dn):
    n_rows = xs.shape[0]
    D = D_MODEL
    tm = EXPERT_TILE
    per_e = lambda e, *_: (e, 0, 0)
    grid_spec = pltpu.PrefetchScalarGridSpec(
        num_scalar_prefetch=3,
        grid=(N_EXPERTS,),
        in_specs=[pl.BlockSpec((1, D, 2 * D_FF), per_e),
                  pl.BlockSpec((1, 1, 2 * D_FF), per_e),
                  pl.BlockSpec((1, D_FF, D), per_e),
                  pl.BlockSpec((1, 1, D), per_e),
                  pl.BlockSpec((GU_TILE, GU_TILE), lambda e, *_: (0, 0)),
                  pl.BlockSpec(memory_space=pl.ANY)],
        out_specs=pl.BlockSpec(memory_space=pl.ANY),
        scratch_shapes=[pltpu.VMEM((D, 2 * D_FF), bf16), pltpu.VMEM((D_FF, D), bf16),
                        pltpu.VMEM((2, tm, D // 2), u32), pltpu.VMEM((2, tm, D // 2), u32),
                        pltpu.SemaphoreType.DMA((2,)), pltpu.SemaphoreType.DMA((2,))],
    )
    return pl.pallas_call(
        _expert_kernel,
        grid_spec=grid_spec,
        out_shape=jax.ShapeDtypeStruct((n_rows, D // 2), u32),
        compiler_params=pltpu.CompilerParams(
            dimension_semantics=("arbitrary",), vmem_limit_bytes=VMEM_LIMIT),
        name="expert_mlp",
    )(plan["eblk"], plan["nblk"], plan["n_used"], w_gu, b_gu, w_dn, b_dn, _gate_up_perm(), xs)


def _combine_kernel(coff_s, rows_s, gbase_s, gtot_s, x_ref, pos_ref, route_ref, y_hbm, o_ref, ybuf, sem,
                    *, n_steps):
    step = pl.program_id(0)
    places = range(STEP_GROUPS)
    group = lambda s, i: s * STEP_GROUPS + i
    slot = lambda s, i: lax.rem(s, 2) * STEP_GROUPS + i
    tok = lambda i: slice(i * GROUP, (i + 1) * GROUP)

    def cell_copy(sl, i):
        n = pl.multiple_of(rows_s[i], SUBLANES)
        loc = pl.multiple_of(coff_s[i], SUBLANES)
        src = pl.multiple_of(gbase_s[i], SUBLANES)
        return n, pltpu.make_async_copy(y_hbm.at[pl.ds(src, n)], ybuf.at[sl, pl.ds(loc, n)],
                                        sem.at[sl])

    def start_cell(sl, i):
        n, cp = cell_copy(sl, i)
        pl.when(n > 0)(cp.start)

    def wait_group(sl, gg):
        n = pl.multiple_of(gtot_s[gg], SUBLANES)
        pltpu.make_async_copy(y_hbm.at[pl.ds(0, n)], ybuf.at[sl, pl.ds(0, n)], sem.at[sl]).wait()

    @pl.when(step == 0)
    def _():
        ybuf[...] = jnp.zeros(ybuf.shape, u32)
        for i in places:
            _for_cells(group(step, i), functools.partial(start_cell, slot(step, i)))

    @pl.when(step + 1 < n_steps)
    def _():
        for i in places:
            _for_cells(group(step + 1, i), functools.partial(start_cell, slot(step + 1, i)))

    for i in places:
        wait_group(slot(step, i), group(step, i))

    riota = lax.broadcasted_iota(jnp.int32, (GROUP, GROUP_ROWS), 1)
    pts = []
    for i in places:
        pt = jnp.zeros((GROUP, GROUP_ROWS), f32)
        for kk in range(TOP_K):
            pk = pos_ref[tok(i), kk:kk + 1].astype(jnp.int32)
            pt = jnp.where(riota == pk, route_ref[tok(i), TOP_K + kk:TOP_K + kk + 1], pt)
        pts.append(pt.astype(bf16))
    ys = [_unpack_pairs(ybuf[slot(step, i)]) for i in places]
    outs = [_dot(pts[i], ys[i]) for i in places]
    for i in places:
        o_ref[tok(i), :] = x_ref[tok(i), :] + outs[i]


def _combine(plan, x2, pos, route, y_rows):
    n_tok, D = x2.shape
    n_steps = n_tok // (GROUP * STEP_GROUPS)
    assert n_steps * GROUP * STEP_GROUPS == n_tok
    tok = lambda w: pl.BlockSpec((STEP_GROUPS * GROUP, w), lambda s, *_: (s, 0))
    grid_spec = pltpu.PrefetchScalarGridSpec(
        num_scalar_prefetch=4,
        grid=(n_steps,),
        in_specs=[tok(D), tok(LANES), tok(LANES), pl.BlockSpec(memory_space=pl.ANY)],
        out_specs=tok(D),
        scratch_shapes=[pltpu.VMEM((2 * STEP_GROUPS, GROUP_ROWS, D // 2), u32),
                        pltpu.SemaphoreType.DMA((2 * STEP_GROUPS,))],
    )
    return pl.pallas_call(
        functools.partial(_combine_kernel, n_steps=n_steps),
        grid_spec=grid_spec,
        out_shape=jax.ShapeDtypeStruct((n_tok, D), f32),
        compiler_params=pltpu.CompilerParams(
            dimension_semantics=("arbitrary",), vmem_limit_bytes=VMEM_LIMIT),
        name="combine",
    )(plan["coff"], plan["rows"], plan["gbase"], plan["gtot"], x2, pos, route, y_rows)


def _routing_plan(cnt, n_tok):
    tm = EXPERT_TILE
    n_groups = n_tok // GROUP
    per_tile = SEQ_TILE // GROUP
    cnt = cnt[:, :, :per_tile].transpose(0, 2, 1).reshape(n_groups, N_EXPERTS).astype(jnp.int32)
    rows = (cnt + SUBLANES - 1) // SUBLANES * SUBLANES
    coff = jnp.cumsum(rows, axis=1) - rows
    tot = jnp.sum(rows, axis=0)
    padded = (tot + tm - 1) // tm * tm
    pend = jnp.cumsum(padded)
    ebase = pend - padded
    gbase = ebase[None, :] + jnp.cumsum(rows, axis=0) - rows
    flat = lambda a: a.reshape(-1).astype(jnp.int32)
    return dict(
        coff=flat(coff), rows=flat(rows), gbase=flat(gbase), gtot=flat(jnp.sum(rows, axis=1)),
        tailoff=flat(ebase + tot), tailn=flat(padded - tot),
        eblk=flat(ebase // tm), nblk=flat(padded // tm), n_used=flat(pend[-1] // tm),
        coffv=jnp.zeros((n_groups, 1, LANES), f32).at[:, 0, :N_EXPERTS].set(coff.astype(f32)))


def _max_rows(n_tok):
    tm = EXPERT_TILE
    worst = (n_tok * TOP_K + (n_tok // GROUP) * N_EXPERTS * (SUBLANES - 1)
             + N_EXPERTS * (tm - SUBLANES))
    return (worst + tm - 1) // tm * tm


def _moe(x2, hf, route, cnt, w_gate_up, b_gate_up, w_down, b_down):
    n_tok, D = x2.shape
    plan = _routing_plan(cnt, n_tok)
    xs, pos = _dispatch(plan, hf, route, _max_rows(n_tok))
    b_gu = (b_gate_up.reshape(N_EXPERTS, 2 * D_FF // GU_TILE, LANES, 2).transpose(0, 1, 3, 2)
            .reshape(N_EXPERTS, 1, 2 * D_FF))
    y_rows = _expert_mlp(plan, xs, w_gate_up, b_gu, w_down, b_down.reshape(N_EXPERTS, 1, D))
    return _combine(plan, x2, pos, route, y_rows)


BIAS_ROW = (2 * Q_TILE + LEFT_CHUNKS * CHUNK + LANES - 2) // LANES * LANES


def _band_bias_kernel(r_ref, o_ref):
    qc = lax.broadcasted_iota(jnp.int32, (Q_TILE, K_BAND), 0) // CHUNK
    kc = lax.broadcasted_iota(jnp.int32, (Q_TILE, K_BAND), 1) // CHUNK
    first_chunk = LEFT_CHUNKS - pl.program_id(0) * (Q_TILE // CHUNK)
    visible = (kc >= qc) & (kc <= qc + LEFT_CHUNKS) & (kc >= first_chunk)
    for h in range(r_ref.shape[0]):
        x = jnp.broadcast_to(r_ref[h], (Q_TILE, BIAS_ROW))
        x = pltpu.roll(x, BIAS_ROW - (Q_TILE - 1), axis=1, stride=1, stride_axis=0)
        o_ref[0, h] = jnp.where(visible, x[:, :K_BAND] * LOG2E, NEG_INF)


def _band_bias(rel_bias):
    H = rel_bias.shape[0]
    d_max = Q_TILE - 1 + LEFT_CHUNKS * CHUNK
    n_far = d_max - MAX_REL + 1
    r = jnp.concatenate(
        [jnp.broadcast_to(rel_bias[:, 2 * MAX_REL:], (H, n_far)),
         rel_bias[:, MAX_REL - (Q_TILE - 1):2 * MAX_REL][:, ::-1],
         jnp.zeros((H, BIAS_ROW - (2 * Q_TILE - 1 + LEFT_CHUNKS * CHUNK)), rel_bias.dtype)], axis=1)
    return pl.pallas_call(
        _band_bias_kernel,
        grid=(START_TILES + 1,),
        in_specs=[pl.BlockSpec((H, 1, BIAS_ROW), lambda t: (0, 0, 0))],
        out_specs=pl.BlockSpec((1, H, Q_TILE, K_BAND), lambda t: (t, 0, 0, 0)),
        out_shape=jax.ShapeDtypeStruct((START_TILES + 1, H, Q_TILE, K_BAND), f32),
        name="band_bias",
    )(r.astype(f32).reshape(H, 1, BIAS_ROW))


def _layer(x, mem, norm_mix_g, w_in, q_norm_g, k_norm_g, rel_bias, conv_w, out_norm_a_g,
           out_norm_b_g, w_out, norm_xattn_g, norm_mem_g, w_xq, w_xkv, xq_norm_g, xk_norm_g,
           w_xo, norm_ffn_g, w_router, b_router, w_gate_up, b_gate_up, w_down, b_down):
    B, S, D = x.shape
    n_tok = B * S
    row = lambda a: a.reshape(1, -1).astype(f32)

    qg = row(jnp.tile(q_norm_g, ATT_HEADS) * (ATT_HEAD_DIM ** -0.5 * LOG2E))
    kg = row(jnp.tile(k_norm_g, ATT_HEADS))
    head_ones = jnp.asarray(np.kron(np.eye(ATT_HEADS), np.ones((ATT_HEAD_DIM, ATT_HEAD_DIM))), bf16)
    q, k, v, yb = _mix_in(x, row(norm_mix_g), w_in.astype(bf16), qg, kg, head_ones,
                          conv_w.astype(f32), row(out_norm_b_g))
    x1 = _attn_out(q, k, v, _band_bias(rel_bias), yb, x, w_out.astype(bf16), row(out_norm_a_g))

    kx, vx = _mem_kv(mem, row(norm_mem_g), w_xkv.astype(bf16), row(xk_norm_g))
    wr = jnp.zeros((D, LANES), f32).at[:, :N_EXPERTS].set(w_router)
    wr_hi = wr.astype(bf16)
    wr_lo = (wr - wr_hi.astype(f32)).astype(bf16)
    br = jnp.full((1, LANES), -jnp.inf, f32).at[0, :N_EXPERTS].set(b_router)
    x2, hf, route, cnt = _xattn_router(
        x1, row(norm_xattn_g), w_xq.astype(bf16), row(xq_norm_g * (X_HEAD_DIM ** -0.5 * LOG2E)),
        kx, vx, w_xo.astype(bf16), row(norm_ffn_g), jnp.concatenate([wr_hi, wr_lo], axis=1), br)

    out = _moe(x2.reshape(n_tok, D), hf.reshape(n_tok, D), route.reshape(n_tok, LANES), cnt,
               w_gate_up, b_gate_up, w_down, b_down)
    return out.reshape(B, S, D)


def kernel(x, mem, norm_mix_g, w_in, q_norm_g, k_norm_g, rel_bias, conv_w, out_norm_a_g,
           out_norm_b_g, w_out, norm_xattn_g, norm_mem_g, w_xq, w_xkv, xq_norm_g, xk_norm_g,
           w_xo, norm_ffn_g, w_router, b_router, w_gate_up, b_gate_up, w_down, b_down):
    depth = norm_mix_g.shape[0]
    for l in range(depth):
        x = _layer(x, mem, norm_mix_g[l], w_in[l], q_norm_g[l], k_norm_g[l], rel_bias[l],
                   conv_w[l], out_norm_a_g[l], out_norm_b_g[l], w_out[l], norm_xattn_g[l],
                   norm_mem_g[l], w_xq[l], w_xkv[l], xq_norm_g[l], xk_norm_g[l], w_xo[l],
                   norm_ffn_g[l], w_router[l], b_router[l], w_gate_up[l], b_gate_up[l],
                   w_down[l], b_down[l])
    return x
```

```python
import functools

import jax
import jax.numpy as jnp
import numpy as np
from jax import lax
from jax.experimental import pallas as pl
from jax.experimental.pallas import tpu as pltpu

D_MODEL = 1024
CHUNK = 64
LEFT_CHUNKS = 8
N_MEM = 256
ATT_HEADS = 8
ATT_HEAD_DIM = 64
D_ATT = ATT_HEADS * ATT_HEAD_DIM
D_CONV = 512
MAX_REL = 256
X_HEADS = 4
X_HEAD_DIM = 128
D_X = X_HEADS * X_HEAD_DIM
N_EXPERTS = 32
TOP_K = 4
D_FF = D_MODEL
SWIGLU_LIMIT = 7.0
SWIGLU_ALPHA = 1.702
EPS = 1e-6
NEG_INF = -1e30

LANES = 128
SEQ_TILE = 1024
Q_TILE = 256
K_BAND = Q_TILE + LEFT_CHUNKS * CHUNK
START_TILES = LEFT_CHUNKS * CHUNK // Q_TILE
LOG2E = 1.4426950408889634
HEAD_ROUND = 4
EXPERT_TILE = 256
GU_TILE = 2 * LANES
GROUP = 256
CELL_UNROLL = 4
STEP_GROUPS = 2
SUBLANES = 8
GROUP_ROWS = 1280
VMEM_LIMIT = 56 * 1024 * 1024

bf16 = jnp.bfloat16
f32 = jnp.float32


def _rms(x, g):
    return x * lax.rsqrt(jnp.mean(x * x, axis=-1, keepdims=True) + EPS) * g


def _dot(a, b):
    return jnp.dot(a, b, preferred_element_type=f32)


def _dot_nt(a, b):
    return lax.dot_general(a, b, (((1,), (1,)), ((), ())), preferred_element_type=f32)


def _mix_in_kernel(x_ref, g_ref, w_ref, qg_ref, kg_ref, hm_ref, cw_ref, gb_ref,
                   q_ref, k_ref, v_ref, yb_ref, ub_ref):
    ts = x_ref.shape[1]

    @pl.when(pl.program_id(1) == 0)
    def _():
        ub_ref[0:8, :] = jnp.zeros((8, D_CONV), f32)

    hb = _rms(x_ref[0], g_ref[...]).astype(bf16)

    def head_norm(t, gain):
        ms = _dot((t * t).astype(bf16), hm_ref[...]) * (1.0 / ATT_HEAD_DIM)
        return t * lax.rsqrt(ms + EPS) * gain

    o = 3 * D_ATT
    bg = _dot(hb, w_ref[:, o:o + D_CONV])
    cg = _dot(hb, w_ref[:, o + D_CONV:o + 2 * D_CONV])
    xv = _dot(hb, w_ref[:, o + 2 * D_CONV:o + 3 * D_CONV])
    u = cg * xv
    ub_ref[8:8 + ts, :] = u
    conv = (cw_ref[0:1, :] * ub_ref[6:6 + ts, :] + cw_ref[1:2, :] * ub_ref[7:7 + ts, :]
            + cw_ref[2:3, :] * u)
    ub_ref[0:8, :] = u[ts - 8:ts, :]
    yb_ref[0] = _rms(bg * conv, gb_ref[...]).astype(bf16)

    q = _dot(hb, w_ref[:, 0:D_ATT])
    q_ref[0] = head_norm(q, qg_ref[...]).astype(bf16)
    k = _dot(hb, w_ref[:, D_ATT:2 * D_ATT])
    k_ref[0] = head_norm(k, kg_ref[...]).astype(bf16)
    v_ref[0] = _dot(hb, w_ref[:, 2 * D_ATT:3 * D_ATT]).astype(bf16)


def _mix_in(x, g, w_in, qg, kg, hm, cw, gb):
    B, S, D = x.shape
    ts = SEQ_TILE
    full = lambda shape: pl.BlockSpec(shape, lambda b, j: (0,) * len(shape))
    tile = lambda w: pl.BlockSpec((1, ts, w), lambda b, j: (b, j, 0))
    out = jax.ShapeDtypeStruct((B, S, D_ATT), bf16)
    return pl.pallas_call(
        _mix_in_kernel,
        grid=(B, S // ts),
        in_specs=[tile(D), full((1, D)), full(w_in.shape), full((1, D_ATT)), full((1, D_ATT)),
                  full((D_ATT, D_ATT)), full((3, D_CONV)), full((1, D_CONV))],
        out_specs=[tile(D_ATT)] * 4,
        out_shape=[out] * 4,
        scratch_shapes=[pltpu.VMEM((ts + 8, D_CONV), f32)],
        compiler_params=pltpu.CompilerParams(
            dimension_semantics=("arbitrary", "arbitrary"), vmem_limit_bytes=VMEM_LIMIT),
        name="mix_in",
    )(x, g, w_in, qg, kg, hm, cw, gb)


def _attn_out_kernel(q_ref, k_ref, v_ref, bias_ref, yb_ref, x_ref, w_ref, ga_ref,
                     o_ref, kb_ref, vb_ref):
    tq = q_ref.shape[1]
    n_kc = K_BAND // CHUNK
    c0 = pl.program_id(1) * (tq // CHUNK)
    for kc in range(n_kc):
        src = pl.multiple_of(jnp.maximum(c0 - LEFT_CHUNKS + kc, 0) * CHUNK, CHUNK)
        kb_ref[kc * CHUNK:(kc + 1) * CHUNK, :] = k_ref[0, pl.ds(src, CHUNK), :]
        vb_ref[kc * CHUNK:(kc + 1) * CHUNK, :] = v_ref[0, pl.ds(src, CHUNK), :]

    lane = lax.broadcasted_iota(jnp.int32, (1, LANES), 1)
    low = lane < ATT_HEAD_DIM

    outs = []
    for h0 in range(0, ATT_HEADS, HEAD_ROUND):
        heads = range(h0, h0 + HEAD_ROUND)
        scores, probs, denoms = {}, {}, {}
        for h in heads:
            sl = slice((h // 2) * LANES, (h // 2 + 1) * LANES)
            qp = q_ref[0, :, sl]
            keep = low if h % 2 == 0 else jnp.logical_not(low)
            scores[h] = _dot_nt(jnp.where(keep, qp, jnp.zeros_like(qp)), kb_ref[:, sl])
        for h in heads:
            s = scores[h] + bias_ref[0, h]
            p = jnp.exp2(s - jnp.max(s, axis=-1, keepdims=True))
            denoms[h] = jnp.sum(p, axis=-1, keepdims=True)
            probs[h] = p.astype(bf16)
        for h in heads:
            sl = slice((h // 2) * LANES, (h // 2 + 1) * LANES)
            outs.append(_dot(probs[h], vb_ref[:, sl]) / denoms[h])
    ya = jnp.concatenate([jnp.where(low, outs[2 * hp], outs[2 * hp + 1])
                          for hp in range(ATT_HEADS // 2)], axis=-1)
    yan = _rms(ya, ga_ref[...]).astype(bf16)
    o_ref[0] = (x_ref[0] + _dot(yan, w_ref[0:D_ATT, :]) + _dot(yb_ref[0], w_ref[D_ATT:, :]))


def _attn_out(q, k, v, bias, yb, x, w_out, ga):
    B, S, D = x.shape
    tq = Q_TILE
    full = lambda shape: pl.BlockSpec(shape, lambda b, j: (0,) * len(shape))
    tile = lambda w: pl.BlockSpec((1, tq, w), lambda b, j: (b, j, 0))
    seq = pl.BlockSpec((1, S, D_ATT), lambda b, j: (b, 0, 0))
    return pl.pallas_call(
        _attn_out_kernel,
        grid=(B, S // tq),
        in_specs=[tile(D_ATT), seq, seq,
                  pl.BlockSpec((1,) + bias.shape[1:], lambda b, j: (jnp.minimum(j, START_TILES), 0, 0, 0)),
                  tile(D_CONV), tile(D),
                  full(w_out.shape), full((1, D_ATT))],
        out_specs=tile(D),
        out_shape=jax.ShapeDtypeStruct((B, S, D), f32),
        scratch_shapes=[pltpu.VMEM((K_BAND, D_ATT), bf16), pltpu.VMEM((K_BAND, D_ATT), bf16)],
        compiler_params=pltpu.CompilerParams(
            dimension_semantics=("parallel", "parallel"), vmem_limit_bytes=VMEM_LIMIT),
        name="attn_out",
    )(q, k, v, bias, yb, x, w_out, ga)


def _mem_kv_kernel(m_ref, g_ref, w_ref, kg_ref, k_ref, v_ref):
    hm = _rms(m_ref[...], g_ref[...]).astype(bf16)
    kv = _dot(hm, w_ref[...])
    for h in range(X_HEADS):
        sl = slice(h * X_HEAD_DIM, (h + 1) * X_HEAD_DIM)
        k_ref[:, sl] = _rms(kv[:, sl], kg_ref[...]).astype(bf16)
    v_ref[...] = kv[:, D_X:].astype(bf16)


def _mem_kv(mem, g, w_xkv, kg):
    B = mem.shape[0]
    rows = B * N_MEM
    tile = min(SEQ_TILE, rows)
    full = lambda shape: pl.BlockSpec(shape, lambda i: (0,) * len(shape))
    out = jax.ShapeDtypeStruct((rows, D_X), bf16)
    kx, vx = pl.pallas_call(
        _mem_kv_kernel,
        grid=(rows // tile,),
        in_specs=[pl.BlockSpec((tile, D_MODEL), lambda i: (i, 0)), full((1, D_MODEL)),
                  full(w_xkv.shape), full((1, X_HEAD_DIM))],
        out_specs=[pl.BlockSpec((tile, D_X), lambda i: (i, 0))] * 2,
        out_shape=[out, out],
        compiler_params=pltpu.CompilerParams(
            dimension_semantics=("parallel",), vmem_limit_bytes=VMEM_LIMIT),
        name="mem_kv",
    )(mem.reshape(rows, D_MODEL), g, w_xkv, kg)
    return kx.reshape(B, N_MEM, D_X), vx.reshape(B, N_MEM, D_X)


def _xattn_router_kernel(x_ref, gx_ref, wq_ref, qg_ref, k_ref, v_ref, wo_ref, gf_ref,
                         wr_ref, br_ref,
                         x2_ref, hf_ref, route_ref, cnt_ref):
    x1 = x_ref[0]
    hb = _rms(x1, gx_ref[...]).astype(bf16)
    q = _dot(hb, wq_ref[...])
    hsl = [slice(h * X_HEAD_DIM, (h + 1) * X_HEAD_DIM) for h in range(X_HEADS)]
    scores = [_dot_nt(_rms(q[:, sl], qg_ref[...]).astype(bf16), k_ref[0, :, sl]) for sl in hsl]
    probs, denoms = [], []
    for s in scores:
        p = jnp.exp2(s - jnp.max(s, axis=-1, keepdims=True))
        denoms.append(jnp.sum(p, axis=-1, keepdims=True))
        probs.append(p.astype(bf16))
    heads = [(_dot(p, v_ref[0, :, sl]) / l).astype(bf16) for p, l, sl in zip(probs, denoms, hsl)]
    o = jnp.concatenate(heads, axis=-1)
    x2 = x1 + _dot(o, wo_ref[...])
    x2_ref[0] = x2

    hf = _rms(x2, gf_ref[...])
    hf_hi = hf.astype(bf16)
    hf_ref[0] = hf_hi
    hf_lo = (hf - hf_hi.astype(f32)).astype(bf16)
    hi_prod = _dot(hf_hi, wr_ref[...])
    logits = (hi_prod[:, :LANES] + hi_prod[:, LANES:] + _dot(hf_lo, wr_ref[:, :LANES])
              + br_ref[...])
    lt = logits.T[:N_EXPERTS]
    eidx = lax.broadcasted_iota(jnp.int32, lt.shape, 0)
    vals, idxs = [], []
    for _ in range(TOP_K):
        m = jnp.max(lt, axis=0, keepdims=True)
        i = jnp.min(jnp.where(lt == m, eidx, N_EXPERTS), axis=0, keepdims=True)
        lt = jnp.where(eidx == i, -jnp.inf, lt)
        vals.append(m)
        idxs.append(i)
    es = [jnp.exp(v - vals[0]) for v in vals]
    tot = es[0] + es[1] + es[2] + es[3]
    packed = jnp.concatenate([i.astype(f32) for i in idxs] + [e / tot for e in es]
                             + [jnp.zeros((LANES - 2 * TOP_K, lt.shape[1]), f32)], axis=0)
    route_ref[0] = packed.T
    sel = jnp.zeros(lt.shape, f32)
    for i in idxs:
        sel = jnp.where(eidx == i, 1.0, sel)
    lane = lax.broadcasted_iota(jnp.int32, (N_EXPERTS, LANES), 1)
    cnt = jnp.zeros((N_EXPERTS, LANES), f32)
    for j in range(lt.shape[1] // GROUP):
        cnt = jnp.where(lane == j, jnp.sum(sel[:, j * GROUP:(j + 1) * GROUP], axis=1, keepdims=True), cnt)
    cnt_ref[0] = cnt


def _xattn_router(x1, gx, w_xq, qg, kx, vx, w_xo, gf, wr, br):
    B, S, D = x1.shape
    ts = SEQ_TILE
    full = lambda shape: pl.BlockSpec(shape, lambda b, j: (0,) * len(shape))
    tile = lambda w: pl.BlockSpec((1, ts, w), lambda b, j: (b, j, 0))
    mem = pl.BlockSpec((1, N_MEM, D_X), lambda b, j: (b, 0, 0))
    return pl.pallas_call(
        _xattn_router_kernel,
        grid=(B, S // ts),
        in_specs=[tile(D), full((1, D)), full(w_xq.shape), full((1, X_HEAD_DIM)), mem, mem,
                  full(w_xo.shape), full((1, D)), full(wr.shape),
                  full((1, LANES))],
        out_specs=[tile(D), tile(D), tile(LANES),
                   pl.BlockSpec((1, N_EXPERTS, LANES), lambda b, j: (b * (S // ts) + j, 0, 0))],
        out_shape=[jax.ShapeDtypeStruct((B, S, D), f32), jax.ShapeDtypeStruct((B, S, D), bf16),
                   jax.ShapeDtypeStruct((B, S, LANES), f32),
                   jax.ShapeDtypeStruct((B * S // ts, N_EXPERTS, LANES), f32)],
        compiler_params=pltpu.CompilerParams(
            dimension_semantics=("parallel", "parallel"), vmem_limit_bytes=VMEM_LIMIT),
        name="xattn_router",
    )(x1, gx, w_xq, qg, kx, vx, w_xo, gf, wr, br)


u32 = jnp.uint32
HIGH_HALF = 0xFFFF0000


def _pack_pairs(x):
    w = x.shape[1] // 2
    lo = pltpu.bitcast(x[:, :w], u32) >> 16
    hi = pltpu.bitcast(x[:, w:], u32) & u32(HIGH_HALF)
    return lo | hi


def _unpack_pairs(u):
    lo = pltpu.bitcast(u << 16, f32)
    hi = pltpu.bitcast(u & u32(HIGH_HALF), f32)
    return jnp.concatenate([lo, hi], axis=-1).astype(bf16)


def _for_cells(g, fn):
    def body(j, carry):
        for u in range(CELL_UNROLL):
            fn(g * N_EXPERTS + j * CELL_UNROLL + u)
        return carry
    lax.fori_loop(0, N_EXPERTS // CELL_UNROLL, body, 0)


def _group_positions(route, coffv, ltri):
    lane = lax.broadcasted_iota(jnp.int32, route.shape, 1).astype(f32)
    hots = [lane == route[:, kk:kk + 1] for kk in range(TOP_K)]
    multi = jnp.zeros(route.shape, f32)
    for h in hots:
        multi = jnp.where(h, 1.0, multi)
    rank = _dot(ltri, multi.astype(bf16))
    posmat = coffv + rank
    return [jnp.sum(jnp.where(h, posmat, 0.0), axis=-1, keepdims=True) for h in hots]


def _dispatch_kernel(coff_s, rows_s, gbase_s, gtot_s, toff_s, tn_s, nu_s,
                     hf_ref, route_ref, coffv_ref, ltri_ref,
                     xs_hbm, pos_ref, xbuf, zbuf, sem, zsem, *, n_steps):
    step = pl.program_id(0)
    places = range(STEP_GROUPS)
    group = lambda i: step * STEP_GROUPS + i
    slot = lambda i: lax.rem(step, 2) * STEP_GROUPS + i
    tok = lambda i: slice(i * GROUP, (i + 1) * GROUP)

    def cell_copy(sl, i):
        n = pl.multiple_of(rows_s[i], SUBLANES)
        src = pl.multiple_of(coff_s[i], SUBLANES)
        dst = pl.multiple_of(gbase_s[i], SUBLANES)
        return n, pltpu.make_async_copy(xbuf.at[sl, pl.ds(src, n)], xs_hbm.at[pl.ds(dst, n)],
                                        sem.at[sl])

    def start_cell(sl, i):
        n, cp = cell_copy(sl, i)
        pl.when(n > 0)(cp.start)

    def wait_group(sl, gg):
        n = pl.multiple_of(gtot_s[gg], SUBLANES)
        pltpu.make_async_copy(xbuf.at[sl, pl.ds(0, n)], xs_hbm.at[pl.ds(0, n)], sem.at[sl]).wait()

    @pl.when(step >= 2)
    def _():
        for i in places:
            wait_group(slot(i), group(i) - 2 * STEP_GROUPS)

    pos = [_group_positions(route_ref[tok(i), :], coffv_ref[i], ltri_ref[...]) for i in places]
    lane = lax.broadcasted_iota(jnp.int32, (GROUP, LANES), 1)
    riota = lax.broadcasted_iota(jnp.int32, (GROUP, GROUP_ROWS), 1)
    pts = []
    for i in places:
        pt = jnp.zeros((GROUP, GROUP_ROWS), f32)
        pos_out = jnp.zeros((GROUP, LANES), f32)
        for kk in range(TOP_K):
            pt = jnp.where(riota == pos[i][kk].astype(jnp.int32), 1.0, pt)
            pos_out = jnp.where(lane == kk, pos[i][kk], pos_out)
        pos_ref[tok(i), :] = pos_out
        pts.append(pt.astype(bf16))
    rows = [lax.dot_general(pts[i], hf_ref[tok(i), :], (((0,), (0,)), ((), ())),
                            preferred_element_type=f32) for i in places]
    for i in places:
        xbuf[slot(i)] = _pack_pairs(rows[i])
    for i in places:
        _for_cells(group(i), functools.partial(start_cell, slot(i)))

    @pl.when(step == n_steps - 1)
    def _():
        if n_steps >= 2:
            for i in places:
                prev_slot = (1 - lax.rem(step, 2)) * STEP_GROUPS + i
                wait_group(prev_slot, group(i) - STEP_GROUPS)
        for i in places:
            wait_group(slot(i), group(i))
        zbuf[...] = jnp.zeros(zbuf.shape, u32)

        def tail_copy(e):
            n = pl.multiple_of(tn_s[e], SUBLANES)
            dst = pl.multiple_of(toff_s[e], SUBLANES)
            return n, pltpu.make_async_copy(zbuf.at[pl.ds(0, n)], xs_hbm.at[pl.ds(dst, n)], zsem)

        def unused_copy(b):
            dst = pl.multiple_of(b * EXPERT_TILE, EXPERT_TILE)
            return pltpu.make_async_copy(zbuf, xs_hbm.at[pl.ds(dst, EXPERT_TILE)], zsem)

        def run(method):
            def tail(e, carry):
                n, cp = tail_copy(e)
                pl.when(n > 0)(getattr(cp, method))
                return carry
            lax.fori_loop(0, N_EXPERTS, tail, 0)

            def unused(b, carry):
                getattr(unused_copy(b), method)()
                return carry
            lax.fori_loop(nu_s[0], xs_hbm.shape[0] // EXPERT_TILE, unused, 0)
        run("start")
        run("wait")


def _dispatch(plan, hf, route, n_rows):
    n_tok, D = hf.shape
    n_steps = n_tok // (GROUP * STEP_GROUPS)
    assert n_steps * GROUP * STEP_GROUPS == n_tok
    ltri = jnp.asarray(np.tril(np.ones((GROUP, GROUP)), -1), bf16)
    tok = lambda w: pl.BlockSpec((STEP_GROUPS * GROUP, w), lambda s, *_: (s, 0))
    grid_spec = pltpu.PrefetchScalarGridSpec(
        num_scalar_prefetch=7,
        grid=(n_steps,),
        in_specs=[tok(D), tok(LANES),
                  pl.BlockSpec((STEP_GROUPS, 1, LANES), lambda s, *_: (s, 0, 0)),
                  pl.BlockSpec((GROUP, GROUP), lambda s, *_: (0, 0))],
        out_specs=[pl.BlockSpec(memory_space=pl.ANY), tok(LANES)],
        scratch_shapes=[pltpu.VMEM((2 * STEP_GROUPS, GROUP_ROWS, D // 2), u32),
                        pltpu.VMEM((EXPERT_TILE, D // 2), u32),
                        pltpu.SemaphoreType.DMA((2 * STEP_GROUPS,)), pltpu.SemaphoreType.DMA(())],
    )
    return pl.pallas_call(
        functools.partial(_dispatch_kernel, n_steps=n_steps),
        grid_spec=grid_spec,
        out_shape=[jax.ShapeDtypeStruct((n_rows, D // 2), u32),
                   jax.ShapeDtypeStruct((n_tok, LANES), f32)],
        compiler_params=pltpu.CompilerParams(
            dimension_semantics=("arbitrary",), vmem_limit_bytes=VMEM_LIMIT),
        name="dispatch",
    )(plan["coff"], plan["rows"], plan["gbase"], plan["gtot"], plan["tailoff"], plan["tailn"],
      plan["n_used"], hf, route, plan["coffv"], ltri)


def _expert_kernel(eb_ref, nb_ref, nu_ref, wgu_ref, bgu_ref, wdn_ref, bdn_ref, pm_ref, xs_hbm,
                   y_hbm, wgu_b, wdn_b, xin, yout, in_sem, out_sem):
    e = pl.program_id(0)
    tm = xin.shape[1]
    n_used = nu_ref[0]

    def in_copy(b):
        sl = lax.rem(b, 2)
        rows = pl.ds(pl.multiple_of(b * tm, tm), tm)
        return pltpu.make_async_copy(xs_hbm.at[rows], xin.at[sl], in_sem.at[sl])

    def out_copy(b):
        sl = lax.rem(b, 2)
        rows = pl.ds(pl.multiple_of(b * tm, tm), tm)
        return pltpu.make_async_copy(yout.at[sl], y_hbm.at[rows], out_sem.at[sl])

    @pl.when(e == 0)
    def _():
        in_copy(0).start()

    @pl.when(nb_ref[e] > 0)
    def _():
        for c in range(2 * D_FF // GU_TILE):
            cols = slice(c * GU_TILE, (c + 1) * GU_TILE)
            wgu_b[:, cols] = _dot(wgu_ref[0, :, cols].astype(bf16), pm_ref[...]).astype(bf16)
        for c in range(D_FF // LANES):
            rows = slice(c * LANES, (c + 1) * LANES)
            wdn_b[rows, :] = wdn_ref[0, rows, :].astype(bf16)

        def block(b, carry):
            sl = lax.rem(b, 2)
            in_copy(b).wait()

            @pl.when(b + 1 < n_used)
            def _():
                in_copy(b + 1).start()

            @pl.when(b >= 2)
            def _():
                out_copy(b - 2).wait()

            _expert_block(xin.at[sl], bgu_ref, bdn_ref, yout.at[sl], wgu_b, wdn_b)
            out_copy(b).start()
            return carry
        lax.fori_loop(eb_ref[e], eb_ref[e] + nb_ref[e], block, 0)

    @pl.when(e == pl.num_programs(0) - 1)
    def _():
        @pl.when(n_used >= 2)
        def _():
            out_copy(n_used - 2).wait()
        out_copy(n_used - 1).wait()
        yout[0] = jnp.zeros(yout.shape[1:], u32)

        def run(method):
            def unused(b, carry):
                rows = pl.ds(pl.multiple_of(b * tm, tm), tm)
                getattr(pltpu.make_async_copy(yout.at[0], y_hbm.at[rows], out_sem.at[0]), method)()
                return carry
            lax.fori_loop(n_used, y_hbm.shape[0] // tm, unused, 0)
        run("start")
        run("wait")


def _expert_block(x_ref, bgu_ref, bdn_ref, y_ref, wgu_b, wdn_b):
    gu = _dot(_unpack_pairs(x_ref[...]), wgu_b[...]) + bgu_ref[0]
    acts = []
    for c in range(2 * D_FF // GU_TILE):
        g = jnp.minimum(gu[:, c * GU_TILE:c * GU_TILE + LANES], SWIGLU_LIMIT)
        lin = jnp.clip(gu[:, c * GU_TILE + LANES:(c + 1) * GU_TILE], -SWIGLU_LIMIT, SWIGLU_LIMIT)
        acts.append((g * jax.nn.sigmoid(SWIGLU_ALPHA * g) * (lin + 1.0)).astype(bf16))
    act = jnp.concatenate(acts, axis=-1)
    y = _dot(act, wdn_b[...]) + bdn_ref[0]
    y_ref[...] = _pack_pairs(y.astype(bf16).astype(f32))


def _gate_up_perm():
    j = np.arange(GU_TILE)
    p = np.zeros((GU_TILE, GU_TILE), np.float32)
    p[np.where(j < LANES, 2 * j, 2 * (j - LANES) + 1), j] = 1.0
    return jnp.asarray(p, bf16)


def _expert_mlp(plan, xs, w_gu, b_gu, w_dn, b_dn):
    n_rows = xs.shape[0]
    D = D_MODEL
    tm = EXPERT_TILE
    per_e = lambda e, *_: (e, 0, 0)
    grid_spec = pltpu.PrefetchScalarGridSpec(
        num_scalar_prefetch=3,
        grid=(N_EXPERTS,),
        in_specs=[pl.BlockSpec((1, D, 2 * D_FF), per_e),
                  pl.BlockSpec((1, 1, 2 * D_FF), per_e),
                  pl.BlockSpec((1, D_FF, D), per_e),
                  pl.BlockSpec((1, 1, D), per_e),
                  pl.BlockSpec((GU_TILE, GU_TILE), lambda e, *_: (0, 0)),
                  pl.BlockSpec(memory_space=pl.ANY)],
        out_specs=pl.BlockSpec(memory_space=pl.ANY),
        scratch_shapes=[pltpu.VMEM((D, 2 * D_FF), bf16), pltpu.VMEM((D_FF, D), bf16),
                        pltpu.VMEM((2, tm, D // 2), u32), pltpu.VMEM((2, tm, D // 2), u32),
                        pltpu.SemaphoreType.DMA((2,)), pltpu.SemaphoreType.DMA((2,))],
    )
    return pl.pallas_call(
        _expert_kernel,
        grid_spec=grid_spec,
        out_shape=jax.ShapeDtypeStruct((n_rows, D // 2), u32),
        compiler_params=pltpu.CompilerParams(
            dimension_semantics=("arbitrary",), vmem_limit_bytes=VMEM_LIMIT),
        name="expert_mlp",
    )(plan["eblk"], plan["nblk"], plan["n_used"], w_gu, b_gu, w_dn, b_dn, _gate_up_perm(), xs)


def _combine_kernel(coff_s, rows_s, gbase_s, gtot_s, x_ref, pos_ref, route_ref, y_hbm, o_ref, ybuf, sem,
                    *, n_steps):
    step = pl.program_id(0)
    places = range(STEP_GROUPS)
    group = lambda s, i: s * STEP_GROUPS + i
    slot = lambda s, i: lax.rem(s, 2) * STEP_GROUPS + i
    tok = lambda i: slice(i * GROUP, (i + 1) * GROUP)

    def cell_copy(sl, i):
        n = pl.multiple_of(rows_s[i], SUBLANES)
        loc = pl.multiple_of(coff_s[i], SUBLANES)
        src = pl.multiple_of(gbase_s[i], SUBLANES)
        return n, pltpu.make_async_copy(y_hbm.at[pl.ds(src, n)], ybuf.at[sl, pl.ds(loc, n)],
                                        sem.at[sl])

    def start_cell(sl, i):
        n, cp = cell_copy(sl, i)
        pl.when(n > 0)(cp.start)

    def wait_group(sl, gg):
        n = pl.multiple_of(gtot_s[gg], SUBLANES)
        pltpu.make_async_copy(y_hbm.at[pl.ds(0, n)], ybuf.at[sl, pl.ds(0, n)], sem.at[sl]).wait()

    @pl.when(step == 0)
    def _():
        ybuf[...] = jnp.zeros(ybuf.shape, u32)
        for i in places:
            _for_cells(group(step, i), functools.partial(start_cell, slot(step, i)))

    @pl.when(step + 1 < n_steps)
    def _():
        for i in places:
            _for_cells(group(step + 1, i), functools.partial(start_cell, slot(step + 1, i)))

    for i in places:
        wait_group(slot(step, i), group(step, i))

    riota = lax.broadcasted_iota(jnp.int32, (GROUP, GROUP_ROWS), 1)
    pts = []
    for i in places:
        pt = jnp.zeros((GROUP, GROUP_ROWS), f32)
        for kk in range(TOP_K):
            pk = pos_ref[tok(i), kk:kk + 1].astype(jnp.int32)
            pt = jnp.where(riota == pk, route_ref[tok(i), TOP_K + kk:TOP_K + kk + 1], pt)
        pts.append(pt.astype(bf16))
    ys = [_unpack_pairs(ybuf[slot(step, i)]) for i in places]
    outs = [_dot(pts[i], ys[i]) for i in places]
    for i in places:
        o_ref[tok(i), :] = x_ref[tok(i), :] + outs[i]


def _combine(plan, x2, pos, route, y_rows):
    n_tok, D = x2.shape
    n_steps = n_tok // (GROUP * STEP_GROUPS)
    assert n_steps * GROUP * STEP_GROUPS == n_tok
    tok = lambda w: pl.BlockSpec((STEP_GROUPS * GROUP, w), lambda s, *_: (s, 0))
    grid_spec = pltpu.PrefetchScalarGridSpec(
        num_scalar_prefetch=4,
        grid=(n_steps,),
        in_specs=[tok(D), tok(LANES), tok(LANES), pl.BlockSpec(memory_space=pl.ANY)],
        out_specs=tok(D),
        scratch_shapes=[pltpu.VMEM((2 * STEP_GROUPS, GROUP_ROWS, D // 2), u32),
                        pltpu.SemaphoreType.DMA((2 * STEP_GROUPS,))],
    )
    return pl.pallas_call(
        functools.partial(_combine_kernel, n_steps=n_steps),
        grid_spec=grid_spec,
        out_shape=jax.ShapeDtypeStruct((n_tok, D), f32),
        compiler_params=pltpu.CompilerParams(
            dimension_semantics=("arbitrary",), vmem_limit_bytes=VMEM_LIMIT),
        name="combine",
    )(plan["coff"], plan["rows"], plan["gbase"], plan["gtot"], x2, pos, route, y_rows)


def _routing_plan(cnt, n_tok):
    tm = EXPERT_TILE
    n_groups = n_tok // GROUP
    per_tile = SEQ_TILE // GROUP
    cnt = cnt[:, :, :per_tile].transpose(0, 2, 1).reshape(n_groups, N_EXPERTS).astype(jnp.int32)
    rows = (cnt + SUBLANES - 1) // SUBLANES * SUBLANES
    coff = jnp.cumsum(rows, axis=1) - rows
    tot = jnp.sum(rows, axis=0)
    padded = (tot + tm - 1) // tm * tm
    pend = jnp.cumsum(padded)
    ebase = pend - padded
    gbase = ebase[None, :] + jnp.cumsum(rows, axis=0) - rows
    flat = lambda a: a.reshape(-1).astype(jnp.int32)
    return dict(
        coff=flat(coff), rows=flat(rows), gbase=flat(gbase), gtot=flat(jnp.sum(rows, axis=1)),
        tailoff=flat(ebase + tot), tailn=flat(padded - tot),
        eblk=flat(ebase // tm), nblk=flat(padded // tm), n_used=flat(pend[-1] // tm),
        coffv=jnp.zeros((n_groups, 1, LANES), f32).at[:, 0, :N_EXPERTS].set(coff.astype(f32)))


def _max_rows(n_tok):
    tm = EXPERT_TILE
    worst = (n_tok * TOP_K + (n_tok // GROUP) * N_EXPERTS * (SUBLANES - 1)
             + N_EXPERTS * (tm - SUBLANES))
    return (worst + tm - 1) // tm * tm


def _moe(x2, hf, route, cnt, w_gate_up, b_gate_up, w_down, b_down):
    n_tok, D = x2.shape
    plan = _routing_plan(cnt, n_tok)
    xs, pos = _dispatch(plan, hf, route, _max_rows(n_tok))
    b_gu = (b_gate_up.reshape(N_EXPERTS, 2 * D_FF // GU_TILE, LANES, 2).transpose(0, 1, 3, 2)
            .reshape(N_EXPERTS, 1, 2 * D_FF))
    y_rows = _expert_mlp(plan, xs, w_gate_up, b_gu, w_down, b_down.reshape(N_EXPERTS, 1, D))
    return _combine(plan, x2, pos, route, y_rows)


BIAS_ROW = (2 * Q_TILE + LEFT_CHUNKS * CHUNK + LANES - 2) // LANES * LANES


def _band_bias_kernel(r_ref, o_ref):
    qc = lax.broadcasted_iota(jnp.int32, (Q_TILE, K_BAND), 0) // CHUNK
    kc = lax.broadcasted_iota(jnp.int32, (Q_TILE, K_BAND), 1) // CHUNK
    first_chunk = LEFT_CHUNKS - pl.program_id(0) * (Q_TILE // CHUNK)
    visible = (kc >= qc) & (kc <= qc + LEFT_CHUNKS) & (kc >= first_chunk)
    for h in range(r_ref.shape[0]):
        x = jnp.broadcast_to(r_ref[h], (Q_TILE, BIAS_ROW))
        x = pltpu.roll(x, BIAS_ROW - (Q_TILE - 1), axis=1, stride=1, stride_axis=0)
        o_ref[0, h] = jnp.where(visible, x[:, :K_BAND] * LOG2E, NEG_INF)


def _band_bias(rel_bias):
    H = rel_bias.shape[0]
    d_max = Q_TILE - 1 + LEFT_CHUNKS * CHUNK
    n_far = d_max - MAX_REL + 1
    r = jnp.concatenate(
        [jnp.broadcast_to(rel_bias[:, 2 * MAX_REL:], (H, n_far)),
         rel_bias[:, MAX_REL - (Q_TILE - 1):2 * MAX_REL][:, ::-1],
         jnp.zeros((H, BIAS_ROW - (2 * Q_TILE - 1 + LEFT_CHUNKS * CHUNK)), rel_bias.dtype)], axis=1)
    return pl.pallas_call(
        _band_bias_kernel,
        grid=(START_TILES + 1,),
        in_specs=[pl.BlockSpec((H, 1, BIAS_ROW), lambda t: (0, 0, 0))],
        out_specs=pl.BlockSpec((1, H, Q_TILE, K_BAND), lambda t: (t, 0, 0, 0)),
        out_shape=jax.ShapeDtypeStruct((START_TILES + 1, H, Q_TILE, K_BAND), f32),
        name="band_bias",
    )(r.astype(f32).reshape(H, 1, BIAS_ROW))


def _layer(x, mem, norm_mix_g, w_in, q_norm_g, k_norm_g, rel_bias, conv_w, out_norm_a_g,
           out_norm_b_g, w_out, norm_xattn_g, norm_mem_g, w_xq, w_xkv, xq_norm_g, xk_norm_g,
           w_xo, norm_ffn_g, w_router, b_router, w_gate_up, b_gate_up, w_down, b_down):
    B, S, D = x.shape
    n_tok = B * S
    row = lambda a: a.reshape(1, -1).astype(f32)

    qg = row(jnp.tile(q_norm_g, ATT_HEADS) * (ATT_HEAD_DIM ** -0.5 * LOG2E))
    kg = row(jnp.tile(k_norm_g, ATT_HEADS))
    head_ones = jnp.asarray(np.kron(np.eye(ATT_HEADS), np.ones((ATT_HEAD_DIM, ATT_HEAD_DIM))), bf16)
    q, k, v, yb = _mix_in(x, row(norm_mix_g), w_in.astype(bf16), qg, kg, head_ones,
                          conv_w.astype(f32), row(out_norm_b_g))
    x1 = _attn_out(q, k, v, _band_bias(rel_bias), yb, x, w_out.astype(bf16), row(out_norm_a_g))

    kx, vx = _mem_kv(mem, row(norm_mem_g), w_xkv.astype(bf16), row(xk_norm_g))
    wr = jnp.zeros((D, LANES), f32).at[:, :N_EXPERTS].set(w_router)
    wr_hi = wr.astype(bf16)
    wr_lo = (wr - wr_hi.astype(f32)).astype(bf16)
    br = jnp.full((1, LANES), -jnp.inf, f32).at[0, :N_EXPERTS].set(b_router)
    x2, hf, route, cnt = _xattn_router(
        x1, row(norm_xattn_g), w_xq.astype(bf16), row(xq_norm_g * (X_HEAD_DIM ** -0.5 * LOG2E)),
        kx, vx, w_xo.astype(bf16), row(norm_ffn_g), jnp.concatenate([wr_hi, wr_lo], axis=1), br)

    out = _moe(x2.reshape(n_tok, D), hf.reshape(n_tok, D), route.reshape(n_tok, LANES), cnt,
               w_gate_up, b_gate_up, w_down, b_down)
    return out.reshape(B, S, D)


def kernel(x, mem, norm_mix_g, w_in, q_norm_g, k_norm_g, rel_bias, conv_w, out_norm_a_g,
           out_norm_b_g, w_out, norm_xattn_g, norm_mem_g, w_xq, w_xkv, xq_norm_g, xk_norm_g,
           w_xo, norm_ffn_g, w_router, b_router, w_gate_up, b_gate_up, w_down, b_down):
    depth = norm_mix_g.shape[0]
    for l in range(depth):
        x = _layer(x, mem, norm_mix_g[l], w_in[l], q_norm_g[l], k_norm_g[l], rel_bias[l],
                   conv_w[l], out_norm_a_g[l], out_norm_b_g[l], w_out[l], norm_xattn_g[l],
                   norm_mem_g[l], w_xq[l], w_xkv[l], xq_norm_g[l], xk_norm_g[l], w_xo[l],
                   norm_ffn_g[l], w_router[l], b_router[l], w_gate_up[l], b_gate_up[l],
                   w_down[l], b_down[l])
    return x
```

```python
import functools

import jax
import jax.numpy as jnp
import numpy as np
from jax import lax
from jax.experimental import pallas as pl
from jax.experimental.pallas import tpu as pltpu

D_MODEL = 1024
CHUNK = 64
LEFT_CHUNKS = 8
N_MEM = 256
ATT_HEADS = 8
ATT_HEAD_DIM = 64
D_ATT = ATT_HEADS * ATT_HEAD_DIM
D_CONV = 512
MAX_REL = 256
X_HEADS = 4
X_HEAD_DIM = 128
D_X = X_HEADS * X_HEAD_DIM
N_EXPERTS = 32
TOP_K = 4
D_FF = D_MODEL
SWIGLU_LIMIT = 7.0
SWIGLU_ALPHA = 1.702
EPS = 1e-6
NEG_INF = -1e30

LANES = 128
SEQ_TILE = 1024
Q_TILE = 256
K_BAND = Q_TILE + LEFT_CHUNKS * CHUNK
START_TILES = LEFT_CHUNKS * CHUNK // Q_TILE
LOG2E = 1.4426950408889634
HEAD_ROUND = 4
EXPERT_TILE = 512
EXPERT_CHUNK = 4
GU_TILE = 2 * LANES
GROUP = 256
CELL_UNROLL = 4
STEP_GROUPS = 2
SUBLANES = 8
GROUP_ROWS = 1280
VMEM_LIMIT = 56 * 1024 * 1024

bf16 = jnp.bfloat16
f32 = jnp.float32


def _rms(x, g):
    return x * lax.rsqrt(jnp.mean(x * x, axis=-1, keepdims=True) + EPS) * g


def _dot(a, b):
    return jnp.dot(a, b, preferred_element_type=f32)


def _dot_nt(a, b):
    return lax.dot_general(a, b, (((1,), (1,)), ((), ())), preferred_element_type=f32)


def _mix_in_kernel(x_ref, g_ref, w_ref, qg_ref, kg_ref, hm_ref, cw_ref, gb_ref,
                   q_ref, k_ref, v_ref, yb_ref, ub_ref):
    ts = x_ref.shape[1]

    @pl.when(pl.program_id(1) == 0)
    def _():
        ub_ref[0:8, :] = jnp.zeros((8, D_CONV), f32)

    hb = _rms(x_ref[0], g_ref[...]).astype(bf16)

    def head_norm(t, gain):
        ms = _dot((t * t).astype(bf16), hm_ref[...]) * (1.0 / ATT_HEAD_DIM)
        return t * lax.rsqrt(ms + EPS) * gain

    o = 3 * D_ATT
    bg = _dot(hb, w_ref[:, o:o + D_CONV])
    cg = _dot(hb, w_ref[:, o + D_CONV:o + 2 * D_CONV])
    xv = _dot(hb, w_ref[:, o + 2 * D_CONV:o + 3 * D_CONV])
    u = cg * xv
    ub_ref[8:8 + ts, :] = u
    conv = (cw_ref[0:1, :] * ub_ref[6:6 + ts, :] + cw_ref[1:2, :] * ub_ref[7:7 + ts, :]
            + cw_ref[2:3, :] * u)
    ub_ref[0:8, :] = u[ts - 8:ts, :]
    yb_ref[0] = _rms(bg * conv, gb_ref[...]).astype(bf16)

    q = _dot(hb, w_ref[:, 0:D_ATT])
    q_ref[0] = head_norm(q, qg_ref[...]).astype(bf16)
    k = _dot(hb, w_ref[:, D_ATT:2 * D_ATT])
    k_ref[0] = head_norm(k, kg_ref[...]).astype(bf16)
    v_ref[0] = _dot(hb, w_ref[:, 2 * D_ATT:3 * D_ATT]).astype(bf16)


def _mix_in(x, g, w_in, qg, kg, hm, cw, gb):
    B, S, D = x.shape
    ts = SEQ_TILE
    full = lambda shape: pl.BlockSpec(shape, lambda b, j: (0,) * len(shape))
    tile = lambda w: pl.BlockSpec((1, ts, w), lambda b, j: (b, j, 0))
    out = jax.ShapeDtypeStruct((B, S, D_ATT), bf16)
    return pl.pallas_call(
        _mix_in_kernel,
        grid=(B, S // ts),
        in_specs=[tile(D), full((1, D)), full(w_in.shape), full((1, D_ATT)), full((1, D_ATT)),
                  full((D_ATT, D_ATT)), full((3, D_CONV)), full((1, D_CONV))],
        out_specs=[tile(D_ATT)] * 4,
        out_shape=[out] * 4,
        scratch_shapes=[pltpu.VMEM((ts + 8, D_CONV), f32)],
        compiler_params=pltpu.CompilerParams(
            dimension_semantics=("arbitrary", "arbitrary"), vmem_limit_bytes=VMEM_LIMIT),
        name="mix_in",
    )(x, g, w_in, qg, kg, hm, cw, gb)


def _attn_out_kernel(q_ref, k_ref, v_ref, bias_ref, yb_ref, x_ref, w_ref, ga_ref,
                     o_ref, kb_ref, vb_ref):
    tq = q_ref.shape[1]
    n_kc = K_BAND // CHUNK
    c0 = pl.program_id(1) * (tq // CHUNK)
    for kc in range(n_kc):
        src = pl.multiple_of(jnp.maximum(c0 - LEFT_CHUNKS + kc, 0) * CHUNK, CHUNK)
        kb_ref[kc * CHUNK:(kc + 1) * CHUNK, :] = k_ref[0, pl.ds(src, CHUNK), :]
        vb_ref[kc * CHUNK:(kc + 1) * CHUNK, :] = v_ref[0, pl.ds(src, CHUNK), :]

    lane = lax.broadcasted_iota(jnp.int32, (1, LANES), 1)
    low = lane < ATT_HEAD_DIM

    outs = []
    for h0 in range(0, ATT_HEADS, HEAD_ROUND):
        heads = range(h0, h0 + HEAD_ROUND)
        scores, probs, denoms = {}, {}, {}
        for h in heads:
            sl = slice((h // 2) * LANES, (h // 2 + 1) * LANES)
            qp = q_ref[0, :, sl]
            keep = low if h % 2 == 0 else jnp.logical_not(low)
            scores[h] = _dot_nt(jnp.where(keep, qp, jnp.zeros_like(qp)), kb_ref[:, sl])
        for h in heads:
            s = scores[h] + bias_ref[0, h]
            p = jnp.exp2(s - jnp.max(s, axis=-1, keepdims=True))
            denoms[h] = jnp.sum(p, axis=-1, keepdims=True)
            probs[h] = p.astype(bf16)
        for h in heads:
            sl = slice((h // 2) * LANES, (h // 2 + 1) * LANES)
            outs.append(_dot(probs[h], vb_ref[:, sl]) / denoms[h])
    ya = jnp.concatenate([jnp.where(low, outs[2 * hp], outs[2 * hp + 1])
                          for hp in range(ATT_HEADS // 2)], axis=-1)
    yan = _rms(ya, ga_ref[...]).astype(bf16)
    o_ref[0] = (x_ref[0] + _dot(yan, w_ref[0:D_ATT, :]) + _dot(yb_ref[0], w_ref[D_ATT:, :]))


def _attn_out(q, k, v, bias, yb, x, w_out, ga):
    B, S, D = x.shape
    tq = Q_TILE
    full = lambda shape: pl.BlockSpec(shape, lambda b, j: (0,) * len(shape))
    tile = lambda w: pl.BlockSpec((1, tq, w), lambda b, j: (b, j, 0))
    seq = pl.BlockSpec((1, S, D_ATT), lambda b, j: (b, 0, 0))
    return pl.pallas_call(
        _attn_out_kernel,
        grid=(B, S // tq),
        in_specs=[tile(D_ATT), seq, seq,
                  pl.BlockSpec((1,) + bias.shape[1:], lambda b, j: (jnp.minimum(j, START_TILES), 0, 0, 0)),
                  tile(D_CONV), tile(D),
                  full(w_out.shape), full((1, D_ATT))],
        out_specs=tile(D),
        out_shape=jax.ShapeDtypeStruct((B, S, D), f32),
        scratch_shapes=[pltpu.VMEM((K_BAND, D_ATT), bf16), pltpu.VMEM((K_BAND, D_ATT), bf16)],
        compiler_params=pltpu.CompilerParams(
            dimension_semantics=("parallel", "parallel"), vmem_limit_bytes=VMEM_LIMIT),
        name="attn_out",
    )(q, k, v, bias, yb, x, w_out, ga)


def _mem_kv_kernel(m_ref, g_ref, w_ref, kg_ref, k_ref, v_ref):
    hm = _rms(m_ref[...], g_ref[...]).astype(bf16)
    kv = _dot(hm, w_ref[...])
    for h in range(X_HEADS):
        sl = slice(h * X_HEAD_DIM, (h + 1) * X_HEAD_DIM)
        k_ref[:, sl] = _rms(kv[:, sl], kg_ref[...]).astype(bf16)
    v_ref[...] = kv[:, D_X:].astype(bf16)


def _mem_kv(mem, g, w_xkv, kg):
    B = mem.shape[0]
    rows = B * N_MEM
    tile = min(SEQ_TILE, rows)
    full = lambda shape: pl.BlockSpec(shape, lambda i: (0,) * len(shape))
    out = jax.ShapeDtypeStruct((rows, D_X), bf16)
    kx, vx = pl.pallas_call(
        _mem_kv_kernel,
        grid=(rows // tile,),
        in_specs=[pl.BlockSpec((tile, D_MODEL), lambda i: (i, 0)), full((1, D_MODEL)),
                  full(w_xkv.shape), full((1, X_HEAD_DIM))],
        out_specs=[pl.BlockSpec((tile, D_X), lambda i: (i, 0))] * 2,
        out_shape=[out, out],
        compiler_params=pltpu.CompilerParams(
            dimension_semantics=("parallel",), vmem_limit_bytes=VMEM_LIMIT),
        name="mem_kv",
    )(mem.reshape(rows, D_MODEL), g, w_xkv, kg)
    return kx.reshape(B, N_MEM, D_X), vx.reshape(B, N_MEM, D_X)


def _xattn_router_kernel(x_ref, gx_ref, wq_ref, qg_ref, k_ref, v_ref, wo_ref, gf_ref,
                         wr_ref, br_ref,
                         x2_ref, hf_ref, route_ref, cnt_ref):
    x1 = x_ref[0]
    hb = _rms(x1, gx_ref[...]).astype(bf16)
    q = _dot(hb, wq_ref[...])
    hsl = [slice(h * X_HEAD_DIM, (h + 1) * X_HEAD_DIM) for h in range(X_HEADS)]
    scores = [_dot_nt(_rms(q[:, sl], qg_ref[...]).astype(bf16), k_ref[0, :, sl]) for sl in hsl]
    probs, denoms = [], []
    for s in scores:
        p = jnp.exp2(s - jnp.max(s, axis=-1, keepdims=True))
        denoms.append(jnp.sum(p, axis=-1, keepdims=True))
        probs.append(p.astype(bf16))
    heads = [(_dot(p, v_ref[0, :, sl]) / l).astype(bf16) for p, l, sl in zip(probs, denoms, hsl)]
    o = jnp.concatenate(heads, axis=-1)
    x2 = x1 + _dot(o, wo_ref[...])
    x2_ref[0] = x2

    hf = _rms(x2, gf_ref[...])
    hf_hi = hf.astype(bf16)
    hf_ref[0] = hf_hi
    hf_lo = (hf - hf_hi.astype(f32)).astype(bf16)
    hi_prod = _dot(hf_hi, wr_ref[...])
    logits = (hi_prod[:, :LANES] + hi_prod[:, LANES:] + _dot(hf_lo, wr_ref[:, :LANES])
              + br_ref[...])
    lt = logits.T[:N_EXPERTS]
    eidx = lax.broadcasted_iota(jnp.int32, lt.shape, 0)
    vals, idxs = [], []
    for _ in range(TOP_K):
        m = jnp.max(lt, axis=0, keepdims=True)
        i = jnp.min(jnp.where(lt == m, eidx, N_EXPERTS), axis=0, keepdims=True)
        lt = jnp.where(eidx == i, -jnp.inf, lt)
        vals.append(m)
        idxs.append(i)
    es = [jnp.exp(v - vals[0]) for v in vals]
    tot = es[0] + es[1] + es[2] + es[3]
    packed = jnp.concatenate([i.astype(f32) for i in idxs] + [e / tot for e in es]
                             + [jnp.zeros((LANES - 2 * TOP_K, lt.shape[1]), f32)], axis=0)
    route_ref[0] = packed.T
    sel = jnp.zeros(lt.shape, f32)
    for i in idxs:
        sel = jnp.where(eidx == i, 1.0, sel)
    lane = lax.broadcasted_iota(jnp.int32, (N_EXPERTS, LANES), 1)
    cnt = jnp.zeros((N_EXPERTS, LANES), f32)
    for j in range(lt.shape[1] // GROUP):
        cnt = jnp.where(lane == j, jnp.sum(sel[:, j * GROUP:(j + 1) * GROUP], axis=1, keepdims=True), cnt)
    cnt_ref[0] = cnt


def _xattn_router(x1, gx, w_xq, qg, kx, vx, w_xo, gf, wr, br):
    B, S, D = x1.shape
    ts = SEQ_TILE
    full = lambda shape: pl.BlockSpec(shape, lambda b, j: (0,) * len(shape))
    tile = lambda w: pl.BlockSpec((1, ts, w), lambda b, j: (b, j, 0))
    mem = pl.BlockSpec((1, N_MEM, D_X), lambda b, j: (b, 0, 0))
    return pl.pallas_call(
        _xattn_router_kernel,
        grid=(B, S // ts),
        in_specs=[tile(D), full((1, D)), full(w_xq.shape), full((1, X_HEAD_DIM)), mem, mem,
                  full(w_xo.shape), full((1, D)), full(wr.shape),
                  full((1, LANES))],
        out_specs=[tile(D), tile(D), tile(LANES),
                   pl.BlockSpec((1, N_EXPERTS, LANES), lambda b, j: (b * (S // ts) + j, 0, 0))],
        out_shape=[jax.ShapeDtypeStruct((B, S, D), f32), jax.ShapeDtypeStruct((B, S, D), bf16),
                   jax.ShapeDtypeStruct((B, S, LANES), f32),
                   jax.ShapeDtypeStruct((B * S // ts, N_EXPERTS, LANES), f32)],
        compiler_params=pltpu.CompilerParams(
            dimension_semantics=("parallel", "parallel"), vmem_limit_bytes=VMEM_LIMIT),
        name="xattn_router",
    )(x1, gx, w_xq, qg, kx, vx, w_xo, gf, wr, br)


u32 = jnp.uint32
HIGH_HALF = 0xFFFF0000


def _pack_pairs(x):
    w = x.shape[1] // 2
    lo = pltpu.bitcast(x[:, :w], u32) >> 16
    hi = pltpu.bitcast(x[:, w:], u32) & u32(HIGH_HALF)
    return lo | hi


def _unpack_pairs(u):
    lo = pltpu.bitcast(u << 16, f32)
    hi = pltpu.bitcast(u & u32(HIGH_HALF), f32)
    return jnp.concatenate([lo, hi], axis=-1).astype(bf16)


def _for_cells(g, fn):
    def body(j, carry):
        for u in range(CELL_UNROLL):
            fn(g * N_EXPERTS + j * CELL_UNROLL + u)
        return carry
    lax.fori_loop(0, N_EXPERTS // CELL_UNROLL, body, 0)


def _group_positions(route, coffv, ltri):
    lane = lax.broadcasted_iota(jnp.int32, route.shape, 1).astype(f32)
    hots = [lane == route[:, kk:kk + 1] for kk in range(TOP_K)]
    multi = jnp.zeros(route.shape, f32)
    for h in hots:
        multi = jnp.where(h, 1.0, multi)
    rank = _dot(ltri, multi.astype(bf16))
    posmat = coffv + rank
    return [jnp.sum(jnp.where(h, posmat, 0.0), axis=-1, keepdims=True) for h in hots]


def _dispatch_kernel(coff_s, rows_s, gbase_s, gtot_s, toff_s, tn_s, nu_s,
                     hf_ref, route_ref, coffv_ref, ltri_ref,
                     xs_hbm, pos_ref, xbuf, zbuf, sem, zsem, *, n_steps):
    step = pl.program_id(0)
    places = range(STEP_GROUPS)
    group = lambda i: step * STEP_GROUPS + i
    slot = lambda i: lax.rem(step, 2) * STEP_GROUPS + i
    tok = lambda i: slice(i * GROUP, (i + 1) * GROUP)

    def cell_copy(sl, i):
        n = pl.multiple_of(rows_s[i], SUBLANES)
        src = pl.multiple_of(coff_s[i], SUBLANES)
        dst = pl.multiple_of(gbase_s[i], SUBLANES)
        return n, pltpu.make_async_copy(xbuf.at[sl, pl.ds(src, n)], xs_hbm.at[pl.ds(dst, n)],
                                        sem.at[sl])

    def start_cell(sl, i):
        n, cp = cell_copy(sl, i)
        pl.when(n > 0)(cp.start)

    def wait_group(sl, gg):
        n = pl.multiple_of(gtot_s[gg], SUBLANES)
        pltpu.make_async_copy(xbuf.at[sl, pl.ds(0, n)], xs_hbm.at[pl.ds(0, n)], sem.at[sl]).wait()

    @pl.when(step >= 2)
    def _():
        for i in places:
            wait_group(slot(i), group(i) - 2 * STEP_GROUPS)

    pos = [_group_positions(route_ref[tok(i), :], coffv_ref[i], ltri_ref[...]) for i in places]
    lane = lax.broadcasted_iota(jnp.int32, (GROUP, LANES), 1)
    riota = lax.broadcasted_iota(jnp.int32, (GROUP, GROUP_ROWS), 1)
    pts = []
    for i in places:
        pt = jnp.zeros((GROUP, GROUP_ROWS), f32)
        pos_out = jnp.zeros((GROUP, LANES), f32)
        for kk in range(TOP_K):
            pt = jnp.where(riota == pos[i][kk].astype(jnp.int32), 1.0, pt)
            pos_out = jnp.where(lane == kk, pos[i][kk], pos_out)
        pos_ref[tok(i), :] = pos_out
        pts.append(pt.astype(bf16))
    rows = [lax.dot_general(pts[i], hf_ref[tok(i), :], (((0,), (0,)), ((), ())),
                            preferred_element_type=f32) for i in places]
    for i in places:
        xbuf[slot(i)] = _pack_pairs(rows[i])
    for i in places:
        _for_cells(group(i), functools.partial(start_cell, slot(i)))

    @pl.when(step == n_steps - 1)
    def _():
        if n_steps >= 2:
            for i in places:
                prev_slot = (1 - lax.rem(step, 2)) * STEP_GROUPS + i
                wait_group(prev_slot, group(i) - STEP_GROUPS)
        for i in places:
            wait_group(slot(i), group(i))
        zbuf[...] = jnp.zeros(zbuf.shape, u32)

        def tail_copy(e):
            n = pl.multiple_of(tn_s[e], SUBLANES)
            dst = pl.multiple_of(toff_s[e], SUBLANES)
            return n, pltpu.make_async_copy(zbuf.at[pl.ds(0, n)], xs_hbm.at[pl.ds(dst, n)], zsem)

        def unused_copy(b):
            dst = pl.multiple_of(b * EXPERT_TILE, EXPERT_TILE)
            return pltpu.make_async_copy(zbuf, xs_hbm.at[pl.ds(dst, EXPERT_TILE)], zsem)

        def run(method):
            def tail(e, carry):
                n, cp = tail_copy(e)
                pl.when(n > 0)(getattr(cp, method))
                return carry
            lax.fori_loop(0, N_EXPERTS, tail, 0)

            def unused(b, carry):
                getattr(unused_copy(b), method)()
                return carry
            lax.fori_loop(nu_s[0], xs_hbm.shape[0] // EXPERT_TILE, unused, 0)
        run("start")
        run("wait")


def _dispatch(plan, hf, route, n_rows):
    n_tok, D = hf.shape
    n_steps = n_tok // (GROUP * STEP_GROUPS)
    assert n_steps * GROUP * STEP_GROUPS == n_tok
    ltri = jnp.asarray(np.tril(np.ones((GROUP, GROUP)), -1), bf16)
    tok = lambda w: pl.BlockSpec((STEP_GROUPS * GROUP, w), lambda s, *_: (s, 0))
    grid_spec = pltpu.PrefetchScalarGridSpec(
        num_scalar_prefetch=7,
        grid=(n_steps,),
        in_specs=[tok(D), tok(LANES),
                  pl.BlockSpec((STEP_GROUPS, 1, LANES), lambda s, *_: (s, 0, 0)),
                  pl.BlockSpec((GROUP, GROUP), lambda s, *_: (0, 0))],
        out_specs=[pl.BlockSpec(memory_space=pl.ANY), tok(LANES)],
        scratch_shapes=[pltpu.VMEM((2 * STEP_GROUPS, GROUP_ROWS, D // 2), u32),
                        pltpu.VMEM((EXPERT_TILE, D // 2), u32),
                        pltpu.SemaphoreType.DMA((2 * STEP_GROUPS,)), pltpu.SemaphoreType.DMA(())],
    )
    return pl.pallas_call(
        functools.partial(_dispatch_kernel, n_steps=n_steps),
        grid_spec=grid_spec,
        out_shape=[jax.ShapeDtypeStruct((n_rows, D // 2), u32),
                   jax.ShapeDtypeStruct((n_tok, LANES), f32)],
        compiler_params=pltpu.CompilerParams(
            dimension_semantics=("arbitrary",), vmem_limit_bytes=VMEM_LIMIT),
        name="dispatch",
    )(plan["coff"], plan["rows"], plan["gbase"], plan["gtot"], plan["tailoff"], plan["tailn"],
      plan["n_used"], hf, route, plan["coffv"], ltri)


def _expert_kernel(eb_ref, nb_ref, nu_ref, wgu_ref, bgu_ref, wdn_ref, bdn_ref, pm_ref, xs_hbm,
                   y_hbm, wgu_b, wdn_b, xin, yout, in_sem, out_sem):
    e = pl.program_id(0)
    tm = EXPERT_TILE
    chunk_rows = EXPERT_CHUNK * tm
    n_used = nu_ref[0]
    n_chunks = (n_used + EXPERT_CHUNK - 1) // EXPERT_CHUNK

    def in_copy(c):
        sl = lax.rem(c, 2)
        rows = pl.ds(pl.multiple_of(c * chunk_rows, chunk_rows), chunk_rows)
        return pltpu.make_async_copy(xs_hbm.at[rows], xin.at[sl], in_sem.at[sl])

    def out_copy(c):
        sl = lax.rem(c, 2)
        rows = pl.ds(pl.multiple_of(c * chunk_rows, chunk_rows), chunk_rows)
        return pltpu.make_async_copy(yout.at[sl], y_hbm.at[rows], out_sem.at[sl])

    @pl.when(e == 0)
    def _():
        in_copy(0).start()
        yout[...] = jnp.zeros(yout.shape, u32)

    @pl.when(nb_ref[e] > 0)
    def _():
        for c in range(2 * D_FF // GU_TILE):
            cols = slice(c * GU_TILE, (c + 1) * GU_TILE)
            wgu_b[:, cols] = _dot(wgu_ref[0, :, cols].astype(bf16), pm_ref[...]).astype(bf16)
        for c in range(D_FF // LANES):
            rows = slice(c * LANES, (c + 1) * LANES)
            wdn_b[rows, :] = wdn_ref[0, rows, :].astype(bf16)

        def block(b, carry):
            c = b // EXPERT_CHUNK
            j = lax.rem(b, EXPERT_CHUNK)
            sl = lax.rem(c, 2)

            @pl.when(j == 0)
            def _():
                in_copy(c).wait()

                @pl.when(c + 1 < n_chunks)
                def _():
                    in_copy(c + 1).start()

                @pl.when(c >= 2)
                def _():
                    out_copy(c - 2).wait()

            rows = pl.ds(pl.multiple_of(j * tm, tm), tm)
            _expert_block(xin.at[sl, rows], bgu_ref, bdn_ref, yout.at[sl, rows], wgu_b, wdn_b)

            @pl.when((j == EXPERT_CHUNK - 1) | (b == n_used - 1))
            def _():
                out_copy(c).start()
            return carry
        lax.fori_loop(eb_ref[e], eb_ref[e] + nb_ref[e], block, 0)

    @pl.when(e == pl.num_programs(0) - 1)
    def _():
        @pl.when(n_chunks >= 2)
        def _():
            out_copy(n_chunks - 2).wait()
        out_copy(n_chunks - 1).wait()
        yout[0] = jnp.zeros(yout.shape[1:], u32)

        def run(method):
            def unused(c, carry):
                rows = pl.ds(pl.multiple_of(c * chunk_rows, chunk_rows), chunk_rows)
                getattr(pltpu.make_async_copy(yout.at[0], y_hbm.at[rows], out_sem.at[0]), method)()
                return carry
            lax.fori_loop(n_chunks, y_hbm.shape[0] // chunk_rows, unused, 0)
        run("start")
        run("wait")


def _expert_block(x_ref, bgu_ref, bdn_ref, y_ref, wgu_b, wdn_b):
    gu = _dot(_unpack_pairs(x_ref[...]), wgu_b[...]) + bgu_ref[0]
    acts = []
    for c in range(2 * D_FF // GU_TILE):
        g = jnp.minimum(gu[:, c * GU_TILE:c * GU_TILE + LANES], SWIGLU_LIMIT)
        lin = jnp.clip(gu[:, c * GU_TILE + LANES:(c + 1) * GU_TILE], -SWIGLU_LIMIT, SWIGLU_LIMIT)
        acts.append((g * jax.nn.sigmoid(SWIGLU_ALPHA * g) * (lin + 1.0)).astype(bf16))
    act = jnp.concatenate(acts, axis=-1)
    y = _dot(act, wdn_b[...]) + bdn_ref[0]
    y_ref[...] = _pack_pairs(y.astype(bf16).astype(f32))


def _gate_up_perm():
    j = np.arange(GU_TILE)
    p = np.zeros((GU_TILE, GU_TILE), np.float32)
    p[np.where(j < LANES, 2 * j, 2 * (j - LANES) + 1), j] = 1.0
    return jnp.asarray(p, bf16)


def _expert_mlp(plan, xs, w_gu, b_gu, w_dn, b_dn):
    n_rows = xs.shape[0]
    D = D_MODEL
    tm = EXPERT_TILE
    per_e = lambda e, *_: (e, 0, 0)
    grid_spec = pltpu.PrefetchScalarGridSpec(
        num_scalar_prefetch=3,
        grid=(N_EXPERTS,),
        in_specs=[pl.BlockSpec((1, D, 2 * D_FF), per_e),
                  pl.BlockSpec((1, 1, 2 * D_FF), per_e),
                  pl.BlockSpec((1, D_FF, D), per_e),
                  pl.BlockSpec((1, 1, D), per_e),
                  pl.BlockSpec((GU_TILE, GU_TILE), lambda e, *_: (0, 0)),
                  pl.BlockSpec(memory_space=pl.ANY)],
        out_specs=pl.BlockSpec(memory_space=pl.ANY),
        scratch_shapes=[pltpu.VMEM((D, 2 * D_FF), bf16), pltpu.VMEM((D_FF, D), bf16),
                        pltpu.VMEM((2, EXPERT_CHUNK * tm, D // 2), u32),
                        pltpu.VMEM((2, EXPERT_CHUNK * tm, D // 2), u32),
                        pltpu.SemaphoreType.DMA((2,)), pltpu.SemaphoreType.DMA((2,))],
    )
    return pl.pallas_call(
        _expert_kernel,
        grid_spec=grid_spec,
        out_shape=jax.ShapeDtypeStruct((n_rows, D // 2), u32),
        compiler_params=pltpu.CompilerParams(
            dimension_semantics=("arbitrary",), vmem_limit_bytes=VMEM_LIMIT),
        name="expert_mlp",
    )(plan["eblk"], plan["nblk"], plan["n_used"], w_gu, b_gu, w_dn, b_dn, _gate_up_perm(), xs)


def _combine_kernel(coff_s, rows_s, gbase_s, gtot_s, x_ref, pos_ref, route_ref, y_hbm, o_ref, ybuf, sem,
                    *, n_steps):
    step = pl.program_id(0)
    places = range(STEP_GROUPS)
    group = lambda s, i: s * STEP_GROUPS + i
    slot = lambda s, i: lax.rem(s, 2) * STEP_GROUPS + i
    tok = lambda i: slice(i * GROUP, (i + 1) * GROUP)

    def cell_copy(sl, i):
        n = pl.multiple_of(rows_s[i], SUBLANES)
        loc = pl.multiple_of(coff_s[i], SUBLANES)
        src = pl.multiple_of(gbase_s[i], SUBLANES)
        return n, pltpu.make_async_copy(y_hbm.at[pl.ds(src, n)], ybuf.at[sl, pl.ds(loc, n)],
                                        sem.at[sl])

    def start_cell(sl, i):
        n, cp = cell_copy(sl, i)
        pl.when(n > 0)(cp.start)

    def wait_group(sl, gg):
        n = pl.multiple_of(gtot_s[gg], SUBLANES)
        pltpu.make_async_copy(y_hbm.at[pl.ds(0, n)], ybuf.at[sl, pl.ds(0, n)], sem.at[sl]).wait()

    @pl.when(step == 0)
    def _():
        ybuf[...] = jnp.zeros(ybuf.shape, u32)
        for i in places:
            _for_cells(group(step, i), functools.partial(start_cell, slot(step, i)))

    @pl.when(step + 1 < n_steps)
    def _():
        for i in places:
            _for_cells(group(step + 1, i), functools.partial(start_cell, slot(step + 1, i)))

    for i in places:
        wait_group(slot(step, i), group(step, i))

    riota = lax.broadcasted_iota(jnp.int32, (GROUP, GROUP_ROWS), 1)
    pts = []
    for i in places:
        pt = jnp.zeros((GROUP, GROUP_ROWS), f32)
        for kk in range(TOP_K):
            pk = pos_ref[tok(i), kk:kk + 1].astype(jnp.int32)
            pt = jnp.where(riota == pk, route_ref[tok(i), TOP_K + kk:TOP_K + kk + 1], pt)
        pts.append(pt.astype(bf16))
    ys = [_unpack_pairs(ybuf[slot(step, i)]) for i in places]
    outs = [_dot(pts[i], ys[i]) for i in places]
    for i in places:
        o_ref[tok(i), :] = x_ref[tok(i), :] + outs[i]


def _combine(plan, x2, pos, route, y_rows):
    n_tok, D = x2.shape
    n_steps = n_tok // (GROUP * STEP_GROUPS)
    assert n_steps * GROUP * STEP_GROUPS == n_tok
    tok = lambda w: pl.BlockSpec((STEP_GROUPS * GROUP, w), lambda s, *_: (s, 0))
    grid_spec = pltpu.PrefetchScalarGridSpec(
        num_scalar_prefetch=4,
        grid=(n_steps,),
        in_specs=[tok(D), tok(LANES), tok(LANES), pl.BlockSpec(memory_space=pl.ANY)],
        out_specs=tok(D),
        scratch_shapes=[pltpu.VMEM((2 * STEP_GROUPS, GROUP_ROWS, D // 2), u32),
                        pltpu.SemaphoreType.DMA((2 * STEP_GROUPS,))],
    )
    return pl.pallas_call(
        functools.partial(_combine_kernel, n_steps=n_steps),
        grid_spec=grid_spec,
        out_shape=jax.ShapeDtypeStruct((n_tok, D), f32),
        compiler_params=pltpu.CompilerParams(
            dimension_semantics=("arbitrary",), vmem_limit_bytes=VMEM_LIMIT),
        name="combine",
    )(plan["coff"], plan["rows"], plan["gbase"], plan["gtot"], x2, pos, route, y_rows)


def _routing_plan(cnt, n_tok):
    tm = EXPERT_TILE
    n_groups = n_tok // GROUP
    per_tile = SEQ_TILE // GROUP
    cnt = cnt[:, :, :per_tile].transpose(0, 2, 1).reshape(n_groups, N_EXPERTS).astype(jnp.int32)
    rows = (cnt + SUBLANES - 1) // SUBLANES * SUBLANES
    coff = jnp.cumsum(rows, axis=1) - rows
    tot = jnp.sum(rows, axis=0)
    padded = (tot + tm - 1) // tm * tm
    pend = jnp.cumsum(padded)
    ebase = pend - padded
    gbase = ebase[None, :] + jnp.cumsum(rows, axis=0) - rows
    flat = lambda a: a.reshape(-1).astype(jnp.int32)
    return dict(
        coff=flat(coff), rows=flat(rows), gbase=flat(gbase), gtot=flat(jnp.sum(rows, axis=1)),
        tailoff=flat(ebase + tot), tailn=flat(padded - tot),
        eblk=flat(ebase // tm), nblk=flat(padded // tm), n_used=flat(pend[-1] // tm),
        coffv=jnp.zeros((n_groups, 1, LANES), f32).at[:, 0, :N_EXPERTS].set(coff.astype(f32)))


def _max_rows(n_tok):
    tm = EXPERT_TILE
    worst = (n_tok * TOP_K + (n_tok // GROUP) * N_EXPERTS * (SUBLANES - 1)
             + N_EXPERTS * (tm - SUBLANES))
    chunk = EXPERT_CHUNK * tm
    return (worst + chunk - 1) // chunk * chunk


def _moe(x2, hf, route, cnt, w_gate_up, b_gate_up, w_down, b_down):
    n_tok, D = x2.shape
    plan = _routing_plan(cnt, n_tok)
    xs, pos = _dispatch(plan, hf, route, _max_rows(n_tok))
    b_gu = (b_gate_up.reshape(N_EXPERTS, 2 * D_FF // GU_TILE, LANES, 2).transpose(0, 1, 3, 2)
            .reshape(N_EXPERTS, 1, 2 * D_FF))
    y_rows = _expert_mlp(plan, xs, w_gate_up, b_gu, w_down, b_down.reshape(N_EXPERTS, 1, D))
    return _combine(plan, x2, pos, route, y_rows)


BIAS_ROW = (2 * Q_TILE + LEFT_CHUNKS * CHUNK + LANES - 2) // LANES * LANES


def _band_bias_kernel(r_ref, o_ref):
    qc = lax.broadcasted_iota(jnp.int32, (Q_TILE, K_BAND), 0) // CHUNK
    kc = lax.broadcasted_iota(jnp.int32, (Q_TILE, K_BAND), 1) // CHUNK
    first_chunk = LEFT_CHUNKS - pl.program_id(0) * (Q_TILE // CHUNK)
    visible = (kc >= qc) & (kc <= qc + LEFT_CHUNKS) & (kc >= first_chunk)
    for h in range(r_ref.shape[0]):
        x = jnp.broadcast_to(r_ref[h], (Q_TILE, BIAS_ROW))
        x = pltpu.roll(x, BIAS_ROW - (Q_TILE - 1), axis=1, stride=1, stride_axis=0)
        o_ref[0, h] = jnp.where(visible, x[:, :K_BAND] * LOG2E, NEG_INF)


def _band_bias(rel_bias):
    H = rel_bias.shape[0]
    d_max = Q_TILE - 1 + LEFT_CHUNKS * CHUNK
    n_far = d_max - MAX_REL + 1
    r = jnp.concatenate(
        [jnp.broadcast_to(rel_bias[:, 2 * MAX_REL:], (H, n_far)),
         rel_bias[:, MAX_REL - (Q_TILE - 1):2 * MAX_REL][:, ::-1],
         jnp.zeros((H, BIAS_ROW - (2 * Q_TILE - 1 + LEFT_CHUNKS * CHUNK)), rel_bias.dtype)], axis=1)
    return pl.pallas_call(
        _band_bias_kernel,
        grid=(START_TILES + 1,),
        in_specs=[pl.BlockSpec((H, 1, BIAS_ROW), lambda t: (0, 0, 0))],
        out_specs=pl.BlockSpec((1, H, Q_TILE, K_BAND), lambda t: (t, 0, 0, 0)),
        out_shape=jax.ShapeDtypeStruct((START_TILES + 1, H, Q_TILE, K_BAND), f32),
        name="band_bias",
    )(r.astype(f32).reshape(H, 1, BIAS_ROW))


def _layer(x, mem, norm_mix_g, w_in, q_norm_g, k_norm_g, rel_bias, conv_w, out_norm_a_g,
           out_norm_b_g, w_out, norm_xattn_g, norm_mem_g, w_xq, w_xkv, xq_norm_g, xk_norm_g,
           w_xo, norm_ffn_g, w_router, b_router, w_gate_up, b_gate_up, w_down, b_down):
    B, S, D = x.shape
    n_tok = B * S
    row = lambda a: a.reshape(1, -1).astype(f32)

    qg = row(jnp.tile(q_norm_g, ATT_HEADS) * (ATT_HEAD_DIM ** -0.5 * LOG2E))
    kg = row(jnp.tile(k_norm_g, ATT_HEADS))
    head_ones = jnp.asarray(np.kron(np.eye(ATT_HEADS), np.ones((ATT_HEAD_DIM, ATT_HEAD_DIM))), bf16)
    q, k, v, yb = _mix_in(x, row(norm_mix_g), w_in.astype(bf16), qg, kg, head_ones,
                          conv_w.astype(f32), row(out_norm_b_g))
    x1 = _attn_out(q, k, v, _band_bias(rel_bias), yb, x, w_out.astype(bf16), row(out_norm_a_g))

    kx, vx = _mem_kv(mem, row(norm_mem_g), w_xkv.astype(bf16), row(xk_norm_g))
    wr = jnp.zeros((D, LANES), f32).at[:, :N_EXPERTS].set(w_router)
    wr_hi = wr.astype(bf16)
    wr_lo = (wr - wr_hi.astype(f32)).astype(bf16)
    br = jnp.full((1, LANES), -jnp.inf, f32).at[0, :N_EXPERTS].set(b_router)
    x2, hf, route, cnt = _xattn_router(
        x1, row(norm_xattn_g), w_xq.astype(bf16), row(xq_norm_g * (X_HEAD_DIM ** -0.5 * LOG2E)),
        kx, vx, w_xo.astype(bf16), row(norm_ffn_g), jnp.concatenate([wr_hi, wr_lo], axis=1), br)

    out = _moe(x2.reshape(n_tok, D), hf.reshape(n_tok, D), route.reshape(n_tok, LANES), cnt,
               w_gate_up, b_gate_up, w_down, b_down)
    return out.reshape(B, S, D)


def kernel(x, mem, norm_mix_g, w_in, q_norm_g, k_norm_g, rel_bias, conv_w, out_norm_a_g,
           out_norm_b_g, w_out, norm_xattn_g, norm_mem_g, w_xq, w_xkv, xq_norm_g, xk_norm_g,
           w_xo, norm_ffn_g, w_router, b_router, w_gate_up, b_gate_up, w_down, b_down):
    depth = norm_mix_g.shape[0]
    for l in range(depth):
        x = _layer(x, mem, norm_mix_g[l], w_in[l], q_norm_g[l], k_norm_g[l], rel_bias[l],
                   conv_w[l], out_norm_a_g[l], out_norm_b_g[l], w_out[l], norm_xattn_g[l],
                   norm_mem_g[l], w_xq[l], w_xkv[l], xq_norm_g[l], xk_norm_g[l], w_xo[l],
                   norm_ffn_g[l], w_router[l], b_router[l], w_gate_up[l], b_gate_up[l],
                   w_down[l], b_down[l])
    return x
```

```python
import functools

import jax
import jax.numpy as jnp
import numpy as np
from jax import lax
from jax.experimental import pallas as pl
from jax.experimental.pallas import tpu as pltpu

D_MODEL = 1024
CHUNK = 64
LEFT_CHUNKS = 8
N_MEM = 256
ATT_HEADS = 8
ATT_HEAD_DIM = 64
D_ATT = ATT_HEADS * ATT_HEAD_DIM
D_CONV = 512
MAX_REL = 256
X_HEADS = 4
X_HEAD_DIM = 128
D_X = X_HEADS * X_HEAD_DIM
N_EXPERTS = 32
TOP_K = 4
D_FF = D_MODEL
SWIGLU_LIMIT = 7.0
SWIGLU_ALPHA = 1.702
EPS = 1e-6
NEG_INF = -1e30

LANES = 128
SEQ_TILE = 1024
Q_TILE = 256
K_BAND = Q_TILE + LEFT_CHUNKS * CHUNK
START_TILES = LEFT_CHUNKS * CHUNK // Q_TILE
LOG2E = 1.4426950408889634
HEAD_ROUND = 4
EXPERT_TILE = 512
EXPERT_CHUNK = 2
GU_TILE = 2 * LANES
GROUP = 256
CELL_UNROLL = 4
STEP_GROUPS = 2
SUBLANES = 8
GROUP_ROWS = 1280
VMEM_LIMIT = 56 * 1024 * 1024

bf16 = jnp.bfloat16
f32 = jnp.float32


def _rms(x, g):
    return x * lax.rsqrt(jnp.mean(x * x, axis=-1, keepdims=True) + EPS) * g


def _dot(a, b):
    return jnp.dot(a, b, preferred_element_type=f32)


def _dot_nt(a, b):
    return lax.dot_general(a, b, (((1,), (1,)), ((), ())), preferred_element_type=f32)


def _mix_in_kernel(x_ref, g_ref, w_ref, qg_ref, kg_ref, hm_ref, cw_ref, gb_ref,
                   q_ref, k_ref, v_ref, yb_ref, ub_ref):
    ts = x_ref.shape[1]

    @pl.when(pl.program_id(1) == 0)
    def _():
        ub_ref[0:8, :] = jnp.zeros((8, D_CONV), f32)

    hb = _rms(x_ref[0], g_ref[...]).astype(bf16)

    def head_norm(t, gain):
        ms = _dot((t * t).astype(bf16), hm_ref[...]) * (1.0 / ATT_HEAD_DIM)
        return t * lax.rsqrt(ms + EPS) * gain

    o = 3 * D_ATT
    bg = _dot(hb, w_ref[:, o:o + D_CONV])
    cg = _dot(hb, w_ref[:, o + D_CONV:o + 2 * D_CONV])
    xv = _dot(hb, w_ref[:, o + 2 * D_CONV:o + 3 * D_CONV])
    u = cg * xv
    ub_ref[8:8 + ts, :] = u
    conv = (cw_ref[0:1, :] * ub_ref[6:6 + ts, :] + cw_ref[1:2, :] * ub_ref[7:7 + ts, :]
            + cw_ref[2:3, :] * u)
    ub_ref[0:8, :] = u[ts - 8:ts, :]
    yb_ref[0] = _rms(bg * conv, gb_ref[...]).astype(bf16)

    q = _dot(hb, w_ref[:, 0:D_ATT])
    q_ref[0] = head_norm(q, qg_ref[...]).astype(bf16)
    k = _dot(hb, w_ref[:, D_ATT:2 * D_ATT])
    k_ref[0] = head_norm(k, kg_ref[...]).astype(bf16)
    v_ref[0] = _dot(hb, w_ref[:, 2 * D_ATT:3 * D_ATT]).astype(bf16)


def _mix_in(x, g, w_in, qg, kg, hm, cw, gb):
    B, S, D = x.shape
    ts = SEQ_TILE
    full = lambda shape: pl.BlockSpec(shape, lambda b, j: (0,) * len(shape))
    tile = lambda w: pl.BlockSpec((1, ts, w), lambda b, j: (b, j, 0))
    out = jax.ShapeDtypeStruct((B, S, D_ATT), bf16)
    return pl.pallas_call(
        _mix_in_kernel,
        grid=(B, S // ts),
        in_specs=[tile(D), full((1, D)), full(w_in.shape), full((1, D_ATT)), full((1, D_ATT)),
                  full((D_ATT, D_ATT)), full((3, D_CONV)), full((1, D_CONV))],
        out_specs=[tile(D_ATT)] * 4,
        out_shape=[out] * 4,
        scratch_shapes=[pltpu.VMEM((ts + 8, D_CONV), f32)],
        compiler_params=pltpu.CompilerParams(
            dimension_semantics=("arbitrary", "arbitrary"), vmem_limit_bytes=VMEM_LIMIT),
        name="mix_in",
    )(x, g, w_in, qg, kg, hm, cw, gb)


def _attn_out_kernel(q_ref, k_ref, v_ref, bias_ref, yb_ref, x_ref, w_ref, ga_ref,
                     o_ref, kb_ref, vb_ref):
    tq = q_ref.shape[1]
    n_kc = K_BAND // CHUNK
    c0 = pl.program_id(1) * (tq // CHUNK)
    for kc in range(n_kc):
        src = pl.multiple_of(jnp.maximum(c0 - LEFT_CHUNKS + kc, 0) * CHUNK, CHUNK)
        kb_ref[kc * CHUNK:(kc + 1) * CHUNK, :] = k_ref[0, pl.ds(src, CHUNK), :]
        vb_ref[kc * CHUNK:(kc + 1) * CHUNK, :] = v_ref[0, pl.ds(src, CHUNK), :]

    lane = lax.broadcasted_iota(jnp.int32, (1, LANES), 1)
    low = lane < ATT_HEAD_DIM

    outs = []
    for h0 in range(0, ATT_HEADS, HEAD_ROUND):
        heads = range(h0, h0 + HEAD_ROUND)
        scores, probs, denoms = {}, {}, {}
        for h in heads:
            sl = slice((h // 2) * LANES, (h // 2 + 1) * LANES)
            qp = q_ref[0, :, sl]
            keep = low if h % 2 == 0 else jnp.logical_not(low)
            scores[h] = _dot_nt(jnp.where(keep, qp, jnp.zeros_like(qp)), kb_ref[:, sl])
        for h in heads:
            s = scores[h] + bias_ref[0, h]
            p = jnp.exp2(s - jnp.max(s, axis=-1, keepdims=True))
            denoms[h] = jnp.sum(p, axis=-1, keepdims=True)
            probs[h] = p.astype(bf16)
        for h in heads:
            sl = slice((h // 2) * LANES, (h // 2 + 1) * LANES)
            outs.append(_dot(probs[h], vb_ref[:, sl]) / denoms[h])
    ya = jnp.concatenate([jnp.where(low, outs[2 * hp], outs[2 * hp + 1])
                          for hp in range(ATT_HEADS // 2)], axis=-1)
    yan = _rms(ya, ga_ref[...]).astype(bf16)
    o_ref[0] = (x_ref[0] + _dot(yan, w_ref[0:D_ATT, :]) + _dot(yb_ref[0], w_ref[D_ATT:, :]))


def _attn_out(q, k, v, bias, yb, x, w_out, ga):
    B, S, D = x.shape
    tq = Q_TILE
    full = lambda shape: pl.BlockSpec(shape, lambda b, j: (0,) * len(shape))
    tile = lambda w: pl.BlockSpec((1, tq, w), lambda b, j: (b, j, 0))
    seq = pl.BlockSpec((1, S, D_ATT), lambda b, j: (b, 0, 0))
    return pl.pallas_call(
        _attn_out_kernel,
        grid=(B, S // tq),
        in_specs=[tile(D_ATT), seq, seq,
                  pl.BlockSpec((1,) + bias.shape[1:], lambda b, j: (jnp.minimum(j, START_TILES), 0, 0, 0)),
                  tile(D_CONV), tile(D),
                  full(w_out.shape), full((1, D_ATT))],
        out_specs=tile(D),
        out_shape=jax.ShapeDtypeStruct((B, S, D), f32),
        scratch_shapes=[pltpu.VMEM((K_BAND, D_ATT), bf16), pltpu.VMEM((K_BAND, D_ATT), bf16)],
        compiler_params=pltpu.CompilerParams(
            dimension_semantics=("parallel", "parallel"), vmem_limit_bytes=VMEM_LIMIT),
        name="attn_out",
    )(q, k, v, bias, yb, x, w_out, ga)


def _mem_kv_kernel(m_ref, g_ref, w_ref, kg_ref, k_ref, v_ref):
    hm = _rms(m_ref[...], g_ref[...]).astype(bf16)
    kv = _dot(hm, w_ref[...])
    for h in range(X_HEADS):
        sl = slice(h * X_HEAD_DIM, (h + 1) * X_HEAD_DIM)
        k_ref[:, sl] = _rms(kv[:, sl], kg_ref[...]).astype(bf16)
    v_ref[...] = kv[:, D_X:].astype(bf16)


def _mem_kv(mem, g, w_xkv, kg):
    B = mem.shape[0]
    rows = B * N_MEM
    tile = min(SEQ_TILE, rows)
    full = lambda shape: pl.BlockSpec(shape, lambda i: (0,) * len(shape))
    out = jax.ShapeDtypeStruct((rows, D_X), bf16)
    kx, vx = pl.pallas_call(
        _mem_kv_kernel,
        grid=(rows // tile,),
        in_specs=[pl.BlockSpec((tile, D_MODEL), lambda i: (i, 0)), full((1, D_MODEL)),
                  full(w_xkv.shape), full((1, X_HEAD_DIM))],
        out_specs=[pl.BlockSpec((tile, D_X), lambda i: (i, 0))] * 2,
        out_shape=[out, out],
        compiler_params=pltpu.CompilerParams(
            dimension_semantics=("parallel",), vmem_limit_bytes=VMEM_LIMIT),
        name="mem_kv",
    )(mem.reshape(rows, D_MODEL), g, w_xkv, kg)
    return kx.reshape(B, N_MEM, D_X), vx.reshape(B, N_MEM, D_X)


def _xattn_router_kernel(x_ref, gx_ref, wq_ref, qg_ref, k_ref, v_ref, wo_ref, gf_ref,
                         wr_ref, br_ref,
                         x2_ref, hf_ref, route_ref, cnt_ref):
    x1 = x_ref[0]
    hb = _rms(x1, gx_ref[...]).astype(bf16)
    q = _dot(hb, wq_ref[...])
    hsl = [slice(h * X_HEAD_DIM, (h + 1) * X_HEAD_DIM) for h in range(X_HEADS)]
    scores = [_dot_nt(_rms(q[:, sl], qg_ref[...]).astype(bf16), k_ref[0, :, sl]) for sl in hsl]
    probs, denoms = [], []
    for s in scores:
        p = jnp.exp2(s - jnp.max(s, axis=-1, keepdims=True))
        denoms.append(jnp.sum(p, axis=-1, keepdims=True))
        probs.append(p.astype(bf16))
    heads = [(_dot(p, v_ref[0, :, sl]) / l).astype(bf16) for p, l, sl in zip(probs, denoms, hsl)]
    o = jnp.concatenate(heads, axis=-1)
    x2 = x1 + _dot(o, wo_ref[...])
    x2_ref[0] = x2

    hf = _rms(x2, gf_ref[...])
    hf_hi = hf.astype(bf16)
    hf_ref[0] = hf_hi
    hf_lo = (hf - hf_hi.astype(f32)).astype(bf16)
    hi_prod = _dot(hf_hi, wr_ref[...])
    logits = (hi_prod[:, :LANES] + hi_prod[:, LANES:] + _dot(hf_lo, wr_ref[:, :LANES])
              + br_ref[...])
    lt = logits.T[:N_EXPERTS]
    eidx = lax.broadcasted_iota(jnp.int32, lt.shape, 0)
    vals, idxs = [], []
    for _ in range(TOP_K):
        m = jnp.max(lt, axis=0, keepdims=True)
        i = jnp.min(jnp.where(lt == m, eidx, N_EXPERTS), axis=0, keepdims=True)
        lt = jnp.where(eidx == i, -jnp.inf, lt)
        vals.append(m)
        idxs.append(i)
    es = [jnp.exp(v - vals[0]) for v in vals]
    tot = es[0] + es[1] + es[2] + es[3]
    packed = jnp.concatenate([i.astype(f32) for i in idxs] + [e / tot for e in es]
                             + [jnp.zeros((LANES - 2 * TOP_K, lt.shape[1]), f32)], axis=0)
    route_ref[0] = packed.T
    sel = jnp.zeros(lt.shape, f32)
    for i in idxs:
        sel = jnp.where(eidx == i, 1.0, sel)
    lane = lax.broadcasted_iota(jnp.int32, (N_EXPERTS, LANES), 1)
    cnt = jnp.zeros((N_EXPERTS, LANES), f32)
    for j in range(lt.shape[1] // GROUP):
        cnt = jnp.where(lane == j, jnp.sum(sel[:, j * GROUP:(j + 1) * GROUP], axis=1, keepdims=True), cnt)
    cnt_ref[0] = cnt


def _xattn_router(x1, gx, w_xq, qg, kx, vx, w_xo, gf, wr, br):
    B, S, D = x1.shape
    ts = SEQ_TILE
    full = lambda shape: pl.BlockSpec(shape, lambda b, j: (0,) * len(shape))
    tile = lambda w: pl.BlockSpec((1, ts, w), lambda b, j: (b, j, 0))
    mem = pl.BlockSpec((1, N_MEM, D_X), lambda b, j: (b, 0, 0))
    return pl.pallas_call(
        _xattn_router_kernel,
        grid=(B, S // ts),
        in_specs=[tile(D), full((1, D)), full(w_xq.shape), full((1, X_HEAD_DIM)), mem, mem,
                  full(w_xo.shape), full((1, D)), full(wr.shape),
                  full((1, LANES))],
        out_specs=[tile(D), tile(D), tile(LANES),
                   pl.BlockSpec((1, N_EXPERTS, LANES), lambda b, j: (b * (S // ts) + j, 0, 0))],
        out_shape=[jax.ShapeDtypeStruct((B, S, D), f32), jax.ShapeDtypeStruct((B, S, D), bf16),
                   jax.ShapeDtypeStruct((B, S, LANES), f32),
                   jax.ShapeDtypeStruct((B * S // ts, N_EXPERTS, LANES), f32)],
        compiler_params=pltpu.CompilerParams(
            dimension_semantics=("parallel", "parallel"), vmem_limit_bytes=VMEM_LIMIT),
        name="xattn_router",
    )(x1, gx, w_xq, qg, kx, vx, w_xo, gf, wr, br)


u32 = jnp.uint32
HIGH_HALF = 0xFFFF0000


def _pack_pairs(x):
    w = x.shape[1] // 2
    lo = pltpu.bitcast(x[:, :w], u32) >> 16
    hi = pltpu.bitcast(x[:, w:], u32) & u32(HIGH_HALF)
    return lo | hi


def _unpack_pairs(u):
    lo = pltpu.bitcast(u << 16, f32)
    hi = pltpu.bitcast(u & u32(HIGH_HALF), f32)
    return jnp.concatenate([lo, hi], axis=-1).astype(bf16)


def _for_cells(g, fn):
    def body(j, carry):
        for u in range(CELL_UNROLL):
            fn(g * N_EXPERTS + j * CELL_UNROLL + u)
        return carry
    lax.fori_loop(0, N_EXPERTS // CELL_UNROLL, body, 0)


def _group_positions(route, coffv, ltri):
    lane = lax.broadcasted_iota(jnp.int32, route.shape, 1).astype(f32)
    hots = [lane == route[:, kk:kk + 1] for kk in range(TOP_K)]
    multi = jnp.zeros(route.shape, f32)
    for h in hots:
        multi = jnp.where(h, 1.0, multi)
    rank = _dot(ltri, multi.astype(bf16))
    posmat = coffv + rank
    return [jnp.sum(jnp.where(h, posmat, 0.0), axis=-1, keepdims=True) for h in hots]


def _dispatch_kernel(coff_s, rows_s, gbase_s, gtot_s, toff_s, tn_s, nu_s,
                     hf_ref, route_ref, coffv_ref, ltri_ref,
                     xs_hbm, pos_ref, xbuf, zbuf, sem, zsem, *, n_steps):
    step = pl.program_id(0)
    places = range(STEP_GROUPS)
    group = lambda i: step * STEP_GROUPS + i
    slot = lambda i: lax.rem(step, 2) * STEP_GROUPS + i
    tok = lambda i: slice(i * GROUP, (i + 1) * GROUP)

    def cell_copy(sl, i):
        n = pl.multiple_of(rows_s[i], SUBLANES)
        src = pl.multiple_of(coff_s[i], SUBLANES)
        dst = pl.multiple_of(gbase_s[i], SUBLANES)
        return n, pltpu.make_async_copy(xbuf.at[sl, pl.ds(src, n)], xs_hbm.at[pl.ds(dst, n)],
                                        sem.at[sl])

    def start_cell(sl, i):
        n, cp = cell_copy(sl, i)
        pl.when(n > 0)(cp.start)

    def wait_group(sl, gg):
        n = pl.multiple_of(gtot_s[gg], SUBLANES)
        pltpu.make_async_copy(xbuf.at[sl, pl.ds(0, n)], xs_hbm.at[pl.ds(0, n)], sem.at[sl]).wait()

    @pl.when(step >= 2)
    def _():
        for i in places:
            wait_group(slot(i), group(i) - 2 * STEP_GROUPS)

    pos = [_group_positions(route_ref[tok(i), :], coffv_ref[i], ltri_ref[...]) for i in places]
    lane = lax.broadcasted_iota(jnp.int32, (GROUP, LANES), 1)
    riota = lax.broadcasted_iota(jnp.int32, (GROUP, GROUP_ROWS), 1)
    pts = []
    for i in places:
        pt = jnp.zeros((GROUP, GROUP_ROWS), f32)
        pos_out = jnp.zeros((GROUP, LANES), f32)
        for kk in range(TOP_K):
            pt = jnp.where(riota == pos[i][kk].astype(jnp.int32), 1.0, pt)
            pos_out = jnp.where(lane == kk, pos[i][kk], pos_out)
        pos_ref[tok(i), :] = pos_out
        pts.append(pt.astype(bf16))
    rows = [lax.dot_general(pts[i], hf_ref[tok(i), :], (((0,), (0,)), ((), ())),
                            preferred_element_type=f32) for i in places]
    for i in places:
        xbuf[slot(i)] = _pack_pairs(rows[i])
    for i in places:
        _for_cells(group(i), functools.partial(start_cell, slot(i)))

    @pl.when(step == n_steps - 1)
    def _():
        if n_steps >= 2:
            for i in places:
                prev_slot = (1 - lax.rem(step, 2)) * STEP_GROUPS + i
                wait_group(prev_slot, group(i) - STEP_GROUPS)
        for i in places:
            wait_group(slot(i), group(i))
        zbuf[...] = jnp.zeros(zbuf.shape, u32)

        def tail_copy(e):
            n = pl.multiple_of(tn_s[e], SUBLANES)
            dst = pl.multiple_of(toff_s[e], SUBLANES)
            return n, pltpu.make_async_copy(zbuf.at[pl.ds(0, n)], xs_hbm.at[pl.ds(dst, n)], zsem)

        def unused_copy(b):
            dst = pl.multiple_of(b * EXPERT_TILE, EXPERT_TILE)
            return pltpu.make_async_copy(zbuf, xs_hbm.at[pl.ds(dst, EXPERT_TILE)], zsem)

        def run(method):
            def tail(e, carry):
                n, cp = tail_copy(e)
                pl.when(n > 0)(getattr(cp, method))
                return carry
            lax.fori_loop(0, N_EXPERTS, tail, 0)

            def unused(b, carry):
                getattr(unused_copy(b), method)()
                return carry
            lax.fori_loop(nu_s[0], xs_hbm.shape[0] // EXPERT_TILE, unused, 0)
        run("start")
        run("wait")


def _dispatch(plan, hf, route, n_rows):
    n_tok, D = hf.shape
    n_steps = n_tok // (GROUP * STEP_GROUPS)
    assert n_steps * GROUP * STEP_GROUPS == n_tok
    ltri = jnp.asarray(np.tril(np.ones((GROUP, GROUP)), -1), bf16)
    tok = lambda w: pl.BlockSpec((STEP_GROUPS * GROUP, w), lambda s, *_: (s, 0))
    grid_spec = pltpu.PrefetchScalarGridSpec(
        num_scalar_prefetch=7,
        grid=(n_steps,),
        in_specs=[tok(D), tok(LANES),
                  pl.BlockSpec((STEP_GROUPS, 1, LANES), lambda s, *_: (s, 0, 0)),
                  pl.BlockSpec((GROUP, GROUP), lambda s, *_: (0, 0))],
        out_specs=[pl.BlockSpec(memory_space=pl.ANY), tok(LANES)],
        scratch_shapes=[pltpu.VMEM((2 * STEP_GROUPS, GROUP_ROWS, D // 2), u32),
                        pltpu.VMEM((EXPERT_TILE, D // 2), u32),
                        pltpu.SemaphoreType.DMA((2 * STEP_GROUPS,)), pltpu.SemaphoreType.DMA(())],
    )
    return pl.pallas_call(
        functools.partial(_dispatch_kernel, n_steps=n_steps),
        grid_spec=grid_spec,
        out_shape=[jax.ShapeDtypeStruct((n_rows, D // 2), u32),
                   jax.ShapeDtypeStruct((n_tok, LANES), f32)],
        compiler_params=pltpu.CompilerParams(
            dimension_semantics=("arbitrary",), vmem_limit_bytes=VMEM_LIMIT),
        name="dispatch",
    )(plan["coff"], plan["rows"], plan["gbase"], plan["gtot"], plan["tailoff"], plan["tailn"],
      plan["n_used"], hf, route, plan["coffv"], ltri)


def _expert_kernel(eb_ref, nb_ref, nu_ref, wgu_ref, bgu_ref, wdn_ref, bdn_ref, pm_ref, xs_hbm,
                   y_hbm, wgu_b, wdn_b, xin, yout, in_sem, out_sem):
    e = pl.program_id(0)
    tm = EXPERT_TILE
    chunk_rows = EXPERT_CHUNK * tm
    n_used = nu_ref[0]
    n_chunks = (n_used + EXPERT_CHUNK - 1) // EXPERT_CHUNK

    def in_copy(c):
        sl = lax.rem(c, 2)
        rows = pl.ds(pl.multiple_of(c * chunk_rows, chunk_rows), chunk_rows)
        return pltpu.make_async_copy(xs_hbm.at[rows], xin.at[sl], in_sem.at[sl])

    def out_copy(c):
        sl = lax.rem(c, 2)
        rows = pl.ds(pl.multiple_of(c * chunk_rows, chunk_rows), chunk_rows)
        return pltpu.make_async_copy(yout.at[sl], y_hbm.at[rows], out_sem.at[sl])

    @pl.when(e == 0)
    def _():
        in_copy(0).start()
        yout[...] = jnp.zeros(yout.shape, u32)

    @pl.when(nb_ref[e] > 0)
    def _():
        for c in range(2 * D_FF // GU_TILE):
            cols = slice(c * GU_TILE, (c + 1) * GU_TILE)
            wgu_b[:, cols] = _dot(wgu_ref[0, :, cols].astype(bf16), pm_ref[...]).astype(bf16)
        for c in range(D_FF // LANES):
            rows = slice(c * LANES, (c + 1) * LANES)
            wdn_b[rows, :] = wdn_ref[0, rows, :].astype(bf16)

        def item(b, n_blk):
            c = b // EXPERT_CHUNK
            j = lax.rem(b, EXPERT_CHUNK)
            sl = lax.rem(c, 2)

            @pl.when(j == 0)
            def _():
                in_copy(c).wait()

                @pl.when(c + 1 < n_chunks)
                def _():
                    in_copy(c + 1).start()

                @pl.when(c >= 2)
                def _():
                    out_copy(c - 2).wait()

            rows = pl.ds(pl.multiple_of(j * tm, tm), n_blk * tm)
            _expert_block(xin.at[sl, rows], bgu_ref, bdn_ref, yout.at[sl, rows], wgu_b, wdn_b)
            last = b + n_blk - 1

            @pl.when((lax.rem(last, EXPERT_CHUNK) == EXPERT_CHUNK - 1) | (last == n_used - 1))
            def _():
                out_copy(c).start()

        first = eb_ref[e]
        n_head = jnp.minimum(lax.rem(EXPERT_CHUNK - lax.rem(first, EXPERT_CHUNK), EXPERT_CHUNK),
                             nb_ref[e])
        n_whole = (nb_ref[e] - n_head) // EXPERT_CHUNK
        n_tail = nb_ref[e] - n_head - n_whole * EXPERT_CHUNK

        def single(b, carry):
            item(b, 1)
            return carry

        def whole(i, carry):
            item(first + n_head + i * EXPERT_CHUNK, EXPERT_CHUNK)
            return carry
        lax.fori_loop(first, first + n_head, single, 0)
        lax.fori_loop(0, n_whole, whole, 0)
        tail0 = first + n_head + n_whole * EXPERT_CHUNK
        lax.fori_loop(tail0, tail0 + n_tail, single, 0)

    @pl.when(e == pl.num_programs(0) - 1)
    def _():
        @pl.when(n_chunks >= 2)
        def _():
            out_copy(n_chunks - 2).wait()
        out_copy(n_chunks - 1).wait()
        yout[0] = jnp.zeros(yout.shape[1:], u32)

        def run(method):
            def unused(c, carry):
                rows = pl.ds(pl.multiple_of(c * chunk_rows, chunk_rows), chunk_rows)
                getattr(pltpu.make_async_copy(yout.at[0], y_hbm.at[rows], out_sem.at[0]), method)()
                return carry
            lax.fori_loop(n_chunks, y_hbm.shape[0] // chunk_rows, unused, 0)
        run("start")
        run("wait")


def _expert_block(x_ref, bgu_ref, bdn_ref, y_ref, wgu_b, wdn_b):
    gu = _dot(_unpack_pairs(x_ref[...]), wgu_b[...]) + bgu_ref[0]
    acts = []
    for c in range(2 * D_FF // GU_TILE):
        g = jnp.minimum(gu[:, c * GU_TILE:c * GU_TILE + LANES], SWIGLU_LIMIT)
        lin = jnp.clip(gu[:, c * GU_TILE + LANES:(c + 1) * GU_TILE], -SWIGLU_LIMIT, SWIGLU_LIMIT)
        acts.append((g * jax.nn.sigmoid(SWIGLU_ALPHA * g) * (lin + 1.0)).astype(bf16))
    act = jnp.concatenate(acts, axis=-1)
    y = _dot(act, wdn_b[...]) + bdn_ref[0]
    y_ref[...] = _pack_pairs(y.astype(bf16).astype(f32))


def _gate_up_perm():
    j = np.arange(GU_TILE)
    p = np.zeros((GU_TILE, GU_TILE), np.float32)
    p[np.where(j < LANES, 2 * j, 2 * (j - LANES) + 1), j] = 1.0
    return jnp.asarray(p, bf16)


def _expert_mlp(plan, xs, w_gu, b_gu, w_dn, b_dn):
    n_rows = xs.shape[0]
    D = D_MODEL
    tm = EXPERT_TILE
    per_e = lambda e, *_: (e, 0, 0)
    grid_spec = pltpu.PrefetchScalarGridSpec(
        num_scalar_prefetch=3,
        grid=(N_EXPERTS,),
        in_specs=[pl.BlockSpec((1, D, 2 * D_FF), per_e),
                  pl.BlockSpec((1, 1, 2 * D_FF), per_e),
                  pl.BlockSpec((1, D_FF, D), per_e),
                  pl.BlockSpec((1, 1, D), per_e),
                  pl.BlockSpec((GU_TILE, GU_TILE), lambda e, *_: (0, 0)),
                  pl.BlockSpec(memory_space=pl.ANY)],
        out_specs=pl.BlockSpec(memory_space=pl.ANY),
        scratch_shapes=[pltpu.VMEM((D, 2 * D_FF), bf16), pltpu.VMEM((D_FF, D), bf16),
                        pltpu.VMEM((2, EXPERT_CHUNK * tm, D // 2), u32),
                        pltpu.VMEM((2, EXPERT_CHUNK * tm, D // 2), u32),
                        pltpu.SemaphoreType.DMA((2,)), pltpu.SemaphoreType.DMA((2,))],
    )
    return pl.pallas_call(
        _expert_kernel,
        grid_spec=grid_spec,
        out_shape=jax.ShapeDtypeStruct((n_rows, D // 2), u32),
        compiler_params=pltpu.CompilerParams(
            dimension_semantics=("arbitrary",), vmem_limit_bytes=VMEM_LIMIT),
        name="expert_mlp",
    )(plan["eblk"], plan["nblk"], plan["n_used"], w_gu, b_gu, w_dn, b_dn, _gate_up_perm(), xs)


def _combine_kernel(coff_s, rows_s, gbase_s, gtot_s, x_ref, pos_ref, route_ref, y_hbm, o_ref, ybuf, sem,
                    *, n_steps):
    step = pl.program_id(0)
    places = range(STEP_GROUPS)
    group = lambda s, i: s * STEP_GROUPS + i
    slot = lambda s, i: lax.rem(s, 2) * STEP_GROUPS + i
    tok = lambda i: slice(i * GROUP, (i + 1) * GROUP)

    def cell_copy(sl, i):
        n = pl.multiple_of(rows_s[i], SUBLANES)
        loc = pl.multiple_of(coff_s[i], SUBLANES)
        src = pl.multiple_of(gbase_s[i], SUBLANES)
        return n, pltpu.make_async_copy(y_hbm.at[pl.ds(src, n)], ybuf.at[sl, pl.ds(loc, n)],
                                        sem.at[sl])

    def start_cell(sl, i):
        n, cp = cell_copy(sl, i)
        pl.when(n > 0)(cp.start)

    def wait_group(sl, gg):
        n = pl.multiple_of(gtot_s[gg], SUBLANES)
        pltpu.make_async_copy(y_hbm.at[pl.ds(0, n)], ybuf.at[sl, pl.ds(0, n)], sem.at[sl]).wait()

    @pl.when(step == 0)
    def _():
        ybuf[...] = jnp.zeros(ybuf.shape, u32)
        for i in places:
            _for_cells(group(step, i), functools.partial(start_cell, slot(step, i)))

    @pl.when(step + 1 < n_steps)
    def _():
        for i in places:
            _for_cells(group(step + 1, i), functools.partial(start_cell, slot(step + 1, i)))

    for i in places:
        wait_group(slot(step, i), group(step, i))

    riota = lax.broadcasted_iota(jnp.int32, (GROUP, GROUP_ROWS), 1)
    pts = []
    for i in places:
        pt = jnp.zeros((GROUP, GROUP_ROWS), f32)
        for kk in range(TOP_K):
            pk = pos_ref[tok(i), kk:kk + 1].astype(jnp.int32)
            pt = jnp.where(riota == pk, route_ref[tok(i), TOP_K + kk:TOP_K + kk + 1], pt)
        pts.append(pt.astype(bf16))
    ys = [_unpack_pairs(ybuf[slot(step, i)]) for i in places]
    outs = [_dot(pts[i], ys[i]) for i in places]
    for i in places:
        o_ref[tok(i), :] = x_ref[tok(i), :] + outs[i]


def _combine(plan, x2, pos, route, y_rows):
    n_tok, D = x2.shape
    n_steps = n_tok // (GROUP * STEP_GROUPS)
    assert n_steps * GROUP * STEP_GROUPS == n_tok
    tok = lambda w: pl.BlockSpec((STEP_GROUPS * GROUP, w), lambda s, *_: (s, 0))
    grid_spec = pltpu.PrefetchScalarGridSpec(
        num_scalar_prefetch=4,
        grid=(n_steps,),
        in_specs=[tok(D), tok(LANES), tok(LANES), pl.BlockSpec(memory_space=pl.ANY)],
        out_specs=tok(D),
        scratch_shapes=[pltpu.VMEM((2 * STEP_GROUPS, GROUP_ROWS, D // 2), u32),
                        pltpu.SemaphoreType.DMA((2 * STEP_GROUPS,))],
    )
    return pl.pallas_call(
        functools.partial(_combine_kernel, n_steps=n_steps),
        grid_spec=grid_spec,
        out_shape=jax.ShapeDtypeStruct((n_tok, D), f32),
        compiler_params=pltpu.CompilerParams(
            dimension_semantics=("arbitrary",), vmem_limit_bytes=VMEM_LIMIT),
        name="combine",
    )(plan["coff"], plan["rows"], plan["gbase"], plan["gtot"], x2, pos, route, y_rows)


def _routing_plan(cnt, n_tok):
    tm = EXPERT_TILE
    n_groups = n_tok // GROUP
    per_tile = SEQ_TILE // GROUP
    cnt = cnt[:, :, :per_tile].transpose(0, 2, 1).reshape(n_groups, N_EXPERTS).astype(jnp.int32)
    rows = (cnt + SUBLANES - 1) // SUBLANES * SUBLANES
    coff = jnp.cumsum(rows, axis=1) - rows
    tot = jnp.sum(rows, axis=0)
    padded = (tot + tm - 1) // tm * tm
    pend = jnp.cumsum(padded)
    ebase = pend - padded
    gbase = ebase[None, :] + jnp.cumsum(rows, axis=0) - rows
    flat = lambda a: a.reshape(-1).astype(jnp.int32)
    return dict(
        coff=flat(coff), rows=flat(rows), gbase=flat(gbase), gtot=flat(jnp.sum(rows, axis=1)),
        tailoff=flat(ebase + tot), tailn=flat(padded - tot),
        eblk=flat(ebase // tm), nblk=flat(padded // tm), n_used=flat(pend[-1] // tm),
        coffv=jnp.zeros((n_groups, 1, LANES), f32).at[:, 0, :N_EXPERTS].set(coff.astype(f32)))


def _max_rows(n_tok):
    tm = EXPERT_TILE
    worst = (n_tok * TOP_K + (n_tok // GROUP) * N_EXPERTS * (SUBLANES - 1)
             + N_EXPERTS * (tm - SUBLANES))
    chunk = EXPERT_CHUNK * tm
    return (worst + chunk - 1) // chunk * chunk


def _moe(x2, hf, route, cnt, w_gate_up, b_gate_up, w_down, b_down):
    n_tok, D = x2.shape
    plan = _routing_plan(cnt, n_tok)
    xs, pos = _dispatch(plan, hf, route, _max_rows(n_tok))
    b_gu = (b_gate_up.reshape(N_EXPERTS, 2 * D_FF // GU_TILE, LANES, 2).transpose(0, 1, 3, 2)
            .reshape(N_EXPERTS, 1, 2 * D_FF))
    y_rows = _expert_mlp(plan, xs, w_gate_up, b_gu, w_down, b_down.reshape(N_EXPERTS, 1, D))
    return _combine(plan, x2, pos, route, y_rows)


BIAS_ROW = (2 * Q_TILE + LEFT_CHUNKS * CHUNK + LANES - 2) // LANES * LANES


def _band_bias_kernel(r_ref, o_ref):
    qc = lax.broadcasted_iota(jnp.int32, (Q_TILE, K_BAND), 0) // CHUNK
    kc = lax.broadcasted_iota(jnp.int32, (Q_TILE, K_BAND), 1) // CHUNK
    first_chunk = LEFT_CHUNKS - pl.program_id(0) * (Q_TILE // CHUNK)
    visible = (kc >= qc) & (kc <= qc + LEFT_CHUNKS) & (kc >= first_chunk)
    for h in range(r_ref.shape[0]):
        x = jnp.broadcast_to(r_ref[h], (Q_TILE, BIAS_ROW))
        x = pltpu.roll(x, BIAS_ROW - (Q_TILE - 1), axis=1, stride=1, stride_axis=0)
        o_ref[0, h] = jnp.where(visible, x[:, :K_BAND] * LOG2E, NEG_INF)


def _band_bias(rel_bias):
    H = rel_bias.shape[0]
    d_max = Q_TILE - 1 + LEFT_CHUNKS * CHUNK
    n_far = d_max - MAX_REL + 1
    r = jnp.concatenate(
        [jnp.broadcast_to(rel_bias[:, 2 * MAX_REL:], (H, n_far)),
         rel_bias[:, MAX_REL - (Q_TILE - 1):2 * MAX_REL][:, ::-1],
         jnp.zeros((H, BIAS_ROW - (2 * Q_TILE - 1 + LEFT_CHUNKS * CHUNK)), rel_bias.dtype)], axis=1)
    return pl.pallas_call(
        _band_bias_kernel,
        grid=(START_TILES + 1,),
        in_specs=[pl.BlockSpec((H, 1, BIAS_ROW), lambda t: (0, 0, 0))],
        out_specs=pl.BlockSpec((1, H, Q_TILE, K_BAND), lambda t: (t, 0, 0, 0)),
        out_shape=jax.ShapeDtypeStruct((START_TILES + 1, H, Q_TILE, K_BAND), f32),
        name="band_bias",
    )(r.astype(f32).reshape(H, 1, BIAS_ROW))


def _layer(x, mem, norm_mix_g, w_in, q_norm_g, k_norm_g, rel_bias, conv_w, out_norm_a_g,
           out_norm_b_g, w_out, norm_xattn_g, norm_mem_g, w_xq, w_xkv, xq_norm_g, xk_norm_g,
           w_xo, norm_ffn_g, w_router, b_router, w_gate_up, b_gate_up, w_down, b_down):
    B, S, D = x.shape
    n_tok = B * S
    row = lambda a: a.reshape(1, -1).astype(f32)

    qg = row(jnp.tile(q_norm_g, ATT_HEADS) * (ATT_HEAD_DIM ** -0.5 * LOG2E))
    kg = row(jnp.tile(k_norm_g, ATT_HEADS))
    head_ones = jnp.asarray(np.kron(np.eye(ATT_HEADS), np.ones((ATT_HEAD_DIM, ATT_HEAD_DIM))), bf16)
    q, k, v, yb = _mix_in(x, row(norm_mix_g), w_in.astype(bf16), qg, kg, head_ones,
                          conv_w.astype(f32), row(out_norm_b_g))
    x1 = _attn_out(q, k, v, _band_bias(rel_bias), yb, x, w_out.astype(bf16), row(out_norm_a_g))

    kx, vx = _mem_kv(mem, row(norm_mem_g), w_xkv.astype(bf16), row(xk_norm_g))
    wr = jnp.zeros((D, LANES), f32).at[:, :N_EXPERTS].set(w_router)
    wr_hi = wr.astype(bf16)
    wr_lo = (wr - wr_hi.astype(f32)).astype(bf16)
    br = jnp.full((1, LANES), -jnp.inf, f32).at[0, :N_EXPERTS].set(b_router)
    x2, hf, route, cnt = _xattn_router(
        x1, row(norm_xattn_g), w_xq.astype(bf16), row(xq_norm_g * (X_HEAD_DIM ** -0.5 * LOG2E)),
        kx, vx, w_xo.astype(bf16), row(norm_ffn_g), jnp.concatenate([wr_hi, wr_lo], axis=1), br)

    out = _moe(x2.reshape(n_tok, D), hf.reshape(n_tok, D), route.reshape(n_tok, LANES), cnt,
               w_gate_up, b_gate_up, w_down, b_down)
    return out.reshape(B, S, D)


def kernel(x, mem, norm_mix_g, w_in, q_norm_g, k_norm_g, rel_bias, conv_w, out_norm_a_g,
           out_norm_b_g, w_out, norm_xattn_g, norm_mem_g, w_xq, w_xkv, xq_norm_g, xk_norm_g,
           w_xo, norm_ffn_g, w_router, b_router, w_gate_up, b_gate_up, w_down, b_down):
    depth = norm_mix_g.shape[0]
    for l in range(depth):
        x = _layer(x, mem, norm_mix_g[l], w_in[l], q_norm_g[l], k_norm_g[l], rel_bias[l],
                   conv_w[l], out_norm_a_g[l], out_norm_b_g[l], w_out[l], norm_xattn_g[l],
                   norm_mem_g[l], w_xq[l], w_xkv[l], xq_norm_g[l], xk_norm_g[l], w_xo[l],
                   norm_ffn_g[l], w_router[l], b_router[l], w_gate_up[l], b_gate_up[l],
                   w_down[l], b_down[l])
    return x
```

```python
import functools

import jax
import jax.numpy as jnp
import numpy as np
from jax import lax
from jax.experimental import pallas as pl
from jax.experimental.pallas import tpu as pltpu

D_MODEL = 1024
CHUNK = 64
LEFT_CHUNKS = 8
N_MEM = 256
ATT_HEADS = 8
ATT_HEAD_DIM = 64
D_ATT = ATT_HEADS * ATT_HEAD_DIM
D_CONV = 512
MAX_REL = 256
X_HEADS = 4
X_HEAD_DIM = 128
D_X = X_HEADS * X_HEAD_DIM
N_EXPERTS = 32
TOP_K = 4
D_FF = D_MODEL
SWIGLU_LIMIT = 7.0
SWIGLU_ALPHA = 1.702
EPS = 1e-6
NEG_INF = -1e30

LANES = 128
SEQ_TILE = 1024
Q_TILE = 256
K_BAND = Q_TILE + LEFT_CHUNKS * CHUNK
START_TILES = LEFT_CHUNKS * CHUNK // Q_TILE
LOG2E = 1.4426950408889634
PAIR_ROUND = 2
EXPERT_TILE = 512
EXPERT_CHUNK = 2
GU_TILE = 2 * LANES
GROUP = 256
CELL_UNROLL = 4
STEP_GROUPS = 2
SUBLANES = 8
GROUP_ROWS = 1280
VMEM_LIMIT = 56 * 1024 * 1024

bf16 = jnp.bfloat16
f32 = jnp.float32


def _rms(x, g):
    return x * lax.rsqrt(jnp.mean(x * x, axis=-1, keepdims=True) + EPS) * g


def _dot(a, b):
    return jnp.dot(a, b, preferred_element_type=f32)


def _dot_nt(a, b):
    return lax.dot_general(a, b, (((1,), (1,)), ((), ())), preferred_element_type=f32)


def _mix_in_kernel(x_ref, g_ref, w_ref, qg_ref, kg_ref, hm_ref, cw_ref, gb_ref,
                   q_ref, k_ref, v_ref, yb_ref, ub_ref):
    ts = x_ref.shape[1]

    @pl.when(pl.program_id(1) == 0)
    def _():
        ub_ref[0:8, :] = jnp.zeros((8, D_CONV), f32)

    hb = _rms(x_ref[0], g_ref[...]).astype(bf16)

    def head_norm(t, gain):
        ms = _dot((t * t).astype(bf16), hm_ref[...]) * (1.0 / ATT_HEAD_DIM)
        return t * lax.rsqrt(ms + EPS) * gain

    o = 3 * D_ATT
    bg = _dot(hb, w_ref[:, o:o + D_CONV])
    cg = _dot(hb, w_ref[:, o + D_CONV:o + 2 * D_CONV])
    xv = _dot(hb, w_ref[:, o + 2 * D_CONV:o + 3 * D_CONV])
    u = cg * xv
    ub_ref[8:8 + ts, :] = u
    conv = (cw_ref[0:1, :] * ub_ref[6:6 + ts, :] + cw_ref[1:2, :] * ub_ref[7:7 + ts, :]
            + cw_ref[2:3, :] * u)
    ub_ref[0:8, :] = u[ts - 8:ts, :]
    yb_ref[0] = _rms(bg * conv, gb_ref[...]).astype(bf16)

    q = _dot(hb, w_ref[:, 0:D_ATT])
    q_ref[0] = head_norm(q, qg_ref[...]).astype(bf16)
    k = _dot(hb, w_ref[:, D_ATT:2 * D_ATT])
    k_ref[0] = head_norm(k, kg_ref[...]).astype(bf16)
    v_ref[0] = _dot(hb, w_ref[:, 2 * D_ATT:3 * D_ATT]).astype(bf16)


def _mix_in(x, g, w_in, qg, kg, hm, cw, gb):
    B, S, D = x.shape
    ts = SEQ_TILE
    full = lambda shape: pl.BlockSpec(shape, lambda b, j: (0,) * len(shape))
    tile = lambda w: pl.BlockSpec((1, ts, w), lambda b, j: (b, j, 0))
    out = jax.ShapeDtypeStruct((B, S, D_ATT), bf16)
    return pl.pallas_call(
        _mix_in_kernel,
        grid=(B, S // ts),
        in_specs=[tile(D), full((1, D)), full(w_in.shape), full((1, D_ATT)), full((1, D_ATT)),
                  full((D_ATT, D_ATT)), full((3, D_CONV)), full((1, D_CONV))],
        out_specs=[tile(D_ATT)] * 4,
        out_shape=[out] * 4,
        scratch_shapes=[pltpu.VMEM((ts + 8, D_CONV), f32)],
        compiler_params=pltpu.CompilerParams(
            dimension_semantics=("arbitrary", "arbitrary"), vmem_limit_bytes=VMEM_LIMIT),
        name="mix_in",
    )(x, g, w_in, qg, kg, hm, cw, gb)


def _attn_out_kernel(q_ref, k_ref, v_ref, bias_ref, yb_ref, x_ref, w_ref, ga_ref,
                     o_ref, kb_ref, vb_ref):
    tq = q_ref.shape[1]
    n_kc = K_BAND // CHUNK
    c0 = pl.program_id(1) * (tq // CHUNK)
    for kc in range(n_kc):
        src = pl.multiple_of(jnp.maximum(c0 - LEFT_CHUNKS + kc, 0) * CHUNK, CHUNK)
        kb_ref[kc * CHUNK:(kc + 1) * CHUNK, :] = k_ref[0, pl.ds(src, CHUNK), :]
        vb_ref[kc * CHUNK:(kc + 1) * CHUNK, :] = v_ref[0, pl.ds(src, CHUNK), :]

    lane = lax.broadcasted_iota(jnp.int32, (1, LANES), 1)
    low = lane < ATT_HEAD_DIM

    slab = lambda hp: slice(hp * LANES, (hp + 1) * LANES)
    outs = []
    for p0 in range(0, ATT_HEADS // 2, PAIR_ROUND):
        pairs = range(p0, p0 + PAIR_ROUND)
        scores, probs, denoms = {}, {}, {}
        for hp in pairs:
            qp = q_ref[0, :, slab(hp)]
            zero = jnp.zeros_like(qp)
            q2 = jnp.concatenate([jnp.where(low, qp, zero), jnp.where(low, zero, qp)], axis=0)
            scores[hp] = _dot_nt(q2, kb_ref[:, slab(hp)])
        for hp in pairs:
            s = scores[hp] + bias_ref[0, hp]
            p = jnp.exp2(s - jnp.max(s, axis=-1, keepdims=True))
            denoms[hp] = jnp.sum(p, axis=-1, keepdims=True)
            probs[hp] = p.astype(bf16)
        for hp in pairs:
            o2 = _dot(probs[hp], vb_ref[:, slab(hp)]) / denoms[hp]
            outs.append(jnp.where(low, o2[:tq], o2[tq:]))
    ya = jnp.concatenate(outs, axis=-1)
    yan = _rms(ya, ga_ref[...]).astype(bf16)
    o_ref[0] = (x_ref[0] + _dot(yan, w_ref[0:D_ATT, :]) + _dot(yb_ref[0], w_ref[D_ATT:, :]))


def _attn_out(q, k, v, bias, yb, x, w_out, ga):
    B, S, D = x.shape
    tq = Q_TILE
    full = lambda shape: pl.BlockSpec(shape, lambda b, j: (0,) * len(shape))
    tile = lambda w: pl.BlockSpec((1, tq, w), lambda b, j: (b, j, 0))
    seq = pl.BlockSpec((1, S, D_ATT), lambda b, j: (b, 0, 0))
    return pl.pallas_call(
        _attn_out_kernel,
        grid=(B, S // tq),
        in_specs=[tile(D_ATT), seq, seq,
                  pl.BlockSpec((1,) + bias.shape[1:], lambda b, j: (jnp.minimum(j, START_TILES), 0, 0, 0)),
                  tile(D_CONV), tile(D),
                  full(w_out.shape), full((1, D_ATT))],
        out_specs=tile(D),
        out_shape=jax.ShapeDtypeStruct((B, S, D), f32),
        scratch_shapes=[pltpu.VMEM((K_BAND, D_ATT), bf16), pltpu.VMEM((K_BAND, D_ATT), bf16)],
        compiler_params=pltpu.CompilerParams(
            dimension_semantics=("parallel", "parallel"), vmem_limit_bytes=VMEM_LIMIT),
        name="attn_out",
    )(q, k, v, bias, yb, x, w_out, ga)


def _mem_kv_kernel(m_ref, g_ref, w_ref, kg_ref, k_ref, v_ref):
    hm = _rms(m_ref[...], g_ref[...]).astype(bf16)
    kv = _dot(hm, w_ref[...])
    for h in range(X_HEADS):
        sl = slice(h * X_HEAD_DIM, (h + 1) * X_HEAD_DIM)
        k_ref[:, sl] = _rms(kv[:, sl], kg_ref[...]).astype(bf16)
    v_ref[...] = kv[:, D_X:].astype(bf16)


def _mem_kv(mem, g, w_xkv, kg):
    B = mem.shape[0]
    rows = B * N_MEM
    tile = min(SEQ_TILE, rows)
    full = lambda shape: pl.BlockSpec(shape, lambda i: (0,) * len(shape))
    out = jax.ShapeDtypeStruct((rows, D_X), bf16)
    kx, vx = pl.pallas_call(
        _mem_kv_kernel,
        grid=(rows // tile,),
        in_specs=[pl.BlockSpec((tile, D_MODEL), lambda i: (i, 0)), full((1, D_MODEL)),
                  full(w_xkv.shape), full((1, X_HEAD_DIM))],
        out_specs=[pl.BlockSpec((tile, D_X), lambda i: (i, 0))] * 2,
        out_shape=[out, out],
        compiler_params=pltpu.CompilerParams(
            dimension_semantics=("parallel",), vmem_limit_bytes=VMEM_LIMIT),
        name="mem_kv",
    )(mem.reshape(rows, D_MODEL), g, w_xkv, kg)
    return kx.reshape(B, N_MEM, D_X), vx.reshape(B, N_MEM, D_X)


def _xattn_router_kernel(x_ref, gx_ref, wq_ref, qg_ref, k_ref, v_ref, wo_ref, gf_ref,
                         wr_ref, br_ref,
                         x2_ref, hf_ref, route_ref, cnt_ref):
    x1 = x_ref[0]
    hb = _rms(x1, gx_ref[...]).astype(bf16)
    q = _dot(hb, wq_ref[...])
    hsl = [slice(h * X_HEAD_DIM, (h + 1) * X_HEAD_DIM) for h in range(X_HEADS)]
    scores = [_dot_nt(_rms(q[:, sl], qg_ref[...]).astype(bf16), k_ref[0, :, sl]) for sl in hsl]
    probs, denoms = [], []
    for s in scores:
        p = jnp.exp2(s - jnp.max(s, axis=-1, keepdims=True))
        denoms.append(jnp.sum(p, axis=-1, keepdims=True))
        probs.append(p.astype(bf16))
    heads = [(_dot(p, v_ref[0, :, sl]) / l).astype(bf16) for p, l, sl in zip(probs, denoms, hsl)]
    o = jnp.concatenate(heads, axis=-1)
    x2 = x1 + _dot(o, wo_ref[...])
    x2_ref[0] = x2

    hf = _rms(x2, gf_ref[...])
    hf_hi = hf.astype(bf16)
    hf_ref[0] = hf_hi
    hf_lo = (hf - hf_hi.astype(f32)).astype(bf16)
    hi_prod = _dot(hf_hi, wr_ref[...])
    logits = (hi_prod[:, :LANES] + hi_prod[:, LANES:] + _dot(hf_lo, wr_ref[:, :LANES])
              + br_ref[...])
    lt = logits.T[:N_EXPERTS]
    eidx = lax.broadcasted_iota(jnp.int32, lt.shape, 0)
    vals, idxs = [], []
    for _ in range(TOP_K):
        m = jnp.max(lt, axis=0, keepdims=True)
        i = jnp.min(jnp.where(lt == m, eidx, N_EXPERTS), axis=0, keepdims=True)
        lt = jnp.where(eidx == i, -jnp.inf, lt)
        vals.append(m)
        idxs.append(i)
    es = [jnp.exp(v - vals[0]) for v in vals]
    tot = es[0] + es[1] + es[2] + es[3]
    packed = jnp.concatenate([i.astype(f32) for i in idxs] + [e / tot for e in es]
                             + [jnp.zeros((LANES - 2 * TOP_K, lt.shape[1]), f32)], axis=0)
    route_ref[0] = packed.T
    sel = jnp.zeros(lt.shape, f32)
    for i in idxs:
        sel = jnp.where(eidx == i, 1.0, sel)
    lane = lax.broadcasted_iota(jnp.int32, (N_EXPERTS, LANES), 1)
    cnt = jnp.zeros((N_EXPERTS, LANES), f32)
    for j in range(lt.shape[1] // GROUP):
        cnt = jnp.where(lane == j, jnp.sum(sel[:, j * GROUP:(j + 1) * GROUP], axis=1, keepdims=True), cnt)
    cnt_ref[0] = cnt


def _xattn_router(x1, gx, w_xq, qg, kx, vx, w_xo, gf, wr, br):
    B, S, D = x1.shape
    ts = SEQ_TILE
    full = lambda shape: pl.BlockSpec(shape, lambda b, j: (0,) * len(shape))
    tile = lambda w: pl.BlockSpec((1, ts, w), lambda b, j: (b, j, 0))
    mem = pl.BlockSpec((1, N_MEM, D_X), lambda b, j: (b, 0, 0))
    return pl.pallas_call(
        _xattn_router_kernel,
        grid=(B, S // ts),
        in_specs=[tile(D), full((1, D)), full(w_xq.shape), full((1, X_HEAD_DIM)), mem, mem,
                  full(w_xo.shape), full((1, D)), full(wr.shape),
                  full((1, LANES))],
        out_specs=[tile(D), tile(D), tile(LANES),
                   pl.BlockSpec((1, N_EXPERTS, LANES), lambda b, j: (b * (S // ts) + j, 0, 0))],
        out_shape=[jax.ShapeDtypeStruct((B, S, D), f32), jax.ShapeDtypeStruct((B, S, D), bf16),
                   jax.ShapeDtypeStruct((B, S, LANES), f32),
                   jax.ShapeDtypeStruct((B * S // ts, N_EXPERTS, LANES), f32)],
        compiler_params=pltpu.CompilerParams(
            dimension_semantics=("parallel", "parallel"), vmem_limit_bytes=VMEM_LIMIT),
        name="xattn_router",
    )(x1, gx, w_xq, qg, kx, vx, w_xo, gf, wr, br)


u32 = jnp.uint32
HIGH_HALF = 0xFFFF0000


def _pack_pairs(x):
    w = x.shape[1] // 2
    lo = pltpu.bitcast(x[:, :w], u32) >> 16
    hi = pltpu.bitcast(x[:, w:], u32) & u32(HIGH_HALF)
    return lo | hi


def _unpack_pairs(u):
    lo = pltpu.bitcast(u << 16, f32)
    hi = pltpu.bitcast(u & u32(HIGH_HALF), f32)
    return jnp.concatenate([lo, hi], axis=-1).astype(bf16)


def _for_cells(g, fn):
    def body(j, carry):
        for u in range(CELL_UNROLL):
            fn(g * N_EXPERTS + j * CELL_UNROLL + u)
        return carry
    lax.fori_loop(0, N_EXPERTS // CELL_UNROLL, body, 0)


def _group_positions(route, coffv, ltri):
    lane = lax.broadcasted_iota(jnp.int32, route.shape, 1).astype(f32)
    hots = [lane == route[:, kk:kk + 1] for kk in range(TOP_K)]
    multi = jnp.zeros(route.shape, f32)
    for h in hots:
        multi = jnp.where(h, 1.0, multi)
    rank = _dot(ltri, multi.astype(bf16))
    posmat = coffv + rank
    return [jnp.sum(jnp.where(h, posmat, 0.0), axis=-1, keepdims=True) for h in hots]


def _dispatch_kernel(coff_s, rows_s, gbase_s, gtot_s, toff_s, tn_s, nu_s,
                     hf_ref, route_ref, coffv_ref, ltri_ref,
                     xs_hbm, pos_ref, xbuf, zbuf, sem, zsem, *, n_steps):
    step = pl.program_id(0)
    places = range(STEP_GROUPS)
    group = lambda i: step * STEP_GROUPS + i
    slot = lambda i: lax.rem(step, 2) * STEP_GROUPS + i
    tok = lambda i: slice(i * GROUP, (i + 1) * GROUP)

    def cell_copy(sl, i):
        n = pl.multiple_of(rows_s[i], SUBLANES)
        src = pl.multiple_of(coff_s[i], SUBLANES)
        dst = pl.multiple_of(gbase_s[i], SUBLANES)
        return n, pltpu.make_async_copy(xbuf.at[sl, pl.ds(src, n)], xs_hbm.at[pl.ds(dst, n)],
                                        sem.at[sl])

    def start_cell(sl, i):
        n, cp = cell_copy(sl, i)
        pl.when(n > 0)(cp.start)

    def wait_group(sl, gg):
        n = pl.multiple_of(gtot_s[gg], SUBLANES)
        pltpu.make_async_copy(xbuf.at[sl, pl.ds(0, n)], xs_hbm.at[pl.ds(0, n)], sem.at[sl]).wait()

    @pl.when(step >= 2)
    def _():
        for i in places:
            wait_group(slot(i), group(i) - 2 * STEP_GROUPS)

    pos = [_group_positions(route_ref[tok(i), :], coffv_ref[i], ltri_ref[...]) for i in places]
    lane = lax.broadcasted_iota(jnp.int32, (GROUP, LANES), 1)
    riota = lax.broadcasted_iota(jnp.int32, (GROUP, GROUP_ROWS), 1)
    pts = []
    for i in places:
        pt = jnp.zeros((GROUP, GROUP_ROWS), f32)
        pos_out = jnp.zeros((GROUP, LANES), f32)
        for kk in range(TOP_K):
            pt = jnp.where(riota == pos[i][kk].astype(jnp.int32), 1.0, pt)
            pos_out = jnp.where(lane == kk, pos[i][kk], pos_out)
        pos_ref[tok(i), :] = pos_out
        pts.append(pt.astype(bf16))
    rows = [lax.dot_general(pts[i], hf_ref[tok(i), :], (((0,), (0,)), ((), ())),
                            preferred_element_type=f32) for i in places]
    for i in places:
        xbuf[slot(i)] = _pack_pairs(rows[i])
    for i in places:
        _for_cells(group(i), functools.partial(start_cell, slot(i)))

    @pl.when(step == n_steps - 1)
    def _():
        if n_steps >= 2:
            for i in places:
                prev_slot = (1 - lax.rem(step, 2)) * STEP_GROUPS + i
                wait_group(prev_slot, group(i) - STEP_GROUPS)
        for i in places:
            wait_group(slot(i), group(i))
        zbuf[...] = jnp.zeros(zbuf.shape, u32)

        def tail_copy(e):
            n = pl.multiple_of(tn_s[e], SUBLANES)
            dst = pl.multiple_of(toff_s[e], SUBLANES)
            return n, pltpu.make_async_copy(zbuf.at[pl.ds(0, n)], xs_hbm.at[pl.ds(dst, n)], zsem)

        def unused_copy(b):
            dst = pl.multiple_of(b * EXPERT_TILE, EXPERT_TILE)
            return pltpu.make_async_copy(zbuf, xs_hbm.at[pl.ds(dst, EXPERT_TILE)], zsem)

        def run(method):
            def tail(e, carry):
                n, cp = tail_copy(e)
                pl.when(n > 0)(getattr(cp, method))
                return carry
            lax.fori_loop(0, N_EXPERTS, tail, 0)

            def unused(b, carry):
                getattr(unused_copy(b), method)()
                return carry
            lax.fori_loop(nu_s[0], xs_hbm.shape[0] // EXPERT_TILE, unused, 0)
        run("start")
        run("wait")


def _dispatch(plan, hf, route, n_rows):
    n_tok, D = hf.shape
    n_steps = n_tok // (GROUP * STEP_GROUPS)
    assert n_steps * GROUP * STEP_GROUPS == n_tok
    ltri = jnp.asarray(np.tril(np.ones((GROUP, GROUP)), -1), bf16)
    tok = lambda w: pl.BlockSpec((STEP_GROUPS * GROUP, w), lambda s, *_: (s, 0))
    grid_spec = pltpu.PrefetchScalarGridSpec(
        num_scalar_prefetch=7,
        grid=(n_steps,),
        in_specs=[tok(D), tok(LANES),
                  pl.BlockSpec((STEP_GROUPS, 1, LANES), lambda s, *_: (s, 0, 0)),
                  pl.BlockSpec((GROUP, GROUP), lambda s, *_: (0, 0))],
        out_specs=[pl.BlockSpec(memory_space=pl.ANY), tok(LANES)],
        scratch_shapes=[pltpu.VMEM((2 * STEP_GROUPS, GROUP_ROWS, D // 2), u32),
                        pltpu.VMEM((EXPERT_TILE, D // 2), u32),
                        pltpu.SemaphoreType.DMA((2 * STEP_GROUPS,)), pltpu.SemaphoreType.DMA(())],
    )
    return pl.pallas_call(
        functools.partial(_dispatch_kernel, n_steps=n_steps),
        grid_spec=grid_spec,
        out_shape=[jax.ShapeDtypeStruct((n_rows, D // 2), u32),
                   jax.ShapeDtypeStruct((n_tok, LANES), f32)],
        compiler_params=pltpu.CompilerParams(
            dimension_semantics=("arbitrary",), vmem_limit_bytes=VMEM_LIMIT),
        name="dispatch",
    )(plan["coff"], plan["rows"], plan["gbase"], plan["gtot"], plan["tailoff"], plan["tailn"],
      plan["n_used"], hf, route, plan["coffv"], ltri)


def _expert_kernel(eb_ref, nb_ref, nu_ref, wgu_ref, bgu_ref, wdn_ref, bdn_ref, pm_ref, xs_hbm,
                   y_hbm, wgu_b, wdn_b, xin, yout, in_sem, out_sem):
    e = pl.program_id(0)
    tm = EXPERT_TILE
    chunk_rows = EXPERT_CHUNK * tm
    n_used = nu_ref[0]
    n_chunks = (n_used + EXPERT_CHUNK - 1) // EXPERT_CHUNK

    def in_copy(c):
        sl = lax.rem(c, 2)
        rows = pl.ds(pl.multiple_of(c * chunk_rows, chunk_rows), chunk_rows)
        return pltpu.make_async_copy(xs_hbm.at[rows], xin.at[sl], in_sem.at[sl])

    def out_copy(c):
        sl = lax.rem(c, 2)
        rows = pl.ds(pl.multiple_of(c * chunk_rows, chunk_rows), chunk_rows)
        return pltpu.make_async_copy(yout.at[sl], y_hbm.at[rows], out_sem.at[sl])

    @pl.when(e == 0)
    def _():
        in_copy(0).start()
        yout[...] = jnp.zeros(yout.shape, u32)

    @pl.when(nb_ref[e] > 0)
    def _():
        for c in range(2 * D_FF // GU_TILE):
            cols = slice(c * GU_TILE, (c + 1) * GU_TILE)
            wgu_b[:, cols] = _dot(wgu_ref[0, :, cols].astype(bf16), pm_ref[...]).astype(bf16)
        for c in range(D_FF // LANES):
            rows = slice(c * LANES, (c + 1) * LANES)
            wdn_b[rows, :] = wdn_ref[0, rows, :].astype(bf16)

        def item(b, n_blk):
            c = b // EXPERT_CHUNK
            j = lax.rem(b, EXPERT_CHUNK)
            sl = lax.rem(c, 2)

            @pl.when(j == 0)
            def _():
                in_copy(c).wait()

                @pl.when(c + 1 < n_chunks)
                def _():
                    in_copy(c + 1).start()

                @pl.when(c >= 2)
                def _():
                    out_copy(c - 2).wait()

            rows = pl.ds(pl.multiple_of(j * tm, tm), n_blk * tm)
            _expert_block(xin.at[sl, rows], bgu_ref, bdn_ref, yout.at[sl, rows], wgu_b, wdn_b)
            last = b + n_blk - 1

            @pl.when((lax.rem(last, EXPERT_CHUNK) == EXPERT_CHUNK - 1) | (last == n_used - 1))
            def _():
                out_copy(c).start()

        first = eb_ref[e]
        n_head = jnp.minimum(lax.rem(EXPERT_CHUNK - lax.rem(first, EXPERT_CHUNK), EXPERT_CHUNK),
                             nb_ref[e])
        n_whole = (nb_ref[e] - n_head) // EXPERT_CHUNK
        n_tail = nb_ref[e] - n_head - n_whole * EXPERT_CHUNK

        def single(b, carry):
            item(b, 1)
            return carry

        def whole(i, carry):
            item(first + n_head + i * EXPERT_CHUNK, EXPERT_CHUNK)
            return carry
        lax.fori_loop(first, first + n_head, single, 0)
        lax.fori_loop(0, n_whole, whole, 0)
        tail0 = first + n_head + n_whole * EXPERT_CHUNK
        lax.fori_loop(tail0, tail0 + n_tail, single, 0)

    @pl.when(e == pl.num_programs(0) - 1)
    def _():
        @pl.when(n_chunks >= 2)
        def _():
            out_copy(n_chunks - 2).wait()
        out_copy(n_chunks - 1).wait()
        yout[0] = jnp.zeros(yout.shape[1:], u32)

        def run(method):
            def unused(c, carry):
                rows = pl.ds(pl.multiple_of(c * chunk_rows, chunk_rows), chunk_rows)
                getattr(pltpu.make_async_copy(yout.at[0], y_hbm.at[rows], out_sem.at[0]), method)()
                return carry
            lax.fori_loop(n_chunks, y_hbm.shape[0] // chunk_rows, unused, 0)
        run("start")
        run("wait")


def _expert_block(x_ref, bgu_ref, bdn_ref, y_ref, wgu_b, wdn_b):
    gu = _dot(_unpack_pairs(x_ref[...]), wgu_b[...]) + bgu_ref[0]
    acts = []
    for c in range(2 * D_FF // GU_TILE):
        g = jnp.minimum(gu[:, c * GU_TILE:c * GU_TILE + LANES], SWIGLU_LIMIT)
        lin = jnp.clip(gu[:, c * GU_TILE + LANES:(c + 1) * GU_TILE], -SWIGLU_LIMIT, SWIGLU_LIMIT)
        acts.append((g * jax.nn.sigmoid(SWIGLU_ALPHA * g) * (lin + 1.0)).astype(bf16))
    act = jnp.concatenate(acts, axis=-1)
    y = _dot(act, wdn_b[...]) + bdn_ref[0]
    y_ref[...] = _pack_pairs(y.astype(bf16).astype(f32))


def _gate_up_perm():
    j = np.arange(GU_TILE)
    p = np.zeros((GU_TILE, GU_TILE), np.float32)
    p[np.where(j < LANES, 2 * j, 2 * (j - LANES) + 1), j] = 1.0
    return jnp.asarray(p, bf16)


def _expert_mlp(plan, xs, w_gu, b_gu, w_dn, b_dn):
    n_rows = xs.shape[0]
    D = D_MODEL
    tm = EXPERT_TILE
    per_e = lambda e, *_: (e, 0, 0)
    grid_spec = pltpu.PrefetchScalarGridSpec(
        num_scalar_prefetch=3,
        grid=(N_EXPERTS,),
        in_specs=[pl.BlockSpec((1, D, 2 * D_FF), per_e),
                  pl.BlockSpec((1, 1, 2 * D_FF), per_e),
                  pl.BlockSpec((1, D_FF, D), per_e),
                  pl.BlockSpec((1, 1, D), per_e),
                  pl.BlockSpec((GU_TILE, GU_TILE), lambda e, *_: (0, 0)),
                  pl.BlockSpec(memory_space=pl.ANY)],
        out_specs=pl.BlockSpec(memory_space=pl.ANY),
        scratch_shapes=[pltpu.VMEM((D, 2 * D_FF), bf16), pltpu.VMEM((D_FF, D), bf16),
                        pltpu.VMEM((2, EXPERT_CHUNK * tm, D // 2), u32),
                        pltpu.VMEM((2, EXPERT_CHUNK * tm, D // 2), u32),
                        pltpu.SemaphoreType.DMA((2,)), pltpu.SemaphoreType.DMA((2,))],
    )
    return pl.pallas_call(
        _expert_kernel,
        grid_spec=grid_spec,
        out_shape=jax.ShapeDtypeStruct((n_rows, D // 2), u32),
        compiler_params=pltpu.CompilerParams(
            dimension_semantics=("arbitrary",), vmem_limit_bytes=VMEM_LIMIT),
        name="expert_mlp",
    )(plan["eblk"], plan["nblk"], plan["n_used"], w_gu, b_gu, w_dn, b_dn, _gate_up_perm(), xs)


def _combine_kernel(coff_s, rows_s, gbase_s, gtot_s, x_ref, pos_ref, route_ref, y_hbm, o_ref, ybuf, sem,
                    *, n_steps):
    step = pl.program_id(0)
    places = range(STEP_GROUPS)
    group = lambda s, i: s * STEP_GROUPS + i
    slot = lambda s, i: lax.rem(s, 2) * STEP_GROUPS + i
    tok = lambda i: slice(i * GROUP, (i + 1) * GROUP)

    def cell_copy(sl, i):
        n = pl.multiple_of(rows_s[i], SUBLANES)
        loc = pl.multiple_of(coff_s[i], SUBLANES)
        src = pl.multiple_of(gbase_s[i], SUBLANES)
        return n, pltpu.make_async_copy(y_hbm.at[pl.ds(src, n)], ybuf.at[sl, pl.ds(loc, n)],
                                        sem.at[sl])

    def start_cell(sl, i):
        n, cp = cell_copy(sl, i)
        pl.when(n > 0)(cp.start)

    def wait_group(sl, gg):
        n = pl.multiple_of(gtot_s[gg], SUBLANES)
        pltpu.make_async_copy(y_hbm.at[pl.ds(0, n)], ybuf.at[sl, pl.ds(0, n)], sem.at[sl]).wait()

    @pl.when(step == 0)
    def _():
        ybuf[...] = jnp.zeros(ybuf.shape, u32)
        for i in places:
            _for_cells(group(step, i), functools.partial(start_cell, slot(step, i)))

    @pl.when(step + 1 < n_steps)
    def _():
        for i in places:
            _for_cells(group(step + 1, i), functools.partial(start_cell, slot(step + 1, i)))

    for i in places:
        wait_group(slot(step, i), group(step, i))

    riota = lax.broadcasted_iota(jnp.int32, (GROUP, GROUP_ROWS), 1)
    pts = []
    for i in places:
        pt = jnp.zeros((GROUP, GROUP_ROWS), f32)
        for kk in range(TOP_K):
            pk = pos_ref[tok(i), kk:kk + 1].astype(jnp.int32)
            pt = jnp.where(riota == pk, route_ref[tok(i), TOP_K + kk:TOP_K + kk + 1], pt)
        pts.append(pt.astype(bf16))
    ys = [_unpack_pairs(ybuf[slot(step, i)]) for i in places]
    outs = [_dot(pts[i], ys[i]) for i in places]
    for i in places:
        o_ref[tok(i), :] = x_ref[tok(i), :] + outs[i]


def _combine(plan, x2, pos, route, y_rows):
    n_tok, D = x2.shape
    n_steps = n_tok // (GROUP * STEP_GROUPS)
    assert n_steps * GROUP * STEP_GROUPS == n_tok
    tok = lambda w: pl.BlockSpec((STEP_GROUPS * GROUP, w), lambda s, *_: (s, 0))
    grid_spec = pltpu.PrefetchScalarGridSpec(
        num_scalar_prefetch=4,
        grid=(n_steps,),
        in_specs=[tok(D), tok(LANES), tok(LANES), pl.BlockSpec(memory_space=pl.ANY)],
        out_specs=tok(D),
        scratch_shapes=[pltpu.VMEM((2 * STEP_GROUPS, GROUP_ROWS, D // 2), u32),
                        pltpu.SemaphoreType.DMA((2 * STEP_GROUPS,))],
    )
    return pl.pallas_call(
        functools.partial(_combine_kernel, n_steps=n_steps),
        grid_spec=grid_spec,
        out_shape=jax.ShapeDtypeStruct((n_tok, D), f32),
        compiler_params=pltpu.CompilerParams(
            dimension_semantics=("arbitrary",), vmem_limit_bytes=VMEM_LIMIT),
        name="combine",
    )(plan["coff"], plan["rows"], plan["gbase"], plan["gtot"], x2, pos, route, y_rows)


def _routing_plan(cnt, n_tok):
    tm = EXPERT_TILE
    n_groups = n_tok // GROUP
    per_tile = SEQ_TILE // GROUP
    cnt = cnt[:, :, :per_tile].transpose(0, 2, 1).reshape(n_groups, N_EXPERTS).astype(jnp.int32)
    rows = (cnt + SUBLANES - 1) // SUBLANES * SUBLANES
    coff = jnp.cumsum(rows, axis=1) - rows
    tot = jnp.sum(rows, axis=0)
    padded = (tot + tm - 1) // tm * tm
    pend = jnp.cumsum(padded)
    ebase = pend - padded
    gbase = ebase[None, :] + jnp.cumsum(rows, axis=0) - rows
    flat = lambda a: a.reshape(-1).astype(jnp.int32)
    return dict(
        coff=flat(coff), rows=flat(rows), gbase=flat(gbase), gtot=flat(jnp.sum(rows, axis=1)),
        tailoff=flat(ebase + tot), tailn=flat(padded - tot),
        eblk=flat(ebase // tm), nblk=flat(padded // tm), n_used=flat(pend[-1] // tm),
        coffv=jnp.zeros((n_groups, 1, LANES), f32).at[:, 0, :N_EXPERTS].set(coff.astype(f32)))


def _max_rows(n_tok):
    tm = EXPERT_TILE
    worst = (n_tok * TOP_K + (n_tok // GROUP) * N_EXPERTS * (SUBLANES - 1)
             + N_EXPERTS * (tm - SUBLANES))
    chunk = EXPERT_CHUNK * tm
    return (worst + chunk - 1) // chunk * chunk


def _moe(x2, hf, route, cnt, w_gate_up, b_gate_up, w_down, b_down):
    n_tok, D = x2.shape
    plan = _routing_plan(cnt, n_tok)
    xs, pos = _dispatch(plan, hf, route, _max_rows(n_tok))
    b_gu = (b_gate_up.reshape(N_EXPERTS, 2 * D_FF // GU_TILE, LANES, 2).transpose(0, 1, 3, 2)
            .reshape(N_EXPERTS, 1, 2 * D_FF))
    y_rows = _expert_mlp(plan, xs, w_gate_up, b_gu, w_down, b_down.reshape(N_EXPERTS, 1, D))
    return _combine(plan, x2, pos, route, y_rows)


BIAS_ROW = (2 * Q_TILE + LEFT_CHUNKS * CHUNK + LANES - 2) // LANES * LANES


def _band_bias_kernel(r_ref, o_ref):
    qc = lax.broadcasted_iota(jnp.int32, (Q_TILE, K_BAND), 0) // CHUNK
    kc = lax.broadcasted_iota(jnp.int32, (Q_TILE, K_BAND), 1) // CHUNK
    first_chunk = LEFT_CHUNKS - pl.program_id(0) * (Q_TILE // CHUNK)
    visible = (kc >= qc) & (kc <= qc + LEFT_CHUNKS) & (kc >= first_chunk)
    for h in range(r_ref.shape[0]):
        x = jnp.broadcast_to(r_ref[h], (Q_TILE, BIAS_ROW))
        x = pltpu.roll(x, BIAS_ROW - (Q_TILE - 1), axis=1, stride=1, stride_axis=0)
        o_ref[0, h] = jnp.where(visible, x[:, :K_BAND] * LOG2E, NEG_INF)


def _band_bias(rel_bias):
    H = rel_bias.shape[0]
    d_max = Q_TILE - 1 + LEFT_CHUNKS * CHUNK
    n_far = d_max - MAX_REL + 1
    r = jnp.concatenate(
        [jnp.broadcast_to(rel_bias[:, 2 * MAX_REL:], (H, n_far)),
         rel_bias[:, MAX_REL - (Q_TILE - 1):2 * MAX_REL][:, ::-1],
         jnp.zeros((H, BIAS_ROW - (2 * Q_TILE - 1 + LEFT_CHUNKS * CHUNK)), rel_bias.dtype)], axis=1)
    return pl.pallas_call(
        _band_bias_kernel,
        grid=(START_TILES + 1,),
        in_specs=[pl.BlockSpec((H, 1, BIAS_ROW), lambda t: (0, 0, 0))],
        out_specs=pl.BlockSpec((1, H, Q_TILE, K_BAND), lambda t: (t, 0, 0, 0)),
        out_shape=jax.ShapeDtypeStruct((START_TILES + 1, H, Q_TILE, K_BAND), f32),
        name="band_bias",
    )(r.astype(f32).reshape(H, 1, BIAS_ROW))


def _layer(x, mem, norm_mix_g, w_in, q_norm_g, k_norm_g, rel_bias, conv_w, out_norm_a_g,
           out_norm_b_g, w_out, norm_xattn_g, norm_mem_g, w_xq, w_xkv, xq_norm_g, xk_norm_g,
           w_xo, norm_ffn_g, w_router, b_router, w_gate_up, b_gate_up, w_down, b_down):
    B, S, D = x.shape
    n_tok = B * S
    row = lambda a: a.reshape(1, -1).astype(f32)

    qg = row(jnp.tile(q_norm_g, ATT_HEADS) * (ATT_HEAD_DIM ** -0.5 * LOG2E))
    kg = row(jnp.tile(k_norm_g, ATT_HEADS))
    head_ones = jnp.asarray(np.kron(np.eye(ATT_HEADS), np.ones((ATT_HEAD_DIM, ATT_HEAD_DIM))), bf16)
    q, k, v, yb = _mix_in(x, row(norm_mix_g), w_in.astype(bf16), qg, kg, head_ones,
                          conv_w.astype(f32), row(out_norm_b_g))
    bias = _band_bias(rel_bias).reshape(START_TILES + 1, ATT_HEADS // 2, 2 * Q_TILE, K_BAND)
    x1 = _attn_out(q, k, v, bias, yb, x, w_out.astype(bf16), row(out_norm_a_g))

    kx, vx = _mem_kv(mem, row(norm_mem_g), w_xkv.astype(bf16), row(xk_norm_g))
    wr = jnp.zeros((D, LANES), f32).at[:, :N_EXPERTS].set(w_router)
    wr_hi = wr.astype(bf16)
    wr_lo = (wr - wr_hi.astype(f32)).astype(bf16)
    br = jnp.full((1, LANES), -jnp.inf, f32).at[0, :N_EXPERTS].set(b_router)
    x2, hf, route, cnt = _xattn_router(
        x1, row(norm_xattn_g), w_xq.astype(bf16), row(xq_norm_g * (X_HEAD_DIM ** -0.5 * LOG2E)),
        kx, vx, w_xo.astype(bf16), row(norm_ffn_g), jnp.concatenate([wr_hi, wr_lo], axis=1), br)

    out = _moe(x2.reshape(n_tok, D), hf.reshape(n_tok, D), route.reshape(n_tok, LANES), cnt,
               w_gate_up, b_gate_up, w_down, b_down)
    return out.reshape(B, S, D)


def kernel(x, mem, norm_mix_g, w_in, q_norm_g, k_norm_g, rel_bias, conv_w, out_norm_a_g,
           out_norm_b_g, w_out, norm_xattn_g, norm_mem_g, w_xq, w_xkv, xq_norm_g, xk_norm_g,
           w_xo, norm_ffn_g, w_router, b_router, w_gate_up, b_gate_up, w_down, b_down):
    depth = norm_mix_g.shape[0]
    for l in range(depth):
        x = _layer(x, mem, norm_mix_g[l], w_in[l], q_norm_g[l], k_norm_g[l], rel_bias[l],
                   conv_w[l], out_norm_a_g[l], out_norm_b_g[l], w_out[l], norm_xattn_g[l],
                   norm_mem_g[l], w_xq[l], w_xkv[l], xq_norm_g[l], xk_norm_g[l], w_xo[l],
                   norm_ffn_g[l], w_router[l], b_router[l], w_gate_up[l], b_gate_up[l],
                   w_down[l], b_down[l])
    return x
```

```python
import functools

import jax
import jax.numpy as jnp
import numpy as np
from jax import lax
from jax.experimental import pallas as pl
from jax.experimental.pallas import tpu as pltpu

D_MODEL = 1024
CHUNK = 64
LEFT_CHUNKS = 8
N_MEM = 256
ATT_HEADS = 8
ATT_HEAD_DIM = 64
D_ATT = ATT_HEADS * ATT_HEAD_DIM
D_CONV = 512
MAX_REL = 256
X_HEADS = 4
X_HEAD_DIM = 128
D_X = X_HEADS * X_HEAD_DIM
N_EXPERTS = 32
TOP_K = 4
D_FF = D_MODEL
SWIGLU_LIMIT = 7.0
SWIGLU_ALPHA = 1.702
EPS = 1e-6
NEG_INF = -1e30

LANES = 128
SEQ_TILE = 1024
Q_TILE = 256
K_BAND = Q_TILE + LEFT_CHUNKS * CHUNK
START_TILES = LEFT_CHUNKS * CHUNK // Q_TILE
LOG2E = 1.4426950408889634
HEAD_ROUND = 4
EXPERT_TILE = 512
EXPERT_CHUNK = 2
GU_TILE = 2 * LANES
GROUP = 256
CELL_UNROLL = 4
STEP_GROUPS = 2
SUBLANES = 8
GROUP_ROWS = 1280
VMEM_LIMIT = 56 * 1024 * 1024

bf16 = jnp.bfloat16
f32 = jnp.float32


def _rms(x, g):
    return x * lax.rsqrt(jnp.mean(x * x, axis=-1, keepdims=True) + EPS) * g


def _dot(a, b):
    return jnp.dot(a, b, preferred_element_type=f32)


def _dot_nt(a, b):
    return lax.dot_general(a, b, (((1,), (1,)), ((), ())), preferred_element_type=f32)


def _mix_in_kernel(x_ref, g_ref, w_ref, qg_ref, kg_ref, hm_ref, cw_ref, gb_ref,
                   q_ref, k_ref, v_ref, yb_ref, ub_ref):
    ts = x_ref.shape[1]

    @pl.when(pl.program_id(1) == 0)
    def _():
        ub_ref[0:8, :] = jnp.zeros((8, D_CONV), f32)

    hb = _rms(x_ref[0], g_ref[...]).astype(bf16)

    def head_norm(t, gain):
        ms = _dot((t * t).astype(bf16), hm_ref[...]) * (1.0 / ATT_HEAD_DIM)
        return t * lax.rsqrt(ms + EPS) * gain

    o = 3 * D_ATT
    bg = _dot(hb, w_ref[:, o:o + D_CONV])
    cg = _dot(hb, w_ref[:, o + D_CONV:o + 2 * D_CONV])
    xv = _dot(hb, w_ref[:, o + 2 * D_CONV:o + 3 * D_CONV])
    u = cg * xv
    ub_ref[8:8 + ts, :] = u
    conv = (cw_ref[0:1, :] * ub_ref[6:6 + ts, :] + cw_ref[1:2, :] * ub_ref[7:7 + ts, :]
            + cw_ref[2:3, :] * u)
    ub_ref[0:8, :] = u[ts - 8:ts, :]
    yb_ref[0] = _rms(bg * conv, gb_ref[...]).astype(bf16)

    q = _dot(hb, w_ref[:, 0:D_ATT])
    q_ref[0] = head_norm(q, qg_ref[...]).astype(bf16)
    k = _dot(hb, w_ref[:, D_ATT:2 * D_ATT])
    k_ref[0] = head_norm(k, kg_ref[...]).astype(bf16)
    v_ref[0] = _dot(hb, w_ref[:, 2 * D_ATT:3 * D_ATT]).astype(bf16)


def _mix_in(x, g, w_in, qg, kg, hm, cw, gb):
    B, S, D = x.shape
    ts = SEQ_TILE
    full = lambda shape: pl.BlockSpec(shape, lambda b, j: (0,) * len(shape))
    tile = lambda w: pl.BlockSpec((1, ts, w), lambda b, j: (b, j, 0))
    out = jax.ShapeDtypeStruct((B, S, D_ATT), bf16)
    return pl.pallas_call(
        _mix_in_kernel,
        grid=(B, S // ts),
        in_specs=[tile(D), full((1, D)), full(w_in.shape), full((1, D_ATT)), full((1, D_ATT)),
                  full((D_ATT, D_ATT)), full((3, D_CONV)), full((1, D_CONV))],
        out_specs=[tile(D_ATT)] * 4,
        out_shape=[out] * 4,
        scratch_shapes=[pltpu.VMEM((ts + 8, D_CONV), f32)],
        compiler_params=pltpu.CompilerParams(
            dimension_semantics=("arbitrary", "arbitrary"), vmem_limit_bytes=VMEM_LIMIT),
        name="mix_in",
    )(x, g, w_in, qg, kg, hm, cw, gb)


def _attn_out_kernel(q_ref, k_ref, v_ref, bias_ref, yb_ref, x_ref, w_ref, ga_ref,
                     o_ref, kb_ref, vb_ref):
    tq = q_ref.shape[1]
    n_kc = K_BAND // CHUNK
    c0 = pl.program_id(1) * (tq // CHUNK)
    for kc in range(n_kc):
        src = pl.multiple_of(jnp.maximum(c0 - LEFT_CHUNKS + kc, 0) * CHUNK, CHUNK)
        kb_ref[kc * CHUNK:(kc + 1) * CHUNK, :] = k_ref[0, pl.ds(src, CHUNK), :]
        vb_ref[kc * CHUNK:(kc + 1) * CHUNK, :] = v_ref[0, pl.ds(src, CHUNK), :]

    lane = lax.broadcasted_iota(jnp.int32, (1, LANES), 1)
    low = lane < ATT_HEAD_DIM

    outs = []
    for h0 in range(0, ATT_HEADS, HEAD_ROUND):
        heads = range(h0, h0 + HEAD_ROUND)
        scores, probs, denoms = {}, {}, {}
        for h in heads:
            sl = slice((h // 2) * LANES, (h // 2 + 1) * LANES)
            qp = q_ref[0, :, sl]
            keep = low if h % 2 == 0 else jnp.logical_not(low)
            scores[h] = _dot_nt(jnp.where(keep, qp, jnp.zeros_like(qp)), kb_ref[:, sl])
        for h in heads:
            s = scores[h] + bias_ref[0, h]
            p = jnp.exp2(s - jnp.max(s, axis=-1, keepdims=True))
            denoms[h] = jnp.sum(p, axis=-1, keepdims=True)
            probs[h] = p.astype(bf16)
        for h in heads:
            sl = slice((h // 2) * LANES, (h // 2 + 1) * LANES)
            outs.append(_dot(probs[h], vb_ref[:, sl]) / denoms[h])
    ya = jnp.concatenate([jnp.where(low, outs[2 * hp], outs[2 * hp + 1])
                          for hp in range(ATT_HEADS // 2)], axis=-1)
    yan = _rms(ya, ga_ref[...]).astype(bf16)
    o_ref[0] = (x_ref[0] + _dot(yan, w_ref[0:D_ATT, :]) + _dot(yb_ref[0], w_ref[D_ATT:, :]))


def _attn_out(q, k, v, bias, yb, x, w_out, ga):
    B, S, D = x.shape
    tq = Q_TILE
    full = lambda shape: pl.BlockSpec(shape, lambda b, j: (0,) * len(shape))
    tile = lambda w: pl.BlockSpec((1, tq, w), lambda b, j: (b, j, 0))
    seq = pl.BlockSpec((1, S, D_ATT), lambda b, j: (b, 0, 0))
    return pl.pallas_call(
        _attn_out_kernel,
        grid=(B, S // tq),
        in_specs=[tile(D_ATT), seq, seq,
                  pl.BlockSpec((1,) + bias.shape[1:], lambda b, j: (jnp.minimum(j, START_TILES), 0, 0, 0)),
                  tile(D_CONV), tile(D),
                  full(w_out.shape), full((1, D_ATT))],
        out_specs=tile(D),
        out_shape=jax.ShapeDtypeStruct((B, S, D), f32),
        scratch_shapes=[pltpu.VMEM((K_BAND, D_ATT), bf16), pltpu.VMEM((K_BAND, D_ATT), bf16)],
        compiler_params=pltpu.CompilerParams(
            dimension_semantics=("parallel", "parallel"), vmem_limit_bytes=VMEM_LIMIT),
        name="attn_out",
    )(q, k, v, bias, yb, x, w_out, ga)


def _mem_kv_kernel(m_ref, g_ref, w_ref, kg_ref, k_ref, v_ref):
    hm = _rms(m_ref[...], g_ref[...]).astype(bf16)
    kv = _dot(hm, w_ref[...])
    for h in range(X_HEADS):
        sl = slice(h * X_HEAD_DIM, (h + 1) * X_HEAD_DIM)
        k_ref[:, sl] = _rms(kv[:, sl], kg_ref[...]).astype(bf16)
    v_ref[...] = kv[:, D_X:].astype(bf16)


def _mem_kv(mem, g, w_xkv, kg):
    B = mem.shape[0]
    rows = B * N_MEM
    tile = min(SEQ_TILE, rows)
    full = lambda shape: pl.BlockSpec(shape, lambda i: (0,) * len(shape))
    out = jax.ShapeDtypeStruct((rows, D_X), bf16)
    kx, vx = pl.pallas_call(
        _mem_kv_kernel,
        grid=(rows // tile,),
        in_specs=[pl.BlockSpec((tile, D_MODEL), lambda i: (i, 0)), full((1, D_MODEL)),
                  full(w_xkv.shape), full((1, X_HEAD_DIM))],
        out_specs=[pl.BlockSpec((tile, D_X), lambda i: (i, 0))] * 2,
        out_shape=[out, out],
        compiler_params=pltpu.CompilerParams(
            dimension_semantics=("parallel",), vmem_limit_bytes=VMEM_LIMIT),
        name="mem_kv",
    )(mem.reshape(rows, D_MODEL), g, w_xkv, kg)
    return kx.reshape(B, N_MEM, D_X), vx.reshape(B, N_MEM, D_X)


def _xattn_router_kernel(x_ref, gx_ref, wq_ref, qg_ref, k_ref, v_ref, wo_ref, gf_ref,
                         wr_ref, br_ref, earlier_ref,
                         x2_ref, hf_ref, route_ref, cnt_ref, post_ref):
    x1 = x_ref[0]
    hb = _rms(x1, gx_ref[...]).astype(bf16)
    q = _dot(hb, wq_ref[...])
    hsl = [slice(h * X_HEAD_DIM, (h + 1) * X_HEAD_DIM) for h in range(X_HEADS)]
    scores = [_dot_nt(_rms(q[:, sl], qg_ref[...]).astype(bf16), k_ref[0, :, sl]) for sl in hsl]
    probs, denoms = [], []
    for s in scores:
        p = jnp.exp2(s - jnp.max(s, axis=-1, keepdims=True))
        denoms.append(jnp.sum(p, axis=-1, keepdims=True))
        probs.append(p.astype(bf16))
    heads = [(_dot(p, v_ref[0, :, sl]) / l).astype(bf16) for p, l, sl in zip(probs, denoms, hsl)]
    o = jnp.concatenate(heads, axis=-1)
    x2 = x1 + _dot(o, wo_ref[...])
    x2_ref[0] = x2

    hf = _rms(x2, gf_ref[...])
    hf_hi = hf.astype(bf16)
    hf_ref[0] = hf_hi
    hf_lo = (hf - hf_hi.astype(f32)).astype(bf16)
    hi_prod = _dot(hf_hi, wr_ref[...])
    logits = (hi_prod[:, :LANES] + hi_prod[:, LANES:] + _dot(hf_lo, wr_ref[:, :LANES])
              + br_ref[...])
    lt = logits.T[:N_EXPERTS]
    eidx = lax.broadcasted_iota(jnp.int32, lt.shape, 0)
    vals, idxs = [], []
    for _ in range(TOP_K):
        m = jnp.max(lt, axis=0, keepdims=True)
        i = jnp.min(jnp.where(lt == m, eidx, N_EXPERTS), axis=0, keepdims=True)
        lt = jnp.where(eidx == i, -jnp.inf, lt)
        vals.append(m)
        idxs.append(i)
    es = [jnp.exp(v - vals[0]) for v in vals]
    tot = es[0] + es[1] + es[2] + es[3]
    sel = jnp.zeros(lt.shape, f32)
    for i in idxs:
        sel = jnp.where(eidx == i, 1.0, sel)
    n_grp = lt.shape[1] // GROUP
    grp = lambda j: slice(j * GROUP, (j + 1) * GROUP)
    lane = lax.broadcasted_iota(jnp.int32, (N_EXPERTS, LANES), 1)
    cnt = jnp.zeros((N_EXPERTS, LANES), f32)
    for j in range(n_grp):
        cnt = jnp.where(lane == j, jnp.sum(sel[:, grp(j)], axis=1, keepdims=True), cnt)
    cnt_ref[0] = cnt
    cell_rows = jnp.floor((cnt + (SUBLANES - 1)) * (1.0 / SUBLANES)) * SUBLANES
    below = (lax.broadcasted_iota(jnp.int32, (N_EXPERTS, N_EXPERTS), 0)
             > lax.broadcasted_iota(jnp.int32, (N_EXPERTS, N_EXPERTS), 1))
    coff = _dot(jnp.where(below, 1.0, 0.0).astype(bf16), cell_rows.astype(bf16))
    posmat = jnp.concatenate(
        [coff[:, j:j + 1] + _dot(sel[:, grp(j)].astype(bf16), earlier_ref[...]) for j in range(n_grp)], axis=1)
    pos = [jnp.sum(jnp.where(eidx == i, posmat, 0.0), axis=0, keepdims=True) for i in idxs]
    post_ref[0] = jnp.concatenate(pos + [jnp.zeros((SUBLANES - TOP_K, lt.shape[1]), f32)], axis=0)
    packed = jnp.concatenate([i.astype(f32) for i in idxs] + [e / tot for e in es] + pos
                             + [jnp.zeros((LANES - 3 * TOP_K, lt.shape[1]), f32)], axis=0)
    route_ref[0] = packed.T


def _xattn_router(x1, gx, w_xq, qg, kx, vx, w_xo, gf, wr, br):
    B, S, D = x1.shape
    ts = SEQ_TILE
    full = lambda shape: pl.BlockSpec(shape, lambda b, j: (0,) * len(shape))
    tile = lambda w: pl.BlockSpec((1, ts, w), lambda b, j: (b, j, 0))
    mem = pl.BlockSpec((1, N_MEM, D_X), lambda b, j: (b, 0, 0))
    return pl.pallas_call(
        _xattn_router_kernel,
        grid=(B, S // ts),
        in_specs=[tile(D), full((1, D)), full(w_xq.shape), full((1, X_HEAD_DIM)), mem, mem,
                  full(w_xo.shape), full((1, D)), full(wr.shape),
                  full((1, LANES)), full((GROUP, GROUP))],
        out_specs=[tile(D), tile(D), tile(LANES),
                   pl.BlockSpec((1, N_EXPERTS, LANES), lambda b, j: (b * (S // ts) + j, 0, 0)),
                   pl.BlockSpec((1, SUBLANES, ts), lambda b, j: (b * (S // ts) + j, 0, 0))],
        out_shape=[jax.ShapeDtypeStruct((B, S, D), f32), jax.ShapeDtypeStruct((B, S, D), bf16),
                   jax.ShapeDtypeStruct((B, S, LANES), f32),
                   jax.ShapeDtypeStruct((B * S // ts, N_EXPERTS, LANES), f32),
                   jax.ShapeDtypeStruct((B * S // ts, SUBLANES, ts), f32)],
        compiler_params=pltpu.CompilerParams(
            dimension_semantics=("parallel", "parallel"), vmem_limit_bytes=VMEM_LIMIT),
        name="xattn_router",
    )(x1, gx, w_xq, qg, kx, vx, w_xo, gf, wr, br,
      jnp.asarray(np.triu(np.ones((GROUP, GROUP)), 1), bf16))


u32 = jnp.uint32
HIGH_HALF = 0xFFFF0000


def _pack_pairs(x):
    w = x.shape[1] // 2
    lo = pltpu.bitcast(x[:, :w], u32) >> 16
    hi = pltpu.bitcast(x[:, w:], u32) & u32(HIGH_HALF)
    return lo | hi


def _unpack_pairs(u):
    lo = pltpu.bitcast(u << 16, f32)
    hi = pltpu.bitcast(u & u32(HIGH_HALF), f32)
    return jnp.concatenate([lo, hi], axis=-1).astype(bf16)


def _for_cells(g, fn):
    def body(j, carry):
        for u in range(CELL_UNROLL):
            fn(g * N_EXPERTS + j * CELL_UNROLL + u)
        return carry
    lax.fori_loop(0, N_EXPERTS // CELL_UNROLL, body, 0)


def _dispatch_kernel(coff_s, rows_s, gbase_s, gtot_s, toff_s, tn_s, nu_s,
                     hf_ref, post_ref, xs_hbm, xbuf, zbuf, sem, zsem, *, n_steps):
    step = pl.program_id(0)
    places = range(STEP_GROUPS)
    group = lambda i: step * STEP_GROUPS + i
    slot = lambda i: lax.rem(step, 2) * STEP_GROUPS + i
    tok = lambda i: slice(i * GROUP, (i + 1) * GROUP)

    def cell_copy(sl, i):
        n = pl.multiple_of(rows_s[i], SUBLANES)
        src = pl.multiple_of(coff_s[i], SUBLANES)
        dst = pl.multiple_of(gbase_s[i], SUBLANES)
        return n, pltpu.make_async_copy(xbuf.at[sl, pl.ds(src, n)], xs_hbm.at[pl.ds(dst, n)],
                                        sem.at[sl])

    def start_cell(sl, i):
        n, cp = cell_copy(sl, i)
        pl.when(n > 0)(cp.start)

    def wait_group(sl, gg):
        n = pl.multiple_of(gtot_s[gg], SUBLANES)
        pltpu.make_async_copy(xbuf.at[sl, pl.ds(0, n)], xs_hbm.at[pl.ds(0, n)], sem.at[sl]).wait()

    @pl.when(step >= 2)
    def _():
        for i in places:
            wait_group(slot(i), group(i) - 2 * STEP_GROUPS)

    riota = lax.broadcasted_iota(jnp.int32, (GROUP_ROWS, GROUP), 0)
    ps = []
    for i in places:
        p = jnp.zeros((GROUP_ROWS, GROUP), f32)
        for kk in range(TOP_K):
            p = jnp.where(riota == post_ref[0, kk:kk + 1, tok(i)].astype(jnp.int32), 1.0, p)
        ps.append(p.astype(bf16))
    rows = [_dot(ps[i], hf_ref[tok(i), :]) for i in places]
    for i in places:
        xbuf[slot(i)] = _pack_pairs(rows[i])
    for i in places:
        _for_cells(group(i), functools.partial(start_cell, slot(i)))

    @pl.when(step == n_steps - 1)
    def _():
        if n_steps >= 2:
            for i in places:
                prev_slot = (1 - lax.rem(step, 2)) * STEP_GROUPS + i
                wait_group(prev_slot, group(i) - STEP_GROUPS)
        for i in places:
            wait_group(slot(i), group(i))
        zbuf[...] = jnp.zeros(zbuf.shape, u32)

        def tail_copy(e):
            n = pl.multiple_of(tn_s[e], SUBLANES)
            dst = pl.multiple_of(toff_s[e], SUBLANES)
            return n, pltpu.make_async_copy(zbuf.at[pl.ds(0, n)], xs_hbm.at[pl.ds(dst, n)], zsem)

        def unused_copy(b):
            dst = pl.multiple_of(b * EXPERT_TILE, EXPERT_TILE)
            return pltpu.make_async_copy(zbuf, xs_hbm.at[pl.ds(dst, EXPERT_TILE)], zsem)

        def run(method):
            def tail(e, carry):
                n, cp = tail_copy(e)
                pl.when(n > 0)(getattr(cp, method))
                return carry
            lax.fori_loop(0, N_EXPERTS, tail, 0)

            def unused(b, carry):
                getattr(unused_copy(b), method)()
                return carry
            lax.fori_loop(nu_s[0], xs_hbm.shape[0] // EXPERT_TILE, unused, 0)
        run("start")
        run("wait")


def _dispatch(plan, hf, post, n_rows):
    n_tok, D = hf.shape
    n_steps = n_tok // (GROUP * STEP_GROUPS)
    assert n_steps * GROUP * STEP_GROUPS == n_tok
    step_tok = STEP_GROUPS * GROUP
    per_tile = SEQ_TILE // step_tok
    grid_spec = pltpu.PrefetchScalarGridSpec(
        num_scalar_prefetch=7,
        grid=(n_steps,),
        in_specs=[pl.BlockSpec((step_tok, D), lambda s, *_: (s, 0)),
                  pl.BlockSpec((1, SUBLANES, step_tok), lambda s, *_: (s // per_tile, 0, s % per_tile))],
        out_specs=pl.BlockSpec(memory_space=pl.ANY),
        scratch_shapes=[pltpu.VMEM((2 * STEP_GROUPS, GROUP_ROWS, D // 2), u32),
                        pltpu.VMEM((EXPERT_TILE, D // 2), u32),
                        pltpu.SemaphoreType.DMA((2 * STEP_GROUPS,)), pltpu.SemaphoreType.DMA(())],
    )
    return pl.pallas_call(
        functools.partial(_dispatch_kernel, n_steps=n_steps),
        grid_spec=grid_spec,
        out_shape=jax.ShapeDtypeStruct((n_rows, D // 2), u32),
        compiler_params=pltpu.CompilerParams(
            dimension_semantics=("arbitrary",), vmem_limit_bytes=VMEM_LIMIT),
        name="dispatch",
    )(plan["coff"], plan["rows"], plan["gbase"], plan["gtot"], plan["tailoff"], plan["tailn"],
      plan["n_used"], hf, post)


def _expert_kernel(eb_ref, nb_ref, nu_ref, wgu_ref, bgu_ref, wdn_ref, bdn_ref, pm_ref, xs_hbm,
                   y_hbm, wgu_b, wdn_b, xin, yout, in_sem, out_sem):
    e = pl.program_id(0)
    tm = EXPERT_TILE
    chunk_rows = EXPERT_CHUNK * tm
    n_used = nu_ref[0]
    n_chunks = (n_used + EXPERT_CHUNK - 1) // EXPERT_CHUNK

    def in_copy(c):
        sl = lax.rem(c, 2)
        rows = pl.ds(pl.multiple_of(c * chunk_rows, chunk_rows), chunk_rows)
        return pltpu.make_async_copy(xs_hbm.at[rows], xin.at[sl], in_sem.at[sl])

    def out_copy(c):
        sl = lax.rem(c, 2)
        rows = pl.ds(pl.multiple_of(c * chunk_rows, chunk_rows), chunk_rows)
        return pltpu.make_async_copy(yout.at[sl], y_hbm.at[rows], out_sem.at[sl])

    @pl.when(e == 0)
    def _():
        in_copy(0).start()
        yout[...] = jnp.zeros(yout.shape, u32)

    @pl.when(nb_ref[e] > 0)
    def _():
        for c in range(2 * D_FF // GU_TILE):
            cols = slice(c * GU_TILE, (c + 1) * GU_TILE)
            wgu_b[:, cols] = _dot(wgu_ref[0, :, cols].astype(bf16), pm_ref[...]).astype(bf16)
        for c in range(D_FF // LANES):
            rows = slice(c * LANES, (c + 1) * LANES)
            wdn_b[rows, :] = wdn_ref[0, rows, :].astype(bf16)

        def item(b, n_blk):
            c = b // EXPERT_CHUNK
            j = lax.rem(b, EXPERT_CHUNK)
            sl = lax.rem(c, 2)

            @pl.when(j == 0)
            def _():
                in_copy(c).wait()

                @pl.when(c + 1 < n_chunks)
                def _():
                    in_copy(c + 1).start()

                @pl.when(c >= 2)
                def _():
                    out_copy(c - 2).wait()

            rows = pl.ds(pl.multiple_of(j * tm, tm), n_blk * tm)
            _expert_block(xin.at[sl, rows], bgu_ref, bdn_ref, yout.at[sl, rows], wgu_b, wdn_b)
            last = b + n_blk - 1

            @pl.when((lax.rem(last, EXPERT_CHUNK) == EXPERT_CHUNK - 1) | (last == n_used - 1))
            def _():
                out_copy(c).start()

        first = eb_ref[e]
        n_head = jnp.minimum(lax.rem(EXPERT_CHUNK - lax.rem(first, EXPERT_CHUNK), EXPERT_CHUNK),
                             nb_ref[e])
        n_whole = (nb_ref[e] - n_head) // EXPERT_CHUNK
        n_tail = nb_ref[e] - n_head - n_whole * EXPERT_CHUNK

        def single(b, carry):
            item(b, 1)
            return carry

        def whole(i, carry):
            item(first + n_head + i * EXPERT_CHUNK, EXPERT_CHUNK)
            return carry
        lax.fori_loop(first, first + n_head, single, 0)
        lax.fori_loop(0, n_whole, whole, 0)
        tail0 = first + n_head + n_whole * EXPERT_CHUNK
        lax.fori_loop(tail0, tail0 + n_tail, single, 0)

    @pl.when(e == pl.num_programs(0) - 1)
    def _():
        @pl.when(n_chunks >= 2)
        def _():
            out_copy(n_chunks - 2).wait()
        out_copy(n_chunks - 1).wait()
        yout[0] = jnp.zeros(yout.shape[1:], u32)

        def run(method):
            def unused(c, carry):
                rows = pl.ds(pl.multiple_of(c * chunk_rows, chunk_rows), chunk_rows)
                getattr(pltpu.make_async_copy(yout.at[0], y_hbm.at[rows], out_sem.at[0]), method)()
                return carry
            lax.fori_loop(n_chunks, y_hbm.shape[0] // chunk_rows, unused, 0)
        run("start")
        run("wait")


def _expert_block(x_ref, bgu_ref, bdn_ref, y_ref, wgu_b, wdn_b):
    gu = _dot(_unpack_pairs(x_ref[...]), wgu_b[...]) + bgu_ref[0]
    acts = []
    for c in range(2 * D_FF // GU_TILE):
        g = jnp.minimum(gu[:, c * GU_TILE:c * GU_TILE + LANES], SWIGLU_LIMIT)
        lin = jnp.clip(gu[:, c * GU_TILE + LANES:(c + 1) * GU_TILE], -SWIGLU_LIMIT, SWIGLU_LIMIT)
        acts.append((g * jax.nn.sigmoid(SWIGLU_ALPHA * g) * (lin + 1.0)).astype(bf16))
    act = jnp.concatenate(acts, axis=-1)
    y = _dot(act, wdn_b[...]) + bdn_ref[0]
    y_ref[...] = _pack_pairs(y.astype(bf16).astype(f32))


def _gate_up_perm():
    j = np.arange(GU_TILE)
    p = np.zeros((GU_TILE, GU_TILE), np.float32)
    p[np.where(j < LANES, 2 * j, 2 * (j - LANES) + 1), j] = 1.0
    return jnp.asarray(p, bf16)


def _expert_mlp(plan, xs, w_gu, b_gu, w_dn, b_dn):
    n_rows = xs.shape[0]
    D = D_MODEL
    tm = EXPERT_TILE
    per_e = lambda e, *_: (e, 0, 0)
    grid_spec = pltpu.PrefetchScalarGridSpec(
        num_scalar_prefetch=3,
        grid=(N_EXPERTS,),
        in_specs=[pl.BlockSpec((1, D, 2 * D_FF), per_e),
                  pl.BlockSpec((1, 1, 2 * D_FF), per_e),
                  pl.BlockSpec((1, D_FF, D), per_e),
                  pl.BlockSpec((1, 1, D), per_e),
                  pl.BlockSpec((GU_TILE, GU_TILE), lambda e, *_: (0, 0)),
                  pl.BlockSpec(memory_space=pl.ANY)],
        out_specs=pl.BlockSpec(memory_space=pl.ANY),
        scratch_shapes=[pltpu.VMEM((D, 2 * D_FF), bf16), pltpu.VMEM((D_FF, D), bf16),
                        pltpu.VMEM((2, EXPERT_CHUNK * tm, D // 2), u32),
                        pltpu.VMEM((2, EXPERT_CHUNK * tm, D // 2), u32),
                        pltpu.SemaphoreType.DMA((2,)), pltpu.SemaphoreType.DMA((2,))],
    )
    return pl.pallas_call(
        _expert_kernel,
        grid_spec=grid_spec,
        out_shape=jax.ShapeDtypeStruct((n_rows, D // 2), u32),
        compiler_params=pltpu.CompilerParams(
            dimension_semantics=("arbitrary",), vmem_limit_bytes=VMEM_LIMIT),
        name="expert_mlp",
    )(plan["eblk"], plan["nblk"], plan["n_used"], w_gu, b_gu, w_dn, b_dn, _gate_up_perm(), xs)


def _combine_kernel(coff_s, rows_s, gbase_s, gtot_s, x_ref, route_ref, y_hbm, o_ref, ybuf, sem,
                    *, n_steps):
    step = pl.program_id(0)
    places = range(STEP_GROUPS)
    group = lambda s, i: s * STEP_GROUPS + i
    slot = lambda s, i: lax.rem(s, 2) * STEP_GROUPS + i
    tok = lambda i: slice(i * GROUP, (i + 1) * GROUP)

    def cell_copy(sl, i):
        n = pl.multiple_of(rows_s[i], SUBLANES)
        loc = pl.multiple_of(coff_s[i], SUBLANES)
        src = pl.multiple_of(gbase_s[i], SUBLANES)
        return n, pltpu.make_async_copy(y_hbm.at[pl.ds(src, n)], ybuf.at[sl, pl.ds(loc, n)],
                                        sem.at[sl])

    def start_cell(sl, i):
        n, cp = cell_copy(sl, i)
        pl.when(n > 0)(cp.start)

    def wait_group(sl, gg):
        n = pl.multiple_of(gtot_s[gg], SUBLANES)
        pltpu.make_async_copy(y_hbm.at[pl.ds(0, n)], ybuf.at[sl, pl.ds(0, n)], sem.at[sl]).wait()

    @pl.when(step == 0)
    def _():
        ybuf[...] = jnp.zeros(ybuf.shape, u32)
        for i in places:
            _for_cells(group(step, i), functools.partial(start_cell, slot(step, i)))

    @pl.when(step + 1 < n_steps)
    def _():
        for i in places:
            _for_cells(group(step + 1, i), functools.partial(start_cell, slot(step + 1, i)))

    for i in places:
        wait_group(slot(step, i), group(step, i))

    riota = lax.broadcasted_iota(jnp.int32, (GROUP, GROUP_ROWS), 1)
    pts = []
    for i in places:
        pt = jnp.zeros((GROUP, GROUP_ROWS), f32)
        for kk in range(TOP_K):
            pk = route_ref[tok(i), 2 * TOP_K + kk:2 * TOP_K + kk + 1].astype(jnp.int32)
            pt = jnp.where(riota == pk, route_ref[tok(i), TOP_K + kk:TOP_K + kk + 1], pt)
        pts.append(pt.astype(bf16))
    ys = [_unpack_pairs(ybuf[slot(step, i)]) for i in places]
    outs = [_dot(pts[i], ys[i]) for i in places]
    for i in places:
        o_ref[tok(i), :] = x_ref[tok(i), :] + outs[i]


def _combine(plan, x2, route, y_rows):
    n_tok, D = x2.shape
    n_steps = n_tok // (GROUP * STEP_GROUPS)
    assert n_steps * GROUP * STEP_GROUPS == n_tok
    tok = lambda w: pl.BlockSpec((STEP_GROUPS * GROUP, w), lambda s, *_: (s, 0))
    grid_spec = pltpu.PrefetchScalarGridSpec(
        num_scalar_prefetch=4,
        grid=(n_steps,),
        in_specs=[tok(D), tok(LANES), pl.BlockSpec(memory_space=pl.ANY)],
        out_specs=tok(D),
        scratch_shapes=[pltpu.VMEM((2 * STEP_GROUPS, GROUP_ROWS, D // 2), u32),
                        pltpu.SemaphoreType.DMA((2 * STEP_GROUPS,))],
    )
    return pl.pallas_call(
        functools.partial(_combine_kernel, n_steps=n_steps),
        grid_spec=grid_spec,
        out_shape=jax.ShapeDtypeStruct((n_tok, D), f32),
        compiler_params=pltpu.CompilerParams(
            dimension_semantics=("arbitrary",), vmem_limit_bytes=VMEM_LIMIT),
        name="combine",
    )(plan["coff"], plan["rows"], plan["gbase"], plan["gtot"], x2, route, y_rows)


def _routing_plan(cnt, n_tok):
    tm = EXPERT_TILE
    n_groups = n_tok // GROUP
    per_tile = SEQ_TILE // GROUP
    cnt = cnt[:, :, :per_tile].transpose(0, 2, 1).reshape(n_groups, N_EXPERTS).astype(jnp.int32)
    rows = (cnt + SUBLANES - 1) // SUBLANES * SUBLANES
    coff = jnp.cumsum(rows, axis=1) - rows
    tot = jnp.sum(rows, axis=0)
    padded = (tot + tm - 1) // tm * tm
    pend = jnp.cumsum(padded)
    ebase = pend - padded
    gbase = ebase[None, :] + jnp.cumsum(rows, axis=0) - rows
    flat = lambda a: a.reshape(-1).astype(jnp.int32)
    return dict(
        coff=flat(coff), rows=flat(rows), gbase=flat(gbase), gtot=flat(jnp.sum(rows, axis=1)),
        tailoff=flat(ebase + tot), tailn=flat(padded - tot),
        eblk=flat(ebase // tm), nblk=flat(padded // tm), n_used=flat(pend[-1] // tm))


def _max_rows(n_tok):
    tm = EXPERT_TILE
    worst = (n_tok * TOP_K + (n_tok // GROUP) * N_EXPERTS * (SUBLANES - 1)
             + N_EXPERTS * (tm - SUBLANES))
    chunk = EXPERT_CHUNK * tm
    return (worst + chunk - 1) // chunk * chunk


def _moe(x2, hf, route, post, cnt, w_gate_up, b_gate_up, w_down, b_down):
    n_tok, D = x2.shape
    plan = _routing_plan(cnt, n_tok)
    xs = _dispatch(plan, hf, post, _max_rows(n_tok))
    b_gu = (b_gate_up.reshape(N_EXPERTS, 2 * D_FF // GU_TILE, LANES, 2).transpose(0, 1, 3, 2)
            .reshape(N_EXPERTS, 1, 2 * D_FF))
    y_rows = _expert_mlp(plan, xs, w_gate_up, b_gu, w_down, b_down.reshape(N_EXPERTS, 1, D))
    return _combine(plan, x2, route, y_rows)


BIAS_ROW = (2 * Q_TILE + LEFT_CHUNKS * CHUNK + LANES - 2) // LANES * LANES


def _band_bias_kernel(r_ref, o_ref):
    qc = lax.broadcasted_iota(jnp.int32, (Q_TILE, K_BAND), 0) // CHUNK
    kc = lax.broadcasted_iota(jnp.int32, (Q_TILE, K_BAND), 1) // CHUNK
    first_chunk = LEFT_CHUNKS - pl.program_id(0) * (Q_TILE // CHUNK)
    visible = (kc >= qc) & (kc <= qc + LEFT_CHUNKS) & (kc >= first_chunk)
    for h in range(r_ref.shape[0]):
        x = jnp.broadcast_to(r_ref[h], (Q_TILE, BIAS_ROW))
        x = pltpu.roll(x, BIAS_ROW - (Q_TILE - 1), axis=1, stride=1, stride_axis=0)
        o_ref[0, h] = jnp.where(visible, x[:, :K_BAND] * LOG2E, NEG_INF)


def _band_bias(rel_bias):
    H = rel_bias.shape[0]
    d_max = Q_TILE - 1 + LEFT_CHUNKS * CHUNK
    n_far = d_max - MAX_REL + 1
    r = jnp.concatenate(
        [jnp.broadcast_to(rel_bias[:, 2 * MAX_REL:], (H, n_far)),
         rel_bias[:, MAX_REL - (Q_TILE - 1):2 * MAX_REL][:, ::-1],
         jnp.zeros((H, BIAS_ROW - (2 * Q_TILE - 1 + LEFT_CHUNKS * CHUNK)), rel_bias.dtype)], axis=1)
    return pl.pallas_call(
        _band_bias_kernel,
        grid=(START_TILES + 1,),
        in_specs=[pl.BlockSpec((H, 1, BIAS_ROW), lambda t: (0, 0, 0))],
        out_specs=pl.BlockSpec((1, H, Q_TILE, K_BAND), lambda t: (t, 0, 0, 0)),
        out_shape=jax.ShapeDtypeStruct((START_TILES + 1, H, Q_TILE, K_BAND), f32),
        name="band_bias",
    )(r.astype(f32).reshape(H, 1, BIAS_ROW))


def _layer(x, mem, norm_mix_g, w_in, q_norm_g, k_norm_g, rel_bias, conv_w, out_norm_a_g,
           out_norm_b_g, w_out, norm_xattn_g, norm_mem_g, w_xq, w_xkv, xq_norm_g, xk_norm_g,
           w_xo, norm_ffn_g, w_router, b_router, w_gate_up, b_gate_up, w_down, b_down):
    B, S, D = x.shape
    n_tok = B * S
    row = lambda a: a.reshape(1, -1).astype(f32)

    qg = row(jnp.tile(q_norm_g, ATT_HEADS) * (ATT_HEAD_DIM ** -0.5 * LOG2E))
    kg = row(jnp.tile(k_norm_g, ATT_HEADS))
    head_ones = jnp.asarray(np.kron(np.eye(ATT_HEADS), np.ones((ATT_HEAD_DIM, ATT_HEAD_DIM))), bf16)
    q, k, v, yb = _mix_in(x, row(norm_mix_g), w_in.astype(bf16), qg, kg, head_ones,
                          conv_w.astype(f32), row(out_norm_b_g))
    x1 = _attn_out(q, k, v, _band_bias(rel_bias), yb, x, w_out.astype(bf16), row(out_norm_a_g))

    kx, vx = _mem_kv(mem, row(norm_mem_g), w_xkv.astype(bf16), row(xk_norm_g))
    wr = jnp.zeros((D, LANES), f32).at[:, :N_EXPERTS].set(w_router)
    wr_hi = wr.astype(bf16)
    wr_lo = (wr - wr_hi.astype(f32)).astype(bf16)
    br = jnp.full((1, LANES), -jnp.inf, f32).at[0, :N_EXPERTS].set(b_router)
    x2, hf, route, cnt, post = _xattn_router(
        x1, row(norm_xattn_g), w_xq.astype(bf16), row(xq_norm_g * (X_HEAD_DIM ** -0.5 * LOG2E)),
        kx, vx, w_xo.astype(bf16), row(norm_ffn_g), jnp.concatenate([wr_hi, wr_lo], axis=1), br)

    out = _moe(x2.reshape(n_tok, D), hf.reshape(n_tok, D), route.reshape(n_tok, LANES), post, cnt,
               w_gate_up, b_gate_up, w_down, b_down)
    return out.reshape(B, S, D)


def kernel(x, mem, norm_mix_g, w_in, q_norm_g, k_norm_g, rel_bias, conv_w, out_norm_a_g,
           out_norm_b_g, w_out, norm_xattn_g, norm_mem_g, w_xq, w_xkv, xq_norm_g, xk_norm_g,
           w_xo, norm_ffn_g, w_router, b_router, w_gate_up, b_gate_up, w_down, b_down):
    depth = norm_mix_g.shape[0]
    for l in range(depth):
        x = _layer(x, mem, norm_mix_g[l], w_in[l], q_norm_g[l], k_norm_g[l], rel_bias[l],
                   conv_w[l], out_norm_a_g[l], out_norm_b_g[l], w_out[l], norm_xattn_g[l],
                   norm_mem_g[l], w_xq[l], w_xkv[l], xq_norm_g[l], xk_norm_g[l], w_xo[l],
                   norm_ffn_g[l], w_router[l], b_router[l], w_gate_up[l], b_gate_up[l],
                   w_down[l], b_down[l])
    return x
```

```python
import functools

import jax
import jax.numpy as jnp
import numpy as np
from jax import lax
from jax.experimental import pallas as pl
from jax.experimental.pallas import tpu as pltpu

D_MODEL = 1024
CHUNK = 64
LEFT_CHUNKS = 8
N_MEM = 256
ATT_HEADS = 8
ATT_HEAD_DIM = 64
D_ATT = ATT_HEADS * ATT_HEAD_DIM
D_CONV = 512
MAX_REL = 256
X_HEADS = 4
X_HEAD_DIM = 128
D_X = X_HEADS * X_HEAD_DIM
N_EXPERTS = 32
TOP_K = 4
D_FF = D_MODEL
SWIGLU_LIMIT = 7.0
SWIGLU_ALPHA = 1.702
EPS = 1e-6
NEG_INF = -1e30

LANES = 128
SEQ_TILE = 1024
Q_TILE = 256
K_BAND = Q_TILE + LEFT_CHUNKS * CHUNK
START_TILES = LEFT_CHUNKS * CHUNK // Q_TILE
LOG2E = 1.4426950408889634
HEAD_ROUND = 4
EXPERT_TILE = 512
EXPERT_CHUNK = 2
GU_TILE = 2 * LANES
GROUP = 256
CELL_UNROLL = 4
STEP_GROUPS = 2
SUBLANES = 8
GROUP_ROWS = 1280
VMEM_LIMIT = 56 * 1024 * 1024

bf16 = jnp.bfloat16
f32 = jnp.float32


def _rms(x, g):
    return x * lax.rsqrt(jnp.mean(x * x, axis=-1, keepdims=True) + EPS) * g


def _dot(a, b):
    return jnp.dot(a, b, preferred_element_type=f32)


def _dot_nt(a, b):
    return lax.dot_general(a, b, (((1,), (1,)), ((), ())), preferred_element_type=f32)


def _mix_in_kernel(x_ref, g_ref, w_ref, qg_ref, kg_ref, hm_ref, cw_ref, gb_ref,
                   q_ref, k_ref, v_ref, yb_ref, ub_ref):
    ts = x_ref.shape[1]

    @pl.when(pl.program_id(1) == 0)
    def _():
        ub_ref[0:8, :] = jnp.zeros((8, D_CONV), f32)

    hb = _rms(x_ref[0], g_ref[...]).astype(bf16)

    def head_norm(t, gain):
        ms = _dot((t * t).astype(bf16), hm_ref[...]) * (1.0 / ATT_HEAD_DIM)
        return t * lax.rsqrt(ms + EPS) * gain

    o = 3 * D_ATT
    bg = _dot(hb, w_ref[:, o:o + D_CONV])
    cg = _dot(hb, w_ref[:, o + D_CONV:o + 2 * D_CONV])
    xv = _dot(hb, w_ref[:, o + 2 * D_CONV:o + 3 * D_CONV])
    u = cg * xv
    ub_ref[8:8 + ts, :] = u
    conv = (cw_ref[0:1, :] * ub_ref[6:6 + ts, :] + cw_ref[1:2, :] * ub_ref[7:7 + ts, :]
            + cw_ref[2:3, :] * u)
    ub_ref[0:8, :] = u[ts - 8:ts, :]
    yb_ref[0] = _rms(bg * conv, gb_ref[...]).astype(bf16)

    q = _dot(hb, w_ref[:, 0:D_ATT])
    q_ref[0] = head_norm(q, qg_ref[...]).astype(bf16)
    k = _dot(hb, w_ref[:, D_ATT:2 * D_ATT])
    k_ref[0] = head_norm(k, kg_ref[...]).astype(bf16)
    v_ref[0] = _dot(hb, w_ref[:, 2 * D_ATT:3 * D_ATT]).astype(bf16)


def _mix_in(x, g, w_in, qg, kg, hm, cw, gb):
    B, S, D = x.shape
    ts = SEQ_TILE
    full = lambda shape: pl.BlockSpec(shape, lambda b, j: (0,) * len(shape))
    tile = lambda w: pl.BlockSpec((1, ts, w), lambda b, j: (b, j, 0))
    out = jax.ShapeDtypeStruct((B, S, D_ATT), bf16)
    return pl.pallas_call(
        _mix_in_kernel,
        grid=(B, S // ts),
        in_specs=[tile(D), full((1, D)), full(w_in.shape), full((1, D_ATT)), full((1, D_ATT)),
                  full((D_ATT, D_ATT)), full((3, D_CONV)), full((1, D_CONV))],
        out_specs=[tile(D_ATT)] * 4,
        out_shape=[out] * 4,
        scratch_shapes=[pltpu.VMEM((ts + 8, D_CONV), f32)],
        compiler_params=pltpu.CompilerParams(
            dimension_semantics=("arbitrary", "arbitrary"), vmem_limit_bytes=VMEM_LIMIT),
        name="mix_in",
    )(x, g, w_in, qg, kg, hm, cw, gb)


def _attn_out_kernel(q_ref, k_ref, v_ref, bias_ref, ga_ref, o_ref, kb_ref, vb_ref):
    tq = q_ref.shape[1]
    n_kc = K_BAND // CHUNK
    c0 = pl.program_id(1) * (tq // CHUNK)
    for kc in range(n_kc):
        src = pl.multiple_of(jnp.maximum(c0 - LEFT_CHUNKS + kc, 0) * CHUNK, CHUNK)
        kb_ref[kc * CHUNK:(kc + 1) * CHUNK, :] = k_ref[0, pl.ds(src, CHUNK), :]
        vb_ref[kc * CHUNK:(kc + 1) * CHUNK, :] = v_ref[0, pl.ds(src, CHUNK), :]

    lane = lax.broadcasted_iota(jnp.int32, (1, LANES), 1)
    low = lane < ATT_HEAD_DIM

    outs = []
    for h0 in range(0, ATT_HEADS, HEAD_ROUND):
        heads = range(h0, h0 + HEAD_ROUND)
        scores, probs, denoms = {}, {}, {}
        for h in heads:
            sl = slice((h // 2) * LANES, (h // 2 + 1) * LANES)
            qp = q_ref[0, :, sl]
            keep = low if h % 2 == 0 else jnp.logical_not(low)
            scores[h] = _dot_nt(jnp.where(keep, qp, jnp.zeros_like(qp)), kb_ref[:, sl])
        for h in heads:
            s = scores[h] + bias_ref[0, h]
            p = jnp.exp2(s - jnp.max(s, axis=-1, keepdims=True))
            denoms[h] = jnp.sum(p, axis=-1, keepdims=True)
            probs[h] = p.astype(bf16)
        for h in heads:
            sl = slice((h // 2) * LANES, (h // 2 + 1) * LANES)
            outs.append(_dot(probs[h], vb_ref[:, sl]) / denoms[h])
    ya = jnp.concatenate([jnp.where(low, outs[2 * hp], outs[2 * hp + 1])
                          for hp in range(ATT_HEADS // 2)], axis=-1)
    o_ref[0] = _rms(ya, ga_ref[...]).astype(bf16)


def _attn_out(q, k, v, bias, ga):
    B, S, _ = q.shape
    tq = Q_TILE
    full = lambda shape: pl.BlockSpec(shape, lambda b, j: (0,) * len(shape))
    tile = lambda w: pl.BlockSpec((1, tq, w), lambda b, j: (b, j, 0))
    seq = pl.BlockSpec((1, S, D_ATT), lambda b, j: (b, 0, 0))
    return pl.pallas_call(
        _attn_out_kernel,
        grid=(B, S // tq),
        in_specs=[tile(D_ATT), seq, seq,
                  pl.BlockSpec((1,) + bias.shape[1:], lambda b, j: (jnp.minimum(j, START_TILES), 0, 0, 0)),
                  full((1, D_ATT))],
        out_specs=tile(D_ATT),
        out_shape=jax.ShapeDtypeStruct((B, S, D_ATT), bf16),
        scratch_shapes=[pltpu.VMEM((K_BAND, D_ATT), bf16), pltpu.VMEM((K_BAND, D_ATT), bf16)],
        compiler_params=pltpu.CompilerParams(
            dimension_semantics=("parallel", "parallel"), vmem_limit_bytes=VMEM_LIMIT),
        name="attn_out",
    )(q, k, v, bias, ga)


def _mem_kv_kernel(m_ref, g_ref, w_ref, kg_ref, k_ref, v_ref):
    hm = _rms(m_ref[...], g_ref[...]).astype(bf16)
    kv = _dot(hm, w_ref[...])
    for h in range(X_HEADS):
        sl = slice(h * X_HEAD_DIM, (h + 1) * X_HEAD_DIM)
        k_ref[:, sl] = _rms(kv[:, sl], kg_ref[...]).astype(bf16)
    v_ref[...] = kv[:, D_X:].astype(bf16)


def _mem_kv(mem, g, w_xkv, kg):
    B = mem.shape[0]
    rows = B * N_MEM
    tile = min(SEQ_TILE, rows)
    full = lambda shape: pl.BlockSpec(shape, lambda i: (0,) * len(shape))
    out = jax.ShapeDtypeStruct((rows, D_X), bf16)
    kx, vx = pl.pallas_call(
        _mem_kv_kernel,
        grid=(rows // tile,),
        in_specs=[pl.BlockSpec((tile, D_MODEL), lambda i: (i, 0)), full((1, D_MODEL)),
                  full(w_xkv.shape), full((1, X_HEAD_DIM))],
        out_specs=[pl.BlockSpec((tile, D_X), lambda i: (i, 0))] * 2,
        out_shape=[out, out],
        compiler_params=pltpu.CompilerParams(
            dimension_semantics=("parallel",), vmem_limit_bytes=VMEM_LIMIT),
        name="mem_kv",
    )(mem.reshape(rows, D_MODEL), g, w_xkv, kg)
    return kx.reshape(B, N_MEM, D_X), vx.reshape(B, N_MEM, D_X)


def _xattn_router_kernel(x_ref, ya_ref, yb_ref, wmix_ref, gx_ref, wq_ref, qg_ref, k_ref, v_ref, wo_ref,
                         gf_ref,
                         wr_ref, br_ref, earlier_ref,
                         x2_ref, hf_ref, route_ref, cnt_ref, post_ref):
    x1 = x_ref[0] + _dot(ya_ref[0], wmix_ref[0:D_ATT, :]) + _dot(yb_ref[0], wmix_ref[D_ATT:, :])
    hb = _rms(x1, gx_ref[...]).astype(bf16)
    q = _dot(hb, wq_ref[...])
    hsl = [slice(h * X_HEAD_DIM, (h + 1) * X_HEAD_DIM) for h in range(X_HEADS)]
    scores = [_dot_nt(_rms(q[:, sl], qg_ref[...]).astype(bf16), k_ref[0, :, sl]) for sl in hsl]
    probs, denoms = [], []
    for s in scores:
        p = jnp.exp2(s - jnp.max(s, axis=-1, keepdims=True))
        denoms.append(jnp.sum(p, axis=-1, keepdims=True))
        probs.append(p.astype(bf16))
    heads = [(_dot(p, v_ref[0, :, sl]) / l).astype(bf16) for p, l, sl in zip(probs, denoms, hsl)]
    o = jnp.concatenate(heads, axis=-1)
    x2 = x1 + _dot(o, wo_ref[...])
    x2_ref[0] = x2

    hf = _rms(x2, gf_ref[...])
    hf_hi = hf.astype(bf16)
    hf_ref[0] = hf_hi
    hf_lo = (hf - hf_hi.astype(f32)).astype(bf16)
    hi_prod = _dot(hf_hi, wr_ref[...])
    logits = (hi_prod[:, :LANES] + hi_prod[:, LANES:] + _dot(hf_lo, wr_ref[:, :LANES])
              + br_ref[...])
    lt = logits.T[:N_EXPERTS]
    eidx = lax.broadcasted_iota(jnp.int32, lt.shape, 0)
    vals, idxs = [], []
    for _ in range(TOP_K):
        m = jnp.max(lt, axis=0, keepdims=True)
        i = jnp.min(jnp.where(lt == m, eidx, N_EXPERTS), axis=0, keepdims=True)
        lt = jnp.where(eidx == i, -jnp.inf, lt)
        vals.append(m)
        idxs.append(i)
    es = [jnp.exp(v - vals[0]) for v in vals]
    tot = es[0] + es[1] + es[2] + es[3]
    sel = jnp.zeros(lt.shape, f32)
    for i in idxs:
        sel = jnp.where(eidx == i, 1.0, sel)
    n_grp = lt.shape[1] // GROUP
    grp = lambda j: slice(j * GROUP, (j + 1) * GROUP)
    lane = lax.broadcasted_iota(jnp.int32, (N_EXPERTS, LANES), 1)
    cnt = jnp.zeros((N_EXPERTS, LANES), f32)
    for j in range(n_grp):
        cnt = jnp.where(lane == j, jnp.sum(sel[:, grp(j)], axis=1, keepdims=True), cnt)
    cnt_ref[0] = cnt
    cell_rows = jnp.floor((cnt + (SUBLANES - 1)) * (1.0 / SUBLANES)) * SUBLANES
    below = (lax.broadcasted_iota(jnp.int32, (N_EXPERTS, N_EXPERTS), 0)
             > lax.broadcasted_iota(jnp.int32, (N_EXPERTS, N_EXPERTS), 1))
    coff = _dot(jnp.where(below, 1.0, 0.0).astype(bf16), cell_rows.astype(bf16))
    posmat = jnp.concatenate(
        [coff[:, j:j + 1] + _dot(sel[:, grp(j)].astype(bf16), earlier_ref[...]) for j in range(n_grp)], axis=1)
    pos = [jnp.sum(jnp.where(eidx == i, posmat, 0.0), axis=0, keepdims=True) for i in idxs]
    post_ref[0] = jnp.concatenate(pos + [jnp.zeros((SUBLANES - TOP_K, lt.shape[1]), f32)], axis=0)
    packed = jnp.concatenate([i.astype(f32) for i in idxs] + [e / tot for e in es] + pos
                             + [jnp.zeros((LANES - 3 * TOP_K, lt.shape[1]), f32)], axis=0)
    route_ref[0] = packed.T


def _xattn_router(x, ya, yb, w_mix, gx, w_xq, qg, kx, vx, w_xo, gf, wr, br):
    B, S, D = x.shape
    ts = SEQ_TILE
    full = lambda shape: pl.BlockSpec(shape, lambda b, j: (0,) * len(shape))
    tile = lambda w: pl.BlockSpec((1, ts, w), lambda b, j: (b, j, 0))
    mem = pl.BlockSpec((1, N_MEM, D_X), lambda b, j: (b, 0, 0))
    return pl.pallas_call(
        _xattn_router_kernel,
        grid=(B, S // ts),
        in_specs=[tile(D), tile(D_ATT), tile(D_CONV), full(w_mix.shape),
                  full((1, D)), full(w_xq.shape), full((1, X_HEAD_DIM)), mem, mem,
                  full(w_xo.shape), full((1, D)), full(wr.shape),
                  full((1, LANES)), full((GROUP, GROUP))],
        out_specs=[tile(D), tile(D), tile(LANES),
                   pl.BlockSpec((1, N_EXPERTS, LANES), lambda b, j: (b * (S // ts) + j, 0, 0)),
                   pl.BlockSpec((1, SUBLANES, ts), lambda b, j: (b * (S // ts) + j, 0, 0))],
        out_shape=[jax.ShapeDtypeStruct((B, S, D), f32), jax.ShapeDtypeStruct((B, S, D), bf16),
                   jax.ShapeDtypeStruct((B, S, LANES), f32),
                   jax.ShapeDtypeStruct((B * S // ts, N_EXPERTS, LANES), f32),
                   jax.ShapeDtypeStruct((B * S // ts, SUBLANES, ts), f32)],
        compiler_params=pltpu.CompilerParams(
            dimension_semantics=("parallel", "parallel"), vmem_limit_bytes=VMEM_LIMIT),
        name="xattn_router",
    )(x, ya, yb, w_mix, gx, w_xq, qg, kx, vx, w_xo, gf, wr, br,
      jnp.asarray(np.triu(np.ones((GROUP, GROUP)), 1), bf16))


u32 = jnp.uint32
HIGH_HALF = 0xFFFF0000


def _pack_pairs(x):
    w = x.shape[1] // 2
    lo = pltpu.bitcast(x[:, :w], u32) >> 16
    hi = pltpu.bitcast(x[:, w:], u32) & u32(HIGH_HALF)
    return lo | hi


def _unpack_pairs(u):
    lo = pltpu.bitcast(u << 16, f32)
    hi = pltpu.bitcast(u & u32(HIGH_HALF), f32)
    return jnp.concatenate([lo, hi], axis=-1).astype(bf16)


def _for_cells(g, fn):
    def body(j, carry):
        for u in range(CELL_UNROLL):
            fn(g * N_EXPERTS + j * CELL_UNROLL + u)
        return carry
    lax.fori_loop(0, N_EXPERTS // CELL_UNROLL, body, 0)


def _dispatch_kernel(coff_s, rows_s, gbase_s, gtot_s, toff_s, tn_s, nu_s,
                     hf_ref, post_ref, xs_hbm, xbuf, zbuf, sem, zsem, *, n_steps):
    step = pl.program_id(0)
    places = range(STEP_GROUPS)
    group = lambda i: step * STEP_GROUPS + i
    slot = lambda i: lax.rem(step, 2) * STEP_GROUPS + i
    tok = lambda i: slice(i * GROUP, (i + 1) * GROUP)

    def cell_copy(sl, i):
        n = pl.multiple_of(rows_s[i], SUBLANES)
        src = pl.multiple_of(coff_s[i], SUBLANES)
        dst = pl.multiple_of(gbase_s[i], SUBLANES)
        return n, pltpu.make_async_copy(xbuf.at[sl, pl.ds(src, n)], xs_hbm.at[pl.ds(dst, n)],
                                        sem.at[sl])

    def start_cell(sl, i):
        n, cp = cell_copy(sl, i)
        pl.when(n > 0)(cp.start)

    def wait_group(sl, gg):
        n = pl.multiple_of(gtot_s[gg], SUBLANES)
        pltpu.make_async_copy(xbuf.at[sl, pl.ds(0, n)], xs_hbm.at[pl.ds(0, n)], sem.at[sl]).wait()

    @pl.when(step >= 2)
    def _():
        for i in places:
            wait_group(slot(i), group(i) - 2 * STEP_GROUPS)

    riota = lax.broadcasted_iota(jnp.int32, (GROUP_ROWS, GROUP), 0)
    ps = []
    for i in places:
        p = jnp.zeros((GROUP_ROWS, GROUP), f32)
        for kk in range(TOP_K):
            p = jnp.where(riota == post_ref[0, kk:kk + 1, tok(i)].astype(jnp.int32), 1.0, p)
        ps.append(p.astype(bf16))
    rows = [_dot(ps[i], hf_ref[tok(i), :]) for i in places]
    for i in places:
        xbuf[slot(i)] = _pack_pairs(rows[i])
    for i in places:
        _for_cells(group(i), functools.partial(start_cell, slot(i)))

    @pl.when(step == n_steps - 1)
    def _():
        if n_steps >= 2:
            for i in places:
                prev_slot = (1 - lax.rem(step, 2)) * STEP_GROUPS + i
                wait_group(prev_slot, group(i) - STEP_GROUPS)
        for i in places:
            wait_group(slot(i), group(i))
        zbuf[...] = jnp.zeros(zbuf.shape, u32)

        def tail_copy(e):
            n = pl.multiple_of(tn_s[e], SUBLANES)
            dst = pl.multiple_of(toff_s[e], SUBLANES)
            return n, pltpu.make_async_copy(zbuf.at[pl.ds(0, n)], xs_hbm.at[pl.ds(dst, n)], zsem)

        def unused_copy(b):
            dst = pl.multiple_of(b * EXPERT_TILE, EXPERT_TILE)
            return pltpu.make_async_copy(zbuf, xs_hbm.at[pl.ds(dst, EXPERT_TILE)], zsem)

        def run(method):
            def tail(e, carry):
                n, cp = tail_copy(e)
                pl.when(n > 0)(getattr(cp, method))
                return carry
            lax.fori_loop(0, N_EXPERTS, tail, 0)

            def unused(b, carry):
                getattr(unused_copy(b), method)()
                return carry
            lax.fori_loop(nu_s[0], xs_hbm.shape[0] // EXPERT_TILE, unused, 0)
        run("start")
        run("wait")


def _dispatch(plan, hf, post, n_rows):
    n_tok, D = hf.shape
    n_steps = n_tok // (GROUP * STEP_GROUPS)
    assert n_steps * GROUP * STEP_GROUPS == n_tok
    step_tok = STEP_GROUPS * GROUP
    per_tile = SEQ_TILE // step_tok
    grid_spec = pltpu.PrefetchScalarGridSpec(
        num_scalar_prefetch=7,
        grid=(n_steps,),
        in_specs=[pl.BlockSpec((step_tok, D), lambda s, *_: (s, 0)),
                  pl.BlockSpec((1, SUBLANES, step_tok), lambda s, *_: (s // per_tile, 0, s % per_tile))],
        out_specs=pl.BlockSpec(memory_space=pl.ANY),
        scratch_shapes=[pltpu.VMEM((2 * STEP_GROUPS, GROUP_ROWS, D // 2), u32),
                        pltpu.VMEM((EXPERT_TILE, D // 2), u32),
                        pltpu.SemaphoreType.DMA((2 * STEP_GROUPS,)), pltpu.SemaphoreType.DMA(())],
    )
    return pl.pallas_call(
        functools.partial(_dispatch_kernel, n_steps=n_steps),
        grid_spec=grid_spec,
        out_shape=jax.ShapeDtypeStruct((n_rows, D // 2), u32),
        compiler_params=pltpu.CompilerParams(
            dimension_semantics=("arbitrary",), vmem_limit_bytes=VMEM_LIMIT),
        name="dispatch",
    )(plan["coff"], plan["rows"], plan["gbase"], plan["gtot"], plan["tailoff"], plan["tailn"],
      plan["n_used"], hf, post)


def _expert_kernel(eb_ref, nb_ref, nu_ref, wgu_ref, bgu_ref, wdn_ref, bdn_ref, pm_ref, xs_hbm,
                   y_hbm, wgu_b, wdn_b, xin, yout, in_sem, out_sem):
    e = pl.program_id(0)
    tm = EXPERT_TILE
    chunk_rows = EXPERT_CHUNK * tm
    n_used = nu_ref[0]
    n_chunks = (n_used + EXPERT_CHUNK - 1) // EXPERT_CHUNK

    def in_copy(c):
        sl = lax.rem(c, 2)
        rows = pl.ds(pl.multiple_of(c * chunk_rows, chunk_rows), chunk_rows)
        return pltpu.make_async_copy(xs_hbm.at[rows], xin.at[sl], in_sem.at[sl])

    def out_copy(c):
        sl = lax.rem(c, 2)
        rows = pl.ds(pl.multiple_of(c * chunk_rows, chunk_rows), chunk_rows)
        return pltpu.make_async_copy(yout.at[sl], y_hbm.at[rows], out_sem.at[sl])

    @pl.when(e == 0)
    def _():
        in_copy(0).start()
        yout[...] = jnp.zeros(yout.shape, u32)

    @pl.when(nb_ref[e] > 0)
    def _():
        for c in range(2 * D_FF // GU_TILE):
            cols = slice(c * GU_TILE, (c + 1) * GU_TILE)
            wgu_b[:, cols] = _dot(wgu_ref[0, :, cols].astype(bf16), pm_ref[...]).astype(bf16)
        for c in range(D_FF // LANES):
            rows = slice(c * LANES, (c + 1) * LANES)
            wdn_b[rows, :] = wdn_ref[0, rows, :].astype(bf16)

        def item(b, n_blk):
            c = b // EXPERT_CHUNK
            j = lax.rem(b, EXPERT_CHUNK)
            sl = lax.rem(c, 2)

            @pl.when(j == 0)
            def _():
                in_copy(c).wait()

                @pl.when(c + 1 < n_chunks)
                def _():
                    in_copy(c + 1).start()

                @pl.when(c >= 2)
                def _():
                    out_copy(c - 2).wait()

            rows = pl.ds(pl.multiple_of(j * tm, tm), n_blk * tm)
            _expert_block(xin.at[sl, rows], bgu_ref, bdn_ref, yout.at[sl, rows], wgu_b, wdn_b)
            last = b + n_blk - 1

            @pl.when((lax.rem(last, EXPERT_CHUNK) == EXPERT_CHUNK - 1) | (last == n_used - 1))
            def _():
                out_copy(c).start()

        first = eb_ref[e]
        n_head = jnp.minimum(lax.rem(EXPERT_CHUNK - lax.rem(first, EXPERT_CHUNK), EXPERT_CHUNK),
                             nb_ref[e])
        n_whole = (nb_ref[e] - n_head) // EXPERT_CHUNK
        n_tail = nb_ref[e] - n_head - n_whole * EXPERT_CHUNK

        def single(b, carry):
            item(b, 1)
            return carry

        def whole(i, carry):
            item(first + n_head + i * EXPERT_CHUNK, EXPERT_CHUNK)
            return carry
        lax.fori_loop(first, first + n_head, single, 0)
        lax.fori_loop(0, n_whole, whole, 0)
        tail0 = first + n_head + n_whole * EXPERT_CHUNK
        lax.fori_loop(tail0, tail0 + n_tail, single, 0)

    @pl.when(e == pl.num_programs(0) - 1)
    def _():
        @pl.when(n_chunks >= 2)
        def _():
            out_copy(n_chunks - 2).wait()
        out_copy(n_chunks - 1).wait()
        yout[0] = jnp.zeros(yout.shape[1:], u32)

        def run(method):
            def unused(c, carry):
                rows = pl.ds(pl.multiple_of(c * chunk_rows, chunk_rows), chunk_rows)
                getattr(pltpu.make_async_copy(yout.at[0], y_hbm.at[rows], out_sem.at[0]), method)()
                return carry
            lax.fori_loop(n_chunks, y_hbm.shape[0] // chunk_rows, unused, 0)
        run("start")
        run("wait")


def _expert_block(x_ref, bgu_ref, bdn_ref, y_ref, wgu_b, wdn_b):
    gu = _dot(_unpack_pairs(x_ref[...]), wgu_b[...]) + bgu_ref[0]
    acts = []
    for c in range(2 * D_FF // GU_TILE):
        g = jnp.minimum(gu[:, c * GU_TILE:c * GU_TILE + LANES], SWIGLU_LIMIT)
        lin = jnp.clip(gu[:, c * GU_TILE + LANES:(c + 1) * GU_TILE], -SWIGLU_LIMIT, SWIGLU_LIMIT)
        acts.append((g * jax.nn.sigmoid(SWIGLU_ALPHA * g) * (lin + 1.0)).astype(bf16))
    act = jnp.concatenate(acts, axis=-1)
    y = _dot(act, wdn_b[...]) + bdn_ref[0]
    y_ref[...] = _pack_pairs(y.astype(bf16).astype(f32))


def _gate_up_perm():
    j = np.arange(GU_TILE)
    p = np.zeros((GU_TILE, GU_TILE), np.float32)
    p[np.where(j < LANES, 2 * j, 2 * (j - LANES) + 1), j] = 1.0
    return jnp.asarray(p, bf16)


def _expert_mlp(plan, xs, w_gu, b_gu, w_dn, b_dn):
    n_rows = xs.shape[0]
    D = D_MODEL
    tm = EXPERT_TILE
    per_e = lambda e, *_: (e, 0, 0)
    grid_spec = pltpu.PrefetchScalarGridSpec(
        num_scalar_prefetch=3,
        grid=(N_EXPERTS,),
        in_specs=[pl.BlockSpec((1, D, 2 * D_FF), per_e),
                  pl.BlockSpec((1, 1, 2 * D_FF), per_e),
                  pl.BlockSpec((1, D_FF, D), per_e),
                  pl.BlockSpec((1, 1, D), per_e),
                  pl.BlockSpec((GU_TILE, GU_TILE), lambda e, *_: (0, 0)),
                  pl.BlockSpec(memory_space=pl.ANY)],
        out_specs=pl.BlockSpec(memory_space=pl.ANY),
        scratch_shapes=[pltpu.VMEM((D, 2 * D_FF), bf16), pltpu.VMEM((D_FF, D), bf16),
                        pltpu.VMEM((2, EXPERT_CHUNK * tm, D // 2), u32),
                        pltpu.VMEM((2, EXPERT_CHUNK * tm, D // 2), u32),
                        pltpu.SemaphoreType.DMA((2,)), pltpu.SemaphoreType.DMA((2,))],
    )
    return pl.pallas_call(
        _expert_kernel,
        grid_spec=grid_spec,
        out_shape=jax.ShapeDtypeStruct((n_rows, D // 2), u32),
        compiler_params=pltpu.CompilerParams(
            dimension_semantics=("arbitrary",), vmem_limit_bytes=VMEM_LIMIT),
        name="expert_mlp",
    )(plan["eblk"], plan["nblk"], plan["n_used"], w_gu, b_gu, w_dn, b_dn, _gate_up_perm(), xs)


def _combine_kernel(coff_s, rows_s, gbase_s, gtot_s, x_ref, route_ref, y_hbm, o_ref, ybuf, sem,
                    *, n_steps):
    step = pl.program_id(0)
    places = range(STEP_GROUPS)
    group = lambda s, i: s * STEP_GROUPS + i
    slot = lambda s, i: lax.rem(s, 2) * STEP_GROUPS + i
    tok = lambda i: slice(i * GROUP, (i + 1) * GROUP)

    def cell_copy(sl, i):
        n = pl.multiple_of(rows_s[i], SUBLANES)
        loc = pl.multiple_of(coff_s[i], SUBLANES)
        src = pl.multiple_of(gbase_s[i], SUBLANES)
        return n, pltpu.make_async_copy(y_hbm.at[pl.ds(src, n)], ybuf.at[sl, pl.ds(loc, n)],
                                        sem.at[sl])

    def start_cell(sl, i):
        n, cp = cell_copy(sl, i)
        pl.when(n > 0)(cp.start)

    def wait_group(sl, gg):
        n = pl.multiple_of(gtot_s[gg], SUBLANES)
        pltpu.make_async_copy(y_hbm.at[pl.ds(0, n)], ybuf.at[sl, pl.ds(0, n)], sem.at[sl]).wait()

    @pl.when(step == 0)
    def _():
        ybuf[...] = jnp.zeros(ybuf.shape, u32)
        for i in places:
            _for_cells(group(step, i), functools.partial(start_cell, slot(step, i)))

    @pl.when(step + 1 < n_steps)
    def _():
        for i in places:
            _for_cells(group(step + 1, i), functools.partial(start_cell, slot(step + 1, i)))

    for i in places:
        wait_group(slot(step, i), group(step, i))

    riota = lax.broadcasted_iota(jnp.int32, (GROUP, GROUP_ROWS), 1)
    pts = []
    for i in places:
        pt = jnp.zeros((GROUP, GROUP_ROWS), f32)
        for kk in range(TOP_K):
            pk = route_ref[tok(i), 2 * TOP_K + kk:2 * TOP_K + kk + 1].astype(jnp.int32)
            pt = jnp.where(riota == pk, route_ref[tok(i), TOP_K + kk:TOP_K + kk + 1], pt)
        pts.append(pt.astype(bf16))
    ys = [_unpack_pairs(ybuf[slot(step, i)]) for i in places]
    outs = [_dot(pts[i], ys[i]) for i in places]
    for i in places:
        o_ref[tok(i), :] = x_ref[tok(i), :] + outs[i]


def _combine(plan, x2, route, y_rows):
    n_tok, D = x2.shape
    n_steps = n_tok // (GROUP * STEP_GROUPS)
    assert n_steps * GROUP * STEP_GROUPS == n_tok
    tok = lambda w: pl.BlockSpec((STEP_GROUPS * GROUP, w), lambda s, *_: (s, 0))
    grid_spec = pltpu.PrefetchScalarGridSpec(
        num_scalar_prefetch=4,
        grid=(n_steps,),
        in_specs=[tok(D), tok(LANES), pl.BlockSpec(memory_space=pl.ANY)],
        out_specs=tok(D),
        scratch_shapes=[pltpu.VMEM((2 * STEP_GROUPS, GROUP_ROWS, D // 2), u32),
                        pltpu.SemaphoreType.DMA((2 * STEP_GROUPS,))],
    )
    return pl.pallas_call(
        functools.partial(_combine_kernel, n_steps=n_steps),
        grid_spec=grid_spec,
        out_shape=jax.ShapeDtypeStruct((n_tok, D), f32),
        compiler_params=pltpu.CompilerParams(
            dimension_semantics=("arbitrary",), vmem_limit_bytes=VMEM_LIMIT),
        name="combine",
    )(plan["coff"], plan["rows"], plan["gbase"], plan["gtot"], x2, route, y_rows)


def _routing_plan(cnt, n_tok):
    tm = EXPERT_TILE
    n_groups = n_tok // GROUP
    per_tile = SEQ_TILE // GROUP
    cnt = cnt[:, :, :per_tile].transpose(0, 2, 1).reshape(n_groups, N_EXPERTS).astype(jnp.int32)
    rows = (cnt + SUBLANES - 1) // SUBLANES * SUBLANES
    coff = jnp.cumsum(rows, axis=1) - rows
    tot = jnp.sum(rows, axis=0)
    padded = (tot + tm - 1) // tm * tm
    pend = jnp.cumsum(padded)
    ebase = pend - padded
    gbase = ebase[None, :] + jnp.cumsum(rows, axis=0) - rows
    flat = lambda a: a.reshape(-1).astype(jnp.int32)
    return dict(
        coff=flat(coff), rows=flat(rows), gbase=flat(gbase), gtot=flat(jnp.sum(rows, axis=1)),
        tailoff=flat(ebase + tot), tailn=flat(padded - tot),
        eblk=flat(ebase // tm), nblk=flat(padded // tm), n_used=flat(pend[-1] // tm))


def _max_rows(n_tok):
    tm = EXPERT_TILE
    worst = (n_tok * TOP_K + (n_tok // GROUP) * N_EXPERTS * (SUBLANES - 1)
             + N_EXPERTS * (tm - SUBLANES))
    chunk = EXPERT_CHUNK * tm
    return (worst + chunk - 1) // chunk * chunk


def _moe(x2, hf, route, post, cnt, w_gate_up, b_gate_up, w_down, b_down):
    n_tok, D = x2.shape
    plan = _routing_plan(cnt, n_tok)
    xs = _dispatch(plan, hf, post, _max_rows(n_tok))
    b_gu = (b_gate_up.reshape(N_EXPERTS, 2 * D_FF // GU_TILE, LANES, 2).transpose(0, 1, 3, 2)
            .reshape(N_EXPERTS, 1, 2 * D_FF))
    y_rows = _expert_mlp(plan, xs, w_gate_up, b_gu, w_down, b_down.reshape(N_EXPERTS, 1, D))
    return _combine(plan, x2, route, y_rows)


BIAS_ROW = (2 * Q_TILE + LEFT_CHUNKS * CHUNK + LANES - 2) // LANES * LANES


def _band_bias_kernel(r_ref, o_ref):
    qc = lax.broadcasted_iota(jnp.int32, (Q_TILE, K_BAND), 0) // CHUNK
    kc = lax.broadcasted_iota(jnp.int32, (Q_TILE, K_BAND), 1) // CHUNK
    first_chunk = LEFT_CHUNKS - pl.program_id(0) * (Q_TILE // CHUNK)
    visible = (kc >= qc) & (kc <= qc + LEFT_CHUNKS) & (kc >= first_chunk)
    for h in range(r_ref.shape[0]):
        x = jnp.broadcast_to(r_ref[h], (Q_TILE, BIAS_ROW))
        x = pltpu.roll(x, BIAS_ROW - (Q_TILE - 1), axis=1, stride=1, stride_axis=0)
        o_ref[0, h] = jnp.where(visible, x[:, :K_BAND] * LOG2E, NEG_INF)


def _band_bias(rel_bias):
    H = rel_bias.shape[0]
    d_max = Q_TILE - 1 + LEFT_CHUNKS * CHUNK
    n_far = d_max - MAX_REL + 1
    r = jnp.concatenate(
        [jnp.broadcast_to(rel_bias[:, 2 * MAX_REL:], (H, n_far)),
         rel_bias[:, MAX_REL - (Q_TILE - 1):2 * MAX_REL][:, ::-1],
         jnp.zeros((H, BIAS_ROW - (2 * Q_TILE - 1 + LEFT_CHUNKS * CHUNK)), rel_bias.dtype)], axis=1)
    return pl.pallas_call(
        _band_bias_kernel,
        grid=(START_TILES + 1,),
        in_specs=[pl.BlockSpec((H, 1, BIAS_ROW), lambda t: (0, 0, 0))],
        out_specs=pl.BlockSpec((1, H, Q_TILE, K_BAND), lambda t: (t, 0, 0, 0)),
        out_shape=jax.ShapeDtypeStruct((START_TILES + 1, H, Q_TILE, K_BAND), f32),
        name="band_bias",
    )(r.astype(f32).reshape(H, 1, BIAS_ROW))


def _layer(x, mem, norm_mix_g, w_in, q_norm_g, k_norm_g, rel_bias, conv_w, out_norm_a_g,
           out_norm_b_g, w_out, norm_xattn_g, norm_mem_g, w_xq, w_xkv, xq_norm_g, xk_norm_g,
           w_xo, norm_ffn_g, w_router, b_router, w_gate_up, b_gate_up, w_down, b_down):
    B, S, D = x.shape
    n_tok = B * S
    row = lambda a: a.reshape(1, -1).astype(f32)

    qg = row(jnp.tile(q_norm_g, ATT_HEADS) * (ATT_HEAD_DIM ** -0.5 * LOG2E))
    kg = row(jnp.tile(k_norm_g, ATT_HEADS))
    head_ones = jnp.asarray(np.kron(np.eye(ATT_HEADS), np.ones((ATT_HEAD_DIM, ATT_HEAD_DIM))), bf16)
    q, k, v, yb = _mix_in(x, row(norm_mix_g), w_in.astype(bf16), qg, kg, head_ones,
                          conv_w.astype(f32), row(out_norm_b_g))
    ya = _attn_out(q, k, v, _band_bias(rel_bias), row(out_norm_a_g))

    kx, vx = _mem_kv(mem, row(norm_mem_g), w_xkv.astype(bf16), row(xk_norm_g))
    wr = jnp.zeros((D, LANES), f32).at[:, :N_EXPERTS].set(w_router)
    wr_hi = wr.astype(bf16)
    wr_lo = (wr - wr_hi.astype(f32)).astype(bf16)
    br = jnp.full((1, LANES), -jnp.inf, f32).at[0, :N_EXPERTS].set(b_router)
    x2, hf, route, cnt, post = _xattn_router(
        x, ya, yb, w_out.astype(bf16), row(norm_xattn_g), w_xq.astype(bf16), row(xq_norm_g * (X_HEAD_DIM ** -0.5 * LOG2E)),
        kx, vx, w_xo.astype(bf16), row(norm_ffn_g), jnp.concatenate([wr_hi, wr_lo], axis=1), br)

    out = _moe(x2.reshape(n_tok, D), hf.reshape(n_tok, D), route.reshape(n_tok, LANES), post, cnt,
               w_gate_up, b_gate_up, w_down, b_down)
    return out.reshape(B, S, D)


def kernel(x, mem, norm_mix_g, w_in, q_norm_g, k_norm_g, rel_bias, conv_w, out_norm_a_g,
           out_norm_b_g, w_out, norm_xattn_g, norm_mem_g, w_xq, w_xkv, xq_norm_g, xk_norm_g,
           w_xo, norm_ffn_g, w_router, b_router, w_gate_up, b_gate_up, w_down, b_down):
    depth = norm_mix_g.shape[0]
    for l in range(depth):
        x = _layer(x, mem, norm_mix_g[l], w_in[l], q_norm_g[l], k_norm_g[l], rel_bias[l],
                   conv_w[l], out_norm_a_g[l], out_norm_b_g[l], w_out[l], norm_xattn_g[l],
                   norm_mem_g[l], w_xq[l], w_xkv[l], xq_norm_g[l], xk_norm_g[l], w_xo[l],
                   norm_ffn_g[l], w_router[l], b_router[l], w_gate_up[l], b_gate_up[l],
                   w_down[l], b_down[l])
    return x
```

```python
import functools

import jax
import jax.numpy as jnp
import numpy as np
from jax import lax
from jax.experimental import pallas as pl
from jax.experimental.pallas import tpu as pltpu

D_MODEL = 1024
CHUNK = 64
LEFT_CHUNKS = 8
N_MEM = 256
ATT_HEADS = 8
ATT_HEAD_DIM = 64
D_ATT = ATT_HEADS * ATT_HEAD_DIM
D_CONV = 512
MAX_REL = 256
X_HEADS = 4
X_HEAD_DIM = 128
D_X = X_HEADS * X_HEAD_DIM
N_EXPERTS = 32
TOP_K = 4
D_FF = D_MODEL
SWIGLU_LIMIT = 7.0
SWIGLU_ALPHA = 1.702
EPS = 1e-6
NEG_INF = -1e30

LANES = 128
SEQ_TILE = 1024
Q_TILE = 256
K_BAND = Q_TILE + LEFT_CHUNKS * CHUNK
START_TILES = LEFT_CHUNKS * CHUNK // Q_TILE
LOG2E = 1.4426950408889634
HEAD_ROUND = 4
EXPERT_TILE = 512
EXPERT_CHUNK = 2
GU_TILE = 2 * LANES
GROUP = 256
CELL_UNROLL = 4
STEP_GROUPS = 4
SUBLANES = 8
GROUP_ROWS = 1280
VMEM_LIMIT = 56 * 1024 * 1024

bf16 = jnp.bfloat16
f32 = jnp.float32


def _rms(x, g):
    return x * lax.rsqrt(jnp.mean(x * x, axis=-1, keepdims=True) + EPS) * g


def _dot(a, b):
    return jnp.dot(a, b, preferred_element_type=f32)


def _dot_nt(a, b):
    return lax.dot_general(a, b, (((1,), (1,)), ((), ())), preferred_element_type=f32)


def _mix_in_kernel(x_ref, g_ref, w_ref, qg_ref, kg_ref, hm_ref, cw_ref, gb_ref,
                   q_ref, k_ref, v_ref, yb_ref, ub_ref):
    ts = x_ref.shape[1]

    @pl.when(pl.program_id(1) == 0)
    def _():
        ub_ref[0:8, :] = jnp.zeros((8, D_CONV), f32)

    hb = _rms(x_ref[0], g_ref[...]).astype(bf16)

    def head_norm(t, gain):
        ms = _dot((t * t).astype(bf16), hm_ref[...]) * (1.0 / ATT_HEAD_DIM)
        return t * lax.rsqrt(ms + EPS) * gain

    o = 3 * D_ATT
    bg = _dot(hb, w_ref[:, o:o + D_CONV])
    cg = _dot(hb, w_ref[:, o + D_CONV:o + 2 * D_CONV])
    xv = _dot(hb, w_ref[:, o + 2 * D_CONV:o + 3 * D_CONV])
    u = cg * xv
    ub_ref[8:8 + ts, :] = u
    conv = (cw_ref[0:1, :] * ub_ref[6:6 + ts, :] + cw_ref[1:2, :] * ub_ref[7:7 + ts, :]
            + cw_ref[2:3, :] * u)
    ub_ref[0:8, :] = u[ts - 8:ts, :]
    yb_ref[0] = _rms(bg * conv, gb_ref[...]).astype(bf16)

    q = _dot(hb, w_ref[:, 0:D_ATT])
    q_ref[0] = head_norm(q, qg_ref[...]).astype(bf16)
    k = _dot(hb, w_ref[:, D_ATT:2 * D_ATT])
    k_ref[0] = head_norm(k, kg_ref[...]).astype(bf16)
    v_ref[0] = _dot(hb, w_ref[:, 2 * D_ATT:3 * D_ATT]).astype(bf16)


def _mix_in(x, g, w_in, qg, kg, hm, cw, gb):
    B, S, D = x.shape
    ts = SEQ_TILE
    full = lambda shape: pl.BlockSpec(shape, lambda b, j: (0,) * len(shape))
    tile = lambda w: pl.BlockSpec((1, ts, w), lambda b, j: (b, j, 0))
    out = jax.ShapeDtypeStruct((B, S, D_ATT), bf16)
    return pl.pallas_call(
        _mix_in_kernel,
        grid=(B, S // ts),
        in_specs=[tile(D), full((1, D)), full(w_in.shape), full((1, D_ATT)), full((1, D_ATT)),
                  full((D_ATT, D_ATT)), full((3, D_CONV)), full((1, D_CONV))],
        out_specs=[tile(D_ATT)] * 4,
        out_shape=[out] * 4,
        scratch_shapes=[pltpu.VMEM((ts + 8, D_CONV), f32)],
        compiler_params=pltpu.CompilerParams(
            dimension_semantics=("arbitrary", "arbitrary"), vmem_limit_bytes=VMEM_LIMIT),
        name="mix_in",
    )(x, g, w_in, qg, kg, hm, cw, gb)


def _attn_out_kernel(q_ref, k_ref, v_ref, bias_ref, ga_ref, o_ref, kb_ref, vb_ref):
    tq = q_ref.shape[1]
    n_kc = K_BAND // CHUNK
    c0 = pl.program_id(1) * (tq // CHUNK)
    for kc in range(n_kc):
        src = pl.multiple_of(jnp.maximum(c0 - LEFT_CHUNKS + kc, 0) * CHUNK, CHUNK)
        kb_ref[kc * CHUNK:(kc + 1) * CHUNK, :] = k_ref[0, pl.ds(src, CHUNK), :]
        vb_ref[kc * CHUNK:(kc + 1) * CHUNK, :] = v_ref[0, pl.ds(src, CHUNK), :]

    lane = lax.broadcasted_iota(jnp.int32, (1, LANES), 1)
    low = lane < ATT_HEAD_DIM

    outs = []
    for h0 in range(0, ATT_HEADS, HEAD_ROUND):
        heads = range(h0, h0 + HEAD_ROUND)
        scores, probs, denoms = {}, {}, {}
        for h in heads:
            sl = slice((h // 2) * LANES, (h // 2 + 1) * LANES)
            qp = q_ref[0, :, sl]
            keep = low if h % 2 == 0 else jnp.logical_not(low)
            scores[h] = _dot_nt(jnp.where(keep, qp, jnp.zeros_like(qp)), kb_ref[:, sl])
        for h in heads:
            s = scores[h] + bias_ref[0, h]
            p = jnp.exp2(s - jnp.max(s, axis=-1, keepdims=True))
            denoms[h] = jnp.sum(p, axis=-1, keepdims=True)
            probs[h] = p.astype(bf16)
        for h in heads:
            sl = slice((h // 2) * LANES, (h // 2 + 1) * LANES)
            outs.append(_dot(probs[h], vb_ref[:, sl]) / denoms[h])
    ya = jnp.concatenate([jnp.where(low, outs[2 * hp], outs[2 * hp + 1])
                          for hp in range(ATT_HEADS // 2)], axis=-1)
    o_ref[0] = _rms(ya, ga_ref[...]).astype(bf16)


def _attn_out(q, k, v, bias, ga):
    B, S, _ = q.shape
    tq = Q_TILE
    full = lambda shape: pl.BlockSpec(shape, lambda b, j: (0,) * len(shape))
    tile = lambda w: pl.BlockSpec((1, tq, w), lambda b, j: (b, j, 0))
    seq = pl.BlockSpec((1, S, D_ATT), lambda b, j: (b, 0, 0))
    return pl.pallas_call(
        _attn_out_kernel,
        grid=(B, S // tq),
        in_specs=[tile(D_ATT), seq, seq,
                  pl.BlockSpec((1,) + bias.shape[1:], lambda b, j: (jnp.minimum(j, START_TILES), 0, 0, 0)),
                  full((1, D_ATT))],
        out_specs=tile(D_ATT),
        out_shape=jax.ShapeDtypeStruct((B, S, D_ATT), bf16),
        scratch_shapes=[pltpu.VMEM((K_BAND, D_ATT), bf16), pltpu.VMEM((K_BAND, D_ATT), bf16)],
        compiler_params=pltpu.CompilerParams(
            dimension_semantics=("parallel", "parallel"), vmem_limit_bytes=VMEM_LIMIT),
        name="attn_out",
    )(q, k, v, bias, ga)


def _mem_kv_kernel(m_ref, g_ref, w_ref, kg_ref, k_ref, v_ref):
    hm = _rms(m_ref[...], g_ref[...]).astype(bf16)
    kv = _dot(hm, w_ref[...])
    for h in range(X_HEADS):
        sl = slice(h * X_HEAD_DIM, (h + 1) * X_HEAD_DIM)
        k_ref[:, sl] = _rms(kv[:, sl], kg_ref[...]).astype(bf16)
    v_ref[...] = kv[:, D_X:].astype(bf16)


def _mem_kv(mem, g, w_xkv, kg):
    B = mem.shape[0]
    rows = B * N_MEM
    tile = min(SEQ_TILE, rows)
    full = lambda shape: pl.BlockSpec(shape, lambda i: (0,) * len(shape))
    out = jax.ShapeDtypeStruct((rows, D_X), bf16)
    kx, vx = pl.pallas_call(
        _mem_kv_kernel,
        grid=(rows // tile,),
        in_specs=[pl.BlockSpec((tile, D_MODEL), lambda i: (i, 0)), full((1, D_MODEL)),
                  full(w_xkv.shape), full((1, X_HEAD_DIM))],
        out_specs=[pl.BlockSpec((tile, D_X), lambda i: (i, 0))] * 2,
        out_shape=[out, out],
        compiler_params=pltpu.CompilerParams(
            dimension_semantics=("parallel",), vmem_limit_bytes=VMEM_LIMIT),
        name="mem_kv",
    )(mem.reshape(rows, D_MODEL), g, w_xkv, kg)
    return kx.reshape(B, N_MEM, D_X), vx.reshape(B, N_MEM, D_X)


def _xattn_router_kernel(x_ref, ya_ref, yb_ref, wmix_ref, gx_ref, wq_ref, qg_ref, k_ref, v_ref, wo_ref,
                         gf_ref,
                         wr_ref, br_ref, earlier_ref,
                         x2_ref, hf_ref, route_ref, cnt_ref, post_ref):
    x1 = x_ref[0] + _dot(ya_ref[0], wmix_ref[0:D_ATT, :]) + _dot(yb_ref[0], wmix_ref[D_ATT:, :])
    hb = _rms(x1, gx_ref[...]).astype(bf16)
    q = _dot(hb, wq_ref[...])
    hsl = [slice(h * X_HEAD_DIM, (h + 1) * X_HEAD_DIM) for h in range(X_HEADS)]
    scores = [_dot_nt(_rms(q[:, sl], qg_ref[...]).astype(bf16), k_ref[0, :, sl]) for sl in hsl]
    probs, denoms = [], []
    for s in scores:
        p = jnp.exp2(s - jnp.max(s, axis=-1, keepdims=True))
        denoms.append(jnp.sum(p, axis=-1, keepdims=True))
        probs.append(p.astype(bf16))
    heads = [(_dot(p, v_ref[0, :, sl]) / l).astype(bf16) for p, l, sl in zip(probs, denoms, hsl)]
    o = jnp.concatenate(heads, axis=-1)
    x2 = x1 + _dot(o, wo_ref[...])
    x2_ref[0] = x2

    hf = _rms(x2, gf_ref[...])
    hf_hi = hf.astype(bf16)
    hf_ref[0] = hf_hi
    hf_lo = (hf - hf_hi.astype(f32)).astype(bf16)
    hi_prod = _dot(hf_hi, wr_ref[...])
    logits = (hi_prod[:, :LANES] + hi_prod[:, LANES:] + _dot(hf_lo, wr_ref[:, :LANES])
              + br_ref[...])
    lt = logits.T[:N_EXPERTS]
    eidx = lax.broadcasted_iota(jnp.int32, lt.shape, 0)
    vals, idxs = [], []
    for _ in range(TOP_K):
        m = jnp.max(lt, axis=0, keepdims=True)
        i = jnp.min(jnp.where(lt == m, eidx, N_EXPERTS), axis=0, keepdims=True)
        lt = jnp.where(eidx == i, -jnp.inf, lt)
        vals.append(m)
        idxs.append(i)
    es = [jnp.exp(v - vals[0]) for v in vals]
    tot = es[0] + es[1] + es[2] + es[3]
    sel = jnp.zeros(lt.shape, f32)
    for i in idxs:
        sel = jnp.where(eidx == i, 1.0, sel)
    n_grp = lt.shape[1] // GROUP
    grp = lambda j: slice(j * GROUP, (j + 1) * GROUP)
    lane = lax.broadcasted_iota(jnp.int32, (N_EXPERTS, LANES), 1)
    cnt = jnp.zeros((N_EXPERTS, LANES), f32)
    for j in range(n_grp):
        cnt = jnp.where(lane == j, jnp.sum(sel[:, grp(j)], axis=1, keepdims=True), cnt)
    cnt_ref[0] = cnt
    cell_rows = jnp.floor((cnt + (SUBLANES - 1)) * (1.0 / SUBLANES)) * SUBLANES
    below = (lax.broadcasted_iota(jnp.int32, (N_EXPERTS, N_EXPERTS), 0)
             > lax.broadcasted_iota(jnp.int32, (N_EXPERTS, N_EXPERTS), 1))
    coff = _dot(jnp.where(below, 1.0, 0.0).astype(bf16), cell_rows.astype(bf16))
    posmat = jnp.concatenate(
        [coff[:, j:j + 1] + _dot(sel[:, grp(j)].astype(bf16), earlier_ref[...]) for j in range(n_grp)], axis=1)
    pos = [jnp.sum(jnp.where(eidx == i, posmat, 0.0), axis=0, keepdims=True) for i in idxs]
    post_ref[0] = jnp.concatenate(pos + [jnp.zeros((SUBLANES - TOP_K, lt.shape[1]), f32)], axis=0)
    packed = jnp.concatenate([i.astype(f32) for i in idxs] + [e / tot for e in es] + pos
                             + [jnp.zeros((LANES - 3 * TOP_K, lt.shape[1]), f32)], axis=0)
    route_ref[0] = packed.T


def _xattn_router(x, ya, yb, w_mix, gx, w_xq, qg, kx, vx, w_xo, gf, wr, br):
    B, S, D = x.shape
    ts = SEQ_TILE
    full = lambda shape: pl.BlockSpec(shape, lambda b, j: (0,) * len(shape))
    tile = lambda w: pl.BlockSpec((1, ts, w), lambda b, j: (b, j, 0))
    mem = pl.BlockSpec((1, N_MEM, D_X), lambda b, j: (b, 0, 0))
    return pl.pallas_call(
        _xattn_router_kernel,
        grid=(B, S // ts),
        in_specs=[tile(D), tile(D_ATT), tile(D_CONV), full(w_mix.shape),
                  full((1, D)), full(w_xq.shape), full((1, X_HEAD_DIM)), mem, mem,
                  full(w_xo.shape), full((1, D)), full(wr.shape),
                  full((1, LANES)), full((GROUP, GROUP))],
        out_specs=[tile(D), tile(D), tile(LANES),
                   pl.BlockSpec((1, N_EXPERTS, LANES), lambda b, j: (b * (S // ts) + j, 0, 0)),
                   pl.BlockSpec((1, SUBLANES, ts), lambda b, j: (b * (S // ts) + j, 0, 0))],
        out_shape=[jax.ShapeDtypeStruct((B, S, D), f32), jax.ShapeDtypeStruct((B, S, D), bf16),
                   jax.ShapeDtypeStruct((B, S, LANES), f32),
                   jax.ShapeDtypeStruct((B * S // ts, N_EXPERTS, LANES), f32),
                   jax.ShapeDtypeStruct((B * S // ts, SUBLANES, ts), f32)],
        compiler_params=pltpu.CompilerParams(
            dimension_semantics=("parallel", "parallel"), vmem_limit_bytes=VMEM_LIMIT),
        name="xattn_router",
    )(x, ya, yb, w_mix, gx, w_xq, qg, kx, vx, w_xo, gf, wr, br,
      jnp.asarray(np.triu(np.ones((GROUP, GROUP)), 1), bf16))


u32 = jnp.uint32
HIGH_HALF = 0xFFFF0000


def _pack_pairs(x):
    w = x.shape[1] // 2
    lo = pltpu.bitcast(x[:, :w], u32) >> 16
    hi = pltpu.bitcast(x[:, w:], u32) & u32(HIGH_HALF)
    return lo | hi


def _unpack_pairs(u):
    lo = pltpu.bitcast(u << 16, f32)
    hi = pltpu.bitcast(u & u32(HIGH_HALF), f32)
    return jnp.concatenate([lo, hi], axis=-1).astype(bf16)


def _for_cells(g, fn):
    def body(j, carry):
        for u in range(CELL_UNROLL):
            fn(g * N_EXPERTS + j * CELL_UNROLL + u)
        return carry
    lax.fori_loop(0, N_EXPERTS // CELL_UNROLL, body, 0)


def _dispatch_kernel(coff_s, rows_s, gbase_s, gtot_s, toff_s, tn_s, nu_s,
                     hf_ref, post_ref, xs_hbm, xbuf, zbuf, sem, zsem, *, n_steps):
    step = pl.program_id(0)
    places = range(STEP_GROUPS)
    group = lambda i: step * STEP_GROUPS + i
    slot = lambda i: lax.rem(step, 2) * STEP_GROUPS + i
    tok = lambda i: slice(i * GROUP, (i + 1) * GROUP)

    def cell_copy(sl, i):
        n = pl.multiple_of(rows_s[i], SUBLANES)
        src = pl.multiple_of(coff_s[i], SUBLANES)
        dst = pl.multiple_of(gbase_s[i], SUBLANES)
        return n, pltpu.make_async_copy(xbuf.at[sl, pl.ds(src, n)], xs_hbm.at[pl.ds(dst, n)],
                                        sem.at[sl])

    def start_cell(sl, i):
        n, cp = cell_copy(sl, i)
        pl.when(n > 0)(cp.start)

    def wait_group(sl, gg):
        n = pl.multiple_of(gtot_s[gg], SUBLANES)
        pltpu.make_async_copy(xbuf.at[sl, pl.ds(0, n)], xs_hbm.at[pl.ds(0, n)], sem.at[sl]).wait()

    @pl.when(step >= 2)
    def _():
        for i in places:
            wait_group(slot(i), group(i) - 2 * STEP_GROUPS)

    riota = lax.broadcasted_iota(jnp.int32, (GROUP_ROWS, GROUP), 0)
    ps = []
    for i in places:
        p = jnp.zeros((GROUP_ROWS, GROUP), f32)
        for kk in range(TOP_K):
            p = jnp.where(riota == post_ref[0, kk:kk + 1, tok(i)].astype(jnp.int32), 1.0, p)
        ps.append(p.astype(bf16))
    rows = [_dot(ps[i], hf_ref[tok(i), :]) for i in places]
    for i in places:
        xbuf[slot(i)] = _pack_pairs(rows[i])
    for i in places:
        _for_cells(group(i), functools.partial(start_cell, slot(i)))

    @pl.when(step == n_steps - 1)
    def _():
        if n_steps >= 2:
            for i in places:
                prev_slot = (1 - lax.rem(step, 2)) * STEP_GROUPS + i
                wait_group(prev_slot, group(i) - STEP_GROUPS)
        for i in places:
            wait_group(slot(i), group(i))
        zbuf[...] = jnp.zeros(zbuf.shape, u32)

        def tail_copy(e):
            n = pl.multiple_of(tn_s[e], SUBLANES)
            dst = pl.multiple_of(toff_s[e], SUBLANES)
            return n, pltpu.make_async_copy(zbuf.at[pl.ds(0, n)], xs_hbm.at[pl.ds(dst, n)], zsem)

        def unused_copy(b):
            dst = pl.multiple_of(b * EXPERT_TILE, EXPERT_TILE)
            return pltpu.make_async_copy(zbuf, xs_hbm.at[pl.ds(dst, EXPERT_TILE)], zsem)

        def run(method):
            def tail(e, carry):
                n, cp = tail_copy(e)
                pl.when(n > 0)(getattr(cp, method))
                return carry
            lax.fori_loop(0, N_EXPERTS, tail, 0)

            def unused(b, carry):
                getattr(unused_copy(b), method)()
                return carry
            lax.fori_loop(nu_s[0], xs_hbm.shape[0] // EXPERT_TILE, unused, 0)
        run("start")
        run("wait")


def _dispatch(plan, hf, post, n_rows):
    n_tok, D = hf.shape
    n_steps = n_tok // (GROUP * STEP_GROUPS)
    assert n_steps * GROUP * STEP_GROUPS == n_tok
    step_tok = STEP_GROUPS * GROUP
    per_tile = SEQ_TILE // step_tok
    grid_spec = pltpu.PrefetchScalarGridSpec(
        num_scalar_prefetch=7,
        grid=(n_steps,),
        in_specs=[pl.BlockSpec((step_tok, D), lambda s, *_: (s, 0)),
                  pl.BlockSpec((1, SUBLANES, step_tok), lambda s, *_: (s // per_tile, 0, s % per_tile))],
        out_specs=pl.BlockSpec(memory_space=pl.ANY),
        scratch_shapes=[pltpu.VMEM((2 * STEP_GROUPS, GROUP_ROWS, D // 2), u32),
                        pltpu.VMEM((EXPERT_TILE, D // 2), u32),
                        pltpu.SemaphoreType.DMA((2 * STEP_GROUPS,)), pltpu.SemaphoreType.DMA(())],
    )
    return pl.pallas_call(
        functools.partial(_dispatch_kernel, n_steps=n_steps),
        grid_spec=grid_spec,
        out_shape=jax.ShapeDtypeStruct((n_rows, D // 2), u32),
        compiler_params=pltpu.CompilerParams(
            dimension_semantics=("arbitrary",), vmem_limit_bytes=VMEM_LIMIT),
        name="dispatch",
    )(plan["coff"], plan["rows"], plan["gbase"], plan["gtot"], plan["tailoff"], plan["tailn"],
      plan["n_used"], hf, post)


def _expert_kernel(eb_ref, nb_ref, nu_ref, wgu_ref, bgu_ref, wdn_ref, bdn_ref, pm_ref, xs_hbm,
                   y_hbm, wgu_b, wdn_b, xin, yout, in_sem, out_sem):
    e = pl.program_id(0)
    tm = EXPERT_TILE
    chunk_rows = EXPERT_CHUNK * tm
    n_used = nu_ref[0]
    n_chunks = (n_used + EXPERT_CHUNK - 1) // EXPERT_CHUNK

    def in_copy(c):
        sl = lax.rem(c, 2)
        rows = pl.ds(pl.multiple_of(c * chunk_rows, chunk_rows), chunk_rows)
        return pltpu.make_async_copy(xs_hbm.at[rows], xin.at[sl], in_sem.at[sl])

    def out_copy(c):
        sl = lax.rem(c, 2)
        rows = pl.ds(pl.multiple_of(c * chunk_rows, chunk_rows), chunk_rows)
        return pltpu.make_async_copy(yout.at[sl], y_hbm.at[rows], out_sem.at[sl])

    @pl.when(e == 0)
    def _():
        in_copy(0).start()
        yout[...] = jnp.zeros(yout.shape, u32)

    @pl.when(nb_ref[e] > 0)
    def _():
        for c in range(2 * D_FF // GU_TILE):
            cols = slice(c * GU_TILE, (c + 1) * GU_TILE)
            wgu_b[:, cols] = _dot(wgu_ref[0, :, cols].astype(bf16), pm_ref[...]).astype(bf16)
        for c in range(D_FF // LANES):
            rows = slice(c * LANES, (c + 1) * LANES)
            wdn_b[rows, :] = wdn_ref[0, rows, :].astype(bf16)

        def item(b, n_blk):
            c = b // EXPERT_CHUNK
            j = lax.rem(b, EXPERT_CHUNK)
            sl = lax.rem(c, 2)

            @pl.when(j == 0)
            def _():
                in_copy(c).wait()

                @pl.when(c + 1 < n_chunks)
                def _():
                    in_copy(c + 1).start()

                @pl.when(c >= 2)
                def _():
                    out_copy(c - 2).wait()

            rows = pl.ds(pl.multiple_of(j * tm, tm), n_blk * tm)
            _expert_block(xin.at[sl, rows], bgu_ref, bdn_ref, yout.at[sl, rows], wgu_b, wdn_b)
            last = b + n_blk - 1

            @pl.when((lax.rem(last, EXPERT_CHUNK) == EXPERT_CHUNK - 1) | (last == n_used - 1))
            def _():
                out_copy(c).start()

        first = eb_ref[e]
        n_head = jnp.minimum(lax.rem(EXPERT_CHUNK - lax.rem(first, EXPERT_CHUNK), EXPERT_CHUNK),
                             nb_ref[e])
        n_whole = (nb_ref[e] - n_head) // EXPERT_CHUNK
        n_tail = nb_ref[e] - n_head - n_whole * EXPERT_CHUNK

        def single(b, carry):
            item(b, 1)
            return carry

        def whole(i, carry):
            item(first + n_head + i * EXPERT_CHUNK, EXPERT_CHUNK)
            return carry
        lax.fori_loop(first, first + n_head, single, 0)
        lax.fori_loop(0, n_whole, whole, 0)
        tail0 = first + n_head + n_whole * EXPERT_CHUNK
        lax.fori_loop(tail0, tail0 + n_tail, single, 0)

    @pl.when(e == pl.num_programs(0) - 1)
    def _():
        @pl.when(n_chunks >= 2)
        def _():
            out_copy(n_chunks - 2).wait()
        out_copy(n_chunks - 1).wait()
        yout[0] = jnp.zeros(yout.shape[1:], u32)

        def run(method):
            def unused(c, carry):
                rows = pl.ds(pl.multiple_of(c * chunk_rows, chunk_rows), chunk_rows)
                getattr(pltpu.make_async_copy(yout.at[0], y_hbm.at[rows], out_sem.at[0]), method)()
                return carry
            lax.fori_loop(n_chunks, y_hbm.shape[0] // chunk_rows, unused, 0)
        run("start")
        run("wait")


def _expert_block(x_ref, bgu_ref, bdn_ref, y_ref, wgu_b, wdn_b):
    gu = _dot(_unpack_pairs(x_ref[...]), wgu_b[...]) + bgu_ref[0]
    acts = []
    for c in range(2 * D_FF // GU_TILE):
        g = jnp.minimum(gu[:, c * GU_TILE:c * GU_TILE + LANES], SWIGLU_LIMIT)
        lin = jnp.clip(gu[:, c * GU_TILE + LANES:(c + 1) * GU_TILE], -SWIGLU_LIMIT, SWIGLU_LIMIT)
        acts.append((g * jax.nn.sigmoid(SWIGLU_ALPHA * g) * (lin + 1.0)).astype(bf16))
    act = jnp.concatenate(acts, axis=-1)
    y = _dot(act, wdn_b[...]) + bdn_ref[0]
    y_ref[...] = _pack_pairs(y.astype(bf16).astype(f32))


def _gate_up_perm():
    j = np.arange(GU_TILE)
    p = np.zeros((GU_TILE, GU_TILE), np.float32)
    p[np.where(j < LANES, 2 * j, 2 * (j - LANES) + 1), j] = 1.0
    return jnp.asarray(p, bf16)


def _expert_mlp(plan, xs, w_gu, b_gu, w_dn, b_dn):
    n_rows = xs.shape[0]
    D = D_MODEL
    tm = EXPERT_TILE
    per_e = lambda e, *_: (e, 0, 0)
    grid_spec = pltpu.PrefetchScalarGridSpec(
        num_scalar_prefetch=3,
        grid=(N_EXPERTS,),
        in_specs=[pl.BlockSpec((1, D, 2 * D_FF), per_e),
                  pl.BlockSpec((1, 1, 2 * D_FF), per_e),
                  pl.BlockSpec((1, D_FF, D), per_e),
                  pl.BlockSpec((1, 1, D), per_e),
                  pl.BlockSpec((GU_TILE, GU_TILE), lambda e, *_: (0, 0)),
                  pl.BlockSpec(memory_space=pl.ANY)],
        out_specs=pl.BlockSpec(memory_space=pl.ANY),
        scratch_shapes=[pltpu.VMEM((D, 2 * D_FF), bf16), pltpu.VMEM((D_FF, D), bf16),
                        pltpu.VMEM((2, EXPERT_CHUNK * tm, D // 2), u32),
                        pltpu.VMEM((2, EXPERT_CHUNK * tm, D // 2), u32),
                        pltpu.SemaphoreType.DMA((2,)), pltpu.SemaphoreType.DMA((2,))],
    )
    return pl.pallas_call(
        _expert_kernel,
        grid_spec=grid_spec,
        out_shape=jax.ShapeDtypeStruct((n_rows, D // 2), u32),
        compiler_params=pltpu.CompilerParams(
            dimension_semantics=("arbitrary",), vmem_limit_bytes=VMEM_LIMIT),
        name="expert_mlp",
    )(plan["eblk"], plan["nblk"], plan["n_used"], w_gu, b_gu, w_dn, b_dn, _gate_up_perm(), xs)


def _combine_kernel(coff_s, rows_s, gbase_s, gtot_s, x_ref, route_ref, y_hbm, o_ref, ybuf, sem,
                    *, n_steps):
    step = pl.program_id(0)
    places = range(STEP_GROUPS)
    group = lambda s, i: s * STEP_GROUPS + i
    slot = lambda s, i: lax.rem(s, 2) * STEP_GROUPS + i
    tok = lambda i: slice(i * GROUP, (i + 1) * GROUP)

    def cell_copy(sl, i):
        n = pl.multiple_of(rows_s[i], SUBLANES)
        loc = pl.multiple_of(coff_s[i], SUBLANES)
        src = pl.multiple_of(gbase_s[i], SUBLANES)
        return n, pltpu.make_async_copy(y_hbm.at[pl.ds(src, n)], ybuf.at[sl, pl.ds(loc, n)],
                                        sem.at[sl])

    def start_cell(sl, i):
        n, cp = cell_copy(sl, i)
        pl.when(n > 0)(cp.start)

    def wait_group(sl, gg):
        n = pl.multiple_of(gtot_s[gg], SUBLANES)
        pltpu.make_async_copy(y_hbm.at[pl.ds(0, n)], ybuf.at[sl, pl.ds(0, n)], sem.at[sl]).wait()

    @pl.when(step == 0)
    def _():
        ybuf[...] = jnp.zeros(ybuf.shape, u32)
        for i in places:
            _for_cells(group(step, i), functools.partial(start_cell, slot(step, i)))

    @pl.when(step + 1 < n_steps)
    def _():
        for i in places:
            _for_cells(group(step + 1, i), functools.partial(start_cell, slot(step + 1, i)))

    for i in places:
        wait_group(slot(step, i), group(step, i))

    riota = lax.broadcasted_iota(jnp.int32, (GROUP, GROUP_ROWS), 1)
    pts = []
    for i in places:
        pt = jnp.zeros((GROUP, GROUP_ROWS), f32)
        for kk in range(TOP_K):
            pk = route_ref[tok(i), 2 * TOP_K + kk:2 * TOP_K + kk + 1].astype(jnp.int32)
            pt = jnp.where(riota == pk, route_ref[tok(i), TOP_K + kk:TOP_K + kk + 1], pt)
        pts.append(pt.astype(bf16))
    ys = [_unpack_pairs(ybuf[slot(step, i)]) for i in places]
    outs = [_dot(pts[i], ys[i]) for i in places]
    for i in places:
        o_ref[tok(i), :] = x_ref[tok(i), :] + outs[i]


def _combine(plan, x2, route, y_rows):
    n_tok, D = x2.shape
    n_steps = n_tok // (GROUP * STEP_GROUPS)
    assert n_steps * GROUP * STEP_GROUPS == n_tok
    tok = lambda w: pl.BlockSpec((STEP_GROUPS * GROUP, w), lambda s, *_: (s, 0))
    grid_spec = pltpu.PrefetchScalarGridSpec(
        num_scalar_prefetch=4,
        grid=(n_steps,),
        in_specs=[tok(D), tok(LANES), pl.BlockSpec(memory_space=pl.ANY)],
        out_specs=tok(D),
        scratch_shapes=[pltpu.VMEM((2 * STEP_GROUPS, GROUP_ROWS, D // 2), u32),
                        pltpu.SemaphoreType.DMA((2 * STEP_GROUPS,))],
    )
    return pl.pallas_call(
        functools.partial(_combine_kernel, n_steps=n_steps),
        grid_spec=grid_spec,
        out_shape=jax.ShapeDtypeStruct((n_tok, D), f32),
        compiler_params=pltpu.CompilerParams(
            dimension_semantics=("arbitrary",), vmem_limit_bytes=VMEM_LIMIT),
        name="combine",
    )(plan["coff"], plan["rows"], plan["gbase"], plan["gtot"], x2, route, y_rows)


def _routing_plan(cnt, n_tok):
    tm = EXPERT_TILE
    n_groups = n_tok // GROUP
    per_tile = SEQ_TILE // GROUP
    cnt = cnt[:, :, :per_tile].transpose(0, 2, 1).reshape(n_groups, N_EXPERTS).astype(jnp.int32)
    rows = (cnt + SUBLANES - 1) // SUBLANES * SUBLANES
    coff = jnp.cumsum(rows, axis=1) - rows
    tot = jnp.sum(rows, axis=0)
    padded = (tot + tm - 1) // tm * tm
    pend = jnp.cumsum(padded)
    ebase = pend - padded
    gbase = ebase[None, :] + jnp.cumsum(rows, axis=0) - rows
    flat = lambda a: a.reshape(-1).astype(jnp.int32)
    return dict(
        coff=flat(coff), rows=flat(rows), gbase=flat(gbase), gtot=flat(jnp.sum(rows, axis=1)),
        tailoff=flat(ebase + tot), tailn=flat(padded - tot),
        eblk=flat(ebase // tm), nblk=flat(padded // tm), n_used=flat(pend[-1] // tm))


def _max_rows(n_tok):
    tm = EXPERT_TILE
    worst = (n_tok * TOP_K + (n_tok // GROUP) * N_EXPERTS * (SUBLANES - 1)
             + N_EXPERTS * (tm - SUBLANES))
    chunk = EXPERT_CHUNK * tm
    return (worst + chunk - 1) // chunk * chunk


def _moe(x2, hf, route, post, cnt, w_gate_up, b_gate_up, w_down, b_down):
    n_tok, D = x2.shape
    plan = _routing_plan(cnt, n_tok)
    xs = _dispatch(plan, hf, post, _max_rows(n_tok))
    b_gu = (b_gate_up.reshape(N_EXPERTS, 2 * D_FF // GU_TILE, LANES, 2).transpose(0, 1, 3, 2)
            .reshape(N_EXPERTS, 1, 2 * D_FF))
    y_rows = _expert_mlp(plan, xs, w_gate_up, b_gu, w_down, b_down.reshape(N_EXPERTS, 1, D))
    return _combine(plan, x2, route, y_rows)


BIAS_ROW = (2 * Q_TILE + LEFT_CHUNKS * CHUNK + LANES - 2) // LANES * LANES


def _band_bias_kernel(r_ref, o_ref):
    qc = lax.broadcasted_iota(jnp.int32, (Q_TILE, K_BAND), 0) // CHUNK
    kc = lax.broadcasted_iota(jnp.int32, (Q_TILE, K_BAND), 1) // CHUNK
    first_chunk = LEFT_CHUNKS - pl.program_id(0) * (Q_TILE // CHUNK)
    visible = (kc >= qc) & (kc <= qc + LEFT_CHUNKS) & (kc >= first_chunk)
    for h in range(r_ref.shape[0]):
        x = jnp.broadcast_to(r_ref[h], (Q_TILE, BIAS_ROW))
        x = pltpu.roll(x, BIAS_ROW - (Q_TILE - 1), axis=1, stride=1, stride_axis=0)
        o_ref[0, h] = jnp.where(visible, x[:, :K_BAND] * LOG2E, NEG_INF)


def _band_bias(rel_bias):
    H = rel_bias.shape[0]
    d_max = Q_TILE - 1 + LEFT_CHUNKS * CHUNK
    n_far = d_max - MAX_REL + 1
    r = jnp.concatenate(
        [jnp.broadcast_to(rel_bias[:, 2 * MAX_REL:], (H, n_far)),
         rel_bias[:, MAX_REL - (Q_TILE - 1):2 * MAX_REL][:, ::-1],
         jnp.zeros((H, BIAS_ROW - (2 * Q_TILE - 1 + LEFT_CHUNKS * CHUNK)), rel_bias.dtype)], axis=1)
    return pl.pallas_call(
        _band_bias_kernel,
        grid=(START_TILES + 1,),
        in_specs=[pl.BlockSpec((H, 1, BIAS_ROW), lambda t: (0, 0, 0))],
        out_specs=pl.BlockSpec((1, H, Q_TILE, K_BAND), lambda t: (t, 0, 0, 0)),
        out_shape=jax.ShapeDtypeStruct((START_TILES + 1, H, Q_TILE, K_BAND), f32),
        name="band_bias",
    )(r.astype(f32).reshape(H, 1, BIAS_ROW))


def _layer(x, mem, norm_mix_g, w_in, q_norm_g, k_norm_g, rel_bias, conv_w, out_norm_a_g,
           out_norm_b_g, w_out, norm_xattn_g, norm_mem_g, w_xq, w_xkv, xq_norm_g, xk_norm_g,
           w_xo, norm_ffn_g, w_router, b_router, w_gate_up, b_gate_up, w_down, b_down):
    B, S, D = x.shape
    n_tok = B * S
    row = lambda a: a.reshape(1, -1).astype(f32)

    qg = row(jnp.tile(q_norm_g, ATT_HEADS) * (ATT_HEAD_DIM ** -0.5 * LOG2E))
    kg = row(jnp.tile(k_norm_g, ATT_HEADS))
    head_ones = jnp.asarray(np.kron(np.eye(ATT_HEADS), np.ones((ATT_HEAD_DIM, ATT_HEAD_DIM))), bf16)
    q, k, v, yb = _mix_in(x, row(norm_mix_g), w_in.astype(bf16), qg, kg, head_ones,
                          conv_w.astype(f32), row(out_norm_b_g))
    ya = _attn_out(q, k, v, _band_bias(rel_bias), row(out_norm_a_g))

    kx, vx = _mem_kv(mem, row(norm_mem_g), w_xkv.astype(bf16), row(xk_norm_g))
    wr = jnp.zeros((D, LANES), f32).at[:, :N_EXPERTS].set(w_router)
    wr_hi = wr.astype(bf16)
    wr_lo = (wr - wr_hi.astype(f32)).astype(bf16)
    br = jnp.full((1, LANES), -jnp.inf, f32).at[0, :N_EXPERTS].set(b_router)
    x2, hf, route, cnt, post = _xattn_router(
        x, ya, yb, w_out.astype(bf16), row(norm_xattn_g), w_xq.astype(bf16), row(xq_norm_g * (X_HEAD_DIM ** -0.5 * LOG2E)),
        kx, vx, w_xo.astype(bf16), row(norm_ffn_g), jnp.concatenate([wr_hi, wr_lo], axis=1), br)

    out = _moe(x2.reshape(n_tok, D), hf.reshape(n_tok, D), route.reshape(n_tok, LANES), post, cnt,
               w_gate_up, b_gate_up, w_down, b_down)
    return out.reshape(B, S, D)


def kernel(x, mem, norm_mix_g, w_in, q_norm_g, k_norm_g, rel_bias, conv_w, out_norm_a_g,
           out_norm_b_g, w_out, norm_xattn_g, norm_mem_g, w_xq, w_xkv, xq_norm_g, xk_norm_g,
           w_xo, norm_ffn_g, w_router, b_router, w_gate_up, b_gate_up, w_down, b_down):
    depth = norm_mix_g.shape[0]
    for l in range(depth):
        x = _layer(x, mem, norm_mix_g[l], w_in[l], q_norm_g[l], k_norm_g[l], rel_bias[l],
                   conv_w[l], out_norm_a_g[l], out_norm_b_g[l], w_out[l], norm_xattn_g[l],
                   norm_mem_g[l], w_xq[l], w_xkv[l], xq_norm_g[l], xk_norm_g[l], w_xo[l],
                   norm_ffn_g[l], w_router[l], b_router[l], w_gate_up[l], b_gate_up[l],
                   w_down[l], b_down[l])
    return x
```

```python
import functools

import jax
import jax.numpy as jnp
import numpy as np
from jax import lax
from jax.experimental import pallas as pl
from jax.experimental.pallas import tpu as pltpu

D_MODEL = 1024
CHUNK = 64
LEFT_CHUNKS = 8
N_MEM = 256
ATT_HEADS = 8
ATT_HEAD_DIM = 64
D_ATT = ATT_HEADS * ATT_HEAD_DIM
D_CONV = 512
MAX_REL = 256
X_HEADS = 4
X_HEAD_DIM = 128
D_X = X_HEADS * X_HEAD_DIM
N_EXPERTS = 32
TOP_K = 4
D_FF = D_MODEL
SWIGLU_LIMIT = 7.0
SWIGLU_ALPHA = 1.702
EPS = 1e-6
NEG_INF = -1e30

LANES = 128
SEQ_TILE = 1024
Q_TILE = 256
K_BAND = Q_TILE + LEFT_CHUNKS * CHUNK
START_TILES = LEFT_CHUNKS * CHUNK // Q_TILE
LOG2E = 1.4426950408889634
HEAD_ROUND = 4
EXPERT_TILE = 512
EXPERT_CHUNK = 2
MXU_TILE = 2 * LANES
GU_TILE = MXU_TILE
GROUP = 256
CELL_UNROLL = 4
STEP_GROUPS = 4
SUBLANES = 8
GROUP_ROWS = -(-(GROUP * TOP_K + N_EXPERTS * (SUBLANES - 1)) // MXU_TILE) * MXU_TILE
VMEM_LIMIT = 56 * 1024 * 1024

bf16 = jnp.bfloat16
f32 = jnp.float32


def _rms(x, g):
    return x * lax.rsqrt(jnp.mean(x * x, axis=-1, keepdims=True) + EPS) * g


def _dot(a, b):
    return jnp.dot(a, b, preferred_element_type=f32)


def _dot_nt(a, b):
    return lax.dot_general(a, b, (((1,), (1,)), ((), ())), preferred_element_type=f32)


def _mix_in_kernel(x_ref, g_ref, w_ref, qg_ref, kg_ref, hm_ref, cw_ref, gb_ref,
                   q_ref, k_ref, v_ref, yb_ref, ub_ref):
    ts = x_ref.shape[1]

    @pl.when(pl.program_id(1) == 0)
    def _():
        ub_ref[0:8, :] = jnp.zeros((8, D_CONV), f32)

    hb = _rms(x_ref[0], g_ref[...]).astype(bf16)

    def head_norm(t, gain):
        ms = _dot((t * t).astype(bf16), hm_ref[...]) * (1.0 / ATT_HEAD_DIM)
        return t * lax.rsqrt(ms + EPS) * gain

    o = 3 * D_ATT
    bg = _dot(hb, w_ref[:, o:o + D_CONV])
    cg = _dot(hb, w_ref[:, o + D_CONV:o + 2 * D_CONV])
    xv = _dot(hb, w_ref[:, o + 2 * D_CONV:o + 3 * D_CONV])
    u = cg * xv
    ub_ref[8:8 + ts, :] = u
    conv = (cw_ref[0:1, :] * ub_ref[6:6 + ts, :] + cw_ref[1:2, :] * ub_ref[7:7 + ts, :]
            + cw_ref[2:3, :] * u)
    ub_ref[0:8, :] = u[ts - 8:ts, :]
    yb_ref[0] = _rms(bg * conv, gb_ref[...]).astype(bf16)

    q = _dot(hb, w_ref[:, 0:D_ATT])
    q_ref[0] = head_norm(q, qg_ref[...]).astype(bf16)
    k = _dot(hb, w_ref[:, D_ATT:2 * D_ATT])
    k_ref[0] = head_norm(k, kg_ref[...]).astype(bf16)
    v_ref[0] = _dot(hb, w_ref[:, 2 * D_ATT:3 * D_ATT]).astype(bf16)


def _mix_in(x, g, w_in, qg, kg, hm, cw, gb):
    B, S, D = x.shape
    ts = SEQ_TILE
    full = lambda shape: pl.BlockSpec(shape, lambda b, j: (0,) * len(shape))
    tile = lambda w: pl.BlockSpec((1, ts, w), lambda b, j: (b, j, 0))
    out = jax.ShapeDtypeStruct((B, S, D_ATT), bf16)
    return pl.pallas_call(
        _mix_in_kernel,
        grid=(B, S // ts),
        in_specs=[tile(D), full((1, D)), full(w_in.shape), full((1, D_ATT)), full((1, D_ATT)),
                  full((D_ATT, D_ATT)), full((3, D_CONV)), full((1, D_CONV))],
        out_specs=[tile(D_ATT)] * 4,
        out_shape=[out] * 4,
        scratch_shapes=[pltpu.VMEM((ts + 8, D_CONV), f32)],
        compiler_params=pltpu.CompilerParams(
            dimension_semantics=("arbitrary", "arbitrary"), vmem_limit_bytes=VMEM_LIMIT),
        name="mix_in",
    )(x, g, w_in, qg, kg, hm, cw, gb)


def _attn_out_kernel(q_ref, k_ref, v_ref, bias_ref, ga_ref, o_ref, kb_ref, vb_ref):
    tq = q_ref.shape[1]
    n_kc = K_BAND // CHUNK
    c0 = pl.program_id(1) * (tq // CHUNK)
    for kc in range(n_kc):
        src = pl.multiple_of(jnp.maximum(c0 - LEFT_CHUNKS + kc, 0) * CHUNK, CHUNK)
        kb_ref[kc * CHUNK:(kc + 1) * CHUNK, :] = k_ref[0, pl.ds(src, CHUNK), :]
        vb_ref[kc * CHUNK:(kc + 1) * CHUNK, :] = v_ref[0, pl.ds(src, CHUNK), :]

    lane = lax.broadcasted_iota(jnp.int32, (1, LANES), 1)
    low = lane < ATT_HEAD_DIM

    outs = []
    for h0 in range(0, ATT_HEADS, HEAD_ROUND):
        heads = range(h0, h0 + HEAD_ROUND)
        scores, probs, denoms = {}, {}, {}
        for h in heads:
            sl = slice((h // 2) * LANES, (h // 2 + 1) * LANES)
            qp = q_ref[0, :, sl]
            keep = low if h % 2 == 0 else jnp.logical_not(low)
            scores[h] = _dot_nt(jnp.where(keep, qp, jnp.zeros_like(qp)), kb_ref[:, sl])
        for h in heads:
            s = scores[h] + bias_ref[0, h]
            p = jnp.exp2(s - jnp.max(s, axis=-1, keepdims=True))
            denoms[h] = jnp.sum(p, axis=-1, keepdims=True)
            probs[h] = p.astype(bf16)
        for h in heads:
            sl = slice((h // 2) * LANES, (h // 2 + 1) * LANES)
            outs.append(_dot(probs[h], vb_ref[:, sl]) / denoms[h])
    ya = jnp.concatenate([jnp.where(low, outs[2 * hp], outs[2 * hp + 1])
                          for hp in range(ATT_HEADS // 2)], axis=-1)
    o_ref[0] = _rms(ya, ga_ref[...]).astype(bf16)


def _attn_out(q, k, v, bias, ga):
    B, S, _ = q.shape
    tq = Q_TILE
    full = lambda shape: pl.BlockSpec(shape, lambda b, j: (0,) * len(shape))
    tile = lambda w: pl.BlockSpec((1, tq, w), lambda b, j: (b, j, 0))
    seq = pl.BlockSpec((1, S, D_ATT), lambda b, j: (b, 0, 0))
    return pl.pallas_call(
        _attn_out_kernel,
        grid=(B, S // tq),
        in_specs=[tile(D_ATT), seq, seq,
                  pl.BlockSpec((1,) + bias.shape[1:], lambda b, j: (jnp.minimum(j, START_TILES), 0, 0, 0)),
                  full((1, D_ATT))],
        out_specs=tile(D_ATT),
        out_shape=jax.ShapeDtypeStruct((B, S, D_ATT), bf16),
        scratch_shapes=[pltpu.VMEM((K_BAND, D_ATT), bf16), pltpu.VMEM((K_BAND, D_ATT), bf16)],
        compiler_params=pltpu.CompilerParams(
            dimension_semantics=("parallel", "parallel"), vmem_limit_bytes=VMEM_LIMIT),
        name="attn_out",
    )(q, k, v, bias, ga)


def _mem_kv_kernel(m_ref, g_ref, w_ref, kg_ref, k_ref, v_ref):
    hm = _rms(m_ref[...], g_ref[...]).astype(bf16)
    kv = _dot(hm, w_ref[...])
    for h in range(X_HEADS):
        sl = slice(h * X_HEAD_DIM, (h + 1) * X_HEAD_DIM)
        k_ref[:, sl] = _rms(kv[:, sl], kg_ref[...]).astype(bf16)
    v_ref[...] = kv[:, D_X:].astype(bf16)


def _mem_kv(mem, g, w_xkv, kg):
    B = mem.shape[0]
    rows = B * N_MEM
    tile = min(SEQ_TILE, rows)
    full = lambda shape: pl.BlockSpec(shape, lambda i: (0,) * len(shape))
    out = jax.ShapeDtypeStruct((rows, D_X), bf16)
    kx, vx = pl.pallas_call(
        _mem_kv_kernel,
        grid=(rows // tile,),
        in_specs=[pl.BlockSpec((tile, D_MODEL), lambda i: (i, 0)), full((1, D_MODEL)),
                  full(w_xkv.shape), full((1, X_HEAD_DIM))],
        out_specs=[pl.BlockSpec((tile, D_X), lambda i: (i, 0))] * 2,
        out_shape=[out, out],
        compiler_params=pltpu.CompilerParams(
            dimension_semantics=("parallel",), vmem_limit_bytes=VMEM_LIMIT),
        name="mem_kv",
    )(mem.reshape(rows, D_MODEL), g, w_xkv, kg)
    return kx.reshape(B, N_MEM, D_X), vx.reshape(B, N_MEM, D_X)


def _xattn_router_kernel(x_ref, ya_ref, yb_ref, wmix_ref, gx_ref, wq_ref, qg_ref, k_ref, v_ref, wo_ref,
                         gf_ref,
                         wr_ref, br_ref, earlier_ref,
                         x2_ref, hf_ref, route_ref, cnt_ref, post_ref):
    x1 = x_ref[0] + _dot(ya_ref[0], wmix_ref[0:D_ATT, :]) + _dot(yb_ref[0], wmix_ref[D_ATT:, :])
    hb = _rms(x1, gx_ref[...]).astype(bf16)
    q = _dot(hb, wq_ref[...])
    hsl = [slice(h * X_HEAD_DIM, (h + 1) * X_HEAD_DIM) for h in range(X_HEADS)]
    scores = [_dot_nt(_rms(q[:, sl], qg_ref[...]).astype(bf16), k_ref[0, :, sl]) for sl in hsl]
    probs, denoms = [], []
    for s in scores:
        p = jnp.exp2(s - jnp.max(s, axis=-1, keepdims=True))
        denoms.append(jnp.sum(p, axis=-1, keepdims=True))
        probs.append(p.astype(bf16))
    heads = [(_dot(p, v_ref[0, :, sl]) / l).astype(bf16) for p, l, sl in zip(probs, denoms, hsl)]
    o = jnp.concatenate(heads, axis=-1)
    x2 = x1 + _dot(o, wo_ref[...])
    x2_ref[0] = x2

    hf = _rms(x2, gf_ref[...])
    hf_hi = hf.astype(bf16)
    hf_ref[0] = hf_hi
    hf_lo = (hf - hf_hi.astype(f32)).astype(bf16)
    hi_prod = _dot(hf_hi, wr_ref[...])
    logits = (hi_prod[:, :LANES] + hi_prod[:, LANES:] + _dot(hf_lo, wr_ref[:, :LANES])
              + br_ref[...])
    lt = logits.T[:N_EXPERTS]
    eidx = lax.broadcasted_iota(jnp.int32, lt.shape, 0)
    vals, idxs = [], []
    for _ in range(TOP_K):
        m = jnp.max(lt, axis=0, keepdims=True)
        i = jnp.min(jnp.where(lt == m, eidx, N_EXPERTS), axis=0, keepdims=True)
        lt = jnp.where(eidx == i, -jnp.inf, lt)
        vals.append(m)
        idxs.append(i)
    es = [jnp.exp(v - vals[0]) for v in vals]
    tot = es[0] + es[1] + es[2] + es[3]
    sel = jnp.zeros(lt.shape, f32)
    for i in idxs:
        sel = jnp.where(eidx == i, 1.0, sel)
    n_grp = lt.shape[1] // GROUP
    grp = lambda j: slice(j * GROUP, (j + 1) * GROUP)
    lane = lax.broadcasted_iota(jnp.int32, (N_EXPERTS, LANES), 1)
    cnt = jnp.zeros((N_EXPERTS, LANES), f32)
    for j in range(n_grp):
        cnt = jnp.where(lane == j, jnp.sum(sel[:, grp(j)], axis=1, keepdims=True), cnt)
    cnt_ref[0] = cnt
    cell_rows = jnp.floor((cnt + (SUBLANES - 1)) * (1.0 / SUBLANES)) * SUBLANES
    below = (lax.broadcasted_iota(jnp.int32, (N_EXPERTS, N_EXPERTS), 0)
             > lax.broadcasted_iota(jnp.int32, (N_EXPERTS, N_EXPERTS), 1))
    coff = _dot(jnp.where(below, 1.0, 0.0).astype(bf16), cell_rows.astype(bf16))
    posmat = jnp.concatenate(
        [coff[:, j:j + 1] + _dot(sel[:, grp(j)].astype(bf16), earlier_ref[...]) for j in range(n_grp)], axis=1)
    pos = [jnp.sum(jnp.where(eidx == i, posmat, 0.0), axis=0, keepdims=True) for i in idxs]
    post_ref[0] = jnp.concatenate(pos + [jnp.zeros((SUBLANES - TOP_K, lt.shape[1]), f32)], axis=0)
    packed = jnp.concatenate([i.astype(f32) for i in idxs] + [e / tot for e in es] + pos
                             + [jnp.zeros((LANES - 3 * TOP_K, lt.shape[1]), f32)], axis=0)
    route_ref[0] = packed.T


def _xattn_router(x, ya, yb, w_mix, gx, w_xq, qg, kx, vx, w_xo, gf, wr, br):
    B, S, D = x.shape
    ts = SEQ_TILE
    full = lambda shape: pl.BlockSpec(shape, lambda b, j: (0,) * len(shape))
    tile = lambda w: pl.BlockSpec((1, ts, w), lambda b, j: (b, j, 0))
    mem = pl.BlockSpec((1, N_MEM, D_X), lambda b, j: (b, 0, 0))
    return pl.pallas_call(
        _xattn_router_kernel,
        grid=(B, S // ts),
        in_specs=[tile(D), tile(D_ATT), tile(D_CONV), full(w_mix.shape),
                  full((1, D)), full(w_xq.shape), full((1, X_HEAD_DIM)), mem, mem,
                  full(w_xo.shape), full((1, D)), full(wr.shape),
                  full((1, LANES)), full((GROUP, GROUP))],
        out_specs=[tile(D), tile(D), tile(LANES),
                   pl.BlockSpec((1, N_EXPERTS, LANES), lambda b, j: (b * (S // ts) + j, 0, 0)),
                   pl.BlockSpec((1, SUBLANES, ts), lambda b, j: (b * (S // ts) + j, 0, 0))],
        out_shape=[jax.ShapeDtypeStruct((B, S, D), f32), jax.ShapeDtypeStruct((B, S, D), bf16),
                   jax.ShapeDtypeStruct((B, S, LANES), f32),
                   jax.ShapeDtypeStruct((B * S // ts, N_EXPERTS, LANES), f32),
                   jax.ShapeDtypeStruct((B * S // ts, SUBLANES, ts), f32)],
        compiler_params=pltpu.CompilerParams(
            dimension_semantics=("parallel", "parallel"), vmem_limit_bytes=VMEM_LIMIT),
        name="xattn_router",
    )(x, ya, yb, w_mix, gx, w_xq, qg, kx, vx, w_xo, gf, wr, br,
      jnp.asarray(np.triu(np.ones((GROUP, GROUP)), 1), bf16))


u32 = jnp.uint32
HIGH_HALF = 0xFFFF0000


def _pack_pairs(x):
    w = x.shape[1] // 2
    lo = pltpu.bitcast(x[:, :w], u32) >> 16
    hi = pltpu.bitcast(x[:, w:], u32) & u32(HIGH_HALF)
    return lo | hi


def _unpack_pairs(u):
    lo = pltpu.bitcast(u << 16, f32)
    hi = pltpu.bitcast(u & u32(HIGH_HALF), f32)
    return jnp.concatenate([lo, hi], axis=-1).astype(bf16)


def _for_cells(g, fn):
    def body(j, carry):
        for u in range(CELL_UNROLL):
            fn(g * N_EXPERTS + j * CELL_UNROLL + u)
        return carry
    lax.fori_loop(0, N_EXPERTS // CELL_UNROLL, body, 0)


def _dispatch_kernel(coff_s, rows_s, gbase_s, gtot_s, toff_s, tn_s, nu_s,
                     hf_ref, post_ref, xs_hbm, xbuf, zbuf, sem, zsem, *, n_steps):
    step = pl.program_id(0)
    places = range(STEP_GROUPS)
    group = lambda i: step * STEP_GROUPS + i
    slot = lambda i: lax.rem(step, 2) * STEP_GROUPS + i
    tok = lambda i: slice(i * GROUP, (i + 1) * GROUP)

    def cell_copy(sl, i):
        n = pl.multiple_of(rows_s[i], SUBLANES)
        src = pl.multiple_of(coff_s[i], SUBLANES)
        dst = pl.multiple_of(gbase_s[i], SUBLANES)
        return n, pltpu.make_async_copy(xbuf.at[sl, pl.ds(src, n)], xs_hbm.at[pl.ds(dst, n)],
                                        sem.at[sl])

    def start_cell(sl, i):
        n, cp = cell_copy(sl, i)
        pl.when(n > 0)(cp.start)

    def wait_group(sl, gg):
        n = pl.multiple_of(gtot_s[gg], SUBLANES)
        pltpu.make_async_copy(xbuf.at[sl, pl.ds(0, n)], xs_hbm.at[pl.ds(0, n)], sem.at[sl]).wait()

    @pl.when(step >= 2)
    def _():
        for i in places:
            wait_group(slot(i), group(i) - 2 * STEP_GROUPS)

    riota = lax.broadcasted_iota(jnp.int32, (GROUP_ROWS, GROUP), 0)
    ps = []
    for i in places:
        p = jnp.zeros((GROUP_ROWS, GROUP), f32)
        for kk in range(TOP_K):
            p = jnp.where(riota == post_ref[0, kk:kk + 1, tok(i)].astype(jnp.int32), 1.0, p)
        ps.append(p.astype(bf16))
    rows = [_dot(ps[i], hf_ref[tok(i), :]) for i in places]
    for i in places:
        xbuf[slot(i)] = _pack_pairs(rows[i])
    for i in places:
        _for_cells(group(i), functools.partial(start_cell, slot(i)))

    @pl.when(step == n_steps - 1)
    def _():
        if n_steps >= 2:
            for i in places:
                prev_slot = (1 - lax.rem(step, 2)) * STEP_GROUPS + i
                wait_group(prev_slot, group(i) - STEP_GROUPS)
        for i in places:
            wait_group(slot(i), group(i))
        zbuf[...] = jnp.zeros(zbuf.shape, u32)

        def tail_copy(e):
            n = pl.multiple_of(tn_s[e], SUBLANES)
            dst = pl.multiple_of(toff_s[e], SUBLANES)
            return n, pltpu.make_async_copy(zbuf.at[pl.ds(0, n)], xs_hbm.at[pl.ds(dst, n)], zsem)

        def unused_copy(b):
            dst = pl.multiple_of(b * EXPERT_TILE, EXPERT_TILE)
            return pltpu.make_async_copy(zbuf, xs_hbm.at[pl.ds(dst, EXPERT_TILE)], zsem)

        def run(method):
            def tail(e, carry):
                n, cp = tail_copy(e)
                pl.when(n > 0)(getattr(cp, method))
                return carry
            lax.fori_loop(0, N_EXPERTS, tail, 0)

            def unused(b, carry):
                getattr(unused_copy(b), method)()
                return carry
            lax.fori_loop(nu_s[0], xs_hbm.shape[0] // EXPERT_TILE, unused, 0)
        run("start")
        run("wait")


def _dispatch(plan, hf, post, n_rows):
    n_tok, D = hf.shape
    n_steps = n_tok // (GROUP * STEP_GROUPS)
    assert n_steps * GROUP * STEP_GROUPS == n_tok
    step_tok = STEP_GROUPS * GROUP
    per_tile = SEQ_TILE // step_tok
    grid_spec = pltpu.PrefetchScalarGridSpec(
        num_scalar_prefetch=7,
        grid=(n_steps,),
        in_specs=[pl.BlockSpec((step_tok, D), lambda s, *_: (s, 0)),
                  pl.BlockSpec((1, SUBLANES, step_tok), lambda s, *_: (s // per_tile, 0, s % per_tile))],
        out_specs=pl.BlockSpec(memory_space=pl.ANY),
        scratch_shapes=[pltpu.VMEM((2 * STEP_GROUPS, GROUP_ROWS, D // 2), u32),
                        pltpu.VMEM((EXPERT_TILE, D // 2), u32),
                        pltpu.SemaphoreType.DMA((2 * STEP_GROUPS,)), pltpu.SemaphoreType.DMA(())],
    )
    return pl.pallas_call(
        functools.partial(_dispatch_kernel, n_steps=n_steps),
        grid_spec=grid_spec,
        out_shape=jax.ShapeDtypeStruct((n_rows, D // 2), u32),
        compiler_params=pltpu.CompilerParams(
            dimension_semantics=("arbitrary",), vmem_limit_bytes=VMEM_LIMIT),
        name="dispatch",
    )(plan["coff"], plan["rows"], plan["gbase"], plan["gtot"], plan["tailoff"], plan["tailn"],
      plan["n_used"], hf, post)


def _expert_kernel(eb_ref, nb_ref, nu_ref, wgu_ref, bgu_ref, wdn_ref, bdn_ref, pm_ref, xs_hbm,
                   y_hbm, wgu_b, wdn_b, xin, yout, in_sem, out_sem):
    e = pl.program_id(0)
    tm = EXPERT_TILE
    chunk_rows = EXPERT_CHUNK * tm
    n_used = nu_ref[0]
    n_chunks = (n_used + EXPERT_CHUNK - 1) // EXPERT_CHUNK

    def in_copy(c):
        sl = lax.rem(c, 2)
        rows = pl.ds(pl.multiple_of(c * chunk_rows, chunk_rows), chunk_rows)
        return pltpu.make_async_copy(xs_hbm.at[rows], xin.at[sl], in_sem.at[sl])

    def out_copy(c):
        sl = lax.rem(c, 2)
        rows = pl.ds(pl.multiple_of(c * chunk_rows, chunk_rows), chunk_rows)
        return pltpu.make_async_copy(yout.at[sl], y_hbm.at[rows], out_sem.at[sl])

    @pl.when(e == 0)
    def _():
        in_copy(0).start()
        yout[...] = jnp.zeros(yout.shape, u32)

    @pl.when(nb_ref[e] > 0)
    def _():
        for c in range(2 * D_FF // GU_TILE):
            cols = slice(c * GU_TILE, (c + 1) * GU_TILE)
            wgu_b[:, cols] = _dot(wgu_ref[0, :, cols].astype(bf16), pm_ref[...]).astype(bf16)
        for c in range(D_FF // LANES):
            rows = slice(c * LANES, (c + 1) * LANES)
            wdn_b[rows, :] = wdn_ref[0, rows, :].astype(bf16)

        def item(b, n_blk):
            c = b // EXPERT_CHUNK
            j = lax.rem(b, EXPERT_CHUNK)
            sl = lax.rem(c, 2)

            @pl.when(j == 0)
            def _():
                in_copy(c).wait()

                @pl.when(c + 1 < n_chunks)
                def _():
                    in_copy(c + 1).start()

                @pl.when(c >= 2)
                def _():
                    out_copy(c - 2).wait()

            rows = pl.ds(pl.multiple_of(j * tm, tm), n_blk * tm)
            _expert_block(xin.at[sl, rows], bgu_ref, bdn_ref, yout.at[sl, rows], wgu_b, wdn_b)
            last = b + n_blk - 1

            @pl.when((lax.rem(last, EXPERT_CHUNK) == EXPERT_CHUNK - 1) | (last == n_used - 1))
            def _():
                out_copy(c).start()

        first = eb_ref[e]
        n_head = jnp.minimum(lax.rem(EXPERT_CHUNK - lax.rem(first, EXPERT_CHUNK), EXPERT_CHUNK),
                             nb_ref[e])
        n_whole = (nb_ref[e] - n_head) // EXPERT_CHUNK
        n_tail = nb_ref[e] - n_head - n_whole * EXPERT_CHUNK

        def single(b, carry):
            item(b, 1)
            return carry

        def whole(i, carry):
            item(first + n_head + i * EXPERT_CHUNK, EXPERT_CHUNK)
            return carry
        lax.fori_loop(first, first + n_head, single, 0)
        lax.fori_loop(0, n_whole, whole, 0)
        tail0 = first + n_head + n_whole * EXPERT_CHUNK
        lax.fori_loop(tail0, tail0 + n_tail, single, 0)

    @pl.when(e == pl.num_programs(0) - 1)
    def _():
        @pl.when(n_chunks >= 2)
        def _():
            out_copy(n_chunks - 2).wait()
        out_copy(n_chunks - 1).wait()
        yout[0] = jnp.zeros(yout.shape[1:], u32)

        def run(method):
            def unused(c, carry):
                rows = pl.ds(pl.multiple_of(c * chunk_rows, chunk_rows), chunk_rows)
                getattr(pltpu.make_async_copy(yout.at[0], y_hbm.at[rows], out_sem.at[0]), method)()
                return carry
            lax.fori_loop(n_chunks, y_hbm.shape[0] // chunk_rows, unused, 0)
        run("start")
        run("wait")


def _expert_block(x_ref, bgu_ref, bdn_ref, y_ref, wgu_b, wdn_b):
    gu = _dot(_unpack_pairs(x_ref[...]), wgu_b[...]) + bgu_ref[0]
    acts = []
    for c in range(2 * D_FF // GU_TILE):
        g = jnp.minimum(gu[:, c * GU_TILE:c * GU_TILE + LANES], SWIGLU_LIMIT)
        lin = jnp.clip(gu[:, c * GU_TILE + LANES:(c + 1) * GU_TILE], -SWIGLU_LIMIT, SWIGLU_LIMIT)
        acts.append((g * jax.nn.sigmoid(SWIGLU_ALPHA * g) * (lin + 1.0)).astype(bf16))
    act = jnp.concatenate(acts, axis=-1)
    y = _dot(act, wdn_b[...]) + bdn_ref[0]
    y_ref[...] = _pack_pairs(y.astype(bf16).astype(f32))


def _gate_up_perm():
    j = np.arange(GU_TILE)
    p = np.zeros((GU_TILE, GU_TILE), np.float32)
    p[np.where(j < LANES, 2 * j, 2 * (j - LANES) + 1), j] = 1.0
    return jnp.asarray(p, bf16)


def _expert_mlp(plan, xs, w_gu, b_gu, w_dn, b_dn):
    n_rows = xs.shape[0]
    D = D_MODEL
    tm = EXPERT_TILE
    per_e = lambda e, *_: (e, 0, 0)
    grid_spec = pltpu.PrefetchScalarGridSpec(
        num_scalar_prefetch=3,
        grid=(N_EXPERTS,),
        in_specs=[pl.BlockSpec((1, D, 2 * D_FF), per_e),
                  pl.BlockSpec((1, 1, 2 * D_FF), per_e),
                  pl.BlockSpec((1, D_FF, D), per_e),
                  pl.BlockSpec((1, 1, D), per_e),
                  pl.BlockSpec((GU_TILE, GU_TILE), lambda e, *_: (0, 0)),
                  pl.BlockSpec(memory_space=pl.ANY)],
        out_specs=pl.BlockSpec(memory_space=pl.ANY),
        scratch_shapes=[pltpu.VMEM((D, 2 * D_FF), bf16), pltpu.VMEM((D_FF, D), bf16),
                        pltpu.VMEM((2, EXPERT_CHUNK * tm, D // 2), u32),
                        pltpu.VMEM((2, EXPERT_CHUNK * tm, D // 2), u32),
                        pltpu.SemaphoreType.DMA((2,)), pltpu.SemaphoreType.DMA((2,))],
    )
    return pl.pallas_call(
        _expert_kernel,
        grid_spec=grid_spec,
        out_shape=jax.ShapeDtypeStruct((n_rows, D // 2), u32),
        compiler_params=pltpu.CompilerParams(
            dimension_semantics=("arbitrary",), vmem_limit_bytes=VMEM_LIMIT),
        name="expert_mlp",
    )(plan["eblk"], plan["nblk"], plan["n_used"], w_gu, b_gu, w_dn, b_dn, _gate_up_perm(), xs)


def _combine_kernel(coff_s, rows_s, gbase_s, gtot_s, x_ref, route_ref, y_hbm, o_ref, ybuf, sem,
                    *, n_steps):
    step = pl.program_id(0)
    places = range(STEP_GROUPS)
    group = lambda s, i: s * STEP_GROUPS + i
    slot = lambda s, i: lax.rem(s, 2) * STEP_GROUPS + i
    tok = lambda i: slice(i * GROUP, (i + 1) * GROUP)

    def cell_copy(sl, i):
        n = pl.multiple_of(rows_s[i], SUBLANES)
        loc = pl.multiple_of(coff_s[i], SUBLANES)
        src = pl.multiple_of(gbase_s[i], SUBLANES)
        return n, pltpu.make_async_copy(y_hbm.at[pl.ds(src, n)], ybuf.at[sl, pl.ds(loc, n)],
                                        sem.at[sl])

    def start_cell(sl, i):
        n, cp = cell_copy(sl, i)
        pl.when(n > 0)(cp.start)

    def wait_group(sl, gg):
        n = pl.multiple_of(gtot_s[gg], SUBLANES)
        pltpu.make_async_copy(y_hbm.at[pl.ds(0, n)], ybuf.at[sl, pl.ds(0, n)], sem.at[sl]).wait()

    @pl.when(step == 0)
    def _():
        ybuf[...] = jnp.zeros(ybuf.shape, u32)
        for i in places:
            _for_cells(group(step, i), functools.partial(start_cell, slot(step, i)))

    @pl.when(step + 1 < n_steps)
    def _():
        for i in places:
            _for_cells(group(step + 1, i), functools.partial(start_cell, slot(step + 1, i)))

    for i in places:
        wait_group(slot(step, i), group(step, i))

    riota = lax.broadcasted_iota(jnp.int32, (GROUP, GROUP_ROWS), 1)
    pts = []
    for i in places:
        pt = jnp.zeros((GROUP, GROUP_ROWS), f32)
        for kk in range(TOP_K):
            pk = route_ref[tok(i), 2 * TOP_K + kk:2 * TOP_K + kk + 1].astype(jnp.int32)
            pt = jnp.where(riota == pk, route_ref[tok(i), TOP_K + kk:TOP_K + kk + 1], pt)
        pts.append(pt.astype(bf16))
    ys = [_unpack_pairs(ybuf[slot(step, i)]) for i in places]
    outs = [_dot(pts[i], ys[i]) for i in places]
    for i in places:
        o_ref[tok(i), :] = x_ref[tok(i), :] + outs[i]


def _combine(plan, x2, route, y_rows):
    n_tok, D = x2.shape
    n_steps = n_tok // (GROUP * STEP_GROUPS)
    assert n_steps * GROUP * STEP_GROUPS == n_tok
    tok = lambda w: pl.BlockSpec((STEP_GROUPS * GROUP, w), lambda s, *_: (s, 0))
    grid_spec = pltpu.PrefetchScalarGridSpec(
        num_scalar_prefetch=4,
        grid=(n_steps,),
        in_specs=[tok(D), tok(LANES), pl.BlockSpec(memory_space=pl.ANY)],
        out_specs=tok(D),
        scratch_shapes=[pltpu.VMEM((2 * STEP_GROUPS, GROUP_ROWS, D // 2), u32),
                        pltpu.SemaphoreType.DMA((2 * STEP_GROUPS,))],
    )
    return pl.pallas_call(
        functools.partial(_combine_kernel, n_steps=n_steps),
        grid_spec=grid_spec,
        out_shape=jax.ShapeDtypeStruct((n_tok, D), f32),
        compiler_params=pltpu.CompilerParams(
            dimension_semantics=("arbitrary",), vmem_limit_bytes=VMEM_LIMIT),
        name="combine",
    )(plan["coff"], plan["rows"], plan["gbase"], plan["gtot"], x2, route, y_rows)


def _routing_plan(cnt, n_tok):
    tm = EXPERT_TILE
    n_groups = n_tok // GROUP
    per_tile = SEQ_TILE // GROUP
    cnt = cnt[:, :, :per_tile].transpose(0, 2, 1).reshape(n_groups, N_EXPERTS).astype(jnp.int32)
    rows = (cnt + SUBLANES - 1) // SUBLANES * SUBLANES
    coff = jnp.cumsum(rows, axis=1) - rows
    tot = jnp.sum(rows, axis=0)
    padded = (tot + tm - 1) // tm * tm
    pend = jnp.cumsum(padded)
    ebase = pend - padded
    gbase = ebase[None, :] + jnp.cumsum(rows, axis=0) - rows
    flat = lambda a: a.reshape(-1).astype(jnp.int32)
    return dict(
        coff=flat(coff), rows=flat(rows), gbase=flat(gbase), gtot=flat(jnp.sum(rows, axis=1)),
        tailoff=flat(ebase + tot), tailn=flat(padded - tot),
        eblk=flat(ebase // tm), nblk=flat(padded // tm), n_used=flat(pend[-1] // tm))


def _max_rows(n_tok):
    tm = EXPERT_TILE
    worst = (n_tok * TOP_K + (n_tok // GROUP) * N_EXPERTS * (SUBLANES - 1)
             + N_EXPERTS * (tm - SUBLANES))
    chunk = EXPERT_CHUNK * tm
    return (worst + chunk - 1) // chunk * chunk


def _moe(x2, hf, route, post, cnt, w_gate_up, b_gate_up, w_down, b_down):
    n_tok, D = x2.shape
    plan = _routing_plan(cnt, n_tok)
    xs = _dispatch(plan, hf, post, _max_rows(n_tok))
    b_gu = (b_gate_up.reshape(N_EXPERTS, 2 * D_FF // GU_TILE, LANES, 2).transpose(0, 1, 3, 2)
            .reshape(N_EXPERTS, 1, 2 * D_FF))
    y_rows = _expert_mlp(plan, xs, w_gate_up, b_gu, w_down, b_down.reshape(N_EXPERTS, 1, D))
    return _combine(plan, x2, route, y_rows)


BIAS_ROW = (2 * Q_TILE + LEFT_CHUNKS * CHUNK + LANES - 2) // LANES * LANES


def _band_bias_kernel(r_ref, o_ref):
    qc = lax.broadcasted_iota(jnp.int32, (Q_TILE, K_BAND), 0) // CHUNK
    kc = lax.broadcasted_iota(jnp.int32, (Q_TILE, K_BAND), 1) // CHUNK
    first_chunk = LEFT_CHUNKS - pl.program_id(0) * (Q_TILE // CHUNK)
    visible = (kc >= qc) & (kc <= qc + LEFT_CHUNKS) & (kc >= first_chunk)
    for h in range(r_ref.shape[0]):
        x = jnp.broadcast_to(r_ref[h], (Q_TILE, BIAS_ROW))
        x = pltpu.roll(x, BIAS_ROW - (Q_TILE - 1), axis=1, stride=1, stride_axis=0)
        o_ref[0, h] = jnp.where(visible, x[:, :K_BAND] * LOG2E, NEG_INF)


def _band_bias(rel_bias):
    H = rel_bias.shape[0]
    d_max = Q_TILE - 1 + LEFT_CHUNKS * CHUNK
    n_far = d_max - MAX_REL + 1
    r = jnp.concatenate(
        [jnp.broadcast_to(rel_bias[:, 2 * MAX_REL:], (H, n_far)),
         rel_bias[:, MAX_REL - (Q_TILE - 1):2 * MAX_REL][:, ::-1],
         jnp.zeros((H, BIAS_ROW - (2 * Q_TILE - 1 + LEFT_CHUNKS * CHUNK)), rel_bias.dtype)], axis=1)
    return pl.pallas_call(
        _band_bias_kernel,
        grid=(START_TILES + 1,),
        in_specs=[pl.BlockSpec((H, 1, BIAS_ROW), lambda t: (0, 0, 0))],
        out_specs=pl.BlockSpec((1, H, Q_TILE, K_BAND), lambda t: (t, 0, 0, 0)),
        out_shape=jax.ShapeDtypeStruct((START_TILES + 1, H, Q_TILE, K_BAND), f32),
        name="band_bias",
    )(r.astype(f32).reshape(H, 1, BIAS_ROW))


def _layer(x, mem, norm_mix_g, w_in, q_norm_g, k_norm_g, rel_bias, conv_w, out_norm_a_g,
           out_norm_b_g, w_out, norm_xattn_g, norm_mem_g, w_xq, w_xkv, xq_norm_g, xk_norm_g,
           w_xo, norm_ffn_g, w_router, b_router, w_gate_up, b_gate_up, w_down, b_down):
    B, S, D = x.shape
    n_tok = B * S
    row = lambda a: a.reshape(1, -1).astype(f32)

    qg = row(jnp.tile(q_norm_g, ATT_HEADS) * (ATT_HEAD_DIM ** -0.5 * LOG2E))
    kg = row(jnp.tile(k_norm_g, ATT_HEADS))
    head_ones = jnp.asarray(np.kron(np.eye(ATT_HEADS), np.ones((ATT_HEAD_DIM, ATT_HEAD_DIM))), bf16)
    q, k, v, yb = _mix_in(x, row(norm_mix_g), w_in.astype(bf16), qg, kg, head_ones,
                          conv_w.astype(f32), row(out_norm_b_g))
    ya = _attn_out(q, k, v, _band_bias(rel_bias), row(out_norm_a_g))

    kx, vx = _mem_kv(mem, row(norm_mem_g), w_xkv.astype(bf16), row(xk_norm_g))
    wr = jnp.zeros((D, LANES), f32).at[:, :N_EXPERTS].set(w_router)
    wr_hi = wr.astype(bf16)
    wr_lo = (wr - wr_hi.astype(f32)).astype(bf16)
    br = jnp.full((1, LANES), -jnp.inf, f32).at[0, :N_EXPERTS].set(b_router)
    x2, hf, route, cnt, post = _xattn_router(
        x, ya, yb, w_out.astype(bf16), row(norm_xattn_g), w_xq.astype(bf16), row(xq_norm_g * (X_HEAD_DIM ** -0.5 * LOG2E)),
        kx, vx, w_xo.astype(bf16), row(norm_ffn_g), jnp.concatenate([wr_hi, wr_lo], axis=1), br)

    out = _moe(x2.reshape(n_tok, D), hf.reshape(n_tok, D), route.reshape(n_tok, LANES), post, cnt,
               w_gate_up, b_gate_up, w_down, b_down)
    return out.reshape(B, S, D)


def kernel(x, mem, norm_mix_g, w_in, q_norm_g, k_norm_g, rel_bias, conv_w, out_norm_a_g,
           out_norm_b_g, w_out, norm_xattn_g, norm_mem_g, w_xq, w_xkv, xq_norm_g, xk_norm_g,
           w_xo, norm_ffn_g, w_router, b_router, w_gate_up, b_gate_up, w_down, b_down):
    depth = norm_mix_g.shape[0]
    for l in range(depth):
        x = _layer(x, mem, norm_mix_g[l], w_in[l], q_norm_g[l], k_norm_g[l], rel_bias[l],
                   conv_w[l], out_norm_a_g[l], out_norm_b_g[l], w_out[l], norm_xattn_g[l],
                   norm_mem_g[l], w_xq[l], w_xkv[l], xq_norm_g[l], xk_norm_g[l], w_xo[l],
                   norm_ffn_g[l], w_router[l], b_router[l], w_gate_up[l], b_gate_up[l],
                   w_down[l], b_down[l])
    return x
```

```python
import functools

import jax
import jax.numpy as jnp
import numpy as np
from jax import lax
from jax.experimental import pallas as pl
from jax.experimental.pallas import tpu as pltpu

D_MODEL = 1024
CHUNK = 64
LEFT_CHUNKS = 8
N_MEM = 256
ATT_HEADS = 8
ATT_HEAD_DIM = 64
D_ATT = ATT_HEADS * ATT_HEAD_DIM
D_CONV = 512
MAX_REL = 256
X_HEADS = 4
X_HEAD_DIM = 128
D_X = X_HEADS * X_HEAD_DIM
N_EXPERTS = 32
TOP_K = 4
D_FF = D_MODEL
SWIGLU_LIMIT = 7.0
SWIGLU_ALPHA = 1.702
EPS = 1e-6
NEG_INF = -1e30

LANES = 128
SEQ_TILE = 1024
Q_TILE = 256
K_BAND = Q_TILE + LEFT_CHUNKS * CHUNK
START_TILES = LEFT_CHUNKS * CHUNK // Q_TILE
LOG2E = 1.4426950408889634
HEAD_ROUND = 4
EXPERT_TILE = 512
EXPERT_CHUNK = 2
MXU_TILE = 2 * LANES
GU_TILE = MXU_TILE
GROUP = 256
CELL_UNROLL = 4
STEP_GROUPS = 4
SUBLANES = 8
GROUP_ROWS = -(-(GROUP * TOP_K + N_EXPERTS * (SUBLANES - 1)) // MXU_TILE) * MXU_TILE
VMEM_LIMIT = 56 * 1024 * 1024

bf16 = jnp.bfloat16
f32 = jnp.float32


def _rms(x, g):
    return x * lax.rsqrt(jnp.mean(x * x, axis=-1, keepdims=True) + EPS) * g


def _dot(a, b):
    return jnp.dot(a, b, preferred_element_type=f32)


def _dot_nt(a, b):
    return lax.dot_general(a, b, (((1,), (1,)), ((), ())), preferred_element_type=f32)


def _mix_in_kernel(x_ref, g_ref, w_ref, qg_ref, kg_ref, hm_ref, cw_ref, gb_ref,
                   q_ref, k_ref, v_ref, yb_ref, ub_ref):
    ts = x_ref.shape[1]

    @pl.when(pl.program_id(1) == 0)
    def _():
        ub_ref[0:8, :] = jnp.zeros((8, D_CONV), f32)

    hb = _rms(x_ref[0], g_ref[...]).astype(bf16)

    def head_norm(t, gain):
        ms = _dot((t * t).astype(bf16), hm_ref[...]) * (1.0 / ATT_HEAD_DIM)
        return t * lax.rsqrt(ms + EPS) * gain

    o = 3 * D_ATT
    bg = _dot(hb, w_ref[:, o:o + D_CONV])
    cg = _dot(hb, w_ref[:, o + D_CONV:o + 2 * D_CONV])
    xv = _dot(hb, w_ref[:, o + 2 * D_CONV:o + 3 * D_CONV])
    u = cg * xv
    ub_ref[8:8 + ts, :] = u
    conv = (cw_ref[0:1, :] * ub_ref[6:6 + ts, :] + cw_ref[1:2, :] * ub_ref[7:7 + ts, :]
            + cw_ref[2:3, :] * u)
    ub_ref[0:8, :] = u[ts - 8:ts, :]
    yb_ref[0] = _rms(bg * conv, gb_ref[...]).astype(bf16)

    q = _dot(hb, w_ref[:, 0:D_ATT])
    q_ref[0] = head_norm(q, qg_ref[...]).astype(bf16)
    k = _dot(hb, w_ref[:, D_ATT:2 * D_ATT])
    k_ref[0] = head_norm(k, kg_ref[...]).astype(bf16)
    v_ref[0] = _dot(hb, w_ref[:, 2 * D_ATT:3 * D_ATT]).astype(bf16)


def _mix_in(x, g, w_in, qg, kg, hm, cw, gb):
    B, S, D = x.shape
    ts = SEQ_TILE
    full = lambda shape: pl.BlockSpec(shape, lambda b, j: (0,) * len(shape))
    tile = lambda w: pl.BlockSpec((1, ts, w), lambda b, j: (b, j, 0))
    out = jax.ShapeDtypeStruct((B, S, D_ATT), bf16)
    return pl.pallas_call(
        _mix_in_kernel,
        grid=(B, S // ts),
        in_specs=[tile(D), full((1, D)), full(w_in.shape), full((1, D_ATT)), full((1, D_ATT)),
                  full((D_ATT, D_ATT)), full((3, D_CONV)), full((1, D_CONV))],
        out_specs=[tile(D_ATT)] * 4,
        out_shape=[out] * 4,
        scratch_shapes=[pltpu.VMEM((ts + 8, D_CONV), f32)],
        compiler_params=pltpu.CompilerParams(
            dimension_semantics=("arbitrary", "arbitrary"), vmem_limit_bytes=VMEM_LIMIT),
        name="mix_in",
    )(x, g, w_in, qg, kg, hm, cw, gb)


def _attn_out_kernel(q_ref, k_ref, v_ref, bias_ref, ga_ref, o_ref, kb_ref, vb_ref):
    tq = q_ref.shape[1]
    n_kc = K_BAND // CHUNK
    c0 = pl.program_id(1) * (tq // CHUNK)
    for kc in range(n_kc):
        src = pl.multiple_of(jnp.maximum(c0 - LEFT_CHUNKS + kc, 0) * CHUNK, CHUNK)
        kb_ref[kc * CHUNK:(kc + 1) * CHUNK, :] = k_ref[0, pl.ds(src, CHUNK), :]
        vb_ref[kc * CHUNK:(kc + 1) * CHUNK, :] = v_ref[0, pl.ds(src, CHUNK), :]

    lane = lax.broadcasted_iota(jnp.int32, (1, LANES), 1)
    low = lane < ATT_HEAD_DIM

    outs = []
    for h0 in range(0, ATT_HEADS, HEAD_ROUND):
        heads = range(h0, h0 + HEAD_ROUND)
        scores, probs, denoms = {}, {}, {}
        for h in heads:
            sl = slice((h // 2) * LANES, (h // 2 + 1) * LANES)
            qp = q_ref[0, :, sl]
            keep = low if h % 2 == 0 else jnp.logical_not(low)
            scores[h] = _dot_nt(jnp.where(keep, qp, jnp.zeros_like(qp)), kb_ref[:, sl])
        for h in heads:
            s = scores[h] + bias_ref[0, h]
            p = jnp.exp2(s - jnp.max(s, axis=-1, keepdims=True))
            denoms[h] = jnp.sum(p, axis=-1, keepdims=True)
            probs[h] = p.astype(bf16)
        for h in heads:
            sl = slice((h // 2) * LANES, (h // 2 + 1) * LANES)
            outs.append(_dot(probs[h], vb_ref[:, sl]) / denoms[h])
    ya = jnp.concatenate([jnp.where(low, outs[2 * hp], outs[2 * hp + 1])
                          for hp in range(ATT_HEADS // 2)], axis=-1)
    o_ref[0] = _rms(ya, ga_ref[...]).astype(bf16)


def _attn_out(q, k, v, bias, ga):
    B, S, _ = q.shape
    tq = Q_TILE
    full = lambda shape: pl.BlockSpec(shape, lambda b, j: (0,) * len(shape))
    tile = lambda w: pl.BlockSpec((1, tq, w), lambda b, j: (b, j, 0))
    seq = pl.BlockSpec((1, S, D_ATT), lambda b, j: (b, 0, 0))
    return pl.pallas_call(
        _attn_out_kernel,
        grid=(B, S // tq),
        in_specs=[tile(D_ATT), seq, seq,
                  pl.BlockSpec((1,) + bias.shape[1:], lambda b, j: (jnp.minimum(j, START_TILES), 0, 0, 0)),
                  full((1, D_ATT))],
        out_specs=tile(D_ATT),
        out_shape=jax.ShapeDtypeStruct((B, S, D_ATT), bf16),
        scratch_shapes=[pltpu.VMEM((K_BAND, D_ATT), bf16), pltpu.VMEM((K_BAND, D_ATT), bf16)],
        compiler_params=pltpu.CompilerParams(
            dimension_semantics=("parallel", "parallel"), vmem_limit_bytes=VMEM_LIMIT),
        name="attn_out",
    )(q, k, v, bias, ga)


def _mem_kv_kernel(m_ref, g_ref, w_ref, kg_ref, k_ref, v_ref):
    hm = _rms(m_ref[...], g_ref[...]).astype(bf16)
    kv = _dot(hm, w_ref[...])
    for h in range(X_HEADS):
        sl = slice(h * X_HEAD_DIM, (h + 1) * X_HEAD_DIM)
        k_ref[:, sl] = _rms(kv[:, sl], kg_ref[...]).astype(bf16)
    v_ref[...] = kv[:, D_X:].astype(bf16)


def _mem_kv(mem, g, w_xkv, kg):
    B = mem.shape[0]
    rows = B * N_MEM
    tile = min(SEQ_TILE, rows)
    full = lambda shape: pl.BlockSpec(shape, lambda i: (0,) * len(shape))
    out = jax.ShapeDtypeStruct((rows, D_X), bf16)
    kx, vx = pl.pallas_call(
        _mem_kv_kernel,
        grid=(rows // tile,),
        in_specs=[pl.BlockSpec((tile, D_MODEL), lambda i: (i, 0)), full((1, D_MODEL)),
                  full(w_xkv.shape), full((1, X_HEAD_DIM))],
        out_specs=[pl.BlockSpec((tile, D_X), lambda i: (i, 0))] * 2,
        out_shape=[out, out],
        compiler_params=pltpu.CompilerParams(
            dimension_semantics=("parallel",), vmem_limit_bytes=VMEM_LIMIT),
        name="mem_kv",
    )(mem.reshape(rows, D_MODEL), g, w_xkv, kg)
    return kx.reshape(B, N_MEM, D_X), vx.reshape(B, N_MEM, D_X)


def _xattn_router_kernel(x_ref, ya_ref, yb_ref, wmix_ref, gx_ref, wq_ref, qg_ref, k_ref, v_ref, wo_ref,
                         gf_ref,
                         wr_ref, br_ref, earlier_ref,
                         x2_ref, hf_ref, route_ref, cnt_ref, post_ref):
    x1 = x_ref[0] + _dot(ya_ref[0], wmix_ref[0:D_ATT, :]) + _dot(yb_ref[0], wmix_ref[D_ATT:, :])
    hb = _rms(x1, gx_ref[...]).astype(bf16)
    q = _dot(hb, wq_ref[...])
    hsl = [slice(h * X_HEAD_DIM, (h + 1) * X_HEAD_DIM) for h in range(X_HEADS)]
    scores = [_dot_nt(_rms(q[:, sl], qg_ref[...]).astype(bf16), k_ref[0, :, sl]) for sl in hsl]
    probs, denoms = [], []
    for s in scores:
        p = jnp.exp2(s - jnp.max(s, axis=-1, keepdims=True))
        denoms.append(jnp.sum(p, axis=-1, keepdims=True))
        probs.append(p.astype(bf16))
    heads = [(_dot(p, v_ref[0, :, sl]) / l).astype(bf16) for p, l, sl in zip(probs, denoms, hsl)]
    o = jnp.concatenate(heads, axis=-1)
    x2 = x1 + _dot(o, wo_ref[...])
    x2_ref[0] = x2

    hf = _rms(x2, gf_ref[...])
    hf_hi = hf.astype(bf16)
    hf_ref[0] = hf_hi
    hf_lo = (hf - hf_hi.astype(f32)).astype(bf16)
    hi_prod = _dot(hf_hi, wr_ref[...])
    logits = (hi_prod[:, :LANES] + hi_prod[:, LANES:] + _dot(hf_lo, wr_ref[:, :LANES])
              + br_ref[...])
    lt = logits.T[:N_EXPERTS]
    eidx = lax.broadcasted_iota(jnp.int32, lt.shape, 0)
    vals, idxs = [], []
    for _ in range(TOP_K):
        m = jnp.max(lt, axis=0, keepdims=True)
        i = jnp.min(jnp.where(lt == m, eidx, N_EXPERTS), axis=0, keepdims=True)
        lt = jnp.where(eidx == i, -jnp.inf, lt)
        vals.append(m)
        idxs.append(i)
    es = [jnp.exp(v - vals[0]) for v in vals]
    tot = es[0] + es[1] + es[2] + es[3]
    sel = jnp.zeros(lt.shape, f32)
    for i in idxs:
        sel = jnp.where(eidx == i, 1.0, sel)
    n_grp = lt.shape[1] // GROUP
    grp = lambda j: slice(j * GROUP, (j + 1) * GROUP)
    lane = lax.broadcasted_iota(jnp.int32, (N_EXPERTS, LANES), 1)
    cnt = jnp.zeros((N_EXPERTS, LANES), f32)
    for j in range(n_grp):
        cnt = jnp.where(lane == j, jnp.sum(sel[:, grp(j)], axis=1, keepdims=True), cnt)
    cnt_ref[0] = cnt
    cell_rows = jnp.floor((cnt + (SUBLANES - 1)) * (1.0 / SUBLANES)) * SUBLANES
    below = (lax.broadcasted_iota(jnp.int32, (N_EXPERTS, N_EXPERTS), 0)
             > lax.broadcasted_iota(jnp.int32, (N_EXPERTS, N_EXPERTS), 1))
    coff = _dot(jnp.where(below, 1.0, 0.0).astype(bf16), cell_rows.astype(bf16))
    posmat = jnp.concatenate(
        [coff[:, j:j + 1] + _dot(sel[:, grp(j)].astype(bf16), earlier_ref[...]) for j in range(n_grp)], axis=1)
    pos = [jnp.sum(jnp.where(eidx == i, posmat, 0.0), axis=0, keepdims=True) for i in idxs]
    post_ref[0] = jnp.concatenate(pos + [jnp.zeros((SUBLANES - TOP_K, lt.shape[1]), f32)], axis=0)
    packed = jnp.concatenate([i.astype(f32) for i in idxs] + [e / tot for e in es] + pos
                             + [jnp.zeros((LANES - 3 * TOP_K, lt.shape[1]), f32)], axis=0)
    route_ref[0] = packed.T


def _xattn_router(x, ya, yb, w_mix, gx, w_xq, qg, kx, vx, w_xo, gf, wr, br):
    B, S, D = x.shape
    ts = SEQ_TILE
    full = lambda shape: pl.BlockSpec(shape, lambda b, j: (0,) * len(shape))
    tile = lambda w: pl.BlockSpec((1, ts, w), lambda b, j: (b, j, 0))
    mem = pl.BlockSpec((1, N_MEM, D_X), lambda b, j: (b, 0, 0))
    return pl.pallas_call(
        _xattn_router_kernel,
        grid=(B, S // ts),
        in_specs=[tile(D), tile(D_ATT), tile(D_CONV), full(w_mix.shape),
                  full((1, D)), full(w_xq.shape), full((1, X_HEAD_DIM)), mem, mem,
                  full(w_xo.shape), full((1, D)), full(wr.shape),
                  full((1, LANES)), full((GROUP, GROUP))],
        out_specs=[tile(D), tile(D), tile(LANES),
                   pl.BlockSpec((1, N_EXPERTS, LANES), lambda b, j: (b * (S // ts) + j, 0, 0)),
                   pl.BlockSpec((1, SUBLANES, ts), lambda b, j: (b * (S // ts) + j, 0, 0))],
        out_shape=[jax.ShapeDtypeStruct((B, S, D), f32), jax.ShapeDtypeStruct((B, S, D), bf16),
                   jax.ShapeDtypeStruct((B, S, LANES), f32),
                   jax.ShapeDtypeStruct((B * S // ts, N_EXPERTS, LANES), f32),
                   jax.ShapeDtypeStruct((B * S // ts, SUBLANES, ts), f32)],
        compiler_params=pltpu.CompilerParams(
            dimension_semantics=("parallel", "parallel"), vmem_limit_bytes=VMEM_LIMIT),
        name="xattn_router",
    )(x, ya, yb, w_mix, gx, w_xq, qg, kx, vx, w_xo, gf, wr, br,
      jnp.asarray(np.triu(np.ones((GROUP, GROUP)), 1), bf16))


u32 = jnp.uint32
HIGH_HALF = 0xFFFF0000


def _pack_pairs(x):
    w = x.shape[1] // 2
    lo = pltpu.bitcast(x[:, :w], u32) >> 16
    hi = pltpu.bitcast(x[:, w:], u32) & u32(HIGH_HALF)
    return lo | hi


def _unpack_pairs(u):
    lo = pltpu.bitcast(u << 16, f32)
    hi = pltpu.bitcast(u & u32(HIGH_HALF), f32)
    return jnp.concatenate([lo, hi], axis=-1).astype(bf16)


def _for_cells(g, fn):
    def body(j, carry):
        for u in range(CELL_UNROLL):
            fn(g * N_EXPERTS + j * CELL_UNROLL + u, u % 2)
        return carry
    lax.fori_loop(0, N_EXPERTS // CELL_UNROLL, body, 0)


def _dispatch_kernel(coff_s, rows_s, gbase_s, gtot_s, toff_s, tn_s, nu_s,
                     hf_ref, post_ref, xs_hbm, xbuf, zbuf, sem, zsem, *, n_steps):
    step = pl.program_id(0)
    places = range(STEP_GROUPS)
    group = lambda i: step * STEP_GROUPS + i
    slot = lambda i: lax.rem(step, 2) * STEP_GROUPS + i
    tok = lambda i: slice(i * GROUP, (i + 1) * GROUP)

    def cell_copy(sl, i):
        n = pl.multiple_of(rows_s[i], SUBLANES)
        src = pl.multiple_of(coff_s[i], SUBLANES)
        dst = pl.multiple_of(gbase_s[i], SUBLANES)
        return n, pltpu.make_async_copy(xbuf.at[sl, pl.ds(src, n)], xs_hbm.at[pl.ds(dst, n)],
                                        sem.at[sl])

    def start_cell(sl, i, priority):
        n, cp = cell_copy(sl, i)
        pl.when(n > 0)(functools.partial(cp.start, priority))

    def wait_group(sl, gg):
        n = pl.multiple_of(gtot_s[gg], SUBLANES)
        pltpu.make_async_copy(xbuf.at[sl, pl.ds(0, n)], xs_hbm.at[pl.ds(0, n)], sem.at[sl]).wait()

    @pl.when(step >= 2)
    def _():
        for i in places:
            wait_group(slot(i), group(i) - 2 * STEP_GROUPS)

    riota = lax.broadcasted_iota(jnp.int32, (GROUP_ROWS, GROUP), 0)
    ps = []
    for i in places:
        p = jnp.zeros((GROUP_ROWS, GROUP), f32)
        for kk in range(TOP_K):
            p = jnp.where(riota == post_ref[0, kk:kk + 1, tok(i)].astype(jnp.int32), 1.0, p)
        ps.append(p.astype(bf16))
    rows = [_dot(ps[i], hf_ref[tok(i), :]) for i in places]
    for i in places:
        xbuf[slot(i)] = _pack_pairs(rows[i])
    for i in places:
        _for_cells(group(i), functools.partial(start_cell, slot(i)))

    @pl.when(step == n_steps - 1)
    def _():
        if n_steps >= 2:
            for i in places:
                prev_slot = (1 - lax.rem(step, 2)) * STEP_GROUPS + i
                wait_group(prev_slot, group(i) - STEP_GROUPS)
        for i in places:
            wait_group(slot(i), group(i))
        zbuf[...] = jnp.zeros(zbuf.shape, u32)

        def tail_copy(e):
            n = pl.multiple_of(tn_s[e], SUBLANES)
            dst = pl.multiple_of(toff_s[e], SUBLANES)
            return n, pltpu.make_async_copy(zbuf.at[pl.ds(0, n)], xs_hbm.at[pl.ds(dst, n)], zsem)

        def unused_copy(b):
            dst = pl.multiple_of(b * EXPERT_TILE, EXPERT_TILE)
            return pltpu.make_async_copy(zbuf, xs_hbm.at[pl.ds(dst, EXPERT_TILE)], zsem)

        def run(method):
            def tail(e, carry):
                n, cp = tail_copy(e)
                pl.when(n > 0)(getattr(cp, method))
                return carry
            lax.fori_loop(0, N_EXPERTS, tail, 0)

            def unused(b, carry):
                getattr(unused_copy(b), method)()
                return carry
            lax.fori_loop(nu_s[0], xs_hbm.shape[0] // EXPERT_TILE, unused, 0)
        run("start")
        run("wait")


def _dispatch(plan, hf, post, n_rows):
    n_tok, D = hf.shape
    n_steps = n_tok // (GROUP * STEP_GROUPS)
    assert n_steps * GROUP * STEP_GROUPS == n_tok
    step_tok = STEP_GROUPS * GROUP
    per_tile = SEQ_TILE // step_tok
    grid_spec = pltpu.PrefetchScalarGridSpec(
        num_scalar_prefetch=7,
        grid=(n_steps,),
        in_specs=[pl.BlockSpec((step_tok, D), lambda s, *_: (s, 0)),
                  pl.BlockSpec((1, SUBLANES, step_tok), lambda s, *_: (s // per_tile, 0, s % per_tile))],
        out_specs=pl.BlockSpec(memory_space=pl.ANY),
        scratch_shapes=[pltpu.VMEM((2 * STEP_GROUPS, GROUP_ROWS, D // 2), u32),
                        pltpu.VMEM((EXPERT_TILE, D // 2), u32),
                        pltpu.SemaphoreType.DMA((2 * STEP_GROUPS,)), pltpu.SemaphoreType.DMA(())],
    )
    return pl.pallas_call(
        functools.partial(_dispatch_kernel, n_steps=n_steps),
        grid_spec=grid_spec,
        out_shape=jax.ShapeDtypeStruct((n_rows, D // 2), u32),
        compiler_params=pltpu.CompilerParams(
            dimension_semantics=("arbitrary",), vmem_limit_bytes=VMEM_LIMIT),
        name="dispatch",
    )(plan["coff"], plan["rows"], plan["gbase"], plan["gtot"], plan["tailoff"], plan["tailn"],
      plan["n_used"], hf, post)


def _expert_kernel(eb_ref, nb_ref, nu_ref, wgu_ref, bgu_ref, wdn_ref, bdn_ref, pm_ref, xs_hbm,
                   y_hbm, wgu_b, wdn_b, xin, yout, in_sem, out_sem):
    e = pl.program_id(0)
    tm = EXPERT_TILE
    chunk_rows = EXPERT_CHUNK * tm
    n_used = nu_ref[0]
    n_chunks = (n_used + EXPERT_CHUNK - 1) // EXPERT_CHUNK

    def in_copy(c):
        sl = lax.rem(c, 2)
        rows = pl.ds(pl.multiple_of(c * chunk_rows, chunk_rows), chunk_rows)
        return pltpu.make_async_copy(xs_hbm.at[rows], xin.at[sl], in_sem.at[sl])

    def out_copy(c):
        sl = lax.rem(c, 2)
        rows = pl.ds(pl.multiple_of(c * chunk_rows, chunk_rows), chunk_rows)
        return pltpu.make_async_copy(yout.at[sl], y_hbm.at[rows], out_sem.at[sl])

    @pl.when(e == 0)
    def _():
        in_copy(0).start()
        yout[...] = jnp.zeros(yout.shape, u32)

    @pl.when(nb_ref[e] > 0)
    def _():
        for c in range(2 * D_FF // GU_TILE):
            cols = slice(c * GU_TILE, (c + 1) * GU_TILE)
            wgu_b[:, cols] = _dot(wgu_ref[0, :, cols].astype(bf16), pm_ref[...]).astype(bf16)
        for c in range(D_FF // LANES):
            rows = slice(c * LANES, (c + 1) * LANES)
            wdn_b[rows, :] = wdn_ref[0, rows, :].astype(bf16)

        def item(b, n_blk):
            c = b // EXPERT_CHUNK
            j = lax.rem(b, EXPERT_CHUNK)
            sl = lax.rem(c, 2)

            @pl.when(j == 0)
            def _():
                in_copy(c).wait()

                @pl.when(c + 1 < n_chunks)
                def _():
                    in_copy(c + 1).start()

                @pl.when(c >= 2)
                def _():
                    out_copy(c - 2).wait()

            rows = pl.ds(pl.multiple_of(j * tm, tm), n_blk * tm)
            _expert_block(xin.at[sl, rows], bgu_ref, bdn_ref, yout.at[sl, rows], wgu_b, wdn_b)
            last = b + n_blk - 1

            @pl.when((lax.rem(last, EXPERT_CHUNK) == EXPERT_CHUNK - 1) | (last == n_used - 1))
            def _():
                out_copy(c).start()

        first = eb_ref[e]
        n_head = jnp.minimum(lax.rem(EXPERT_CHUNK - lax.rem(first, EXPERT_CHUNK), EXPERT_CHUNK),
                             nb_ref[e])
        n_whole = (nb_ref[e] - n_head) // EXPERT_CHUNK
        n_tail = nb_ref[e] - n_head - n_whole * EXPERT_CHUNK

        def single(b, carry):
            item(b, 1)
            return carry

        def whole(i, carry):
            item(first + n_head + i * EXPERT_CHUNK, EXPERT_CHUNK)
            return carry
        lax.fori_loop(first, first + n_head, single, 0)
        lax.fori_loop(0, n_whole, whole, 0)
        tail0 = first + n_head + n_whole * EXPERT_CHUNK
        lax.fori_loop(tail0, tail0 + n_tail, single, 0)

    @pl.when(e == pl.num_programs(0) - 1)
    def _():
        @pl.when(n_chunks >= 2)
        def _():
            out_copy(n_chunks - 2).wait()
        out_copy(n_chunks - 1).wait()
        yout[0] = jnp.zeros(yout.shape[1:], u32)

        def run(method):
            def unused(c, carry):
                rows = pl.ds(pl.multiple_of(c * chunk_rows, chunk_rows), chunk_rows)
                getattr(pltpu.make_async_copy(yout.at[0], y_hbm.at[rows], out_sem.at[0]), method)()
                return carry
            lax.fori_loop(n_chunks, y_hbm.shape[0] // chunk_rows, unused, 0)
        run("start")
        run("wait")


def _expert_block(x_ref, bgu_ref, bdn_ref, y_ref, wgu_b, wdn_b):
    gu = _dot(_unpack_pairs(x_ref[...]), wgu_b[...]) + bgu_ref[0]
    acts = []
    for c in range(2 * D_FF // GU_TILE):
        g = jnp.minimum(gu[:, c * GU_TILE:c * GU_TILE + LANES], SWIGLU_LIMIT)
        lin = jnp.clip(gu[:, c * GU_TILE + LANES:(c + 1) * GU_TILE], -SWIGLU_LIMIT, SWIGLU_LIMIT)
        acts.append((g * jax.nn.sigmoid(SWIGLU_ALPHA * g) * (lin + 1.0)).astype(bf16))
    act = jnp.concatenate(acts, axis=-1)
    y = _dot(act, wdn_b[...]) + bdn_ref[0]
    y_ref[...] = _pack_pairs(y.astype(bf16).astype(f32))


def _gate_up_perm():
    j = np.arange(GU_TILE)
    p = np.zeros((GU_TILE, GU_TILE), np.float32)
    p[np.where(j < LANES, 2 * j, 2 * (j - LANES) + 1), j] = 1.0
    return jnp.asarray(p, bf16)


def _expert_mlp(plan, xs, w_gu, b_gu, w_dn, b_dn):
    n_rows = xs.shape[0]
    D = D_MODEL
    tm = EXPERT_TILE
    per_e = lambda e, *_: (e, 0, 0)
    grid_spec = pltpu.PrefetchScalarGridSpec(
        num_scalar_prefetch=3,
        grid=(N_EXPERTS,),
        in_specs=[pl.BlockSpec((1, D, 2 * D_FF), per_e),
                  pl.BlockSpec((1, 1, 2 * D_FF), per_e),
                  pl.BlockSpec((1, D_FF, D), per_e),
                  pl.BlockSpec((1, 1, D), per_e),
                  pl.BlockSpec((GU_TILE, GU_TILE), lambda e, *_: (0, 0)),
                  pl.BlockSpec(memory_space=pl.ANY)],
        out_specs=pl.BlockSpec(memory_space=pl.ANY),
        scratch_shapes=[pltpu.VMEM((D, 2 * D_FF), bf16), pltpu.VMEM((D_FF, D), bf16),
                        pltpu.VMEM((2, EXPERT_CHUNK * tm, D // 2), u32),
                        pltpu.VMEM((2, EXPERT_CHUNK * tm, D // 2), u32),
                        pltpu.SemaphoreType.DMA((2,)), pltpu.SemaphoreType.DMA((2,))],
    )
    return pl.pallas_call(
        _expert_kernel,
        grid_spec=grid_spec,
        out_shape=jax.ShapeDtypeStruct((n_rows, D // 2), u32),
        compiler_params=pltpu.CompilerParams(
            dimension_semantics=("arbitrary",), vmem_limit_bytes=VMEM_LIMIT),
        name="expert_mlp",
    )(plan["eblk"], plan["nblk"], plan["n_used"], w_gu, b_gu, w_dn, b_dn, _gate_up_perm(), xs)


def _combine_kernel(coff_s, rows_s, gbase_s, gtot_s, x_ref, route_ref, y_hbm, o_ref, ybuf, sem,
                    *, n_steps):
    step = pl.program_id(0)
    places = range(STEP_GROUPS)
    group = lambda s, i: s * STEP_GROUPS + i
    slot = lambda s, i: lax.rem(s, 2) * STEP_GROUPS + i
    tok = lambda i: slice(i * GROUP, (i + 1) * GROUP)

    def cell_copy(sl, i):
        n = pl.multiple_of(rows_s[i], SUBLANES)
        loc = pl.multiple_of(coff_s[i], SUBLANES)
        src = pl.multiple_of(gbase_s[i], SUBLANES)
        return n, pltpu.make_async_copy(y_hbm.at[pl.ds(src, n)], ybuf.at[sl, pl.ds(loc, n)],
                                        sem.at[sl])

    def start_cell(sl, i, priority):
        n, cp = cell_copy(sl, i)
        pl.when(n > 0)(functools.partial(cp.start, priority))

    def wait_group(sl, gg):
        n = pl.multiple_of(gtot_s[gg], SUBLANES)
        pltpu.make_async_copy(y_hbm.at[pl.ds(0, n)], ybuf.at[sl, pl.ds(0, n)], sem.at[sl]).wait()

    @pl.when(step == 0)
    def _():
        ybuf[...] = jnp.zeros(ybuf.shape, u32)
        for i in places:
            _for_cells(group(step, i), functools.partial(start_cell, slot(step, i)))

    @pl.when(step + 1 < n_steps)
    def _():
        for i in places:
            _for_cells(group(step + 1, i), functools.partial(start_cell, slot(step + 1, i)))

    for i in places:
        wait_group(slot(step, i), group(step, i))

    riota = lax.broadcasted_iota(jnp.int32, (GROUP, GROUP_ROWS), 1)
    pts = []
    for i in places:
        pt = jnp.zeros((GROUP, GROUP_ROWS), f32)
        for kk in range(TOP_K):
            pk = route_ref[tok(i), 2 * TOP_K + kk:2 * TOP_K + kk + 1].astype(jnp.int32)
            pt = jnp.where(riota == pk, route_ref[tok(i), TOP_K + kk:TOP_K + kk + 1], pt)
        pts.append(pt.astype(bf16))
    ys = [_unpack_pairs(ybuf[slot(step, i)]) for i in places]
    outs = [_dot(pts[i], ys[i]) for i in places]
    for i in places:
        o_ref[tok(i), :] = x_ref[tok(i), :] + outs[i]


def _combine(plan, x2, route, y_rows):
    n_tok, D = x2.shape
    n_steps = n_tok // (GROUP * STEP_GROUPS)
    assert n_steps * GROUP * STEP_GROUPS == n_tok
    tok = lambda w: pl.BlockSpec((STEP_GROUPS * GROUP, w), lambda s, *_: (s, 0))
    grid_spec = pltpu.PrefetchScalarGridSpec(
        num_scalar_prefetch=4,
        grid=(n_steps,),
        in_specs=[tok(D), tok(LANES), pl.BlockSpec(memory_space=pl.ANY)],
        out_specs=tok(D),
        scratch_shapes=[pltpu.VMEM((2 * STEP_GROUPS, GROUP_ROWS, D // 2), u32),
                        pltpu.SemaphoreType.DMA((2 * STEP_GROUPS,))],
    )
    return pl.pallas_call(
        functools.partial(_combine_kernel, n_steps=n_steps),
        grid_spec=grid_spec,
        out_shape=jax.ShapeDtypeStruct((n_tok, D), f32),
        compiler_params=pltpu.CompilerParams(
            dimension_semantics=("arbitrary",), vmem_limit_bytes=VMEM_LIMIT),
        name="combine",
    )(plan["coff"], plan["rows"], plan["gbase"], plan["gtot"], x2, route, y_rows)


def _routing_plan(cnt, n_tok):
    tm = EXPERT_TILE
    n_groups = n_tok // GROUP
    per_tile = SEQ_TILE // GROUP
    cnt = cnt[:, :, :per_tile].transpose(0, 2, 1).reshape(n_groups, N_EXPERTS).astype(jnp.int32)
    rows = (cnt + SUBLANES - 1) // SUBLANES * SUBLANES
    coff = jnp.cumsum(rows, axis=1) - rows
    tot = jnp.sum(rows, axis=0)
    padded = (tot + tm - 1) // tm * tm
    pend = jnp.cumsum(padded)
    ebase = pend - padded
    gbase = ebase[None, :] + jnp.cumsum(rows, axis=0) - rows
    flat = lambda a: a.reshape(-1).astype(jnp.int32)
    return dict(
        coff=flat(coff), rows=flat(rows), gbase=flat(gbase), gtot=flat(jnp.sum(rows, axis=1)),
        tailoff=flat(ebase + tot), tailn=flat(padded - tot),
        eblk=flat(ebase // tm), nblk=flat(padded // tm), n_used=flat(pend[-1] // tm))


def _max_rows(n_tok):
    tm = EXPERT_TILE
    worst = (n_tok * TOP_K + (n_tok // GROUP) * N_EXPERTS * (SUBLANES - 1)
             + N_EXPERTS * (tm - SUBLANES))
    chunk = EXPERT_CHUNK * tm
    return (worst + chunk - 1) // chunk * chunk


def _moe(x2, hf, route, post, cnt, w_gate_up, b_gate_up, w_down, b_down):
    n_tok, D = x2.shape
    plan = _routing_plan(cnt, n_tok)
    xs = _dispatch(plan, hf, post, _max_rows(n_tok))
    b_gu = (b_gate_up.reshape(N_EXPERTS, 2 * D_FF // GU_TILE, LANES, 2).transpose(0, 1, 3, 2)
            .reshape(N_EXPERTS, 1, 2 * D_FF))
    y_rows = _expert_mlp(plan, xs, w_gate_up, b_gu, w_down, b_down.reshape(N_EXPERTS, 1, D))
    return _combine(plan, x2, route, y_rows)


BIAS_ROW = (2 * Q_TILE + LEFT_CHUNKS * CHUNK + LANES - 2) // LANES * LANES


def _band_bias_kernel(r_ref, o_ref):
    qc = lax.broadcasted_iota(jnp.int32, (Q_TILE, K_BAND), 0) // CHUNK
    kc = lax.broadcasted_iota(jnp.int32, (Q_TILE, K_BAND), 1) // CHUNK
    first_chunk = LEFT_CHUNKS - pl.program_id(0) * (Q_TILE // CHUNK)
    visible = (kc >= qc) & (kc <= qc + LEFT_CHUNKS) & (kc >= first_chunk)
    for h in range(r_ref.shape[0]):
        x = jnp.broadcast_to(r_ref[h], (Q_TILE, BIAS_ROW))
        x = pltpu.roll(x, BIAS_ROW - (Q_TILE - 1), axis=1, stride=1, stride_axis=0)
        o_ref[0, h] = jnp.where(visible, x[:, :K_BAND] * LOG2E, NEG_INF)


def _band_bias(rel_bias):
    H = rel_bias.shape[0]
    d_max = Q_TILE - 1 + LEFT_CHUNKS * CHUNK
    n_far = d_max - MAX_REL + 1
    r = jnp.concatenate(
        [jnp.broadcast_to(rel_bias[:, 2 * MAX_REL:], (H, n_far)),
         rel_bias[:, MAX_REL - (Q_TILE - 1):2 * MAX_REL][:, ::-1],
         jnp.zeros((H, BIAS_ROW - (2 * Q_TILE - 1 + LEFT_CHUNKS * CHUNK)), rel_bias.dtype)], axis=1)
    return pl.pallas_call(
        _band_bias_kernel,
        grid=(START_TILES + 1,),
        in_specs=[pl.BlockSpec((H, 1, BIAS_ROW), lambda t: (0, 0, 0))],
        out_specs=pl.BlockSpec((1, H, Q_TILE, K_BAND), lambda t: (t, 0, 0, 0)),
        out_shape=jax.ShapeDtypeStruct((START_TILES + 1, H, Q_TILE, K_BAND), f32),
        name="band_bias",
    )(r.astype(f32).reshape(H, 1, BIAS_ROW))


def _layer(x, mem, norm_mix_g, w_in, q_norm_g, k_norm_g, rel_bias, conv_w, out_norm_a_g,
           out_norm_b_g, w_out, norm_xattn_g, norm_mem_g, w_xq, w_xkv, xq_norm_g, xk_norm_g,
           w_xo, norm_ffn_g, w_router, b_router, w_gate_up, b_gate_up, w_down, b_down):
    B, S, D = x.shape
    n_tok = B * S
    row = lambda a: a.reshape(1, -1).astype(f32)

    qg = row(jnp.tile(q_norm_g, ATT_HEADS) * (ATT_HEAD_DIM ** -0.5 * LOG2E))
    kg = row(jnp.tile(k_norm_g, ATT_HEADS))
    head_ones = jnp.asarray(np.kron(np.eye(ATT_HEADS), np.ones((ATT_HEAD_DIM, ATT_HEAD_DIM))), bf16)
    q, k, v, yb = _mix_in(x, row(norm_mix_g), w_in.astype(bf16), qg, kg, head_ones,
                          conv_w.astype(f32), row(out_norm_b_g))
    ya = _attn_out(q, k, v, _band_bias(rel_bias), row(out_norm_a_g))

    kx, vx = _mem_kv(mem, row(norm_mem_g), w_xkv.astype(bf16), row(xk_norm_g))
    wr = jnp.zeros((D, LANES), f32).at[:, :N_EXPERTS].set(w_router)
    wr_hi = wr.astype(bf16)
    wr_lo = (wr - wr_hi.astype(f32)).astype(bf16)
    br = jnp.full((1, LANES), -jnp.inf, f32).at[0, :N_EXPERTS].set(b_router)
    x2, hf, route, cnt, post = _xattn_router(
        x, ya, yb, w_out.astype(bf16), row(norm_xattn_g), w_xq.astype(bf16), row(xq_norm_g * (X_HEAD_DIM ** -0.5 * LOG2E)),
        kx, vx, w_xo.astype(bf16), row(norm_ffn_g), jnp.concatenate([wr_hi, wr_lo], axis=1), br)

    out = _moe(x2.reshape(n_tok, D), hf.reshape(n_tok, D), route.reshape(n_tok, LANES), post, cnt,
               w_gate_up, b_gate_up, w_down, b_down)
    return out.reshape(B, S, D)


def kernel(x, mem, norm_mix_g, w_in, q_norm_g, k_norm_g, rel_bias, conv_w, out_norm_a_g,
           out_norm_b_g, w_out, norm_xattn_g, norm_mem_g, w_xq, w_xkv, xq_norm_g, xk_norm_g,
           w_xo, norm_ffn_g, w_router, b_router, w_gate_up, b_gate_up, w_down, b_down):
    depth = norm_mix_g.shape[0]
    for l in range(depth):
        x = _layer(x, mem, norm_mix_g[l], w_in[l], q_norm_g[l], k_norm_g[l], rel_bias[l],
                   conv_w[l], out_norm_a_g[l], out_norm_b_g[l], w_out[l], norm_xattn_g[l],
                   norm_mem_g[l], w_xq[l], w_xkv[l], xq_norm_g[l], xk_norm_g[l], w_xo[l],
                   norm_ffn_g[l], w_router[l], b_router[l], w_gate_up[l], b_gate_up[l],
                   w_down[l], b_down[l])
    return x
```
